```python
import math
import jax, jax.numpy as jnp
from jax import lax
import numpy as np


D_MODEL = 1024
BATCH = 8
SEQ = 8192
DEPTH = 4

N_HEADS = 8
KV_HEADS = 2
GROUP = N_HEADS // KV_HEADS
HEAD_DIM = 64
Q_DIM = N_HEADS * HEAD_DIM
KV_DIM = KV_HEADS * HEAD_DIM
WINDOW = 128
BLOCK = 128
NUM_BUCKETS = 32
MAX_DISTANCE = 128
CONV_DIM = 512
CONV_WIDTH = 31
N_BRANCHES = 2
GATE_DIM = N_BRANCHES * D_MODEL
IN_DIM = Q_DIM + 2 * KV_DIM + 2 * CONV_DIM + GATE_DIM
D_FF = 2816
PLE_DIM = 256
NEG_INF = -1e9

kernel_name = "hybrid_gated_swa_conformer_encoder"


def rms_norm(x, g, eps=1e-6):
    xf = x.astype(jnp.float32)
    y = xf * lax.rsqrt(jnp.mean(xf * xf, axis=-1, keepdims=True) + eps)
    return (y * g.astype(jnp.float32)).astype(x.dtype)


def layer_norm(x, g, b, eps=1e-5):
    xf = x.astype(jnp.float32)
    mu = jnp.mean(xf, axis=-1, keepdims=True)
    var = jnp.mean(jnp.square(xf - mu), axis=-1, keepdims=True)
    y = (xf - mu) * lax.rsqrt(var + eps)
    return (y * g.astype(jnp.float32) + b.astype(jnp.float32)).astype(x.dtype)


def swiglu(x, w_in, w_out):
    gate, up = jnp.split(x @ w_in, 2, axis=-1)
    return (jax.nn.silu(gate) * up) @ w_out


def t5_buckets(rel):
    half = NUM_BUCKETS // 2
    max_exact = half // 2
    n = jnp.abs(rel)
    ret = jnp.where(rel > 0, half, 0)
    nf = jnp.maximum(n, 1).astype(jnp.float32)
    large = max_exact + (jnp.log(nf / max_exact) / math.log(MAX_DISTANCE / max_exact)
                         * (half - max_exact)).astype(jnp.int32)
    large = jnp.minimum(large, half - 1)
    return ret + jnp.where(n < max_exact, n, large)


def window_attention(q, k, v, sink, rel_bias):
    B, S = q.shape[0], q.shape[1]
    nb = S // BLOCK
    qb = q.reshape(B, nb, BLOCK, KV_HEADS, GROUP, HEAD_DIM)

    def windows(t):
        tp = jnp.pad(t, ((0, 0), (BLOCK, BLOCK), (0, 0), (0, 0)))
        tp = tp.reshape(B, nb + 2, BLOCK, KV_HEADS, HEAD_DIM)
        return jnp.concatenate([tp[:, :-2], tp[:, 1:-1], tp[:, 2:]], axis=2)

    kw, vw = windows(k), windows(v)
    scale = HEAD_DIM ** -0.5
    s = jnp.einsum('bnqkgd,bnskd->bnkgqs', qb, kw).astype(jnp.float32) * scale

    qi = jnp.arange(BLOCK)
    kj = jnp.arange(3 * BLOCK)
    rel = kj[None, :] - BLOCK - qi[:, None]
    bias = rel_bias.astype(jnp.float32)[t5_buckets(rel)]
    bias = bias.transpose(2, 0, 1).reshape(KV_HEADS, GROUP, BLOCK, 3 * BLOCK)
    kpos = jnp.arange(nb)[:, None] * BLOCK - BLOCK + kj[None, :]
    valid = (jnp.abs(rel) <= WINDOW)[None] & ((kpos >= 0) & (kpos < S))[:, None, :]

    s = s + bias[None, None]
    s = jnp.where(valid[None, :, None, None], s, NEG_INF)
    sink_l = jnp.broadcast_to(sink.astype(jnp.float32).reshape(KV_HEADS, GROUP)[None, None, :, :, None, None],
                              s.shape[:-1] + (1,))
    pr = jax.nn.softmax(jnp.concatenate([s, sink_l], axis=-1), axis=-1)[..., :-1]
    o = jnp.einsum('bnkgqs,bnskd->bnqkgd', pr.astype(v.dtype), vw)
    return o.reshape(B, S, Q_DIM)


def depthwise_conv(c, w, b):
    pad = CONV_WIDTH // 2
    y = lax.conv_general_dilated(c, w.reshape(CONV_WIDTH, 1, CONV_DIM).astype(c.dtype),
                                 window_strides=(1,), padding=[(pad, pad)],
                                 dimension_numbers=('NWC', 'WIO', 'NWC'),
                                 feature_group_count=CONV_DIM)
    return y + b


def _fwd_setup_inputs(seed: int = 0) -> dict:
    key = jax.random.key(seed)
    ks = jax.random.split(key, 26)
    f32 = jnp.float32

    def w(k, shape, fan_in):
        return jax.random.normal(k, shape, f32) * (fan_in ** -0.5)

    def gain(k, shape):
        return 1.0 + 0.02 * jax.random.normal(k, shape, f32)

    L, D = DEPTH, D_MODEL
    return {
        "x": jax.random.normal(ks[0], (BATCH, SEQ, D), f32),
        "p": jax.random.normal(ks[1], (DEPTH, BATCH, SEQ, PLE_DIM), f32),
        "rel_bias": 0.1 * jax.random.normal(ks[2], (NUM_BUCKETS, N_HEADS), f32),
        "norm_ffn1": gain(ks[3], (L, D)),
        "w_ffn1_in": w(ks[4], (L, D, 2 * D_FF), D),
        "w_ffn1_out": w(ks[5], (L, D_FF, D), D_FF),
        "norm_mix": gain(ks[6], (L, D)),
        "w_in": w(ks[7], (L, D, IN_DIM), D),
        "q_norm": gain(ks[8], (L, HEAD_DIM)),
        "k_norm": gain(ks[9], (L, HEAD_DIM)),
        "sink": 0.5 * jax.random.normal(ks[10], (L, N_HEADS), f32),
        "conv_w": w(ks[11], (L, CONV_WIDTH, CONV_DIM), CONV_WIDTH),
        "conv_b": 0.02 * jax.random.normal(ks[12], (L, CONV_DIM), f32),
        "conv_ln_g": gain(ks[13], (L, CONV_DIM)),
        "conv_ln_b": 0.02 * jax.random.normal(ks[14], (L, CONV_DIM), f32),
        "w_attn_out": w(ks[15], (L, Q_DIM, D), Q_DIM),
        "w_conv_out": w(ks[16], (L, CONV_DIM, D), CONV_DIM),
        "w_o": w(ks[17], (L, D, D), D),
        "norm_ffn2": gain(ks[18], (L, D)),
        "w_ffn2_in": w(ks[19], (L, D, 2 * D_FF), D),
        "w_ffn2_out": w(ks[20], (L, D_FF, D), D_FF),
        "norm_pe": gain(ks[21], (L, D)),
        "w_pe_gate": w(ks[22], (L, D, D), D),
        "w_pe_proj": w(ks[23], (L, PLE_DIM, D), PLE_DIM),
    }


def _fwd_reference(x, p, rel_bias, norm_ffn1, w_ffn1_in, w_ffn1_out, norm_mix, w_in, q_norm, k_norm,
              sink, conv_w, conv_b, conv_ln_g, conv_ln_b, w_attn_out, w_conv_out, w_o,
              norm_ffn2, w_ffn2_in, w_ffn2_out, norm_pe, w_pe_gate, w_pe_proj):
    B, S = x.shape[0], x.shape[1]
    splits = [Q_DIM, Q_DIM + KV_DIM, Q_DIM + 2 * KV_DIM, Q_DIM + 2 * KV_DIM + 2 * CONV_DIM]
    for i in range(DEPTH):
        h = x + 0.5 * swiglu(rms_norm(x, norm_ffn1[i]), w_ffn1_in[i], w_ffn1_out[i])

        u = rms_norm(h, norm_mix[i])
        q, k, v, c, g = jnp.split(u @ w_in[i], splits, axis=-1)

        q = rms_norm(q.reshape(B, S, N_HEADS, HEAD_DIM), q_norm[i])
        k = rms_norm(k.reshape(B, S, KV_HEADS, HEAD_DIM), k_norm[i])
        v = v.reshape(B, S, KV_HEADS, HEAD_DIM)
        y_attn = window_attention(q, k, v, sink[i], rel_bias) @ w_attn_out[i]

        c_val, c_gate = jnp.split(c, 2, axis=-1)
        c = c_val * jax.nn.sigmoid(c_gate)
        c = depthwise_conv(c, conv_w[i], conv_b[i])
        c = jax.nn.silu(layer_norm(c, conv_ln_g[i], conv_ln_b[i]))
        y_conv = c @ w_conv_out[i]

        g_attn, g_conv = jnp.split(jax.nn.sigmoid(g), 2, axis=-1)
        h = h + (g_attn * y_attn + g_conv * y_conv) @ w_o[i]

        h = h + 0.5 * swiglu(rms_norm(h, norm_ffn2[i]), w_ffn2_in[i], w_ffn2_out[i])

        x = h + (p[i] @ w_pe_proj[i]) * jax.nn.sigmoid(rms_norm(h, norm_pe[i]) @ w_pe_gate[i])
    return x


import jax as _jax
import jax.numpy as _jnp

TWIN_FORMAT = 'train_step'
FWD_PARAMS = ['x', 'p', 'rel_bias', 'norm_ffn1', 'w_ffn1_in', 'w_ffn1_out', 'norm_mix', 'w_in', 'q_norm', 'k_norm', 'sink', 'conv_w', 'conv_b', 'conv_ln_g', 'conv_ln_b', 'w_attn_out', 'w_conv_out', 'w_o', 'norm_ffn2', 'w_ffn2_in', 'w_ffn2_out', 'norm_pe', 'w_pe_gate', 'w_pe_proj']
TWIN_WEIGHTS = ['rel_bias', 'norm_ffn1', 'w_ffn1_in', 'w_ffn1_out', 'norm_mix', 'w_in', 'q_norm', 'k_norm', 'sink', 'conv_w', 'conv_b', 'conv_ln_g', 'conv_ln_b', 'w_attn_out', 'w_conv_out', 'w_o', 'norm_ffn2', 'w_ffn2_in', 'w_ffn2_out', 'norm_pe', 'w_pe_gate', 'w_pe_proj']
TWIN_DIFF_INPUT = 'x'
TWIN_INPUTS = ['x', 'p', 'rel_bias', 'norm_ffn1', 'w_ffn1_in', 'w_ffn1_out', 'norm_mix', 'w_in', 'q_norm', 'k_norm', 'sink', 'conv_w', 'conv_b', 'conv_ln_g', 'conv_ln_b', 'w_attn_out', 'w_conv_out', 'w_o', 'norm_ffn2', 'w_ffn2_in', 'w_ffn2_out', 'norm_pe', 'w_pe_gate', 'w_pe_proj', 'loss_target', 'm_rel_bias', 'm_norm_ffn1', 'm_w_ffn1_in', 'm_w_ffn1_out', 'm_norm_mix', 'm_w_in', 'm_q_norm', 'm_k_norm', 'm_sink', 'm_conv_w', 'm_conv_b', 'm_conv_ln_g', 'm_conv_ln_b', 'm_w_attn_out', 'm_w_conv_out', 'm_w_o', 'm_norm_ffn2', 'm_w_ffn2_in', 'm_w_ffn2_out', 'm_norm_pe', 'm_w_pe_gate', 'm_w_pe_proj', 'v_rel_bias', 'v_norm_ffn1', 'v_w_ffn1_in', 'v_w_ffn1_out', 'v_norm_mix', 'v_w_in', 'v_q_norm', 'v_k_norm', 'v_sink', 'v_conv_w', 'v_conv_b', 'v_conv_ln_g', 'v_conv_ln_b', 'v_w_attn_out', 'v_w_conv_out', 'v_w_o', 'v_norm_ffn2', 'v_w_ffn2_in', 'v_w_ffn2_out', 'v_norm_pe', 'v_w_pe_gate', 'v_w_pe_proj']
TWIN_OUTPUTS = ['loss', 'grad_x', 'grad_rel_bias', 'grad_norm_ffn1', 'grad_w_ffn1_in', 'grad_w_ffn1_out', 'grad_norm_mix', 'grad_w_in', 'grad_q_norm', 'grad_k_norm', 'grad_sink', 'grad_conv_w', 'grad_conv_b', 'grad_conv_ln_g', 'grad_conv_ln_b', 'grad_w_attn_out', 'grad_w_conv_out', 'grad_w_o', 'grad_norm_ffn2', 'grad_w_ffn2_in', 'grad_w_ffn2_out', 'grad_norm_pe', 'grad_w_pe_gate', 'grad_w_pe_proj', 'delta_rel_bias', 'delta_norm_ffn1', 'delta_w_ffn1_in', 'delta_w_ffn1_out', 'delta_norm_mix', 'delta_w_in', 'delta_q_norm', 'delta_k_norm', 'delta_sink', 'delta_conv_w', 'delta_conv_b', 'delta_conv_ln_g', 'delta_conv_ln_b', 'delta_w_attn_out', 'delta_w_conv_out', 'delta_w_o', 'delta_norm_ffn2', 'delta_w_ffn2_in', 'delta_w_ffn2_out', 'delta_norm_pe', 'delta_w_pe_gate', 'delta_w_pe_proj', 'new_m_rel_bias', 'new_m_norm_ffn1', 'new_m_w_ffn1_in', 'new_m_w_ffn1_out', 'new_m_norm_mix', 'new_m_w_in', 'new_m_q_norm', 'new_m_k_norm', 'new_m_sink', 'new_m_conv_w', 'new_m_conv_b', 'new_m_conv_ln_g', 'new_m_conv_ln_b', 'new_m_w_attn_out', 'new_m_w_conv_out', 'new_m_w_o', 'new_m_norm_ffn2', 'new_m_w_ffn2_in', 'new_m_w_ffn2_out', 'new_m_norm_pe', 'new_m_w_pe_gate', 'new_m_w_pe_proj', 'new_v_rel_bias', 'new_v_norm_ffn1', 'new_v_w_ffn1_in', 'new_v_w_ffn1_out', 'new_v_norm_mix', 'new_v_w_in', 'new_v_q_norm', 'new_v_k_norm', 'new_v_sink', 'new_v_conv_w', 'new_v_conv_b', 'new_v_conv_ln_g', 'new_v_conv_ln_b', 'new_v_w_attn_out', 'new_v_w_conv_out', 'new_v_w_o', 'new_v_norm_ffn2', 'new_v_w_ffn2_in', 'new_v_w_ffn2_out', 'new_v_norm_pe', 'new_v_w_pe_gate', 'new_v_w_pe_proj']
TWIN_LEAF_KINDS = {'loss': 'loss', 'grad_x': 'grad_x', 'grad_rel_bias': 'grad_w', 'grad_norm_ffn1': 'grad_w', 'grad_w_ffn1_in': 'grad_w', 'grad_w_ffn1_out': 'grad_w', 'grad_norm_mix': 'grad_w', 'grad_w_in': 'grad_w', 'grad_q_norm': 'grad_w', 'grad_k_norm': 'grad_w', 'grad_sink': 'grad_w', 'grad_conv_w': 'grad_w', 'grad_conv_b': 'grad_w', 'grad_conv_ln_g': 'grad_w', 'grad_conv_ln_b': 'grad_w', 'grad_w_attn_out': 'grad_w', 'grad_w_conv_out': 'grad_w', 'grad_w_o': 'grad_w', 'grad_norm_ffn2': 'grad_w', 'grad_w_ffn2_in': 'grad_w', 'grad_w_ffn2_out': 'grad_w', 'grad_norm_pe': 'grad_w', 'grad_w_pe_gate': 'grad_w', 'grad_w_pe_proj': 'grad_w', 'delta_rel_bias': 'delta_w', 'delta_norm_ffn1': 'delta_w', 'delta_w_ffn1_in': 'delta_w', 'delta_w_ffn1_out': 'delta_w', 'delta_norm_mix': 'delta_w', 'delta_w_in': 'delta_w', 'delta_q_norm': 'delta_w', 'delta_k_norm': 'delta_w', 'delta_sink': 'delta_w', 'delta_conv_w': 'delta_w', 'delta_conv_b': 'delta_w', 'delta_conv_ln_g': 'delta_w', 'delta_conv_ln_b': 'delta_w', 'delta_w_attn_out': 'delta_w', 'delta_w_conv_out': 'delta_w', 'delta_w_o': 'delta_w', 'delta_norm_ffn2': 'delta_w', 'delta_w_ffn2_in': 'delta_w', 'delta_w_ffn2_out': 'delta_w', 'delta_norm_pe': 'delta_w', 'delta_w_pe_gate': 'delta_w', 'delta_w_pe_proj': 'delta_w', 'new_m_rel_bias': 'new_m', 'new_m_norm_ffn1': 'new_m', 'new_m_w_ffn1_in': 'new_m', 'new_m_w_ffn1_out': 'new_m', 'new_m_norm_mix': 'new_m', 'new_m_w_in': 'new_m', 'new_m_q_norm': 'new_m', 'new_m_k_norm': 'new_m', 'new_m_sink': 'new_m', 'new_m_conv_w': 'new_m', 'new_m_conv_b': 'new_m', 'new_m_conv_ln_g': 'new_m', 'new_m_conv_ln_b': 'new_m', 'new_m_w_attn_out': 'new_m', 'new_m_w_conv_out': 'new_m', 'new_m_w_o': 'new_m', 'new_m_norm_ffn2': 'new_m', 'new_m_w_ffn2_in': 'new_m', 'new_m_w_ffn2_out': 'new_m', 'new_m_norm_pe': 'new_m', 'new_m_w_pe_gate': 'new_m', 'new_m_w_pe_proj': 'new_m', 'new_v_rel_bias': 'new_v', 'new_v_norm_ffn1': 'new_v', 'new_v_w_ffn1_in': 'new_v', 'new_v_w_ffn1_out': 'new_v', 'new_v_norm_mix': 'new_v', 'new_v_w_in': 'new_v', 'new_v_q_norm': 'new_v', 'new_v_k_norm': 'new_v', 'new_v_sink': 'new_v', 'new_v_conv_w': 'new_v', 'new_v_conv_b': 'new_v', 'new_v_conv_ln_g': 'new_v', 'new_v_conv_ln_b': 'new_v', 'new_v_w_attn_out': 'new_v', 'new_v_w_conv_out': 'new_v', 'new_v_w_o': 'new_v', 'new_v_norm_ffn2': 'new_v', 'new_v_w_ffn2_in': 'new_v', 'new_v_w_ffn2_out': 'new_v', 'new_v_norm_pe': 'new_v', 'new_v_w_pe_gate': 'new_v', 'new_v_w_pe_proj': 'new_v'}


def _forward(args):
    return _fwd_reference(*[args[k] for k in FWD_PARAMS])


def _output_shape():
    def fwd():
        inp = _fwd_setup_inputs(0)
        return _fwd_reference(*[inp[k] for k in FWD_PARAMS])
    out = _jax.eval_shape(fwd)
    return out.shape, out.dtype

N_MICROBATCH = 1
ADAM_LR = 0.001
ADAM_B1 = 0.9
ADAM_B2 = 0.999
ADAM_EPS = 1e-08
ADAM_WD = 0.01
ADAM_STEP = 10
PER_EXAMPLE_BATCH_AXIS = {'x': 0, 'p': 1, 'loss_target': 0}
SHARED_INPUTS = []
_WEIGHT_DTYPES = {'rel_bias': _jnp.float32, 'norm_ffn1': _jnp.float32, 'w_ffn1_in': _jnp.float32, 'w_ffn1_out': _jnp.float32, 'norm_mix': _jnp.float32, 'w_in': _jnp.float32, 'q_norm': _jnp.float32, 'k_norm': _jnp.float32, 'sink': _jnp.float32, 'conv_w': _jnp.float32, 'conv_b': _jnp.float32, 'conv_ln_g': _jnp.float32, 'conv_ln_b': _jnp.float32, 'w_attn_out': _jnp.float32, 'w_conv_out': _jnp.float32, 'w_o': _jnp.float32, 'norm_ffn2': _jnp.float32, 'w_ffn2_in': _jnp.float32, 'w_ffn2_out': _jnp.float32, 'norm_pe': _jnp.float32, 'w_pe_gate': _jnp.float32, 'w_pe_proj': _jnp.float32}
MOMENT_SCALE = {'rel_bias': 3.544095e-01, 'norm_ffn1': 1.225664e+01, 'w_ffn1_in': 1.906975e-01, 'w_ffn1_out': 3.311947e-01, 'norm_mix': 1.262613e+00, 'w_in': 4.189220e-01, 'q_norm': 2.124818e+00, 'k_norm': 2.129942e+00, 'sink': 9.341660e-02, 'conv_w': 8.758569e-01, 'conv_b': 1.446498e+01, 'conv_ln_g': 1.771473e+01, 'conv_ln_b': 1.342654e+01, 'w_attn_out': 7.349290e-01, 'w_conv_out': 2.203126e+00, 'w_o': 2.057348e+00, 'norm_ffn2': 1.232408e+01, 'w_ffn2_in': 1.975761e-01, 'w_ffn2_out': 3.412206e-01, 'norm_pe': 2.052417e+00, 'w_pe_gate': 5.816647e-01, 'w_pe_proj': 8.532347e-01}


def _to_microbatches(a, axis):
    t = _jnp.moveaxis(a, axis, 0)
    t = t.reshape((N_MICROBATCH, t.shape[0] // N_MICROBATCH) + t.shape[1:])
    return _jnp.moveaxis(t, 1, axis + 1)


def setup_inputs(seed: int = 0) -> dict:
    inp = _fwd_setup_inputs(seed)
    key = _jax.random.fold_in(_jax.random.key(seed), 7919)
    shape, _ = _output_shape()
    out = dict(inp)
    out["loss_target"] = _jax.random.normal(_jax.random.fold_in(key, 0), shape, _jnp.float32)
    for i, name in enumerate(TWIN_WEIGHTS):
        w = inp[name].astype(_jnp.float32)
        if MOMENT_SCALE is None:
            s = _jnp.sqrt(_jnp.mean(_jnp.square(w)) + 1e-30)
        else:
            s = MOMENT_SCALE[name]
        km, kv = _jax.random.split(_jax.random.fold_in(key, i + 1))
        out[name] = w
        out["m_" + name] = s * _jax.random.normal(km, w.shape, _jnp.float32)
        out["v_" + name] = (s * s) * _jax.random.uniform(kv, w.shape, _jnp.float32, 0.5, 1.5)
    if N_MICROBATCH > 1:
        for name, axis in PER_EXAMPLE_BATCH_AXIS.items():
            out[name] = _to_microbatches(out[name], axis)
    return {'x': out['x'], 'p': out['p'], 'rel_bias': out['rel_bias'], 'norm_ffn1': out['norm_ffn1'], 'w_ffn1_in': out['w_ffn1_in'], 'w_ffn1_out': out['w_ffn1_out'], 'norm_mix': out['norm_mix'], 'w_in': out['w_in'], 'q_norm': out['q_norm'], 'k_norm': out['k_norm'], 'sink': out['sink'], 'conv_w': out['conv_w'], 'conv_b': out['conv_b'], 'conv_ln_g': out['conv_ln_g'], 'conv_ln_b': out['conv_ln_b'], 'w_attn_out': out['w_attn_out'], 'w_conv_out': out['w_conv_out'], 'w_o': out['w_o'], 'norm_ffn2': out['norm_ffn2'], 'w_ffn2_in': out['w_ffn2_in'], 'w_ffn2_out': out['w_ffn2_out'], 'norm_pe': out['norm_pe'], 'w_pe_gate': out['w_pe_gate'], 'w_pe_proj': out['w_pe_proj'], 'loss_target': out['loss_target'], 'm_rel_bias': out['m_rel_bias'], 'm_norm_ffn1': out['m_norm_ffn1'], 'm_w_ffn1_in': out['m_w_ffn1_in'], 'm_w_ffn1_out': out['m_w_ffn1_out'], 'm_norm_mix': out['m_norm_mix'], 'm_w_in': out['m_w_in'], 'm_q_norm': out['m_q_norm'], 'm_k_norm': out['m_k_norm'], 'm_sink': out['m_sink'], 'm_conv_w': out['m_conv_w'], 'm_conv_b': out['m_conv_b'], 'm_conv_ln_g': out['m_conv_ln_g'], 'm_conv_ln_b': out['m_conv_ln_b'], 'm_w_attn_out': out['m_w_attn_out'], 'm_w_conv_out': out['m_w_conv_out'], 'm_w_o': out['m_w_o'], 'm_norm_ffn2': out['m_norm_ffn2'], 'm_w_ffn2_in': out['m_w_ffn2_in'], 'm_w_ffn2_out': out['m_w_ffn2_out'], 'm_norm_pe': out['m_norm_pe'], 'm_w_pe_gate': out['m_w_pe_gate'], 'm_w_pe_proj': out['m_w_pe_proj'], 'v_rel_bias': out['v_rel_bias'], 'v_norm_ffn1': out['v_norm_ffn1'], 'v_w_ffn1_in': out['v_w_ffn1_in'], 'v_w_ffn1_out': out['v_w_ffn1_out'], 'v_norm_mix': out['v_norm_mix'], 'v_w_in': out['v_w_in'], 'v_q_norm': out['v_q_norm'], 'v_k_norm': out['v_k_norm'], 'v_sink': out['v_sink'], 'v_conv_w': out['v_conv_w'], 'v_conv_b': out['v_conv_b'], 'v_conv_ln_g': out['v_conv_ln_g'], 'v_conv_ln_b': out['v_conv_ln_b'], 'v_w_attn_out': out['v_w_attn_out'], 'v_w_conv_out': out['v_w_conv_out'], 'v_w_o': out['v_w_o'], 'v_norm_ffn2': out['v_norm_ffn2'], 'v_w_ffn2_in': out['v_w_ffn2_in'], 'v_w_ffn2_out': out['v_w_ffn2_out'], 'v_norm_pe': out['v_norm_pe'], 'v_w_pe_gate': out['v_w_pe_gate'], 'v_w_pe_proj': out['v_w_pe_proj']}


def _loss(weights, diff, rest, loss_target):
    with _jax.named_scope("forward"):
        args = {**rest, TWIN_DIFF_INPUT: diff, **{k: w.astype(_WEIGHT_DTYPES[k]) for k, w in weights.items()}}
        y = _forward(args)
    with _jax.named_scope("loss_head"):
        err = _jnp.square(y.astype(_jnp.float32) - loss_target)
        return 0.5 * _jnp.sum(_jnp.mean(err, axis=-1)) if err.ndim else 0.5 * err


def _adamw(w, g, m, v):
    m = ADAM_B1 * m + (1.0 - ADAM_B1) * g
    v = ADAM_B2 * v + (1.0 - ADAM_B2) * _jnp.square(g)
    m_hat = m / (1.0 - ADAM_B1 ** ADAM_STEP)
    v_hat = v / (1.0 - ADAM_B2 ** ADAM_STEP)
    delta = -ADAM_LR * (m_hat / (_jnp.sqrt(v_hat) + ADAM_EPS) + ADAM_WD * w)
    return delta, m, v


def reference(x, p, rel_bias, norm_ffn1, w_ffn1_in, w_ffn1_out, norm_mix, w_in, q_norm, k_norm, sink, conv_w, conv_b, conv_ln_g, conv_ln_b, w_attn_out, w_conv_out, w_o, norm_ffn2, w_ffn2_in, w_ffn2_out, norm_pe, w_pe_gate, w_pe_proj, loss_target, m_rel_bias, m_norm_ffn1, m_w_ffn1_in, m_w_ffn1_out, m_norm_mix, m_w_in, m_q_norm, m_k_norm, m_sink, m_conv_w, m_conv_b, m_conv_ln_g, m_conv_ln_b, m_w_attn_out, m_w_conv_out, m_w_o, m_norm_ffn2, m_w_ffn2_in, m_w_ffn2_out, m_norm_pe, m_w_pe_gate, m_w_pe_proj, v_rel_bias, v_norm_ffn1, v_w_ffn1_in, v_w_ffn1_out, v_norm_mix, v_w_in, v_q_norm, v_k_norm, v_sink, v_conv_w, v_conv_b, v_conv_ln_g, v_conv_ln_b, v_w_attn_out, v_w_conv_out, v_w_o, v_norm_ffn2, v_w_ffn2_in, v_w_ffn2_out, v_norm_pe, v_w_pe_gate, v_w_pe_proj):
    given = dict(x=x, p=p, rel_bias=rel_bias, norm_ffn1=norm_ffn1, w_ffn1_in=w_ffn1_in, w_ffn1_out=w_ffn1_out, norm_mix=norm_mix, w_in=w_in, q_norm=q_norm, k_norm=k_norm, sink=sink, conv_w=conv_w, conv_b=conv_b, conv_ln_g=conv_ln_g, conv_ln_b=conv_ln_b, w_attn_out=w_attn_out, w_conv_out=w_conv_out, w_o=w_o, norm_ffn2=norm_ffn2, w_ffn2_in=w_ffn2_in, w_ffn2_out=w_ffn2_out, norm_pe=norm_pe, w_pe_gate=w_pe_gate, w_pe_proj=w_pe_proj, loss_target=loss_target, m_rel_bias=m_rel_bias, m_norm_ffn1=m_norm_ffn1, m_w_ffn1_in=m_w_ffn1_in, m_w_ffn1_out=m_w_ffn1_out, m_norm_mix=m_norm_mix, m_w_in=m_w_in, m_q_norm=m_q_norm, m_k_norm=m_k_norm, m_sink=m_sink, m_conv_w=m_conv_w, m_conv_b=m_conv_b, m_conv_ln_g=m_conv_ln_g, m_conv_ln_b=m_conv_ln_b, m_w_attn_out=m_w_attn_out, m_w_conv_out=m_w_conv_out, m_w_o=m_w_o, m_norm_ffn2=m_norm_ffn2, m_w_ffn2_in=m_w_ffn2_in, m_w_ffn2_out=m_w_ffn2_out, m_norm_pe=m_norm_pe, m_w_pe_gate=m_w_pe_gate, m_w_pe_proj=m_w_pe_proj, v_rel_bias=v_rel_bias, v_norm_ffn1=v_norm_ffn1, v_w_ffn1_in=v_w_ffn1_in, v_w_ffn1_out=v_w_ffn1_out, v_norm_mix=v_norm_mix, v_w_in=v_w_in, v_q_norm=v_q_norm, v_k_norm=v_k_norm, v_sink=v_sink, v_conv_w=v_conv_w, v_conv_b=v_conv_b, v_conv_ln_g=v_conv_ln_g, v_conv_ln_b=v_conv_ln_b, v_w_attn_out=v_w_attn_out, v_w_conv_out=v_w_conv_out, v_w_o=v_w_o, v_norm_ffn2=v_norm_ffn2, v_w_ffn2_in=v_w_ffn2_in, v_w_ffn2_out=v_w_ffn2_out, v_norm_pe=v_norm_pe, v_w_pe_gate=v_w_pe_gate, v_w_pe_proj=v_w_pe_proj)
    weights = {n: given[n] for n in TWIN_WEIGHTS}
    shared = {n: given[n] for n in SHARED_INPUTS}
    per_example = {n: given[n] for n in ['x', 'p']}
    grad_fn = _jax.value_and_grad(_loss, argnums=(0, 1))

    def one_microbatch(ex, loss_target):
        ex = dict(ex)
        diff = ex.pop(TWIN_DIFF_INPUT)
        return grad_fn(weights, diff, {**shared, **ex}, loss_target)

    if N_MICROBATCH == 1:
        loss, (grad_w, grad_x) = one_microbatch(per_example, given["loss_target"])
    else:
        def body(carry, xs):
            loss_sum, grad_sum = carry
            l_k, (gw_k, gx_k) = one_microbatch(xs[0], xs[1])
            with _jax.named_scope("update"):
                return (loss_sum + l_k, _jax.tree.map(_jnp.add, grad_sum, gw_k)), gx_k

        init = (_jnp.zeros((), _jnp.float32), _jax.tree.map(_jnp.zeros_like, weights))
        (loss, grad_w), grad_x = _jax.lax.scan(body, init, (per_example, given["loss_target"]))
    with _jax.named_scope("update"):
        delta_w, new_m, new_v = {}, {}, {}
        for n in TWIN_WEIGHTS:
            delta_w[n], new_m[n], new_v[n] = _adamw(weights[n], grad_w[n], given["m_" + n], given["v_" + n])
    return (loss, grad_x, *[grad_w[n] for n in TWIN_WEIGHTS], *[delta_w[n] for n in TWIN_WEIGHTS],
            *[new_m[n] for n in TWIN_WEIGHTS], *[new_v[n] for n in TWIN_WEIGHTS])
```

```python
import functools

import jax
import jax.numpy as jnp
import numpy as np
from jax import lax
from jax.experimental import pallas as pl
from jax.experimental.pallas import tpu as pltpu

F32 = jnp.float32
BF16 = jnp.bfloat16
MESH_ID = pl.DeviceIdType.MESH
AXES = ("x", "y", "c")
N_DEV = 8

D_MODEL = 1024
N_HEADS = 8
KV_HEADS = 2
HEAD_DIM = 64
Q_DIM = 512
KV_DIM = 128
BLOCK = 128
WIN = 3 * BLOCK
NUM_BUCKETS = 32
MAX_DISTANCE = 128
CONV_DIM = 512
CONV_WIDTH = 31
D_FF = 2816
FF_SHARD = 2 * D_FF // N_DEV
FF_BLOCKS = D_FF // FF_SHARD
QC_DIM = Q_DIM + 2 * KV_DIM + 2 * CONV_DIM
NEG_INF = -1e9
HALO = 16

ADAM_LR = 0.001
ADAM_B1 = 0.9
ADAM_B2 = 0.999
ADAM_EPS = 1e-08
ADAM_WD = 0.01
ADAM_STEP = 10

VMEM_LIMIT = 56 * 1024 * 1024
HI = lax.Precision.HIGHEST


def _params(sem):
    return pltpu.CompilerParams(dimension_semantics=sem, vmem_limit_bytes=VMEM_LIMIT)


def _dot(a, b, precision=None):
    return jnp.dot(a, b, preferred_element_type=F32, precision=precision)


def _dot_nt(a, b):
    return lax.dot_general(a, b, (((1,), (1,)), ((), ())), preferred_element_type=F32)


def _dot_tn(a, b):
    return lax.dot_general(a, b, (((0,), (0,)), ((), ())), preferred_element_type=F32)


def _sig(x):
    return 1.0 / (1.0 + jnp.exp(-x))


def _bf(x):
    return x.astype(BF16)


def _rms_fwd(x, gamma):
    r = lax.rsqrt(jnp.mean(x * x, axis=-1, keepdims=True) + 1e-6)
    return x * r * gamma, r


def _rms_bwd(dy, x, gamma):
    r = lax.rsqrt(jnp.mean(x * x, axis=-1, keepdims=True) + 1e-6)
    xhat = x * r
    dxhat = dy * gamma
    dx = r * (dxhat - xhat * jnp.mean(dxhat * xhat, axis=-1, keepdims=True))
    return dx, dy * xhat


def _head_onehot(width, n_lanes=128):
    c = lax.broadcasted_iota(jnp.int32, (width, n_lanes), 0)
    h = lax.broadcasted_iota(jnp.int32, (width, n_lanes), 1)
    e = ((c >> 6) == h).astype(F32)
    ct = lax.broadcasted_iota(jnp.int32, (n_lanes, width), 1)
    ht = lax.broadcasted_iota(jnp.int32, (n_lanes, width), 0)
    et = ((ct >> 6) == ht).astype(F32)
    return e, et


def _head_fold(width, n_lanes=128):
    c = lax.broadcasted_iota(jnp.int32, (width, n_lanes), 0)
    d = lax.broadcasted_iota(jnp.int32, (width, n_lanes), 1)
    return ((c & (HEAD_DIM - 1)) == d).astype(F32)


def _head_rstd(x, e, et):
    ms = _dot(x * x, e, HI) * (1.0 / HEAD_DIM)
    return _dot(lax.rsqrt(ms + 1e-6), et, HI)


def _head_mean(x, e, et):
    return _dot(_dot(x, e, HI) * (1.0 / HEAD_DIM), et, HI)


def _exchange(arrs, gather, name):
    n = len(arrs)
    out_shape = tuple(
        jax.ShapeDtypeStruct(((N_DEV,) + a.shape) if gather else a.shape, a.dtype) for a in arrs)

    def body(*refs):
        ins, outs = refs[:n], refs[n:2 * n]
        send_sems, recv_sems, local_sems = refs[2 * n:]
        x, y, c = lax.axis_index("x"), lax.axis_index("y"), lax.axis_index("c")
        me = 4 * x + 2 * y + c
        local = []
        for t in range(n):
            src = ins[t] if gather else ins[t].at[me]
            cp = pltpu.make_async_copy(src, outs[t].at[me], local_sems.at[t])
            cp.start()
            local.append(cp)
        copies = []
        for d in range(1, N_DEV):
            px = 1 - x if d & 4 else x
            py = 1 - y if d & 2 else y
            pc = 1 - c if d & 1 else c
            peer = 4 * px + 2 * py + pc
            for t in range(n):
                k = (d - 1) * n + t
                src = ins[t] if gather else ins[t].at[peer]
                cp = pltpu.make_async_remote_copy(
                    src_ref=src, dst_ref=outs[t].at[me], send_sem=send_sems.at[k], recv_sem=recv_sems.at[k],
                    device_id=(px, py, pc), device_id_type=MESH_ID)
                cp.start()
                copies.append(cp)
        for cp in copies:
            cp.wait()
        for cp in local:
            cp.wait()

    any_spec = pl.BlockSpec(memory_space=pl.ANY)
    return pl.pallas_call(
        body, name=name, out_shape=out_shape,
        in_specs=[any_spec] * n, out_specs=tuple([any_spec] * n),
        scratch_shapes=[pltpu.SemaphoreType.DMA((7 * n,)), pltpu.SemaphoreType.DMA((7 * n,)),
                        pltpu.SemaphoreType.DMA((n,))],
    )(*arrs)


def _matmul_tn(a, b, nb, name, scale=1.0, out_dtype=BF16, ts=1024):
    ba, S, K = a.shape
    bb, _, N = b.shape
    ts = min(ts, S)
    tn = N if N <= 1024 else 1024
    assert N % tn == 0 and S % ts == 0
    ns = S // ts

    def body(a_ref, b_ref, o_ref, acc):
        s = pl.program_id(2)

        @pl.when(s == 0)
        def _():
            acc[...] = jnp.zeros_like(acc)

        acc[...] += _dot_tn(_bf(a_ref[...]), _bf(b_ref[...]))

        @pl.when(s == ns - 1)
        def _():
            o_ref[...] = (acc[...] * scale).astype(out_dtype)

    return pl.pallas_call(
        body, name=name, out_shape=jax.ShapeDtypeStruct((nb, K, N), out_dtype),
        grid=(nb, N // tn, ns),
        in_specs=[pl.BlockSpec((None, ts, K), (lambda i, j, s: (i, s, 0)) if ba > 1 else (lambda i, j, s: (0, s, 0))),
                  pl.BlockSpec((None, ts, tn), (lambda i, j, s: (i, s, j)) if bb > 1 else (lambda i, j, s: (0, s, j)))],
        out_specs=pl.BlockSpec((None, K, tn), lambda i, j, s: (i, 0, j)),
        scratch_shapes=[pltpu.VMEM((K, tn), F32)],
        compiler_params=_params(("parallel", "parallel", "arbitrary")),
    )(a, b)


def _ffn_specs(tm):
    wg = pl.BlockSpec((None, D_MODEL, FF_SHARD), lambda i, j: (j, 0, 0))
    wu = pl.BlockSpec((None, D_MODEL, FF_SHARD), lambda i, j: (j + FF_BLOCKS, 0, 0))
    wo = pl.BlockSpec((2, FF_SHARD // 2, D_MODEL), lambda i, j: (j, 0, 0))
    row = pl.BlockSpec((tm, D_MODEL), lambda i, j: (i, 0))
    vec = pl.BlockSpec((1, D_MODEL), lambda i, j: (0, 0))
    hid = pl.BlockSpec((None, tm, FF_SHARD), lambda i, j: (j, i, 0))
    return wg, wu, wo, row, vec, hid


def _ffn_fwd(x, gamma, wi, wo, tm, name):
    S = x.shape[0]
    wg_s, wu_s, wo_s, row, vec, hid = _ffn_specs(tm)

    def body(x_ref, g_ref, wg_ref, wu_ref, wo_ref, y_ref, xn_ref, gs_ref, us_ref, xn_s, acc):
        j = pl.program_id(1)

        @pl.when(j == 0)
        def _():
            xn = _bf(_rms_fwd(x_ref[...], g_ref[...])[0])
            xn_s[...] = xn
            xn_ref[...] = xn
            acc[...] = jnp.zeros_like(acc)

        xn = xn_s[...]
        g = _dot(xn, wg_ref[...])
        u = _dot(xn, wu_ref[...])
        gs_ref[...] = _bf(g)
        us_ref[...] = _bf(u)
        a = g * _sig(g) * u
        acc[...] += _dot(_bf(a), wo_ref[...].reshape(FF_SHARD, D_MODEL))

        @pl.when(j == FF_BLOCKS - 1)
        def _():
            y_ref[...] = x_ref[...] + 0.5 * acc[...]

    return pl.pallas_call(
        body, name=name,
        out_shape=(jax.ShapeDtypeStruct((S, D_MODEL), F32), jax.ShapeDtypeStruct((S, D_MODEL), BF16),
                   jax.ShapeDtypeStruct((FF_BLOCKS, S, FF_SHARD), BF16),
                   jax.ShapeDtypeStruct((FF_BLOCKS, S, FF_SHARD), BF16)),
        grid=(S // tm, FF_BLOCKS),
        in_specs=[row, vec, wg_s, wu_s, wo_s],
        out_specs=(row, row, hid, hid),
        scratch_shapes=[pltpu.VMEM((tm, D_MODEL), BF16), pltpu.VMEM((tm, D_MODEL), F32)],
        compiler_params=_params(("parallel", "arbitrary")),
    )(x, gamma, wi, wi, wo)


def _ffn_bwd(dy, x, gamma, gs, us, wi, wo, tm, name):
    S = x.shape[0]
    wg_s, wu_s, wo_s, row, vec, hid = _ffn_specs(tm)
    dgu_s = pl.BlockSpec((2, None, tm, FF_SHARD), lambda i, j: (0, j, i, 0))

    def body(dy_ref, x_ref, g_ref, gs_ref, us_ref, wg_ref, wu_ref, wo_ref,
             dx_ref, a_ref, dgu_ref, dgam_ref, dyh_s, acc):
        i, j = pl.program_id(0), pl.program_id(1)

        @pl.when(j == 0)
        def _():
            dyh_s[...] = _bf(0.5 * dy_ref[...])
            acc[...] = jnp.zeros_like(acc)

        @pl.when((i == 0) & (j == 0))
        def _():
            dgam_ref[...] = jnp.zeros_like(dgam_ref)

        da = _dot_nt(dyh_s[...], wo_ref[...].reshape(FF_SHARD, D_MODEL))
        g = gs_ref[...].astype(F32)
        u = us_ref[...].astype(F32)
        sg = _sig(g)
        sl = g * sg
        a_ref[...] = _bf(sl * u)
        dg = _bf(da * u * (sg * (1.0 + g * (1.0 - sg))))
        du = _bf(da * sl)
        dgu_ref[0] = dg
        dgu_ref[1] = du
        acc[...] += _dot_nt(dg, wg_ref[...]) + _dot_nt(du, wu_ref[...])

        @pl.when(j == FF_BLOCKS - 1)
        def _():
            dx, dgam = _rms_bwd(acc[...], x_ref[...], g_ref[...])
            dx_ref[...] = dy_ref[...] + dx
            dgam_ref[...] += jnp.sum(dgam, axis=0, keepdims=True)

    return pl.pallas_call(
        body, name=name,
        out_shape=(jax.ShapeDtypeStruct((S, D_MODEL), F32),
                   jax.ShapeDtypeStruct((FF_BLOCKS, S, FF_SHARD), BF16),
                   jax.ShapeDtypeStruct((2, FF_BLOCKS, S, FF_SHARD), BF16),
                   jax.ShapeDtypeStruct((1, D_MODEL), F32)),
        grid=(S // tm, FF_BLOCKS),
        in_specs=[row, row, vec, hid, hid, wg_s, wu_s, wo_s],
        out_specs=(row, hid, dgu_s, vec),
        scratch_shapes=[pltpu.VMEM((tm, D_MODEL), BF16), pltpu.VMEM((tm, D_MODEL), F32)],
        compiler_params=_params(("arbitrary", "arbitrary")),
    )(dy, x, gamma, gs, us, wi, wi, wo)


def _mixin_fwd(h, gamma, w_qc, tm, name):
    S = h.shape[0]
    nqkv = Q_DIM + 2 * KV_DIM

    def body(h_ref, g_ref, w_ref, un_ref, qkv_ref, cvg_ref):
        un = _bf(_rms_fwd(h_ref[...], g_ref[...])[0])
        un_ref[...] = un
        z = _dot(un, w_ref[...])
        qkv_ref[...] = z[:, :nqkv]
        cvg_ref[...] = z[:, nqkv:]

    row = lambda w: pl.BlockSpec((tm, w), lambda i: (i, 0))
    return pl.pallas_call(
        body, name=name,
        out_shape=(jax.ShapeDtypeStruct((S, D_MODEL), BF16), jax.ShapeDtypeStruct((S, nqkv), F32),
                   jax.ShapeDtypeStruct((S, 2 * CONV_DIM), F32)),
        grid=(S // tm,),
        in_specs=[row(D_MODEL), pl.BlockSpec((1, D_MODEL), lambda i: (0, 0)),
                  pl.BlockSpec((D_MODEL, QC_DIM), lambda i: (0, 0))],
        out_specs=(row(D_MODEL), row(nqkv), row(2 * CONV_DIM)),
        compiler_params=_params(("parallel",)),
    )(h, gamma, w_qc)


def _mixin_bwd(dh2, h1, gamma, dq, dkv, dcvg, dgpre, w_qc, w_g, tm, name):
    S = h1.shape[0]
    nqkv = Q_DIM + 2 * KV_DIM

    def body(dh2_ref, h1_ref, g_ref, dq_ref, dkv_ref, dcvg_ref, dgp_ref, wqc_ref, wg_ref, dh1_ref, dgam_ref):
        @pl.when(pl.program_id(0) == 0)
        def _():
            dgam_ref[...] = jnp.zeros_like(dgam_ref)

        wqc = wqc_ref[...]
        dun = _dot_nt(_bf(dq_ref[...]), wqc[:, :Q_DIM])
        dun += _dot_nt(_bf(dkv_ref[...]), wqc[:, Q_DIM:nqkv])
        dun += _dot_nt(_bf(dcvg_ref[...]), wqc[:, nqkv:])
        dun += _dot_nt(dgp_ref[...], wg_ref[...])
        dx, dgam = _rms_bwd(dun, h1_ref[...], g_ref[...])
        dh1_ref[...] = dh2_ref[...] + dx
        dgam_ref[...] += jnp.sum(dgam, axis=0, keepdims=True)

    row = lambda w: pl.BlockSpec((tm, w), lambda i: (i, 0))
    vec = pl.BlockSpec((1, D_MODEL), lambda i: (0, 0))
    return pl.pallas_call(
        body, name=name,
        out_shape=(jax.ShapeDtypeStruct((S, D_MODEL), F32), jax.ShapeDtypeStruct((1, D_MODEL), F32)),
        grid=(S // tm,),
        in_specs=[row(D_MODEL), row(D_MODEL), vec, row(Q_DIM), row(2 * KV_DIM), row(2 * CONV_DIM),
                  row(2 * D_MODEL), pl.BlockSpec((D_MODEL, QC_DIM), lambda i: (0, 0)),
                  pl.BlockSpec((D_MODEL, 2 * D_MODEL), lambda i: (0, 0))],
        out_specs=(row(D_MODEL), vec),
        compiler_params=_params(("arbitrary",)),
    )(dh2, h1, gamma, dq, dkv, dcvg, dgpre, w_qc, w_g)


TQ = 512
QB = TQ // BLOCK


def _attn_in_specs(S):
    nkb = S // BLOCK
    return [
        pl.BlockSpec((TQ, Q_DIM), lambda i: (i, 0)),
        pl.BlockSpec((BLOCK, 2 * KV_DIM), lambda i: (jnp.maximum(i * QB - 1, 0), Q_DIM // (2 * KV_DIM))),
        pl.BlockSpec((TQ, 2 * KV_DIM), lambda i: (i, Q_DIM // (2 * KV_DIM))),
        pl.BlockSpec((BLOCK, 2 * KV_DIM), lambda i: (jnp.minimum(i * QB + QB, nkb - 1), Q_DIM // (2 * KV_DIM))),
        pl.BlockSpec((1, Q_DIM), lambda i: (0, 0)),
        pl.BlockSpec((1, KV_DIM), lambda i: (0, 0)),
        pl.BlockSpec((N_HEADS, 128), lambda i: (0, 0)),
        pl.BlockSpec((N_HEADS, BLOCK, WIN), lambda i: (0, 0, 0)),
    ]


def _attn_prologue(q_ref, kvp_ref, kvc_ref, kvn_ref, qg_ref, kg_ref):
    low = lax.broadcasted_iota(jnp.int32, (1, 128), 1) < HEAD_DIM
    eq, etq = _head_onehot(Q_DIM)
    q = q_ref[...]
    rq = _head_rstd(q, eq, etq)
    qs = q * rq * qg_ref[...] * (HEAD_DIM ** -0.5)
    kv = jnp.concatenate([kvp_ref[...], kvc_ref[...], kvn_ref[...]], axis=0)
    k, v = kv[:, :KV_DIM], kv[:, KV_DIM:]
    ek, etk = _head_onehot(KV_DIM)
    kn = k * _head_rstd(k, ek, etk) * kg_ref[...]
    kr, vr = pltpu.roll(kn, HEAD_DIM, 1), pltpu.roll(v, HEAD_DIM, 1)
    kdup = [_bf(jnp.where(low, kn, kr)), _bf(jnp.where(low, kr, kn))]
    vdup = [_bf(jnp.where(low, v, vr)), _bf(jnp.where(low, vr, v))]
    return q, rq, qs, kdup, vdup, low


def _attn_valid(i, t, S):
    r = lax.broadcasted_iota(jnp.int32, (QB * BLOCK, WIN), 0) & (BLOCK - 1)
    cidx = lax.broadcasted_iota(jnp.int32, (QB * BLOCK, WIN), 1)
    rel = cidx - BLOCK - r
    kpos = i * TQ + (t - 1) * BLOCK + cidx
    return (jnp.abs(rel) <= BLOCK) & (kpos >= 0) & (kpos < S)


def _stack_heads(x, t, kh, low):
    rows = slice(t * BLOCK, (t + 1) * BLOCK)
    xa = x[rows, 256 * kh:256 * kh + 128]
    xb = x[rows, 256 * kh + 128:256 * kh + 256]
    z = jnp.zeros_like(xa)
    return jnp.concatenate([jnp.where(low, xa, z), jnp.where(low, z, xa),
                            jnp.where(low, xb, z), jnp.where(low, z, xb)], axis=0)


def _unstack_heads(ov, low):
    return (jnp.where(low, ov[0:128], ov[128:256]), jnp.where(low, ov[256:384], ov[384:512]))


def _sink_col(sink_ref, kh):
    return jnp.concatenate(
        [jnp.broadcast_to(sink_ref[4 * kh + r:4 * kh + r + 1, 0:1], (BLOCK, 1)) for r in range(4)], axis=0)


def _attn_probs(lhs, kw, bias, valid, sk):
    s = _dot_nt(lhs, kw) + bias
    s = jnp.where(valid, s, NEG_INF)
    m = jnp.maximum(jnp.max(s, axis=-1, keepdims=True), sk)
    e = jnp.exp(s - m)
    es = jnp.exp(sk - m)
    inv = 1.0 / (jnp.sum(e, axis=-1, keepdims=True) + es)
    return e * inv, es * inv


def _attn_fwd(qkv, qg, kg, sinkb, bias, name):
    S = qkv.shape[0]

    def body(q_ref, kvp_ref, kvc_ref, kvn_ref, qg_ref, kg_ref, sink_ref, bias_ref, o_ref):
        i = pl.program_id(0)
        _, _, qs, kdup, vdup, low = _attn_prologue(q_ref, kvp_ref, kvc_ref, kvn_ref, qg_ref, kg_ref)
        for t in range(QB):
            valid = _attn_valid(i, t, S)
            rows = slice(t * BLOCK, (t + 1) * BLOCK)
            for kh in range(KV_HEADS):
                lhs = _bf(_stack_heads(qs, t, kh, low))
                kw = kdup[kh][t * BLOCK:t * BLOCK + WIN]
                vw = vdup[kh][t * BLOCK:t * BLOCK + WIN]
                bias_g = bias_ref[4 * kh:4 * kh + 4].reshape(4 * BLOCK, WIN)
                pr, _ = _attn_probs(lhs, kw, bias_g, valid, _sink_col(sink_ref, kh))
                oa, ob = _unstack_heads(_dot(_bf(pr), vw), low)
                o_ref[rows, 256 * kh:256 * kh + 128] = _bf(oa)
                o_ref[rows, 256 * kh + 128:256 * kh + 256] = _bf(ob)

    return pl.pallas_call(
        body, name=name, out_shape=jax.ShapeDtypeStruct((S, Q_DIM), BF16),
        grid=(S // TQ,), in_specs=_attn_in_specs(S),
        out_specs=pl.BlockSpec((TQ, Q_DIM), lambda i: (i, 0)),
        compiler_params=_params(("parallel",)),
    )(qkv, qkv, qkv, qkv, qg, kg, sinkb, bias)


def _attn_bwd(do, qkv, qg, kg, sinkb, bias, dbias_in, name):
    S = qkv.shape[0]
    nkb = S // BLOCK
    nsteps = S // TQ

    def body(do_ref, q_ref, kvp_ref, kvc_ref, kvn_ref, qg_ref, kg_ref, sink_ref, bias_ref, dbin_ref,
             dq_ref, dkp_ref, dvp_ref, dbias_ref, dsink_ref, dqg_ref, dqs_s, dqg_s):
        i = pl.program_id(0)

        @pl.when(i == 0)
        def _():
            dbias_ref[...] = dbin_ref[...]
            dsink_ref[...] = jnp.zeros_like(dsink_ref)
            dqg_s[...] = jnp.zeros_like(dqg_s)

        q, rq, qs, kdup, vdup, low = _attn_prologue(q_ref, kvp_ref, kvc_ref, kvn_ref, qg_ref, kg_ref)
        do = do_ref[...]
        for t in range(QB):
            valid = _attn_valid(i, t, S)
            rows = slice(t * BLOCK, (t + 1) * BLOCK)
            dk_dup, dv_dup = [], []
            for kh in range(KV_HEADS):
                lhs = _bf(_stack_heads(qs, t, kh, low))
                dos = _bf(_stack_heads(do, t, kh, low))
                kw = kdup[kh][t * BLOCK:t * BLOCK + WIN]
                vw = vdup[kh][t * BLOCK:t * BLOCK + WIN]
                bias_g = bias_ref[4 * kh:4 * kh + 4].reshape(4 * BLOCK, WIN)
                pr, ps = _attn_probs(lhs, kw, bias_g, valid, _sink_col(sink_ref, kh))
                dpr = _dot_nt(dos, vw)
                delta = jnp.sum(pr * dpr, axis=-1, keepdims=True)
                ds = pr * (dpr - delta)
                dbias_ref[4 * kh:4 * kh + 4] += ds.reshape(4, BLOCK, WIN)
                dsk = ps * delta
                for r in range(4):
                    dsink_ref[4 * kh + r:4 * kh + r + 1, :] -= jnp.broadcast_to(
                        jnp.sum(dsk[r * BLOCK:(r + 1) * BLOCK], axis=0, keepdims=True), (1, 128))
                dsb = _bf(ds)
                dqa, dqb = _unstack_heads(_dot(dsb, kw), low)
                dqs_s[rows, 256 * kh:256 * kh + 128] = dqa
                dqs_s[rows, 256 * kh + 128:256 * kh + 256] = dqb
                dkx = _dot_tn(dsb, lhs)
                dvx = _dot_tn(_bf(pr), dos)
                dk_dup.append(dkx + pltpu.roll(dkx, HEAD_DIM, 1))
                dv_dup.append(dvx + pltpu.roll(dvx, HEAD_DIM, 1))
            dkp_ref[t] = jnp.where(low, dk_dup[0], dk_dup[1])
            dvp_ref[t] = jnp.where(low, dv_dup[0], dv_dup[1])

        eq, etq = _head_onehot(Q_DIM)
        dyq = dqs_s[...] * (HEAD_DIM ** -0.5)
        qhat = q * rq
        dxhat = dyq * qg_ref[...]
        dq_ref[...] = rq * (dxhat - qhat * _head_mean(dxhat * qhat, eq, etq))
        dqg_s[...] += jnp.sum((dyq * qhat).reshape(TQ // 8, 8, Q_DIM), axis=0)

        @pl.when(i == nsteps - 1)
        def _():
            folded = _dot(dqg_s[...], _head_fold(Q_DIM), HI)
            dqg_ref[...] = jnp.broadcast_to(jnp.sum(folded, axis=0, keepdims=True), (8, 128))

    const2 = lambda shape: pl.BlockSpec(shape, lambda i: (0,) * len(shape))
    part = pl.BlockSpec((QB, WIN, KV_DIM), lambda i: (i, 0, 0))
    return pl.pallas_call(
        body, name=name,
        out_shape=(jax.ShapeDtypeStruct((S, Q_DIM), F32), jax.ShapeDtypeStruct((nkb, WIN, KV_DIM), F32),
                   jax.ShapeDtypeStruct((nkb, WIN, KV_DIM), F32),
                   jax.ShapeDtypeStruct((N_HEADS, BLOCK, WIN), F32), jax.ShapeDtypeStruct((N_HEADS, 128), F32),
                   jax.ShapeDtypeStruct((8, 128), F32)),
        grid=(nsteps,),
        in_specs=[pl.BlockSpec((TQ, Q_DIM), lambda i: (i, 0))] + _attn_in_specs(S)
        + [const2((N_HEADS, BLOCK, WIN))],
        out_specs=(pl.BlockSpec((TQ, Q_DIM), lambda i: (i, 0)), part, part,
                   const2((N_HEADS, BLOCK, WIN)), const2((N_HEADS, 128)), const2((8, 128))),
        scratch_shapes=[pltpu.VMEM((TQ, Q_DIM), F32), pltpu.VMEM((8, Q_DIM), F32)],
        compiler_params=_params(("arbitrary",)),
    )(do, qkv, qkv, qkv, qkv, qg, kg, sinkb, bias, dbias_in)


def _kv_fold(dkp, dvp, qkv, kg, name):
    nkb = dkp.shape[0]
    S = nkb * BLOCK
    nsteps = S // TQ

    def body(kp_p, kp_c, kp_n, vp_p, vp_c, vp_n, kv_ref, kg_ref, dkv_ref, dkg_ref, dkg_s):
        i = pl.program_id(0)

        @pl.when(i == 0)
        def _():
            dkg_s[...] = jnp.zeros_like(dkg_s)

        def fold(p_ref, c_ref, n_ref):
            blocks = []
            for t in range(QB):
                acc = c_ref[t, BLOCK:2 * BLOCK, :]
                if t > 0:
                    acc = acc + c_ref[t - 1, 2 * BLOCK:, :]
                else:
                    acc = acc + jnp.where(i > 0, p_ref[0, 2 * BLOCK:, :], 0.0)
                if t < QB - 1:
                    acc = acc + c_ref[t + 1, :BLOCK, :]
                else:
                    acc = acc + jnp.where(i < nsteps - 1, n_ref[0, :BLOCK, :], 0.0)
                blocks.append(acc)
            return jnp.concatenate(blocks, axis=0)

        dkn = fold(kp_p, kp_c, kp_n)
        dv = fold(vp_p, vp_c, vp_n)
        k = kv_ref[:, :KV_DIM]
        ek, etk = _head_onehot(KV_DIM)
        rk = _head_rstd(k, ek, etk)
        khat = k * rk
        dxhat = dkn * kg_ref[...]
        dkv_ref[:, :KV_DIM] = rk * (dxhat - khat * _head_mean(dxhat * khat, ek, etk))
        dkv_ref[:, KV_DIM:] = dv
        dkg_s[...] += jnp.sum((dkn * khat).reshape(TQ // 8, 8, KV_DIM), axis=0)

        @pl.when(i == nsteps - 1)
        def _():
            folded = _dot(dkg_s[...], _head_fold(KV_DIM), HI)
            dkg_ref[...] = jnp.broadcast_to(jnp.sum(folded, axis=0, keepdims=True), (8, 128))

    prev = pl.BlockSpec((1, WIN, KV_DIM), lambda i: (jnp.maximum(i * QB - 1, 0), 0, 0))
    cur = pl.BlockSpec((QB, WIN, KV_DIM), lambda i: (i, 0, 0))
    nxt = pl.BlockSpec((1, WIN, KV_DIM), lambda i: (jnp.minimum(i * QB + QB, nkb - 1), 0, 0))
    return pl.pallas_call(
        body, name=name,
        out_shape=(jax.ShapeDtypeStruct((S, 2 * KV_DIM), F32), jax.ShapeDtypeStruct((8, 128), F32)),
        grid=(nsteps,),
        in_specs=[prev, cur, nxt, prev, cur, nxt,
                  pl.BlockSpec((TQ, 2 * KV_DIM), lambda i: (i, Q_DIM // (2 * KV_DIM))),
                  pl.BlockSpec((1, KV_DIM), lambda i: (0, 0))],
        out_specs=(pl.BlockSpec((TQ, 2 * KV_DIM), lambda i: (i, 0)), pl.BlockSpec((8, 128), lambda i: (0, 0))),
        scratch_shapes=[pltpu.VMEM((8, KV_DIM), F32)],
        compiler_params=_params(("arbitrary",)),
    )(dkp, dkp, dkp, dvp, dvp, dvp, qkv, kg)


BIAS_COLS = BLOCK * WIN
BIAS_CHUNK = 6144


def _bias_table(rel_bias_t, onehot):
    def body(rb_ref, oh_ref, o_ref):
        o_ref[...] = _dot(rb_ref[...], oh_ref[...], HI)

    return pl.pallas_call(
        body, name="bias_table", out_shape=jax.ShapeDtypeStruct((N_HEADS, BIAS_COLS), F32),
        grid=(BIAS_COLS // BIAS_CHUNK,),
        in_specs=[pl.BlockSpec((N_HEADS, NUM_BUCKETS), lambda i: (0, 0)),
                  pl.BlockSpec((NUM_BUCKETS, BIAS_CHUNK), lambda i: (0, i))],
        out_specs=pl.BlockSpec((N_HEADS, BIAS_CHUNK), lambda i: (0, i)),
        compiler_params=_params(("parallel",)),
    )(rel_bias_t, onehot)


def _bias_grad(dbias, onehot):
    def body(db_ref, oh_ref, o_ref):
        @pl.when(pl.program_id(0) == 0)
        def _():
            o_ref[...] = jnp.zeros_like(o_ref)

        o_ref[...] += lax.dot_general(db_ref[...], oh_ref[...], (((1,), (1,)), ((), ())),
                                      preferred_element_type=F32, precision=HI)

    return pl.pallas_call(
        body, name="bias_grad", out_shape=jax.ShapeDtypeStruct((N_HEADS, NUM_BUCKETS), F32),
        grid=(BIAS_COLS // BIAS_CHUNK,),
        in_specs=[pl.BlockSpec((N_HEADS, BIAS_CHUNK), lambda i: (0, i)),
                  pl.BlockSpec((NUM_BUCKETS, BIAS_CHUNK), lambda i: (0, i))],
        out_specs=pl.BlockSpec((N_HEADS, NUM_BUCKETS), lambda i: (0, 0)),
        compiler_params=_params(("arbitrary",)),
    )(dbias, onehot)


def _bucket_onehot():
    half = NUM_BUCKETS // 2
    max_exact = half // 2
    rel = jnp.arange(WIN)[None, :] - BLOCK - jnp.arange(BLOCK)[:, None]
    n = jnp.abs(rel)
    ret = jnp.where(rel > 0, half, 0)
    nf = jnp.maximum(n, 1).astype(F32)
    large = max_exact + (jnp.log(nf / max_exact) / np.log(MAX_DISTANCE / max_exact)
                         * (half - max_exact)).astype(jnp.int32)
    large = jnp.minimum(large, half - 1)
    bucket = (ret + jnp.where(n < max_exact, n, large)).reshape(1, BIAS_COLS)
    return (bucket == jnp.arange(NUM_BUCKETS)[:, None]).astype(F32)


def _halo_specs(tm, width, S):
    r = tm // HALO
    last = S // HALO - 1
    return [pl.BlockSpec((HALO, width), lambda i: (jnp.maximum(i * r - 1, 0), 0)),
            pl.BlockSpec((tm, width), lambda i: (i, 0)),
            pl.BlockSpec((HALO, width), lambda i: (jnp.minimum(i * r + r, last), 0))]


def _with_halo(p_ref, c_ref, n_ref):
    return jnp.concatenate([p_ref[...], c_ref[...], n_ref[...]], axis=0)


def _row_valid(i, tm, S):
    g = i * tm - HALO + lax.broadcasted_iota(jnp.int32, (tm + 2 * HALO, 1), 0)
    return (g >= 0) & (g < S)


def _shifted(x, rows):
    n = x.shape[0]
    return [x if b == 0 else pltpu.roll(x, n - b, 0) for b in range(8)]


def _tap(sh, off, tm):
    a, b = off // 8, off % 8
    return sh[b][8 * a:8 * a + tm]


def _conv_fwd(cvg, cw, cb, lg, lb, tm, name):
    S = cvg.shape[0]

    def body(p_ref, c_ref, n_ref, cw_ref, cb_ref, lg_ref, lb_ref, act_ref, yc_ref):
        i = pl.program_id(0)
        z = _with_halo(p_ref, c_ref, n_ref)
        glu = jnp.where(_row_valid(i, tm, S), z[:, :CONV_DIM] * _sig(z[:, CONV_DIM:]), 0.0)
        sh = _shifted(glu, tm)
        y = jnp.zeros((tm, CONV_DIM), F32) + cb_ref[...]
        for w in range(CONV_WIDTH):
            y = y + _tap(sh, w + 1, tm) * cw_ref[w:w + 1, :]
        yc_ref[...] = y
        mu = jnp.mean(y, axis=-1, keepdims=True)
        yc = y - mu
        rstd = lax.rsqrt(jnp.mean(yc * yc, axis=-1, keepdims=True) + 1e-5)
        ln = yc * rstd * lg_ref[...] + lb_ref[...]
        act_ref[...] = _bf(ln * _sig(ln))

    vec = pl.BlockSpec((1, CONV_DIM), lambda i: (0, 0))
    row = pl.BlockSpec((tm, CONV_DIM), lambda i: (i, 0))
    return pl.pallas_call(
        body, name=name,
        out_shape=(jax.ShapeDtypeStruct((S, CONV_DIM), BF16), jax.ShapeDtypeStruct((S, CONV_DIM), F32)),
        grid=(S // tm,),
        in_specs=_halo_specs(tm, 2 * CONV_DIM, S) + [pl.BlockSpec((32, CONV_DIM), lambda i: (0, 0)), vec, vec, vec],
        out_specs=(row, row),
        compiler_params=_params(("parallel",)),
    )(cvg, cvg, cvg, cw, cb, lg, lb)


def _conv_bwd(dact, yconv, cvg, cw, lg, lb, tm, name):
    S = cvg.shape[0]
    nsteps = S // tm

    def body(dp, dc, dn, yp, yc_, yn, zp, zc, zn, cw_ref, lg_ref, lb_ref,
             dz_ref, dcw_ref, dvec_ref, dcw_s, dvec_s):
        i = pl.program_id(0)

        @pl.when(i == 0)
        def _():
            dcw_s[...] = jnp.zeros_like(dcw_s)
            dvec_s[...] = jnp.zeros_like(dvec_s)

        valid = _row_valid(i, tm, S)
        own = (lax.broadcasted_iota(jnp.int32, (tm + 2 * HALO, 1), 0) >= HALO) & (
            lax.broadcasted_iota(jnp.int32, (tm + 2 * HALO, 1), 0) < HALO + tm)
        y = _with_halo(yp, yc_, yn)
        dact_ = _with_halo(dp, dc, dn)
        mu = jnp.mean(y, axis=-1, keepdims=True)
        ycen = y - mu
        rstd = lax.rsqrt(jnp.mean(ycen * ycen, axis=-1, keepdims=True) + 1e-5)
        yhat = ycen * rstd
        ln = yhat * lg_ref[...] + lb_ref[...]
        sg = _sig(ln)
        dln = dact_ * (sg * (1.0 + ln * (1.0 - sg)))
        dyhat = dln * lg_ref[...]
        dy = rstd * (dyhat - jnp.mean(dyhat, axis=-1, keepdims=True)
                     - yhat * jnp.mean(dyhat * yhat, axis=-1, keepdims=True))
        dy = jnp.where(valid, dy, 0.0)
        dln_own = jnp.where(own, dln, 0.0)
        nr = (tm + 2 * HALO) // 8
        dvec_s[0] += jnp.sum(jnp.where(own, dy, 0.0).reshape(nr, 8, CONV_DIM), axis=0)
        dvec_s[1] += jnp.sum((dln_own * yhat).reshape(nr, 8, CONV_DIM), axis=0)
        dvec_s[2] += jnp.sum(dln_own.reshape(nr, 8, CONV_DIM), axis=0)
        z = _with_halo(zp, zc, zn)
        cval, sgate = z[:, :CONV_DIM], _sig(z[:, CONV_DIM:])
        glu = jnp.where(valid, cval * sgate, 0.0)
        sh_g = _shifted(glu, tm)
        sh_d = _shifted(dy, tm)
        dy_own = dy[HALO:HALO + tm]
        dglu = jnp.zeros((tm, CONV_DIM), F32)
        for w in range(CONV_WIDTH):
            dglu = dglu + _tap(sh_d, CONV_WIDTH - w, tm) * cw_ref[w:w + 1, :]
            dcw_s[w] += jnp.sum((dy_own * _tap(sh_g, w + 1, tm)).reshape(tm // 8, 8, CONV_DIM), axis=0)
        cv, sg_o = cval[HALO:HALO + tm], sgate[HALO:HALO + tm]
        dz_ref[:, :CONV_DIM] = dglu * sg_o
        dz_ref[:, CONV_DIM:] = dglu * cv * sg_o * (1.0 - sg_o)

        @pl.when(i == nsteps - 1)
        def _():
            dcw_ref[...] = jnp.sum(dcw_s[...], axis=1)
            dvec_ref[...] = jnp.sum(dvec_s[...], axis=1)

    vec = pl.BlockSpec((1, CONV_DIM), lambda i: (0, 0))
    return pl.pallas_call(
        body, name=name,
        out_shape=(jax.ShapeDtypeStruct((S, 2 * CONV_DIM), F32), jax.ShapeDtypeStruct((32, CONV_DIM), F32),
                   jax.ShapeDtypeStruct((8, CONV_DIM), F32)),
        grid=(nsteps,),
        in_specs=_halo_specs(tm, CONV_DIM, S) + _halo_specs(tm, CONV_DIM, S) + _halo_specs(tm, 2 * CONV_DIM, S)
        + [pl.BlockSpec((32, CONV_DIM), lambda i: (0, 0)), vec, vec],
        out_specs=(pl.BlockSpec((tm, 2 * CONV_DIM), lambda i: (i, 0)),
                   pl.BlockSpec((32, CONV_DIM), lambda i: (0, 0)), pl.BlockSpec((8, CONV_DIM), lambda i: (0, 0))),
        scratch_shapes=[pltpu.VMEM((32, 8, CONV_DIM), F32), pltpu.VMEM((8, 8, CONV_DIM), F32)],
        compiler_params=_params(("arbitrary",)),
    )(dact, dact, dact, yconv, yconv, yconv, cvg, cvg, cvg, cw, lg, lb)


def _merge_parts(un, o, cact, wg_ref, wao_ref, wco_ref):
    g = _dot(un, wg_ref[...])
    ga, gc = _sig(g[:, :D_MODEL]), _sig(g[:, D_MODEL:])
    ya = _dot(o, wao_ref[...])
    yc = _dot(cact, wco_ref[...])
    return ga, gc, ya, yc


def _merge_specs(tm):
    row = lambda w: pl.BlockSpec((tm, w), lambda i: (i, 0))
    full = lambda a, b: pl.BlockSpec((a, b), lambda i: (0, 0))
    weights = [full(D_MODEL, 2 * D_MODEL), full(Q_DIM, D_MODEL), full(CONV_DIM, D_MODEL), full(D_MODEL, D_MODEL)]
    return row, weights


def _merge_fwd(h1, un, o, cact, w_g, w_ao, w_co, w_o, tm, name):
    S = h1.shape[0]
    row, weights = _merge_specs(tm)

    def body(h1_ref, un_ref, o_ref, c_ref, wg_ref, wao_ref, wco_ref, wo_ref, h2_ref):
        ga, gc, ya, yc = _merge_parts(un_ref[...], o_ref[...], c_ref[...], wg_ref, wao_ref, wco_ref)
        h2_ref[...] = h1_ref[...] + _dot(_bf(ga * ya + gc * yc), wo_ref[...])

    return pl.pallas_call(
        body, name=name, out_shape=jax.ShapeDtypeStruct((S, D_MODEL), F32),
        grid=(S // tm,),
        in_specs=[row(D_MODEL), row(D_MODEL), row(Q_DIM), row(CONV_DIM)] + weights,
        out_specs=row(D_MODEL),
        compiler_params=_params(("parallel",)),
    )(h1, un, o, cact, w_g, w_ao, w_co, w_o)


def _merge_bwd(dh2, un, o, cact, w_g, w_ao, w_co, w_o, tm, name):
    S = dh2.shape[0]
    row, weights = _merge_specs(tm)

    def body(dh2_ref, un_ref, o_ref, c_ref, wg_ref, wao_ref, wco_ref, wo_ref,
             do_ref, dc_ref, mix_ref, dya_ref, dyc_ref, dgp_ref):
        ga, gc, ya, yc = _merge_parts(un_ref[...], o_ref[...], c_ref[...], wg_ref, wao_ref, wco_ref)
        mix_ref[...] = _bf(ga * ya + gc * yc)
        dmix = _dot_nt(_bf(dh2_ref[...]), wo_ref[...])
        dya = _bf(dmix * ga)
        dyc = _bf(dmix * gc)
        dya_ref[...] = dya
        dyc_ref[...] = dyc
        dgp_ref[:, :D_MODEL] = _bf(dmix * ya * ga * (1.0 - ga))
        dgp_ref[:, D_MODEL:] = _bf(dmix * yc * gc * (1.0 - gc))
        do_ref[...] = _dot_nt(dya, wao_ref[...])
        dc_ref[...] = _dot_nt(dyc, wco_ref[...])

    return pl.pallas_call(
        body, name=name,
        out_shape=(jax.ShapeDtypeStruct((S, Q_DIM), F32), jax.ShapeDtypeStruct((S, CONV_DIM), F32),
                   jax.ShapeDtypeStruct((S, D_MODEL), BF16), jax.ShapeDtypeStruct((S, D_MODEL), BF16),
                   jax.ShapeDtypeStruct((S, D_MODEL), BF16), jax.ShapeDtypeStruct((S, 2 * D_MODEL), BF16)),
        grid=(S // tm,),
        in_specs=[row(D_MODEL), row(D_MODEL), row(Q_DIM), row(CONV_DIM)] + weights,
        out_specs=(row(Q_DIM), row(CONV_DIM), row(D_MODEL), row(D_MODEL), row(D_MODEL), row(2 * D_MODEL)),
        compiler_params=_params(("parallel",)),
    )(dh2, un, o, cact, w_g, w_ao, w_co, w_o)


def _pe_specs(tm, layer):
    row = pl.BlockSpec((tm, D_MODEL), lambda i: (i, 0))
    vec = pl.BlockSpec((1, D_MODEL), lambda i: (0, 0))
    p_s = pl.BlockSpec((None, None, tm, 256), lambda i: (layer, 0, i, 0))
    wpp = pl.BlockSpec((256, D_MODEL), lambda i: (0, 0))
    wpg = pl.BlockSpec((D_MODEL, D_MODEL), lambda i: (0, 0))
    return row, vec, p_s, wpp, wpg


def _pe_fwd(h, gamma, p, layer, w_pp, w_pg, tm, name):
    S = h.shape[0]
    row, vec, p_s, wpp, wpg = _pe_specs(tm, layer)

    def body(h_ref, g_ref, p_ref, wpp_ref, wpg_ref, x_ref, hn_ref):
        hn = _bf(_rms_fwd(h_ref[...], g_ref[...])[0])
        hn_ref[...] = hn
        gate = _sig(_dot(hn, wpg_ref[...]))
        x_ref[...] = h_ref[...] + _dot(_bf(p_ref[...]), wpp_ref[...]) * gate

    return pl.pallas_call(
        body, name=name,
        out_shape=(jax.ShapeDtypeStruct((S, D_MODEL), F32), jax.ShapeDtypeStruct((S, D_MODEL), BF16)),
        grid=(S // tm,), in_specs=[row, vec, p_s, wpp, wpg], out_specs=(row, row),
        compiler_params=_params(("parallel",)),
    )(h, gamma, p, w_pp, w_pg)


def _pe_bwd(dx, h, gamma, hn, p, layer, w_pp, w_pg, tm, name):
    S = h.shape[0]
    row, vec, p_s, wpp, wpg = _pe_specs(tm, layer)

    def body(dx_ref, h_ref, g_ref, hn_ref, p_ref, wpp_ref, wpg_ref, dh_ref, dgp_ref, dpr_ref, dgam_ref):
        @pl.when(pl.program_id(0) == 0)
        def _():
            dgam_ref[...] = jnp.zeros_like(dgam_ref)

        dxv = dx_ref[...]
        gate = _sig(_dot(hn_ref[...], wpg_ref[...]))
        proj = _dot(_bf(p_ref[...]), wpp_ref[...])
        dpr_ref[...] = _bf(dxv * gate)
        dgp = _bf(dxv * proj * gate * (1.0 - gate))
        dgp_ref[...] = dgp
        dxn, dgam = _rms_bwd(_dot_nt(dgp, wpg_ref[...]), h_ref[...], g_ref[...])
        dh_ref[...] = dxv + dxn
        dgam_ref[...] += jnp.sum(dgam, axis=0, keepdims=True)

    return pl.pallas_call(
        body, name=name,
        out_shape=(jax.ShapeDtypeStruct((S, D_MODEL), F32), jax.ShapeDtypeStruct((S, D_MODEL), BF16),
                   jax.ShapeDtypeStruct((S, D_MODEL), BF16), jax.ShapeDtypeStruct((1, D_MODEL), F32)),
        grid=(S // tm,), in_specs=[row, row, vec, row, p_s, wpp, wpg], out_specs=(row, row, row, vec),
        compiler_params=_params(("arbitrary",)),
    )(dx, h, gamma, hn, p, w_pp, w_pg)


def _loss_head(y, target, tm):
    S = y.shape[0]

    def body(y_ref, t_ref, dy_ref, l_ref):
        @pl.when(pl.program_id(0) == 0)
        def _():
            l_ref[...] = jnp.zeros_like(l_ref)

        diff = y_ref[...] - t_ref[...]
        dy_ref[...] = diff * (1.0 / D_MODEL)
        sq = jnp.sum((diff * diff).reshape(tm // 8, 8, D_MODEL), axis=0)
        part = sq[:, 0:128]
        for k in range(1, D_MODEL // 128):
            part = part + sq[:, 128 * k:128 * (k + 1)]
        l_ref[...] += part

    row = pl.BlockSpec((tm, D_MODEL), lambda i: (i, 0))
    return pl.pallas_call(
        body, name="loss_head",
        out_shape=(jax.ShapeDtypeStruct((S, D_MODEL), F32), jax.ShapeDtypeStruct((8, 128), F32)),
        grid=(S // tm,), in_specs=[row, row], out_specs=(row, pl.BlockSpec((8, 128), lambda i: (0, 0))),
        compiler_params=_params(("arbitrary",)),
    )(y, target)


def _adamw(parts, w, m, v, name):
    R, C = w.shape
    tr = R
    for cand in (256, 128, 64, 32, 16):
        if R % cand == 0:
            tr = cand
            break

    def body(p_ref, w_ref, m_ref, v_ref, g_ref, d_ref, nm_ref, nv_ref):
        g = p_ref[0].astype(F32)
        for k in range(1, N_DEV):
            g = g + p_ref[k].astype(F32)
        g_ref[...] = g
        nm = ADAM_B1 * m_ref[...] + (1.0 - ADAM_B1) * g
        nv = ADAM_B2 * v_ref[...] + (1.0 - ADAM_B2) * (g * g)
        nm_ref[...] = nm
        nv_ref[...] = nv
        m_hat = nm / (1.0 - ADAM_B1 ** ADAM_STEP)
        v_hat = nv / (1.0 - ADAM_B2 ** ADAM_STEP)
        d_ref[...] = -ADAM_LR * (m_hat / (jnp.sqrt(v_hat) + ADAM_EPS) + ADAM_WD * w_ref[...])

    blk = pl.BlockSpec((tr, C), lambda i: (i, 0))
    out = jax.ShapeDtypeStruct((R, C), F32)
    return pl.pallas_call(
        body, name=name, out_shape=(out, out, out, out), grid=(R // tr,),
        in_specs=[pl.BlockSpec((N_DEV, tr, C), lambda i: (0, i, 0)), blk, blk, blk],
        out_specs=(blk, blk, blk, blk),
        compiler_params=_params(("parallel",)),
    )(parts, w, m, v)


SHARDED = ("w_ffn1_in", "w_ffn1_out", "w_in", "conv_w", "w_attn_out", "w_conv_out", "w_o",
           "w_ffn2_in", "w_ffn2_out", "w_pe_gate", "w_pe_proj")
COL_SHARDED = ("w_ffn1_in", "w_in", "conv_w", "w_attn_out", "w_conv_out", "w_ffn2_in", "w_pe_proj")
SMALL = ("rel_bias", "norm_ffn1", "norm_mix", "q_norm", "k_norm", "sink", "conv_b", "conv_ln_g", "conv_ln_b",
         "norm_ffn2", "norm_pe")
WEIGHTS = ("rel_bias", "norm_ffn1", "w_ffn1_in", "w_ffn1_out", "norm_mix", "w_in", "q_norm", "k_norm", "sink",
           "conv_w", "conv_b", "conv_ln_g", "conv_ln_b", "w_attn_out", "w_conv_out", "w_o", "norm_ffn2",
           "w_ffn2_in", "w_ffn2_out", "norm_pe", "w_pe_gate", "w_pe_proj")


def _natural(g):
    k, n = g.shape[1], g.shape[2]
    return jnp.transpose(g, (1, 0, 2)).reshape(k, N_DEV * n)


def _blocked(w):
    k, n = w.shape[0], w.shape[1] // N_DEV
    return jnp.transpose(w.reshape(k, N_DEV, n), (1, 0, 2))


def kernel(x, p, rel_bias, norm_ffn1, w_ffn1_in, w_ffn1_out, norm_mix, w_in, q_norm, k_norm, sink, conv_w, conv_b, conv_ln_g, conv_ln_b, w_attn_out, w_conv_out, w_o, norm_ffn2, w_ffn2_in, w_ffn2_out, norm_pe, w_pe_gate, w_pe_proj, loss_target, m_rel_bias, m_norm_ffn1, m_w_ffn1_in, m_w_ffn1_out, m_norm_mix, m_w_in, m_q_norm, m_k_norm, m_sink, m_conv_w, m_conv_b, m_conv_ln_g, m_conv_ln_b, m_w_attn_out, m_w_conv_out, m_w_o, m_norm_ffn2, m_w_ffn2_in, m_w_ffn2_out, m_norm_pe, m_w_pe_gate, m_w_pe_proj, v_rel_bias, v_norm_ffn1, v_w_ffn1_in, v_w_ffn1_out, v_norm_mix, v_w_in, v_q_norm, v_k_norm, v_sink, v_conv_w, v_conv_b, v_conv_ln_g, v_conv_ln_b, v_w_attn_out, v_w_conv_out, v_w_o, v_norm_ffn2, v_w_ffn2_in, v_w_ffn2_out, v_norm_pe, v_w_pe_gate, v_w_pe_proj):
    W = dict(rel_bias=rel_bias, norm_ffn1=norm_ffn1, w_ffn1_in=w_ffn1_in, w_ffn1_out=w_ffn1_out, norm_mix=norm_mix,
             w_in=w_in, q_norm=q_norm, k_norm=k_norm, sink=sink, conv_w=conv_w, conv_b=conv_b, conv_ln_g=conv_ln_g,
             conv_ln_b=conv_ln_b, w_attn_out=w_attn_out, w_conv_out=w_conv_out, w_o=w_o, norm_ffn2=norm_ffn2,
             w_ffn2_in=w_ffn2_in, w_ffn2_out=w_ffn2_out, norm_pe=norm_pe, w_pe_gate=w_pe_gate, w_pe_proj=w_pe_proj)
    M = dict(rel_bias=m_rel_bias, norm_ffn1=m_norm_ffn1, w_ffn1_in=m_w_ffn1_in, w_ffn1_out=m_w_ffn1_out,
             norm_mix=m_norm_mix, w_in=m_w_in, q_norm=m_q_norm, k_norm=m_k_norm, sink=m_sink, conv_w=m_conv_w,
             conv_b=m_conv_b, conv_ln_g=m_conv_ln_g, conv_ln_b=m_conv_ln_b, w_attn_out=m_w_attn_out,
             w_conv_out=m_w_conv_out, w_o=m_w_o, norm_ffn2=m_norm_ffn2, w_ffn2_in=m_w_ffn2_in,
             w_ffn2_out=m_w_ffn2_out, norm_pe=m_norm_pe, w_pe_gate=m_w_pe_gate, w_pe_proj=m_w_pe_proj)
    V = dict(rel_bias=v_rel_bias, norm_ffn1=v_norm_ffn1, w_ffn1_in=v_w_ffn1_in, w_ffn1_out=v_w_ffn1_out,
             norm_mix=v_norm_mix, w_in=v_w_in, q_norm=v_q_norm, k_norm=v_k_norm, sink=v_sink, conv_w=v_conv_w,
             conv_b=v_conv_b, conv_ln_g=v_conv_ln_g, conv_ln_b=v_conv_ln_b, w_attn_out=v_w_attn_out,
             w_conv_out=v_w_conv_out, w_o=v_w_o, norm_ffn2=v_norm_ffn2, w_ffn2_in=v_w_ffn2_in,
             w_ffn2_out=v_w_ffn2_out, norm_pe=v_norm_pe, w_pe_gate=v_w_pe_gate, w_pe_proj=v_w_pe_proj)

    L = w_in.shape[0]
    S = x.shape[1]
    tm = min(512, S)
    xs = x[0]
    target = loss_target[0]
    vec = lambda a: a.reshape(1, -1)

    def gather_layer(l):
        shards = [W[n][l] if n == "conv_w" else W[n][l].astype(BF16) for n in SHARDED]
        got = dict(zip(SHARDED, _exchange(shards, True, f"allgather_l{l}")))
        w_in_n = _natural(got["w_in"])
        return dict(
            wi1=got["w_ffn1_in"], wo1=got["w_ffn1_out"], wi2=got["w_ffn2_in"], wo2=got["w_ffn2_out"],
            w_qc=w_in_n[:, :QC_DIM], w_g=w_in_n[:, QC_DIM:],
            conv_w=jnp.pad(_natural(got["conv_w"]), ((0, 1), (0, 0))),
            w_ao=_natural(got["w_attn_out"]), w_co=_natural(got["w_conv_out"]),
            w_o=got["w_o"].reshape(D_MODEL, D_MODEL), w_pg=got["w_pe_gate"].reshape(D_MODEL, D_MODEL),
            w_pp=_natural(got["w_pe_proj"]))

    onehot = _bucket_onehot()
    bias = _bias_table(rel_bias.T, onehot).reshape(N_HEADS, BLOCK, WIN)

    layers, saved = [], []
    h = xs
    for l in range(L):
        G = gather_layer(l)
        layers.append(G)
        sv = dict(x0=h)
        h1, sv["xn1"], sv["g1"], sv["u1"] = _ffn_fwd(h, vec(norm_ffn1[l]), G["wi1"], G["wo1"], tm, "ffn1_fwd")
        sv["h1"] = h1
        sv["un"], sv["qkv"], sv["cvg"] = _mixin_fwd(h1, vec(norm_mix[l]), G["w_qc"], tm, "mixin_fwd")
        sv["qg"] = vec(jnp.tile(q_norm[l], N_HEADS))
        sv["kg"] = vec(jnp.tile(k_norm[l], KV_HEADS))
        sv["sinkb"] = jnp.broadcast_to(sink[l][:, None], (N_HEADS, 128))
        sv["o"] = _attn_fwd(sv["qkv"], sv["qg"], sv["kg"], sv["sinkb"], bias, "attn_fwd")
        sv["cact"], sv["yconv"] = _conv_fwd(sv["cvg"], G["conv_w"], vec(conv_b[l]), vec(conv_ln_g[l]),
                                            vec(conv_ln_b[l]), tm, "conv_fwd")
        h2 = _merge_fwd(h1, sv["un"], sv["o"], sv["cact"], G["w_g"], G["w_ao"], G["w_co"], G["w_o"], tm, "merge_fwd")
        sv["h2"] = h2
        h3, sv["xn2"], sv["g2"], sv["u2"] = _ffn_fwd(h2, vec(norm_ffn2[l]), G["wi2"], G["wo2"], tm, "ffn2_fwd")
        sv["h3"] = h3
        h, sv["hn"] = _pe_fwd(h3, vec(norm_pe[l]), p, l, G["w_pp"], G["w_pg"], tm, "pe_fwd")
        saved.append(sv)

    dh, lparts = _loss_head(h, target, tm)
    loss = lax.psum((0.5 / D_MODEL) * jnp.sum(lparts), AXES)

    dbias = jnp.zeros((N_HEADS, BLOCK, WIN), F32)
    small_g = {n: [None] * L for n in SMALL if n != "rel_bias"}
    recv = {n: [None] * L for n in SHARDED}
    for l in reversed(range(L)):
        G, sv = layers[l], saved[l]
        dh3, dgp_pe, dproj, dg_pe = _pe_bwd(dh, sv["h3"], vec(norm_pe[l]), sv["hn"], p, l, G["w_pp"], G["w_pg"],
                                            tm, "pe_bwd")
        gw_pg = _matmul_tn(sv["hn"][None], dgp_pe[None], 1, "dw_pe_gate")
        gw_pp = _matmul_tn(p[l], dproj[None], 1, "dw_pe_proj")
        dh2, a2, dgu2, dg_n2 = _ffn_bwd(dh3, sv["h2"], vec(norm_ffn2[l]), sv["g2"], sv["u2"], G["wi2"], G["wo2"],
                                        tm, "ffn2_bwd")
        gwo2 = _matmul_tn(a2, dh3[None], FF_BLOCKS, "dw_ffn2_out", scale=0.5)
        gwi2 = _matmul_tn(sv["xn2"][None], dgu2.reshape(2 * FF_BLOCKS, S, FF_SHARD), 2 * FF_BLOCKS, "dw_ffn2_in")
        do, dcact, mix, dya, dyc, dgpre = _merge_bwd(dh2, sv["un"], sv["o"], sv["cact"], G["w_g"], G["w_ao"],
                                                     G["w_co"], G["w_o"], tm, "merge_bwd")
        gw_o = _matmul_tn(mix[None], dh2[None], 1, "dw_o")
        gw_ao = _matmul_tn(sv["o"][None], dya[None], 1, "dw_attn_out")
        gw_co = _matmul_tn(sv["cact"][None], dyc[None], 1, "dw_conv_out")
        dq, dkp, dvp, dbias, dsink, dqg = _attn_bwd(do, sv["qkv"], sv["qg"], sv["kg"], sv["sinkb"], bias, dbias,
                                                    "attn_bwd")
        dkv, dkg = _kv_fold(dkp, dvp, sv["qkv"], sv["kg"], "kv_fold")
        dcvg, dcw, dcvec = _conv_bwd(dcact, sv["yconv"], sv["cvg"], G["conv_w"], vec(conv_ln_g[l]),
                                     vec(conv_ln_b[l]), tm, "conv_bwd")
        dh1, dg_mix = _mixin_bwd(dh2, sv["h1"], vec(norm_mix[l]), dq, dkv, dcvg, dgpre, G["w_qc"], G["w_g"],
                                 tm, "mixin_bwd")
        un3 = sv["un"][None]
        gw_in = jnp.concatenate([
            _matmul_tn(un3, dq[None], 1, "dw_in_q")[0], _matmul_tn(un3, dkv[None], 1, "dw_in_kv")[0],
            _matmul_tn(un3, dcvg[None], 1, "dw_in_c")[0], _matmul_tn(un3, dgpre[None], 1, "dw_in_g")[0]], axis=1)
        dh, a1, dgu1, dg_n1 = _ffn_bwd(dh1, sv["x0"], vec(norm_ffn1[l]), sv["g1"], sv["u1"], G["wi1"], G["wo1"],
                                       tm, "ffn1_bwd")
        gwo1 = _matmul_tn(a1, dh1[None], FF_BLOCKS, "dw_ffn1_out", scale=0.5)
        gwi1 = _matmul_tn(sv["xn1"][None], dgu1.reshape(2 * FF_BLOCKS, S, FF_SHARD), 2 * FF_BLOCKS, "dw_ffn1_in")

        send = dict(
            w_ffn1_in=gwi1, w_ffn1_out=gwo1.reshape(N_DEV, FF_SHARD // 2, D_MODEL),
            w_in=_blocked(gw_in), conv_w=_blocked(dcw[:CONV_WIDTH]),
            w_attn_out=_blocked(gw_ao[0]), w_conv_out=_blocked(gw_co[0]),
            w_o=gw_o.reshape(N_DEV, D_MODEL // N_DEV, D_MODEL),
            w_ffn2_in=gwi2, w_ffn2_out=gwo2.reshape(N_DEV, FF_SHARD // 2, D_MODEL),
            w_pe_gate=gw_pg.reshape(N_DEV, D_MODEL // N_DEV, D_MODEL), w_pe_proj=_blocked(gw_pp[0]))
        got = _exchange([send[n] for n in SHARDED], False, f"grad_exchange_l{l}")
        for n, r in zip(SHARDED, got):
            recv[n][l] = r
        small_g["norm_ffn1"][l] = dg_n1[0]
        small_g["norm_mix"][l] = dg_mix[0]
        small_g["q_norm"][l] = dqg[0, :HEAD_DIM]
        small_g["k_norm"][l] = dkg[0, :HEAD_DIM]
        small_g["sink"][l] = dsink[:, 0]
        small_g["conv_b"][l] = dcvec[0]
        small_g["conv_ln_g"][l] = dcvec[1]
        small_g["conv_ln_b"][l] = dcvec[2]
        small_g["norm_ffn2"][l] = dg_n2[0]
        small_g["norm_pe"][l] = dg_pe[0]

    grad_x = dh[None]
    drb = _bias_grad(dbias.reshape(N_HEADS, BIAS_COLS), onehot).T

    res = {}
    for n in SHARDED:
        parts = jnp.stack(recv[n], axis=1)
        shp = W[n].shape
        rows, cols = shp[0] * shp[1], shp[2]
        res[n] = [o.reshape(shp) for o in _adamw(
            parts.reshape(N_DEV, rows, cols), W[n].reshape(rows, cols), M[n].reshape(rows, cols),
            V[n].reshape(rows, cols), "adamw_" + n)]

    flat_g = jnp.concatenate([drb.reshape(-1)] + [jnp.stack(small_g[n]).reshape(-1) for n in SMALL[1:]])
    n_small = flat_g.shape[0]
    rows_s = -(-n_small // 1024 // 8) * 8
    pad = lambda a: jnp.pad(a, (0, rows_s * 1024 - n_small)).reshape(rows_s, 1024)
    flat = lambda d: pad(jnp.concatenate([d[n].reshape(-1) for n in SMALL]))
    (parts_s,) = _exchange([pad(flat_g)], True, "small_allgather")
    outs_s = _adamw(parts_s, flat(W), flat(M), flat(V), "adamw_small")
    off = 0
    for n in SMALL:
        size = W[n].size
        res[n] = [o.reshape(-1)[off:off + size].reshape(W[n].shape) for o in outs_s]
        off += size

    out = [loss, grad_x]
    for k in range(4):
        out += [res[n][k] for n in WEIGHTS]
    return tuple(out)
```

```python
import functools

import jax
import jax.numpy as jnp
import numpy as np
from jax import lax
from jax.experimental import pallas as pl
from jax.experimental.pallas import tpu as pltpu

F32 = jnp.float32
BF16 = jnp.bfloat16
MESH_ID = pl.DeviceIdType.MESH
AXES = ("x", "y", "c")
N_DEV = 8

D_MODEL = 1024
N_HEADS = 8
KV_HEADS = 2
HEAD_DIM = 64
Q_DIM = 512
KV_DIM = 128
BLOCK = 128
WIN = 3 * BLOCK
NUM_BUCKETS = 32
MAX_DISTANCE = 128
CONV_DIM = 512
CONV_WIDTH = 31
D_FF = 2816
FF_SHARD = 2 * D_FF // N_DEV
FF_BLOCKS = D_FF // FF_SHARD
QC_DIM = Q_DIM + 2 * KV_DIM + 2 * CONV_DIM
NEG_INF = -1e9
HALO = 16

ADAM_LR = 0.001
ADAM_B1 = 0.9
ADAM_B2 = 0.999
ADAM_EPS = 1e-08
ADAM_WD = 0.01
ADAM_STEP = 10

VMEM_LIMIT = 56 * 1024 * 1024
HI = lax.Precision.HIGHEST


def _params(sem):
    return pltpu.CompilerParams(dimension_semantics=sem, vmem_limit_bytes=VMEM_LIMIT)


def _dot(a, b, precision=None):
    return jnp.dot(a, b, preferred_element_type=F32, precision=precision)


def _dot_nt(a, b):
    return lax.dot_general(a, b, (((1,), (1,)), ((), ())), preferred_element_type=F32)


def _dot_tn(a, b):
    return lax.dot_general(a, b, (((0,), (0,)), ((), ())), preferred_element_type=F32)


def _sig(x):
    return 1.0 / (1.0 + jnp.exp(-x))


def _bf(x):
    return x.astype(BF16)


def _rms_fwd(x, gamma):
    r = lax.rsqrt(jnp.mean(x * x, axis=-1, keepdims=True) + 1e-6)
    return x * r * gamma, r


def _rms_bwd(dy, x, gamma):
    r = lax.rsqrt(jnp.mean(x * x, axis=-1, keepdims=True) + 1e-6)
    xhat = x * r
    dxhat = dy * gamma
    dx = r * (dxhat - xhat * jnp.mean(dxhat * xhat, axis=-1, keepdims=True))
    return dx, dy * xhat


def _head_onehot(width, n_lanes=128):
    c = lax.broadcasted_iota(jnp.int32, (width, n_lanes), 0)
    h = lax.broadcasted_iota(jnp.int32, (width, n_lanes), 1)
    e = ((c >> 6) == h).astype(F32)
    ct = lax.broadcasted_iota(jnp.int32, (n_lanes, width), 1)
    ht = lax.broadcasted_iota(jnp.int32, (n_lanes, width), 0)
    et = ((ct >> 6) == ht).astype(F32)
    return e, et


def _head_fold(width, n_lanes=128):
    c = lax.broadcasted_iota(jnp.int32, (width, n_lanes), 0)
    d = lax.broadcasted_iota(jnp.int32, (width, n_lanes), 1)
    return ((c & (HEAD_DIM - 1)) == d).astype(F32)


def _head_rstd(x, e, et):
    ms = _dot(x * x, e, HI) * (1.0 / HEAD_DIM)
    return _dot(lax.rsqrt(ms + 1e-6), et, HI)


def _head_mean(x, e, et):
    return _dot(_dot(x, e, HI) * (1.0 / HEAD_DIM), et, HI)


class _Exchange:
    def __init__(self, arrs, gather):
        self.arrs, self.gather, self.n = list(arrs), gather, len(arrs)
        n = self.n
        self.out_shape = tuple(
            jax.ShapeDtypeStruct(((N_DEV,) + a.shape) if gather else a.shape, a.dtype) for a in self.arrs)
        self.specs = [pl.BlockSpec(memory_space=pl.ANY)] * n
        self.scratch = [pltpu.SemaphoreType.DMA((7 * n,)), pltpu.SemaphoreType.DMA((7 * n,)),
                        pltpu.SemaphoreType.DMA((n,))] if n else []

    def _copies(self, ins, outs, sems):
        n, gather = self.n, self.gather
        send_sems, recv_sems, local_sems = sems
        x, y, c = lax.axis_index("x"), lax.axis_index("y"), lax.axis_index("c")
        me = 4 * x + 2 * y + c
        copies = [pltpu.make_async_copy(ins[t] if gather else ins[t].at[me], outs[t].at[me], local_sems.at[t])
                  for t in range(n)]
        for d in range(1, N_DEV):
            px = 1 - x if d & 4 else x
            py = 1 - y if d & 2 else y
            pc = 1 - c if d & 1 else c
            peer = 4 * px + 2 * py + pc
            for t in range(n):
                k = (d - 1) * n + t
                copies.append(pltpu.make_async_remote_copy(
                    src_ref=ins[t] if gather else ins[t].at[peer], dst_ref=outs[t].at[me],
                    send_sem=send_sems.at[k], recv_sem=recv_sems.at[k],
                    device_id=(px, py, pc), device_id_type=MESH_ID))
        return copies

    def start(self, ins, outs, sems):
        for cp in self._copies(ins, outs, sems):
            cp.start()

    def wait(self, ins, outs, sems):
        for cp in self._copies(ins, outs, sems):
            cp.wait()


NO_EXCHANGE = _Exchange([], True)


def _exchange(arrs, gather, name):
    ex = _Exchange(arrs, gather)
    n = ex.n

    def body(*refs):
        ins, outs, sems = refs[:n], refs[n:2 * n], refs[2 * n:]
        ex.start(ins, outs, sems)
        ex.wait(ins, outs, sems)

    return pl.pallas_call(
        body, name=name, out_shape=ex.out_shape, in_specs=ex.specs, out_specs=tuple(ex.specs),
        scratch_shapes=ex.scratch,
    )(*ex.arrs)


def _carrier_call(body, ex, first_last, name, out_shape, grid, in_specs, out_specs, scratch_shapes, args):
    n_in, n_out, n_scr, n = len(in_specs), len(out_shape), len(scratch_shapes), ex.n

    def full(*refs):
        a, ci = refs[:n_in], refs[n_in:n_in + n]
        o = refs[n_in + n:n_in + n + n_out]
        co = refs[n_in + n + n_out:n_in + 2 * n + n_out]
        scr = refs[n_in + 2 * n + n_out:n_in + 2 * n + n_out + n_scr]
        sems = refs[n_in + 2 * n + n_out + n_scr:]
        first, last = first_last()
        if n:
            @pl.when(first)
            def _():
                ex.start(ci, co, sems)

        body(*a, *o, *scr)
        if n:
            @pl.when(last)
            def _():
                ex.wait(ci, co, sems)

    outs = pl.pallas_call(
        full, name=name, out_shape=tuple(out_shape) + ex.out_shape, grid=grid,
        in_specs=list(in_specs) + ex.specs, out_specs=tuple(out_specs) + tuple(ex.specs),
        scratch_shapes=list(scratch_shapes) + ex.scratch,
        compiler_params=_params(("arbitrary",) * len(grid)),
    )(*args, *ex.arrs)
    return outs[:n_out], outs[n_out:]


def _matmul_tn(a, b, nb, name, scale=1.0, out_dtype=BF16, ts=1024):
    ba, S, K = a.shape
    bb, _, N = b.shape
    ts = min(ts, S)
    tn = N if N <= 1024 else 1024
    assert N % tn == 0 and S % ts == 0
    ns = S // ts

    def body(a_ref, b_ref, o_ref, acc):
        s = pl.program_id(2)

        @pl.when(s == 0)
        def _():
            acc[...] = jnp.zeros_like(acc)

        acc[...] += _dot_tn(_bf(a_ref[...]), _bf(b_ref[...]))

        @pl.when(s == ns - 1)
        def _():
            o_ref[...] = (acc[...] * scale).astype(out_dtype)

    return pl.pallas_call(
        body, name=name, out_shape=jax.ShapeDtypeStruct((nb, K, N), out_dtype),
        grid=(nb, N // tn, ns),
        in_specs=[pl.BlockSpec((None, ts, K), (lambda i, j, s: (i, s, 0)) if ba > 1 else (lambda i, j, s: (0, s, 0))),
                  pl.BlockSpec((None, ts, tn), (lambda i, j, s: (i, s, j)) if bb > 1 else (lambda i, j, s: (0, s, j)))],
        out_specs=pl.BlockSpec((None, K, tn), lambda i, j, s: (i, 0, j)),
        scratch_shapes=[pltpu.VMEM((K, tn), F32)],
        compiler_params=_params(("parallel", "parallel", "arbitrary")),
    )(a, b)


def _ffn_specs(tm):
    wg = pl.BlockSpec((None, D_MODEL, FF_SHARD), lambda i, j: (j, 0, 0))
    wu = pl.BlockSpec((None, D_MODEL, FF_SHARD), lambda i, j: (j + FF_BLOCKS, 0, 0))
    wo = pl.BlockSpec((2, FF_SHARD // 2, D_MODEL), lambda i, j: (j, 0, 0))
    row = pl.BlockSpec((tm, D_MODEL), lambda i, j: (i, 0))
    vec = pl.BlockSpec((1, D_MODEL), lambda i, j: (0, 0))
    hid = pl.BlockSpec((None, tm, FF_SHARD), lambda i, j: (j, i, 0))
    return wg, wu, wo, row, vec, hid


def _grid2_ends(ni, nj):
    def first_last():
        i, j = pl.program_id(0), pl.program_id(1)
        return (i == 0) & (j == 0), (i == ni - 1) & (j == nj - 1)
    return first_last


def _grid1_ends(ni):
    def first_last():
        i = pl.program_id(0)
        return i == 0, i == ni - 1
    return first_last


def _ffn_fwd(x, gamma, wi, wo, tm, name, ex=NO_EXCHANGE):
    S = x.shape[0]
    wg_s, wu_s, wo_s, row, vec, hid = _ffn_specs(tm)

    def body(x_ref, g_ref, wg_ref, wu_ref, wo_ref, y_ref, xn_ref, gs_ref, us_ref, xn_s, acc):
        j = pl.program_id(1)

        @pl.when(j == 0)
        def _():
            xn = _bf(_rms_fwd(x_ref[...], g_ref[...])[0])
            xn_s[...] = xn
            xn_ref[...] = xn
            acc[...] = jnp.zeros_like(acc)

        xn = xn_s[...]
        g = _dot(xn, wg_ref[...])
        u = _dot(xn, wu_ref[...])
        gs_ref[...] = _bf(g)
        us_ref[...] = _bf(u)
        a = g * _sig(g) * u
        acc[...] += _dot(_bf(a), wo_ref[...].reshape(FF_SHARD, D_MODEL))

        @pl.when(j == FF_BLOCKS - 1)
        def _():
            y_ref[...] = x_ref[...] + 0.5 * acc[...]

    return _carrier_call(
        body, ex, _grid2_ends(S // tm, FF_BLOCKS), name,
        out_shape=(jax.ShapeDtypeStruct((S, D_MODEL), F32), jax.ShapeDtypeStruct((S, D_MODEL), BF16),
                   jax.ShapeDtypeStruct((FF_BLOCKS, S, FF_SHARD), BF16),
                   jax.ShapeDtypeStruct((FF_BLOCKS, S, FF_SHARD), BF16)),
        grid=(S // tm, FF_BLOCKS),
        in_specs=[row, vec, wg_s, wu_s, wo_s],
        out_specs=(row, row, hid, hid),
        scratch_shapes=[pltpu.VMEM((tm, D_MODEL), BF16), pltpu.VMEM((tm, D_MODEL), F32)],
        args=(x, gamma, wi, wi, wo))


def _ffn_bwd(dy, x, gamma, gs, us, wi, wo, tm, name, ex=NO_EXCHANGE):
    S = x.shape[0]
    wg_s, wu_s, wo_s, row, vec, hid = _ffn_specs(tm)
    dgu_s = pl.BlockSpec((2, None, tm, FF_SHARD), lambda i, j: (0, j, i, 0))

    def body(dy_ref, x_ref, g_ref, gs_ref, us_ref, wg_ref, wu_ref, wo_ref,
             dx_ref, a_ref, dgu_ref, dgam_ref, dyh_s, acc):
        i, j = pl.program_id(0), pl.program_id(1)

        @pl.when(j == 0)
        def _():
            dyh_s[...] = _bf(0.5 * dy_ref[...])
            acc[...] = jnp.zeros_like(acc)

        @pl.when((i == 0) & (j == 0))
        def _():
            dgam_ref[...] = jnp.zeros_like(dgam_ref)

        da = _dot_nt(dyh_s[...], wo_ref[...].reshape(FF_SHARD, D_MODEL))
        g = gs_ref[...].astype(F32)
        u = us_ref[...].astype(F32)
        sg = _sig(g)
        sl = g * sg
        a_ref[...] = _bf(sl * u)
        dg = _bf(da * u * (sg * (1.0 + g * (1.0 - sg))))
        du = _bf(da * sl)
        dgu_ref[0] = dg
        dgu_ref[1] = du
        acc[...] += _dot_nt(dg, wg_ref[...]) + _dot_nt(du, wu_ref[...])

        @pl.when(j == FF_BLOCKS - 1)
        def _():
            dx, dgam = _rms_bwd(acc[...], x_ref[...], g_ref[...])
            dx_ref[...] = dy_ref[...] + dx
            dgam_ref[...] += jnp.sum(dgam, axis=0, keepdims=True)

    return _carrier_call(
        body, ex, _grid2_ends(S // tm, FF_BLOCKS), name,
        out_shape=(jax.ShapeDtypeStruct((S, D_MODEL), F32),
                   jax.ShapeDtypeStruct((FF_BLOCKS, S, FF_SHARD), BF16),
                   jax.ShapeDtypeStruct((2, FF_BLOCKS, S, FF_SHARD), BF16),
                   jax.ShapeDtypeStruct((1, D_MODEL), F32)),
        grid=(S // tm, FF_BLOCKS),
        in_specs=[row, row, vec, hid, hid, wg_s, wu_s, wo_s],
        out_specs=(row, hid, dgu_s, vec),
        scratch_shapes=[pltpu.VMEM((tm, D_MODEL), BF16), pltpu.VMEM((tm, D_MODEL), F32)],
        args=(dy, x, gamma, gs, us, wi, wi, wo))


def _mixin_fwd(h, gamma, w_qc, tm, name):
    S = h.shape[0]
    nqkv = Q_DIM + 2 * KV_DIM

    def body(h_ref, g_ref, w_ref, un_ref, qkv_ref, cvg_ref):
        un = _bf(_rms_fwd(h_ref[...], g_ref[...])[0])
        un_ref[...] = un
        z = _dot(un, w_ref[...])
        qkv_ref[...] = z[:, :nqkv]
        cvg_ref[...] = z[:, nqkv:]

    row = lambda w: pl.BlockSpec((tm, w), lambda i: (i, 0))
    return pl.pallas_call(
        body, name=name,
        out_shape=(jax.ShapeDtypeStruct((S, D_MODEL), BF16), jax.ShapeDtypeStruct((S, nqkv), F32),
                   jax.ShapeDtypeStruct((S, 2 * CONV_DIM), F32)),
        grid=(S // tm,),
        in_specs=[row(D_MODEL), pl.BlockSpec((1, D_MODEL), lambda i: (0, 0)),
                  pl.BlockSpec((D_MODEL, QC_DIM), lambda i: (0, 0))],
        out_specs=(row(D_MODEL), row(nqkv), row(2 * CONV_DIM)),
        compiler_params=_params(("parallel",)),
    )(h, gamma, w_qc)


def _mixin_bwd(dh2, h1, gamma, dq, dkv, dcvg, dgpre, w_qc, w_g, tm, name):
    S = h1.shape[0]
    nqkv = Q_DIM + 2 * KV_DIM

    def body(dh2_ref, h1_ref, g_ref, dq_ref, dkv_ref, dcvg_ref, dgp_ref, wqc_ref, wg_ref, dh1_ref, dgam_ref):
        @pl.when(pl.program_id(0) == 0)
        def _():
            dgam_ref[...] = jnp.zeros_like(dgam_ref)

        wqc = wqc_ref[...]
        dun = _dot_nt(_bf(dq_ref[...]), wqc[:, :Q_DIM])
        dun += _dot_nt(_bf(dkv_ref[...]), wqc[:, Q_DIM:nqkv])
        dun += _dot_nt(_bf(dcvg_ref[...]), wqc[:, nqkv:])
        dun += _dot_nt(dgp_ref[...], wg_ref[...])
        dx, dgam = _rms_bwd(dun, h1_ref[...], g_ref[...])
        dh1_ref[...] = dh2_ref[...] + dx
        dgam_ref[...] += jnp.sum(dgam, axis=0, keepdims=True)

    row = lambda w: pl.BlockSpec((tm, w), lambda i: (i, 0))
    vec = pl.BlockSpec((1, D_MODEL), lambda i: (0, 0))
    return pl.pallas_call(
        body, name=name,
        out_shape=(jax.ShapeDtypeStruct((S, D_MODEL), F32), jax.ShapeDtypeStruct((1, D_MODEL), F32)),
        grid=(S // tm,),
        in_specs=[row(D_MODEL), row(D_MODEL), vec, row(Q_DIM), row(2 * KV_DIM), row(2 * CONV_DIM),
                  row(2 * D_MODEL), pl.BlockSpec((D_MODEL, QC_DIM), lambda i: (0, 0)),
                  pl.BlockSpec((D_MODEL, 2 * D_MODEL), lambda i: (0, 0))],
        out_specs=(row(D_MODEL), vec),
        compiler_params=_params(("arbitrary",)),
    )(dh2, h1, gamma, dq, dkv, dcvg, dgpre, w_qc, w_g)


TQ = 512
QB = TQ // BLOCK


def _attn_in_specs(S):
    nkb = S // BLOCK
    return [
        pl.BlockSpec((TQ, Q_DIM), lambda i: (i, 0)),
        pl.BlockSpec((BLOCK, 2 * KV_DIM), lambda i: (jnp.maximum(i * QB - 1, 0), Q_DIM // (2 * KV_DIM))),
        pl.BlockSpec((TQ, 2 * KV_DIM), lambda i: (i, Q_DIM // (2 * KV_DIM))),
        pl.BlockSpec((BLOCK, 2 * KV_DIM), lambda i: (jnp.minimum(i * QB + QB, nkb - 1), Q_DIM // (2 * KV_DIM))),
        pl.BlockSpec((1, Q_DIM), lambda i: (0, 0)),
        pl.BlockSpec((1, KV_DIM), lambda i: (0, 0)),
        pl.BlockSpec((N_HEADS, 128), lambda i: (0, 0)),
        pl.BlockSpec((N_HEADS, BLOCK, WIN), lambda i: (0, 0, 0)),
    ]


def _attn_prologue(q_ref, kvp_ref, kvc_ref, kvn_ref, qg_ref, kg_ref):
    low = lax.broadcasted_iota(jnp.int32, (1, 128), 1) < HEAD_DIM
    eq, etq = _head_onehot(Q_DIM)
    q = q_ref[...]
    rq = _head_rstd(q, eq, etq)
    qs = q * rq * qg_ref[...] * (HEAD_DIM ** -0.5)
    kv = jnp.concatenate([kvp_ref[...], kvc_ref[...], kvn_ref[...]], axis=0)
    k, v = kv[:, :KV_DIM], kv[:, KV_DIM:]
    ek, etk = _head_onehot(KV_DIM)
    kn = k * _head_rstd(k, ek, etk) * kg_ref[...]
    kr, vr = pltpu.roll(kn, HEAD_DIM, 1), pltpu.roll(v, HEAD_DIM, 1)
    kdup = [_bf(jnp.where(low, kn, kr)), _bf(jnp.where(low, kr, kn))]
    vdup = [_bf(jnp.where(low, v, vr)), _bf(jnp.where(low, vr, v))]
    return q, rq, qs, kdup, vdup, low


def _attn_valid(i, t, S):
    r = lax.broadcasted_iota(jnp.int32, (QB * BLOCK, WIN), 0) & (BLOCK - 1)
    cidx = lax.broadcasted_iota(jnp.int32, (QB * BLOCK, WIN), 1)
    rel = cidx - BLOCK - r
    kpos = i * TQ + (t - 1) * BLOCK + cidx
    return (jnp.abs(rel) <= BLOCK) & (kpos >= 0) & (kpos < S)


def _stack_heads(x, t, kh, low):
    rows = slice(t * BLOCK, (t + 1) * BLOCK)
    xa = x[rows, 256 * kh:256 * kh + 128]
    xb = x[rows, 256 * kh + 128:256 * kh + 256]
    z = jnp.zeros_like(xa)
    return jnp.concatenate([jnp.where(low, xa, z), jnp.where(low, z, xa),
                            jnp.where(low, xb, z), jnp.where(low, z, xb)], axis=0)


def _unstack_heads(ov, low):
    return (jnp.where(low, ov[0:128], ov[128:256]), jnp.where(low, ov[256:384], ov[384:512]))


def _sink_col(sink_ref, kh):
    return jnp.concatenate(
        [jnp.broadcast_to(sink_ref[4 * kh + r:4 * kh + r + 1, 0:1], (BLOCK, 1)) for r in range(4)], axis=0)


def _attn_probs(lhs, kw, bias, valid, sk):
    s = _dot_nt(lhs, kw) + bias
    s = jnp.where(valid, s, NEG_INF)
    m = jnp.maximum(jnp.max(s, axis=-1, keepdims=True), sk)
    e = jnp.exp(s - m)
    es = jnp.exp(sk - m)
    inv = 1.0 / (jnp.sum(e, axis=-1, keepdims=True) + es)
    return e * inv, es * inv


def _attn_fwd(qkv, qg, kg, sinkb, bias, name, ex=NO_EXCHANGE):
    S = qkv.shape[0]

    def body(q_ref, kvp_ref, kvc_ref, kvn_ref, qg_ref, kg_ref, sink_ref, bias_ref, o_ref):
        i = pl.program_id(0)
        _, _, qs, kdup, vdup, low = _attn_prologue(q_ref, kvp_ref, kvc_ref, kvn_ref, qg_ref, kg_ref)
        for t in range(QB):
            valid = _attn_valid(i, t, S)
            rows = slice(t * BLOCK, (t + 1) * BLOCK)
            for kh in range(KV_HEADS):
                lhs = _bf(_stack_heads(qs, t, kh, low))
                kw = kdup[kh][t * BLOCK:t * BLOCK + WIN]
                vw = vdup[kh][t * BLOCK:t * BLOCK + WIN]
                bias_g = bias_ref[4 * kh:4 * kh + 4].reshape(4 * BLOCK, WIN)
                pr, _ = _attn_probs(lhs, kw, bias_g, valid, _sink_col(sink_ref, kh))
                oa, ob = _unstack_heads(_dot(_bf(pr), vw), low)
                o_ref[rows, 256 * kh:256 * kh + 128] = _bf(oa)
                o_ref[rows, 256 * kh + 128:256 * kh + 256] = _bf(ob)

    return _carrier_call(
        body, ex, _grid1_ends(S // TQ), name, out_shape=(jax.ShapeDtypeStruct((S, Q_DIM), BF16),),
        grid=(S // TQ,), in_specs=_attn_in_specs(S),
        out_specs=(pl.BlockSpec((TQ, Q_DIM), lambda i: (i, 0)),), scratch_shapes=[],
        args=(qkv, qkv, qkv, qkv, qg, kg, sinkb, bias))


def _attn_bwd(do, qkv, qg, kg, sinkb, bias, dbias_in, name, ex=NO_EXCHANGE):
    S = qkv.shape[0]
    nkb = S // BLOCK
    nsteps = S // TQ

    def body(do_ref, q_ref, kvp_ref, kvc_ref, kvn_ref, qg_ref, kg_ref, sink_ref, bias_ref, dbin_ref,
             dq_ref, dkp_ref, dvp_ref, dbias_ref, dsink_ref, dqg_ref, dqs_s, dqg_s):
        i = pl.program_id(0)

        @pl.when(i == 0)
        def _():
            dbias_ref[...] = dbin_ref[...]
            dsink_ref[...] = jnp.zeros_like(dsink_ref)
            dqg_s[...] = jnp.zeros_like(dqg_s)

        q, rq, qs, kdup, vdup, low = _attn_prologue(q_ref, kvp_ref, kvc_ref, kvn_ref, qg_ref, kg_ref)
        do = do_ref[...]
        for t in range(QB):
            valid = _attn_valid(i, t, S)
            rows = slice(t * BLOCK, (t + 1) * BLOCK)
            dk_dup, dv_dup = [], []
            for kh in range(KV_HEADS):
                lhs = _bf(_stack_heads(qs, t, kh, low))
                dos = _bf(_stack_heads(do, t, kh, low))
                kw = kdup[kh][t * BLOCK:t * BLOCK + WIN]
                vw = vdup[kh][t * BLOCK:t * BLOCK + WIN]
                bias_g = bias_ref[4 * kh:4 * kh + 4].reshape(4 * BLOCK, WIN)
                pr, ps = _attn_probs(lhs, kw, bias_g, valid, _sink_col(sink_ref, kh))
                dpr = _dot_nt(dos, vw)
                delta = jnp.sum(pr * dpr, axis=-1, keepdims=True)
                ds = pr * (dpr - delta)
                dbias_ref[4 * kh:4 * kh + 4] += ds.reshape(4, BLOCK, WIN)
                dsk = ps * delta
                for r in range(4):
                    dsink_ref[4 * kh + r:4 * kh + r + 1, :] -= jnp.broadcast_to(
                        jnp.sum(dsk[r * BLOCK:(r + 1) * BLOCK], axis=0, keepdims=True), (1, 128))
                dsb = _bf(ds)
                dqa, dqb = _unstack_heads(_dot(dsb, kw), low)
                dqs_s[rows, 256 * kh:256 * kh + 128] = dqa
                dqs_s[rows, 256 * kh + 128:256 * kh + 256] = dqb
                dkx = _dot_tn(dsb, lhs)
                dvx = _dot_tn(_bf(pr), dos)
                dk_dup.append(dkx + pltpu.roll(dkx, HEAD_DIM, 1))
                dv_dup.append(dvx + pltpu.roll(dvx, HEAD_DIM, 1))
            dkp_ref[t] = jnp.where(low, dk_dup[0], dk_dup[1])
            dvp_ref[t] = jnp.where(low, dv_dup[0], dv_dup[1])

        eq, etq = _head_onehot(Q_DIM)
        dyq = dqs_s[...] * (HEAD_DIM ** -0.5)
        qhat = q * rq
        dxhat = dyq * qg_ref[...]
        dq_ref[...] = rq * (dxhat - qhat * _head_mean(dxhat * qhat, eq, etq))
        dqg_s[...] += jnp.sum((dyq * qhat).reshape(TQ // 8, 8, Q_DIM), axis=0)

        @pl.when(i == nsteps - 1)
        def _():
            folded = _dot(dqg_s[...], _head_fold(Q_DIM), HI)
            dqg_ref[...] = jnp.broadcast_to(jnp.sum(folded, axis=0, keepdims=True), (8, 128))

    const2 = lambda shape: pl.BlockSpec(shape, lambda i: (0,) * len(shape))
    part = pl.BlockSpec((QB, WIN, KV_DIM), lambda i: (i, 0, 0))
    return _carrier_call(
        body, ex, _grid1_ends(nsteps), name,
        out_shape=(jax.ShapeDtypeStruct((S, Q_DIM), F32), jax.ShapeDtypeStruct((nkb, WIN, KV_DIM), F32),
                   jax.ShapeDtypeStruct((nkb, WIN, KV_DIM), F32),
                   jax.ShapeDtypeStruct((N_HEADS, BLOCK, WIN), F32), jax.ShapeDtypeStruct((N_HEADS, 128), F32),
                   jax.ShapeDtypeStruct((8, 128), F32)),
        grid=(nsteps,),
        in_specs=[pl.BlockSpec((TQ, Q_DIM), lambda i: (i, 0))] + _attn_in_specs(S)
        + [const2((N_HEADS, BLOCK, WIN))],
        out_specs=(pl.BlockSpec((TQ, Q_DIM), lambda i: (i, 0)), part, part,
                   const2((N_HEADS, BLOCK, WIN)), const2((N_HEADS, 128)), const2((8, 128))),
        scratch_shapes=[pltpu.VMEM((TQ, Q_DIM), F32), pltpu.VMEM((8, Q_DIM), F32)],
        args=(do, qkv, qkv, qkv, qkv, qg, kg, sinkb, bias, dbias_in))


def _kv_fold(dkp, dvp, qkv, kg, name):
    nkb = dkp.shape[0]
    S = nkb * BLOCK
    nsteps = S // TQ

    def body(kp_p, kp_c, kp_n, vp_p, vp_c, vp_n, kv_ref, kg_ref, dkv_ref, dkg_ref, dkg_s):
        i = pl.program_id(0)

        @pl.when(i == 0)
        def _():
            dkg_s[...] = jnp.zeros_like(dkg_s)

        def fold(p_ref, c_ref, n_ref):
            blocks = []
            for t in range(QB):
                acc = c_ref[t, BLOCK:2 * BLOCK, :]
                if t > 0:
                    acc = acc + c_ref[t - 1, 2 * BLOCK:, :]
                else:
                    acc = acc + jnp.where(i > 0, p_ref[0, 2 * BLOCK:, :], 0.0)
                if t < QB - 1:
                    acc = acc + c_ref[t + 1, :BLOCK, :]
                else:
                    acc = acc + jnp.where(i < nsteps - 1, n_ref[0, :BLOCK, :], 0.0)
                blocks.append(acc)
            return jnp.concatenate(blocks, axis=0)

        dkn = fold(kp_p, kp_c, kp_n)
        dv = fold(vp_p, vp_c, vp_n)
        k = kv_ref[:, :KV_DIM]
        ek, etk = _head_onehot(KV_DIM)
        rk = _head_rstd(k, ek, etk)
        khat = k * rk
        dxhat = dkn * kg_ref[...]
        dkv_ref[:, :KV_DIM] = rk * (dxhat - khat * _head_mean(dxhat * khat, ek, etk))
        dkv_ref[:, KV_DIM:] = dv
        dkg_s[...] += jnp.sum((dkn * khat).reshape(TQ // 8, 8, KV_DIM), axis=0)

        @pl.when(i == nsteps - 1)
        def _():
            folded = _dot(dkg_s[...], _head_fold(KV_DIM), HI)
            dkg_ref[...] = jnp.broadcast_to(jnp.sum(folded, axis=0, keepdims=True), (8, 128))

    prev = pl.BlockSpec((1, WIN, KV_DIM), lambda i: (jnp.maximum(i * QB - 1, 0), 0, 0))
    cur = pl.BlockSpec((QB, WIN, KV_DIM), lambda i: (i, 0, 0))
    nxt = pl.BlockSpec((1, WIN, KV_DIM), lambda i: (jnp.minimum(i * QB + QB, nkb - 1), 0, 0))
    return pl.pallas_call(
        body, name=name,
        out_shape=(jax.ShapeDtypeStruct((S, 2 * KV_DIM), F32), jax.ShapeDtypeStruct((8, 128), F32)),
        grid=(nsteps,),
        in_specs=[prev, cur, nxt, prev, cur, nxt,
                  pl.BlockSpec((TQ, 2 * KV_DIM), lambda i: (i, Q_DIM // (2 * KV_DIM))),
                  pl.BlockSpec((1, KV_DIM), lambda i: (0, 0))],
        out_specs=(pl.BlockSpec((TQ, 2 * KV_DIM), lambda i: (i, 0)), pl.BlockSpec((8, 128), lambda i: (0, 0))),
        scratch_shapes=[pltpu.VMEM((8, KV_DIM), F32)],
        compiler_params=_params(("arbitrary",)),
    )(dkp, dkp, dkp, dvp, dvp, dvp, qkv, kg)


BIAS_COLS = BLOCK * WIN
BIAS_CHUNK = 6144


def _bias_table(rel_bias_t, onehot):
    def body(rb_ref, oh_ref, o_ref):
        o_ref[...] = _dot(rb_ref[...], oh_ref[...], HI)

    return pl.pallas_call(
        body, name="bias_table", out_shape=jax.ShapeDtypeStruct((N_HEADS, BIAS_COLS), F32),
        grid=(BIAS_COLS // BIAS_CHUNK,),
        in_specs=[pl.BlockSpec((N_HEADS, NUM_BUCKETS), lambda i: (0, 0)),
                  pl.BlockSpec((NUM_BUCKETS, BIAS_CHUNK), lambda i: (0, i))],
        out_specs=pl.BlockSpec((N_HEADS, BIAS_CHUNK), lambda i: (0, i)),
        compiler_params=_params(("parallel",)),
    )(rel_bias_t, onehot)


def _bias_grad(dbias, onehot):
    def body(db_ref, oh_ref, o_ref):
        @pl.when(pl.program_id(0) == 0)
        def _():
            o_ref[...] = jnp.zeros_like(o_ref)

        o_ref[...] += lax.dot_general(db_ref[...], oh_ref[...], (((1,), (1,)), ((), ())),
                                      preferred_element_type=F32, precision=HI)

    return pl.pallas_call(
        body, name="bias_grad", out_shape=jax.ShapeDtypeStruct((N_HEADS, NUM_BUCKETS), F32),
        grid=(BIAS_COLS // BIAS_CHUNK,),
        in_specs=[pl.BlockSpec((N_HEADS, BIAS_CHUNK), lambda i: (0, i)),
                  pl.BlockSpec((NUM_BUCKETS, BIAS_CHUNK), lambda i: (0, i))],
        out_specs=pl.BlockSpec((N_HEADS, NUM_BUCKETS), lambda i: (0, 0)),
        compiler_params=_params(("arbitrary",)),
    )(dbias, onehot)


def _bucket_onehot():
    half = NUM_BUCKETS // 2
    max_exact = half // 2
    rel = jnp.arange(WIN)[None, :] - BLOCK - jnp.arange(BLOCK)[:, None]
    n = jnp.abs(rel)
    ret = jnp.where(rel > 0, half, 0)
    nf = jnp.maximum(n, 1).astype(F32)
    large = max_exact + (jnp.log(nf / max_exact) / np.log(MAX_DISTANCE / max_exact)
                         * (half - max_exact)).astype(jnp.int32)
    large = jnp.minimum(large, half - 1)
    bucket = (ret + jnp.where(n < max_exact, n, large)).reshape(1, BIAS_COLS)
    return (bucket == jnp.arange(NUM_BUCKETS)[:, None]).astype(F32)


def _halo_specs(tm, width, S):
    r = tm // HALO
    last = S // HALO - 1
    return [pl.BlockSpec((HALO, width), lambda i: (jnp.maximum(i * r - 1, 0), 0)),
            pl.BlockSpec((tm, width), lambda i: (i, 0)),
            pl.BlockSpec((HALO, width), lambda i: (jnp.minimum(i * r + r, last), 0))]


def _with_halo(p_ref, c_ref, n_ref):
    return jnp.concatenate([p_ref[...], c_ref[...], n_ref[...]], axis=0)


def _row_valid(i, tm, S):
    g = i * tm - HALO + lax.broadcasted_iota(jnp.int32, (tm + 2 * HALO, 1), 0)
    return (g >= 0) & (g < S)


def _shifted(x, rows):
    n = x.shape[0]
    return [x if b == 0 else pltpu.roll(x, n - b, 0) for b in range(8)]


def _tap(sh, off, tm):
    a, b = off // 8, off % 8
    return sh[b][8 * a:8 * a + tm]


def _conv_fwd(cvg, cw, cb, lg, lb, tm, name):
    S = cvg.shape[0]

    def body(p_ref, c_ref, n_ref, cw_ref, cb_ref, lg_ref, lb_ref, act_ref, yc_ref):
        i = pl.program_id(0)
        z = _with_halo(p_ref, c_ref, n_ref)
        glu = jnp.where(_row_valid(i, tm, S), z[:, :CONV_DIM] * _sig(z[:, CONV_DIM:]), 0.0)
        sh = _shifted(glu, tm)
        y = jnp.zeros((tm, CONV_DIM), F32) + cb_ref[...]
        for w in range(CONV_WIDTH):
            y = y + _tap(sh, w + 1, tm) * cw_ref[w:w + 1, :]
        yc_ref[...] = y
        mu = jnp.mean(y, axis=-1, keepdims=True)
        yc = y - mu
        rstd = lax.rsqrt(jnp.mean(yc * yc, axis=-1, keepdims=True) + 1e-5)
        ln = yc * rstd * lg_ref[...] + lb_ref[...]
        act_ref[...] = _bf(ln * _sig(ln))

    vec = pl.BlockSpec((1, CONV_DIM), lambda i: (0, 0))
    row = pl.BlockSpec((tm, CONV_DIM), lambda i: (i, 0))
    return pl.pallas_call(
        body, name=name,
        out_shape=(jax.ShapeDtypeStruct((S, CONV_DIM), BF16), jax.ShapeDtypeStruct((S, CONV_DIM), F32)),
        grid=(S // tm,),
        in_specs=_halo_specs(tm, 2 * CONV_DIM, S) + [pl.BlockSpec((32, CONV_DIM), lambda i: (0, 0)), vec, vec, vec],
        out_specs=(row, row),
        compiler_params=_params(("parallel",)),
    )(cvg, cvg, cvg, cw, cb, lg, lb)


def _conv_bwd(dact, yconv, cvg, cw, lg, lb, tm, name):
    S = cvg.shape[0]
    nsteps = S // tm

    def body(dp, dc, dn, yp, yc_, yn, zp, zc, zn, cw_ref, lg_ref, lb_ref,
             dz_ref, dcw_ref, dvec_ref, dcw_s, dvec_s):
        i = pl.program_id(0)

        @pl.when(i == 0)
        def _():
            dcw_s[...] = jnp.zeros_like(dcw_s)
            dvec_s[...] = jnp.zeros_like(dvec_s)

        valid = _row_valid(i, tm, S)
        own = (lax.broadcasted_iota(jnp.int32, (tm + 2 * HALO, 1), 0) >= HALO) & (
            lax.broadcasted_iota(jnp.int32, (tm + 2 * HALO, 1), 0) < HALO + tm)
        y = _with_halo(yp, yc_, yn)
        dact_ = _with_halo(dp, dc, dn)
        mu = jnp.mean(y, axis=-1, keepdims=True)
        ycen = y - mu
        rstd = lax.rsqrt(jnp.mean(ycen * ycen, axis=-1, keepdims=True) + 1e-5)
        yhat = ycen * rstd
        ln = yhat * lg_ref[...] + lb_ref[...]
        sg = _sig(ln)
        dln = dact_ * (sg * (1.0 + ln * (1.0 - sg)))
        dyhat = dln * lg_ref[...]
        dy = rstd * (dyhat - jnp.mean(dyhat, axis=-1, keepdims=True)
                     - yhat * jnp.mean(dyhat * yhat, axis=-1, keepdims=True))
        dy = jnp.where(valid, dy, 0.0)
        dln_own = jnp.where(own, dln, 0.0)
        nr = (tm + 2 * HALO) // 8
        dvec_s[0] += jnp.sum(jnp.where(own, dy, 0.0).reshape(nr, 8, CONV_DIM), axis=0)
        dvec_s[1] += jnp.sum((dln_own * yhat).reshape(nr, 8, CONV_DIM), axis=0)
        dvec_s[2] += jnp.sum(dln_own.reshape(nr, 8, CONV_DIM), axis=0)
        z = _with_halo(zp, zc, zn)
        cval, sgate = z[:, :CONV_DIM], _sig(z[:, CONV_DIM:])
        glu = jnp.where(valid, cval * sgate, 0.0)
        sh_g = _shifted(glu, tm)
        sh_d = _shifted(dy, tm)
        dy_own = dy[HALO:HALO + tm]
        dglu = jnp.zeros((tm, CONV_DIM), F32)
        for w in range(CONV_WIDTH):
            dglu = dglu + _tap(sh_d, CONV_WIDTH - w, tm) * cw_ref[w:w + 1, :]
            dcw_s[w] += jnp.sum((dy_own * _tap(sh_g, w + 1, tm)).reshape(tm // 8, 8, CONV_DIM), axis=0)
        cv, sg_o = cval[HALO:HALO + tm], sgate[HALO:HALO + tm]
        dz_ref[:, :CONV_DIM] = dglu * sg_o
        dz_ref[:, CONV_DIM:] = dglu * cv * sg_o * (1.0 - sg_o)

        @pl.when(i == nsteps - 1)
        def _():
            dcw_ref[...] = jnp.sum(dcw_s[...], axis=1)
            dvec_ref[...] = jnp.sum(dvec_s[...], axis=1)

    vec = pl.BlockSpec((1, CONV_DIM), lambda i: (0, 0))
    return pl.pallas_call(
        body, name=name,
        out_shape=(jax.ShapeDtypeStruct((S, 2 * CONV_DIM), F32), jax.ShapeDtypeStruct((32, CONV_DIM), F32),
                   jax.ShapeDtypeStruct((8, CONV_DIM), F32)),
        grid=(nsteps,),
        in_specs=_halo_specs(tm, CONV_DIM, S) + _halo_specs(tm, CONV_DIM, S) + _halo_specs(tm, 2 * CONV_DIM, S)
        + [pl.BlockSpec((32, CONV_DIM), lambda i: (0, 0)), vec, vec],
        out_specs=(pl.BlockSpec((tm, 2 * CONV_DIM), lambda i: (i, 0)),
                   pl.BlockSpec((32, CONV_DIM), lambda i: (0, 0)), pl.BlockSpec((8, CONV_DIM), lambda i: (0, 0))),
        scratch_shapes=[pltpu.VMEM((32, 8, CONV_DIM), F32), pltpu.VMEM((8, 8, CONV_DIM), F32)],
        compiler_params=_params(("arbitrary",)),
    )(dact, dact, dact, yconv, yconv, yconv, cvg, cvg, cvg, cw, lg, lb)


def _merge_parts(un, o, cact, wg_ref, wao_ref, wco_ref):
    g = _dot(un, wg_ref[...])
    ga, gc = _sig(g[:, :D_MODEL]), _sig(g[:, D_MODEL:])
    ya = _dot(o, wao_ref[...])
    yc = _dot(cact, wco_ref[...])
    return ga, gc, ya, yc


def _merge_specs(tm):
    row = lambda w: pl.BlockSpec((tm, w), lambda i: (i, 0))
    full = lambda a, b: pl.BlockSpec((a, b), lambda i: (0, 0))
    weights = [full(D_MODEL, 2 * D_MODEL), full(Q_DIM, D_MODEL), full(CONV_DIM, D_MODEL), full(D_MODEL, D_MODEL)]
    return row, weights


def _merge_fwd(h1, un, o, cact, w_g, w_ao, w_co, w_o, tm, name):
    S = h1.shape[0]
    row, weights = _merge_specs(tm)

    def body(h1_ref, un_ref, o_ref, c_ref, wg_ref, wao_ref, wco_ref, wo_ref, h2_ref):
        ga, gc, ya, yc = _merge_parts(un_ref[...], o_ref[...], c_ref[...], wg_ref, wao_ref, wco_ref)
        h2_ref[...] = h1_ref[...] + _dot(_bf(ga * ya + gc * yc), wo_ref[...])

    return pl.pallas_call(
        body, name=name, out_shape=jax.ShapeDtypeStruct((S, D_MODEL), F32),
        grid=(S // tm,),
        in_specs=[row(D_MODEL), row(D_MODEL), row(Q_DIM), row(CONV_DIM)] + weights,
        out_specs=row(D_MODEL),
        compiler_params=_params(("parallel",)),
    )(h1, un, o, cact, w_g, w_ao, w_co, w_o)


def _merge_bwd(dh2, un, o, cact, w_g, w_ao, w_co, w_o, tm, name):
    S = dh2.shape[0]
    row, weights = _merge_specs(tm)

    def body(dh2_ref, un_ref, o_ref, c_ref, wg_ref, wao_ref, wco_ref, wo_ref,
             do_ref, dc_ref, mix_ref, dya_ref, dyc_ref, dgp_ref):
        ga, gc, ya, yc = _merge_parts(un_ref[...], o_ref[...], c_ref[...], wg_ref, wao_ref, wco_ref)
        mix_ref[...] = _bf(ga * ya + gc * yc)
        dmix = _dot_nt(_bf(dh2_ref[...]), wo_ref[...])
        dya = _bf(dmix * ga)
        dyc = _bf(dmix * gc)
        dya_ref[...] = dya
        dyc_ref[...] = dyc
        dgp_ref[:, :D_MODEL] = _bf(dmix * ya * ga * (1.0 - ga))
        dgp_ref[:, D_MODEL:] = _bf(dmix * yc * gc * (1.0 - gc))
        do_ref[...] = _dot_nt(dya, wao_ref[...])
        dc_ref[...] = _dot_nt(dyc, wco_ref[...])

    return pl.pallas_call(
        body, name=name,
        out_shape=(jax.ShapeDtypeStruct((S, Q_DIM), F32), jax.ShapeDtypeStruct((S, CONV_DIM), F32),
                   jax.ShapeDtypeStruct((S, D_MODEL), BF16), jax.ShapeDtypeStruct((S, D_MODEL), BF16),
                   jax.ShapeDtypeStruct((S, D_MODEL), BF16), jax.ShapeDtypeStruct((S, 2 * D_MODEL), BF16)),
        grid=(S // tm,),
        in_specs=[row(D_MODEL), row(D_MODEL), row(Q_DIM), row(CONV_DIM)] + weights,
        out_specs=(row(Q_DIM), row(CONV_DIM), row(D_MODEL), row(D_MODEL), row(D_MODEL), row(2 * D_MODEL)),
        compiler_params=_params(("parallel",)),
    )(dh2, un, o, cact, w_g, w_ao, w_co, w_o)


def _pe_specs(tm, layer):
    row = pl.BlockSpec((tm, D_MODEL), lambda i: (i, 0))
    vec = pl.BlockSpec((1, D_MODEL), lambda i: (0, 0))
    p_s = pl.BlockSpec((None, None, tm, 256), lambda i: (layer, 0, i, 0))
    wpp = pl.BlockSpec((256, D_MODEL), lambda i: (0, 0))
    wpg = pl.BlockSpec((D_MODEL, D_MODEL), lambda i: (0, 0))
    return row, vec, p_s, wpp, wpg


def _pe_fwd(h, gamma, p, layer, w_pp, w_pg, tm, name):
    S = h.shape[0]
    row, vec, p_s, wpp, wpg = _pe_specs(tm, layer)

    def body(h_ref, g_ref, p_ref, wpp_ref, wpg_ref, x_ref, hn_ref):
        hn = _bf(_rms_fwd(h_ref[...], g_ref[...])[0])
        hn_ref[...] = hn
        gate = _sig(_dot(hn, wpg_ref[...]))
        x_ref[...] = h_ref[...] + _dot(_bf(p_ref[...]), wpp_ref[...]) * gate

    return pl.pallas_call(
        body, name=name,
        out_shape=(jax.ShapeDtypeStruct((S, D_MODEL), F32), jax.ShapeDtypeStruct((S, D_MODEL), BF16)),
        grid=(S // tm,), in_specs=[row, vec, p_s, wpp, wpg], out_specs=(row, row),
        compiler_params=_params(("parallel",)),
    )(h, gamma, p, w_pp, w_pg)


def _pe_bwd(dx, h, gamma, hn, p, layer, w_pp, w_pg, tm, name):
    S = h.shape[0]
    row, vec, p_s, wpp, wpg = _pe_specs(tm, layer)

    def body(dx_ref, h_ref, g_ref, hn_ref, p_ref, wpp_ref, wpg_ref, dh_ref, dgp_ref, dpr_ref, dgam_ref):
        @pl.when(pl.program_id(0) == 0)
        def _():
            dgam_ref[...] = jnp.zeros_like(dgam_ref)

        dxv = dx_ref[...]
        gate = _sig(_dot(hn_ref[...], wpg_ref[...]))
        proj = _dot(_bf(p_ref[...]), wpp_ref[...])
        dpr_ref[...] = _bf(dxv * gate)
        dgp = _bf(dxv * proj * gate * (1.0 - gate))
        dgp_ref[...] = dgp
        dxn, dgam = _rms_bwd(_dot_nt(dgp, wpg_ref[...]), h_ref[...], g_ref[...])
        dh_ref[...] = dxv + dxn
        dgam_ref[...] += jnp.sum(dgam, axis=0, keepdims=True)

    return pl.pallas_call(
        body, name=name,
        out_shape=(jax.ShapeDtypeStruct((S, D_MODEL), F32), jax.ShapeDtypeStruct((S, D_MODEL), BF16),
                   jax.ShapeDtypeStruct((S, D_MODEL), BF16), jax.ShapeDtypeStruct((1, D_MODEL), F32)),
        grid=(S // tm,), in_specs=[row, row, vec, row, p_s, wpp, wpg], out_specs=(row, row, row, vec),
        compiler_params=_params(("arbitrary",)),
    )(dx, h, gamma, hn, p, w_pp, w_pg)


def _loss_head(y, target, tm):
    S = y.shape[0]

    def body(y_ref, t_ref, dy_ref, l_ref):
        @pl.when(pl.program_id(0) == 0)
        def _():
            l_ref[...] = jnp.zeros_like(l_ref)

        diff = y_ref[...] - t_ref[...]
        dy_ref[...] = diff * (1.0 / D_MODEL)
        sq = jnp.sum((diff * diff).reshape(tm // 8, 8, D_MODEL), axis=0)
        part = sq[:, 0:128]
        for k in range(1, D_MODEL // 128):
            part = part + sq[:, 128 * k:128 * (k + 1)]
        l_ref[...] += part

    row = pl.BlockSpec((tm, D_MODEL), lambda i: (i, 0))
    return pl.pallas_call(
        body, name="loss_head",
        out_shape=(jax.ShapeDtypeStruct((S, D_MODEL), F32), jax.ShapeDtypeStruct((8, 128), F32)),
        grid=(S // tm,), in_specs=[row, row], out_specs=(row, pl.BlockSpec((8, 128), lambda i: (0, 0))),
        compiler_params=_params(("arbitrary",)),
    )(y, target)


def _adamw(parts, w, m, v, name):
    R, C = w.shape
    tr = R
    for cand in (256, 128, 64, 32, 16):
        if R % cand == 0:
            tr = cand
            break

    def body(p_ref, w_ref, m_ref, v_ref, g_ref, d_ref, nm_ref, nv_ref):
        g = p_ref[0].astype(F32)
        for k in range(1, N_DEV):
            g = g + p_ref[k].astype(F32)
        g_ref[...] = g
        nm = ADAM_B1 * m_ref[...] + (1.0 - ADAM_B1) * g
        nv = ADAM_B2 * v_ref[...] + (1.0 - ADAM_B2) * (g * g)
        nm_ref[...] = nm
        nv_ref[...] = nv
        m_hat = nm / (1.0 - ADAM_B1 ** ADAM_STEP)
        v_hat = nv / (1.0 - ADAM_B2 ** ADAM_STEP)
        d_ref[...] = -ADAM_LR * (m_hat / (jnp.sqrt(v_hat) + ADAM_EPS) + ADAM_WD * w_ref[...])

    blk = pl.BlockSpec((tr, C), lambda i: (i, 0))
    out = jax.ShapeDtypeStruct((R, C), F32)
    return pl.pallas_call(
        body, name=name, out_shape=(out, out, out, out), grid=(R // tr,),
        in_specs=[pl.BlockSpec((N_DEV, tr, C), lambda i: (0, i, 0)), blk, blk, blk],
        out_specs=(blk, blk, blk, blk),
        compiler_params=_params(("parallel",)),
    )(parts, w, m, v)


SHARDED = ("w_ffn1_in", "w_ffn1_out", "w_in", "conv_w", "w_attn_out", "w_conv_out", "w_o",
           "w_ffn2_in", "w_ffn2_out", "w_pe_gate", "w_pe_proj")
COL_SHARDED = ("w_ffn1_in", "w_in", "conv_w", "w_attn_out", "w_conv_out", "w_ffn2_in", "w_pe_proj")
SMALL = ("rel_bias", "norm_ffn1", "norm_mix", "q_norm", "k_norm", "sink", "conv_b", "conv_ln_g", "conv_ln_b",
         "norm_ffn2", "norm_pe")
WEIGHTS = ("rel_bias", "norm_ffn1", "w_ffn1_in", "w_ffn1_out", "norm_mix", "w_in", "q_norm", "k_norm", "sink",
           "conv_w", "conv_b", "conv_ln_g", "conv_ln_b", "w_attn_out", "w_conv_out", "w_o", "norm_ffn2",
           "w_ffn2_in", "w_ffn2_out", "norm_pe", "w_pe_gate", "w_pe_proj")


def _natural(g):
    k, n = g.shape[1], g.shape[2]
    return jnp.transpose(g, (1, 0, 2)).reshape(k, N_DEV * n)


def _blocked(w):
    k, n = w.shape[0], w.shape[1] // N_DEV
    return jnp.transpose(w.reshape(k, N_DEV, n), (1, 0, 2))


def kernel(x, p, rel_bias, norm_ffn1, w_ffn1_in, w_ffn1_out, norm_mix, w_in, q_norm, k_norm, sink, conv_w, conv_b, conv_ln_g, conv_ln_b, w_attn_out, w_conv_out, w_o, norm_ffn2, w_ffn2_in, w_ffn2_out, norm_pe, w_pe_gate, w_pe_proj, loss_target, m_rel_bias, m_norm_ffn1, m_w_ffn1_in, m_w_ffn1_out, m_norm_mix, m_w_in, m_q_norm, m_k_norm, m_sink, m_conv_w, m_conv_b, m_conv_ln_g, m_conv_ln_b, m_w_attn_out, m_w_conv_out, m_w_o, m_norm_ffn2, m_w_ffn2_in, m_w_ffn2_out, m_norm_pe, m_w_pe_gate, m_w_pe_proj, v_rel_bias, v_norm_ffn1, v_w_ffn1_in, v_w_ffn1_out, v_norm_mix, v_w_in, v_q_norm, v_k_norm, v_sink, v_conv_w, v_conv_b, v_conv_ln_g, v_conv_ln_b, v_w_attn_out, v_w_conv_out, v_w_o, v_norm_ffn2, v_w_ffn2_in, v_w_ffn2_out, v_norm_pe, v_w_pe_gate, v_w_pe_proj):
    W = dict(rel_bias=rel_bias, norm_ffn1=norm_ffn1, w_ffn1_in=w_ffn1_in, w_ffn1_out=w_ffn1_out, norm_mix=norm_mix,
             w_in=w_in, q_norm=q_norm, k_norm=k_norm, sink=sink, conv_w=conv_w, conv_b=conv_b, conv_ln_g=conv_ln_g,
             conv_ln_b=conv_ln_b, w_attn_out=w_attn_out, w_conv_out=w_conv_out, w_o=w_o, norm_ffn2=norm_ffn2,
             w_ffn2_in=w_ffn2_in, w_ffn2_out=w_ffn2_out, norm_pe=norm_pe, w_pe_gate=w_pe_gate, w_pe_proj=w_pe_proj)
    M = dict(rel_bias=m_rel_bias, norm_ffn1=m_norm_ffn1, w_ffn1_in=m_w_ffn1_in, w_ffn1_out=m_w_ffn1_out,
             norm_mix=m_norm_mix, w_in=m_w_in, q_norm=m_q_norm, k_norm=m_k_norm, sink=m_sink, conv_w=m_conv_w,
             conv_b=m_conv_b, conv_ln_g=m_conv_ln_g, conv_ln_b=m_conv_ln_b, w_attn_out=m_w_attn_out,
             w_conv_out=m_w_conv_out, w_o=m_w_o, norm_ffn2=m_norm_ffn2, w_ffn2_in=m_w_ffn2_in,
             w_ffn2_out=m_w_ffn2_out, norm_pe=m_norm_pe, w_pe_gate=m_w_pe_gate, w_pe_proj=m_w_pe_proj)
    V = dict(rel_bias=v_rel_bias, norm_ffn1=v_norm_ffn1, w_ffn1_in=v_w_ffn1_in, w_ffn1_out=v_w_ffn1_out,
             norm_mix=v_norm_mix, w_in=v_w_in, q_norm=v_q_norm, k_norm=v_k_norm, sink=v_sink, conv_w=v_conv_w,
             conv_b=v_conv_b, conv_ln_g=v_conv_ln_g, conv_ln_b=v_conv_ln_b, w_attn_out=v_w_attn_out,
             w_conv_out=v_w_conv_out, w_o=v_w_o, norm_ffn2=v_norm_ffn2, w_ffn2_in=v_w_ffn2_in,
             w_ffn2_out=v_w_ffn2_out, norm_pe=v_norm_pe, w_pe_gate=v_w_pe_gate, w_pe_proj=v_w_pe_proj)

    L = w_in.shape[0]
    S = x.shape[1]
    tm = min(512, S)
    xs = x[0]
    target = loss_target[0]
    vec = lambda a: a.reshape(1, -1)

    set_ffn1 = ("w_ffn1_in", "w_ffn1_out")
    set_mid = ("w_in", "conv_w", "w_attn_out", "w_conv_out", "w_o")
    set_late = ("w_ffn2_in", "w_ffn2_out", "w_pe_gate", "w_pe_proj")

    def shards(names, l):
        return [W[n][l] if n == "conv_w" else W[n][l].astype(BF16) for n in names]

    onehot = _bucket_onehot()
    bias = _bias_table(rel_bias.T, onehot).reshape(N_HEADS, BLOCK, WIN)

    layers, saved = [], []
    h = xs
    ffn1_w = _exchange(shards(set_ffn1, 0), True, "allgather_first")
    for l in range(L):
        sv = dict(x0=h)
        G = dict(wi1=ffn1_w[0], wo1=ffn1_w[1])
        (h1, sv["xn1"], sv["g1"], sv["u1"]), got = _ffn_fwd(
            h, vec(norm_ffn1[l]), G["wi1"], G["wo1"], tm, "ffn1_fwd", _Exchange(shards(set_mid, l), True))
        mid = dict(zip(set_mid, got))
        w_in_n = _natural(mid["w_in"])
        G.update(w_qc=w_in_n[:, :QC_DIM], w_g=w_in_n[:, QC_DIM:],
                 conv_w=jnp.pad(_natural(mid["conv_w"]), ((0, 1), (0, 0))),
                 w_ao=_natural(mid["w_attn_out"]), w_co=_natural(mid["w_conv_out"]),
                 w_o=mid["w_o"].reshape(D_MODEL, D_MODEL))
        sv["h1"] = h1
        sv["un"], sv["qkv"], sv["cvg"] = _mixin_fwd(h1, vec(norm_mix[l]), G["w_qc"], tm, "mixin_fwd")
        sv["qg"] = vec(jnp.tile(q_norm[l], N_HEADS))
        sv["kg"] = vec(jnp.tile(k_norm[l], KV_HEADS))
        sv["sinkb"] = jnp.broadcast_to(sink[l][:, None], (N_HEADS, 128))
        (sv["o"],), got = _attn_fwd(sv["qkv"], sv["qg"], sv["kg"], sv["sinkb"], bias, "attn_fwd",
                                    _Exchange(shards(set_late, l), True))
        late = dict(zip(set_late, got))
        G.update(wi2=late["w_ffn2_in"], wo2=late["w_ffn2_out"],
                 w_pg=late["w_pe_gate"].reshape(D_MODEL, D_MODEL), w_pp=_natural(late["w_pe_proj"]))
        sv["cact"], sv["yconv"] = _conv_fwd(sv["cvg"], G["conv_w"], vec(conv_b[l]), vec(conv_ln_g[l]),
                                            vec(conv_ln_b[l]), tm, "conv_fwd")
        h2 = _merge_fwd(h1, sv["un"], sv["o"], sv["cact"], G["w_g"], G["w_ao"], G["w_co"], G["w_o"], tm, "merge_fwd")
        sv["h2"] = h2
        nxt = _Exchange(shards(set_ffn1, l + 1), True) if l + 1 < L else NO_EXCHANGE
        (h3, sv["xn2"], sv["g2"], sv["u2"]), ffn1_w = _ffn_fwd(
            h2, vec(norm_ffn2[l]), G["wi2"], G["wo2"], tm, "ffn2_fwd", nxt)
        sv["h3"] = h3
        h, sv["hn"] = _pe_fwd(h3, vec(norm_pe[l]), p, l, G["w_pp"], G["w_pg"], tm, "pe_fwd")
        layers.append(G)
        saved.append(sv)

    dh, lparts = _loss_head(h, target, tm)
    loss = lax.psum((0.5 / D_MODEL) * jnp.sum(lparts), AXES)

    dbias = jnp.zeros((N_HEADS, BLOCK, WIN), F32)
    small_g = {n: [None] * L for n in SMALL if n != "rel_bias"}
    recv = {n: [None] * L for n in SHARDED}

    def keep(names, l, got):
        for n, r in zip(names, got):
            recv[n][l] = r

    pending = None
    for l in reversed(range(L)):
        G, sv = layers[l], saved[l]
        dh3, dgp_pe, dproj, dg_pe = _pe_bwd(dh, sv["h3"], vec(norm_pe[l]), sv["hn"], p, l, G["w_pp"], G["w_pg"],
                                            tm, "pe_bwd")
        gw_pg = _matmul_tn(sv["hn"][None], dgp_pe[None], 1, "dw_pe_gate")
        gw_pp = _matmul_tn(p[l], dproj[None], 1, "dw_pe_proj")
        (dh2, a2, dgu2, dg_n2), got = _ffn_bwd(
            dh3, sv["h2"], vec(norm_ffn2[l]), sv["g2"], sv["u2"], G["wi2"], G["wo2"], tm, "ffn2_bwd",
            _Exchange(pending, False) if pending else NO_EXCHANGE)
        if pending:
            keep(set_ffn1, l + 1, got)
        gwo2 = _matmul_tn(a2, dh3[None], FF_BLOCKS, "dw_ffn2_out", scale=0.5)
        gwi2 = _matmul_tn(sv["xn2"][None], dgu2.reshape(2 * FF_BLOCKS, S, FF_SHARD), 2 * FF_BLOCKS, "dw_ffn2_in")
        do, dcact, mix, dya, dyc, dgpre = _merge_bwd(dh2, sv["un"], sv["o"], sv["cact"], G["w_g"], G["w_ao"],
                                                     G["w_co"], G["w_o"], tm, "merge_bwd")
        gw_o = _matmul_tn(mix[None], dh2[None], 1, "dw_o")
        gw_ao = _matmul_tn(sv["o"][None], dya[None], 1, "dw_attn_out")
        gw_co = _matmul_tn(sv["cact"][None], dyc[None], 1, "dw_conv_out")
        late_send = [gwi2, gwo2.reshape(N_DEV, FF_SHARD // 2, D_MODEL),
                     gw_pg.reshape(N_DEV, D_MODEL // N_DEV, D_MODEL), _blocked(gw_pp[0])]
        (dq, dkp, dvp, dbias, dsink, dqg), got = _attn_bwd(
            do, sv["qkv"], sv["qg"], sv["kg"], sv["sinkb"], bias, dbias, "attn_bwd", _Exchange(late_send, False))
        keep(set_late, l, got)
        dkv, dkg = _kv_fold(dkp, dvp, sv["qkv"], sv["kg"], "kv_fold")
        dcvg, dcw, dcvec = _conv_bwd(dcact, sv["yconv"], sv["cvg"], G["conv_w"], vec(conv_ln_g[l]),
                                     vec(conv_ln_b[l]), tm, "conv_bwd")
        dh1, dg_mix = _mixin_bwd(dh2, sv["h1"], vec(norm_mix[l]), dq, dkv, dcvg, dgpre, G["w_qc"], G["w_g"],
                                 tm, "mixin_bwd")
        un3 = sv["un"][None]
        gw_in = jnp.concatenate([
            _matmul_tn(un3, dq[None], 1, "dw_in_q")[0], _matmul_tn(un3, dkv[None], 1, "dw_in_kv")[0],
            _matmul_tn(un3, dcvg[None], 1, "dw_in_c")[0], _matmul_tn(un3, dgpre[None], 1, "dw_in_g")[0]], axis=1)
        mid_send = [_blocked(gw_in), _blocked(dcw[:CONV_WIDTH]), _blocked(gw_ao[0]), _blocked(gw_co[0]),
                    gw_o.reshape(N_DEV, D_MODEL // N_DEV, D_MODEL)]
        (dh, a1, dgu1, dg_n1), got = _ffn_bwd(
            dh1, sv["x0"], vec(norm_ffn1[l]), sv["g1"], sv["u1"], G["wi1"], G["wo1"], tm, "ffn1_bwd",
            _Exchange(mid_send, False))
        keep(set_mid, l, got)
        gwo1 = _matmul_tn(a1, dh1[None], FF_BLOCKS, "dw_ffn1_out", scale=0.5)
        gwi1 = _matmul_tn(sv["xn1"][None], dgu1.reshape(2 * FF_BLOCKS, S, FF_SHARD), 2 * FF_BLOCKS, "dw_ffn1_in")
        pending = [gwi1, gwo1.reshape(N_DEV, FF_SHARD // 2, D_MODEL)]
        small_g["norm_ffn1"][l] = dg_n1[0]
        small_g["norm_mix"][l] = dg_mix[0]
        small_g["q_norm"][l] = dqg[0, :HEAD_DIM]
        small_g["k_norm"][l] = dkg[0, :HEAD_DIM]
        small_g["sink"][l] = dsink[:, 0]
        small_g["conv_b"][l] = dcvec[0]
        small_g["conv_ln_g"][l] = dcvec[1]
        small_g["conv_ln_b"][l] = dcvec[2]
        small_g["norm_ffn2"][l] = dg_n2[0]
        small_g["norm_pe"][l] = dg_pe[0]

    keep(set_ffn1, 0, _exchange(pending, False, "grad_exchange_last"))
    grad_x = dh[None]
    drb = _bias_grad(dbias.reshape(N_HEADS, BIAS_COLS), onehot).T

    res = {}
    for n in SHARDED:
        parts = jnp.stack(recv[n], axis=1)
        shp = W[n].shape
        rows, cols = shp[0] * shp[1], shp[2]
        res[n] = [o.reshape(shp) for o in _adamw(
            parts.reshape(N_DEV, rows, cols), W[n].reshape(rows, cols), M[n].reshape(rows, cols),
            V[n].reshape(rows, cols), "adamw_" + n)]

    flat_g = jnp.concatenate([drb.reshape(-1)] + [jnp.stack(small_g[n]).reshape(-1) for n in SMALL[1:]])
    n_small = flat_g.shape[0]
    rows_s = -(-n_small // 1024 // 8) * 8
    pad = lambda a: jnp.pad(a, (0, rows_s * 1024 - n_small)).reshape(rows_s, 1024)
    flat = lambda d: pad(jnp.concatenate([d[n].reshape(-1) for n in SMALL]))
    (parts_s,) = _exchange([pad(flat_g)], True, "small_allgather")
    outs_s = _adamw(parts_s, flat(W), flat(M), flat(V), "adamw_small")
    off = 0
    for n in SMALL:
        size = W[n].size
        res[n] = [o.reshape(-1)[off:off + size].reshape(W[n].shape) for o in outs_s]
        off += size

    out = [loss, grad_x]
    for k in range(4):
        out += [res[n][k] for n in WEIGHTS]
    return tuple(out)
```

```python
import functools

import jax
import jax.numpy as jnp
import numpy as np
from jax import lax
from jax.experimental import pallas as pl
from jax.experimental.pallas import tpu as pltpu

F32 = jnp.float32
BF16 = jnp.bfloat16
MESH_ID = pl.DeviceIdType.MESH
AXES = ("x", "y", "c")
N_DEV = 8

D_MODEL = 1024
N_HEADS = 8
KV_HEADS = 2
HEAD_DIM = 64
Q_DIM = 512
KV_DIM = 128
BLOCK = 128
WIN = 3 * BLOCK
NUM_BUCKETS = 32
MAX_DISTANCE = 128
CONV_DIM = 512
CONV_WIDTH = 31
D_FF = 2816
FF_SHARD = 2 * D_FF // N_DEV
FF_BLOCKS = D_FF // FF_SHARD
QC_DIM = Q_DIM + 2 * KV_DIM + 2 * CONV_DIM
NEG_INF = -1e9
HALO = 16
CROWS = 64

ADAM_LR = 0.001
ADAM_B1 = 0.9
ADAM_B2 = 0.999
ADAM_EPS = 1e-08
ADAM_WD = 0.01
ADAM_STEP = 10

VMEM_LIMIT = 56 * 1024 * 1024
HI = lax.Precision.HIGHEST


def _params(sem):
    return pltpu.CompilerParams(dimension_semantics=sem, vmem_limit_bytes=VMEM_LIMIT)


def _dot(a, b, precision=None):
    return jnp.dot(a, b, preferred_element_type=F32, precision=precision)


def _dot_nt(a, b):
    return lax.dot_general(a, b, (((1,), (1,)), ((), ())), preferred_element_type=F32)


def _dot_tn(a, b):
    return lax.dot_general(a, b, (((0,), (0,)), ((), ())), preferred_element_type=F32)


def _sig(x):
    return 1.0 / (1.0 + jnp.exp(-x))


def _bf(x):
    return x.astype(BF16)


def _rms_fwd(x, gamma):
    r = lax.rsqrt(jnp.mean(x * x, axis=-1, keepdims=True) + 1e-6)
    return x * r * gamma, r


def _rms_bwd(dy, x, gamma):
    r = lax.rsqrt(jnp.mean(x * x, axis=-1, keepdims=True) + 1e-6)
    xhat = x * r
    dxhat = dy * gamma
    dx = r * (dxhat - xhat * jnp.mean(dxhat * xhat, axis=-1, keepdims=True))
    return dx, dy * xhat


class _Exchange:
    def __init__(self, arrs, gather):
        self.arrs, self.gather, self.n = list(arrs), gather, len(arrs)
        n = self.n
        self.out_shape = tuple(
            jax.ShapeDtypeStruct(((N_DEV,) + a.shape) if gather else a.shape, a.dtype) for a in self.arrs)
        self.specs = [pl.BlockSpec(memory_space=pl.ANY)] * n
        self.scratch = [pltpu.SemaphoreType.DMA((7 * n,)), pltpu.SemaphoreType.DMA((7 * n,)),
                        pltpu.SemaphoreType.DMA((n,))] if n else []

    def _copies(self, ins, outs, sems):
        n, gather = self.n, self.gather
        send_sems, recv_sems, local_sems = sems
        x, y, c = lax.axis_index("x"), lax.axis_index("y"), lax.axis_index("c")
        me = 4 * x + 2 * y + c
        copies = [pltpu.make_async_copy(ins[t] if gather else ins[t].at[me], outs[t].at[me], local_sems.at[t])
                  for t in range(n)]
        for d in range(1, N_DEV):
            px = 1 - x if d & 4 else x
            py = 1 - y if d & 2 else y
            pc = 1 - c if d & 1 else c
            peer = 4 * px + 2 * py + pc
            for t in range(n):
                k = (d - 1) * n + t
                copies.append(pltpu.make_async_remote_copy(
                    src_ref=ins[t] if gather else ins[t].at[peer], dst_ref=outs[t].at[me],
                    send_sem=send_sems.at[k], recv_sem=recv_sems.at[k],
                    device_id=(px, py, pc), device_id_type=MESH_ID))
        return copies

    def start(self, ins, outs, sems):
        for cp in self._copies(ins, outs, sems):
            cp.start()

    def wait(self, ins, outs, sems):
        for cp in self._copies(ins, outs, sems):
            cp.wait()


NO_EXCHANGE = _Exchange([], True)


def _exchange(arrs, gather, name):
    ex = _Exchange(arrs, gather)
    n = ex.n

    def body(*refs):
        ins, outs, sems = refs[:n], refs[n:2 * n], refs[2 * n:]
        ex.start(ins, outs, sems)
        ex.wait(ins, outs, sems)

    return pl.pallas_call(
        body, name=name, out_shape=ex.out_shape, in_specs=ex.specs, out_specs=tuple(ex.specs),
        scratch_shapes=ex.scratch,
    )(*ex.arrs)


def _carrier_call(body, ex, first_last, name, out_shape, grid, in_specs, out_specs, scratch_shapes, args):
    n_in, n_out, n_scr, n = len(in_specs), len(out_shape), len(scratch_shapes), ex.n

    def full(*refs):
        a, ci = refs[:n_in], refs[n_in:n_in + n]
        o = refs[n_in + n:n_in + n + n_out]
        co = refs[n_in + n + n_out:n_in + 2 * n + n_out]
        scr = refs[n_in + 2 * n + n_out:n_in + 2 * n + n_out + n_scr]
        sems = refs[n_in + 2 * n + n_out + n_scr:]
        first, last = first_last()
        if n:
            @pl.when(first)
            def _():
                ex.start(ci, co, sems)

        body(*a, *o, *scr)
        if n:
            @pl.when(last)
            def _():
                ex.wait(ci, co, sems)

    outs = pl.pallas_call(
        full, name=name, out_shape=tuple(out_shape) + ex.out_shape, grid=grid,
        in_specs=list(in_specs) + ex.specs, out_specs=tuple(out_specs) + tuple(ex.specs),
        scratch_shapes=list(scratch_shapes) + ex.scratch,
        compiler_params=_params(("arbitrary",) * len(grid)),
    )(*args, *ex.arrs)
    return outs[:n_out], outs[n_out:]


def _matmul_tn(a, b, nb, name, scale=1.0, out_dtype=BF16, ts=1024):
    ba, S, K = a.shape
    bb, _, N = b.shape
    ts = min(ts, S)
    tn = N if N <= 1024 else 1024
    assert N % tn == 0 and S % ts == 0
    ns = S // ts

    def body(a_ref, b_ref, o_ref, acc):
        s = pl.program_id(2)

        @pl.when(s == 0)
        def _():
            acc[...] = jnp.zeros_like(acc)

        acc[...] += _dot_tn(_bf(a_ref[...]), _bf(b_ref[...]))

        @pl.when(s == ns - 1)
        def _():
            o_ref[...] = (acc[...] * scale).astype(out_dtype)

    return pl.pallas_call(
        body, name=name, out_shape=jax.ShapeDtypeStruct((nb, K, N), out_dtype),
        grid=(nb, N // tn, ns),
        in_specs=[pl.BlockSpec((None, ts, K), (lambda i, j, s: (i, s, 0)) if ba > 1 else (lambda i, j, s: (0, s, 0))),
                  pl.BlockSpec((None, ts, tn), (lambda i, j, s: (i, s, j)) if bb > 1 else (lambda i, j, s: (0, s, j)))],
        out_specs=pl.BlockSpec((None, K, tn), lambda i, j, s: (i, 0, j)),
        scratch_shapes=[pltpu.VMEM((K, tn), F32)],
        compiler_params=_params(("parallel", "parallel", "arbitrary")),
    )(a, b)


def _ffn_specs(tm):
    wg = pl.BlockSpec((None, D_MODEL, FF_SHARD), lambda i, j: (j, 0, 0))
    wu = pl.BlockSpec((None, D_MODEL, FF_SHARD), lambda i, j: (j + FF_BLOCKS, 0, 0))
    wo = pl.BlockSpec((2, FF_SHARD // 2, D_MODEL), lambda i, j: (j, 0, 0))
    row = pl.BlockSpec((tm, D_MODEL), lambda i, j: (i, 0))
    vec = pl.BlockSpec((1, D_MODEL), lambda i, j: (0, 0))
    hid = pl.BlockSpec((None, tm, FF_SHARD), lambda i, j: (j, i, 0))
    return wg, wu, wo, row, vec, hid


def _grid2_ends(ni, nj):
    def first_last():
        i, j = pl.program_id(0), pl.program_id(1)
        return (i == 0) & (j == 0), (i == ni - 1) & (j == nj - 1)
    return first_last


def _grid1_ends(ni):
    def first_last():
        i = pl.program_id(0)
        return i == 0, i == ni - 1
    return first_last


def _ffn_fwd(x, gamma, wi, wo, tm, name, ex=NO_EXCHANGE):
    S = x.shape[0]
    wg_s, wu_s, wo_s, row, vec, hid = _ffn_specs(tm)

    def body(x_ref, g_ref, wg_ref, wu_ref, wo_ref, y_ref, xn_ref, gs_ref, us_ref, xn_s, acc):
        j = pl.program_id(1)

        @pl.when(j == 0)
        def _():
            xn = _bf(_rms_fwd(x_ref[...], g_ref[...])[0])
            xn_s[...] = xn
            xn_ref[...] = xn
            acc[...] = jnp.zeros_like(acc)

        xn = xn_s[...]
        g = _dot(xn, wg_ref[...])
        u = _dot(xn, wu_ref[...])
        gs_ref[...] = _bf(g)
        us_ref[...] = _bf(u)
        a = g * _sig(g) * u
        acc[...] += _dot(_bf(a), wo_ref[...].reshape(FF_SHARD, D_MODEL))

        @pl.when(j == FF_BLOCKS - 1)
        def _():
            y_ref[...] = x_ref[...] + 0.5 * acc[...]

    return _carrier_call(
        body, ex, _grid2_ends(S // tm, FF_BLOCKS), name,
        out_shape=(jax.ShapeDtypeStruct((S, D_MODEL), F32), jax.ShapeDtypeStruct((S, D_MODEL), BF16),
                   jax.ShapeDtypeStruct((FF_BLOCKS, S, FF_SHARD), BF16),
                   jax.ShapeDtypeStruct((FF_BLOCKS, S, FF_SHARD), BF16)),
        grid=(S // tm, FF_BLOCKS),
        in_specs=[row, vec, wg_s, wu_s, wo_s],
        out_specs=(row, row, hid, hid),
        scratch_shapes=[pltpu.VMEM((tm, D_MODEL), BF16), pltpu.VMEM((tm, D_MODEL), F32)],
        args=(x, gamma, wi, wi, wo))


def _ffn_bwd(dy, x, gamma, gs, us, wi, wo, tm, name, ex=NO_EXCHANGE):
    S = x.shape[0]
    wg_s, wu_s, wo_s, row, vec, hid = _ffn_specs(tm)
    dgu_s = pl.BlockSpec((2, None, tm, FF_SHARD), lambda i, j: (0, j, i, 0))

    def body(dy_ref, x_ref, g_ref, gs_ref, us_ref, wg_ref, wu_ref, wo_ref,
             dx_ref, a_ref, dgu_ref, dgam_ref, dyh_s, acc):
        i, j = pl.program_id(0), pl.program_id(1)

        @pl.when(j == 0)
        def _():
            dyh_s[...] = _bf(0.5 * dy_ref[...])
            acc[...] = jnp.zeros_like(acc)

        @pl.when((i == 0) & (j == 0))
        def _():
            dgam_ref[...] = jnp.zeros_like(dgam_ref)

        da = _dot_nt(dyh_s[...], wo_ref[...].reshape(FF_SHARD, D_MODEL))
        g = gs_ref[...].astype(F32)
        u = us_ref[...].astype(F32)
        sg = _sig(g)
        sl = g * sg
        a_ref[...] = _bf(sl * u)
        dg = _bf(da * u * (sg * (1.0 + g * (1.0 - sg))))
        du = _bf(da * sl)
        dgu_ref[0] = dg
        dgu_ref[1] = du
        acc[...] += _dot_nt(dg, wg_ref[...]) + _dot_nt(du, wu_ref[...])

        @pl.when(j == FF_BLOCKS - 1)
        def _():
            dx, dgam = _rms_bwd(acc[...], x_ref[...], g_ref[...])
            dx_ref[...] = dy_ref[...] + dx
            dgam_ref[...] += jnp.sum(dgam, axis=0, keepdims=True)

    return _carrier_call(
        body, ex, _grid2_ends(S // tm, FF_BLOCKS), name,
        out_shape=(jax.ShapeDtypeStruct((S, D_MODEL), F32),
                   jax.ShapeDtypeStruct((FF_BLOCKS, S, FF_SHARD), BF16),
                   jax.ShapeDtypeStruct((2, FF_BLOCKS, S, FF_SHARD), BF16),
                   jax.ShapeDtypeStruct((1, D_MODEL), F32)),
        grid=(S // tm, FF_BLOCKS),
        in_specs=[row, row, vec, hid, hid, wg_s, wu_s, wo_s],
        out_specs=(row, hid, dgu_s, vec),
        scratch_shapes=[pltpu.VMEM((tm, D_MODEL), BF16), pltpu.VMEM((tm, D_MODEL), F32)],
        args=(dy, x, gamma, gs, us, wi, wi, wo))


def _mixin_fwd(h, gamma, w_qc, tm, name):
    S = h.shape[0]
    nqkv = Q_DIM + 2 * KV_DIM

    def body(h_ref, g_ref, w_ref, un_ref, qkv_ref, cvg_ref):
        un = _bf(_rms_fwd(h_ref[...], g_ref[...])[0])
        un_ref[...] = un
        z = _dot(un, w_ref[...])
        qkv_ref[...] = z[:, :nqkv]
        cvg_ref[...] = z[:, nqkv:]

    row = lambda w: pl.BlockSpec((tm, w), lambda i: (i, 0))
    return pl.pallas_call(
        body, name=name,
        out_shape=(jax.ShapeDtypeStruct((S, D_MODEL), BF16), jax.ShapeDtypeStruct((S, nqkv), F32),
                   jax.ShapeDtypeStruct((S, 2 * CONV_DIM), F32)),
        grid=(S // tm,),
        in_specs=[row(D_MODEL), pl.BlockSpec((1, D_MODEL), lambda i: (0, 0)),
                  pl.BlockSpec((D_MODEL, QC_DIM), lambda i: (0, 0))],
        out_specs=(row(D_MODEL), row(nqkv), row(2 * CONV_DIM)),
        compiler_params=_params(("parallel",)),
    )(h, gamma, w_qc)


def _mixin_bwd(dh2, h1, gamma, dq, dkv, dcvg, dgpre, w_qc, w_g, tm, name):
    S = h1.shape[0]
    nqkv = Q_DIM + 2 * KV_DIM

    def body(dh2_ref, h1_ref, g_ref, dq_ref, dkv_ref, dcvg_ref, dgp_ref, wqc_ref, wg_ref, dh1_ref, dgam_ref):
        @pl.when(pl.program_id(0) == 0)
        def _():
            dgam_ref[...] = jnp.zeros_like(dgam_ref)

        wqc = wqc_ref[...]
        dun = _dot_nt(_bf(dq_ref[...]), wqc[:, :Q_DIM])
        dun += _dot_nt(_bf(dkv_ref[...]), wqc[:, Q_DIM:nqkv])
        dun += _dot_nt(_bf(dcvg_ref[...]), wqc[:, nqkv:])
        dun += _dot_nt(dgp_ref[...], wg_ref[...])
        dx, dgam = _rms_bwd(dun, h1_ref[...], g_ref[...])
        dh1_ref[...] = dh2_ref[...] + dx
        dgam_ref[...] += jnp.sum(dgam, axis=0, keepdims=True)

    row = lambda w: pl.BlockSpec((tm, w), lambda i: (i, 0))
    vec = pl.BlockSpec((1, D_MODEL), lambda i: (0, 0))
    return pl.pallas_call(
        body, name=name,
        out_shape=(jax.ShapeDtypeStruct((S, D_MODEL), F32), jax.ShapeDtypeStruct((1, D_MODEL), F32)),
        grid=(S // tm,),
        in_specs=[row(D_MODEL), row(D_MODEL), vec, row(Q_DIM), row(2 * KV_DIM), row(2 * CONV_DIM),
                  row(2 * D_MODEL), pl.BlockSpec((D_MODEL, QC_DIM), lambda i: (0, 0)),
                  pl.BlockSpec((D_MODEL, 2 * D_MODEL), lambda i: (0, 0))],
        out_specs=(row(D_MODEL), vec),
        compiler_params=_params(("arbitrary",)),
    )(dh2, h1, gamma, dq, dkv, dcvg, dgpre, w_qc, w_g)


TQ = 512
QB = TQ // BLOCK


def _attn_in_specs(S):
    nkb = S // BLOCK
    return [
        pl.BlockSpec((TQ, Q_DIM), lambda i: (i, 0)),
        pl.BlockSpec((BLOCK, 2 * KV_DIM), lambda i: (jnp.maximum(i * QB - 1, 0), Q_DIM // (2 * KV_DIM))),
        pl.BlockSpec((TQ, 2 * KV_DIM), lambda i: (i, Q_DIM // (2 * KV_DIM))),
        pl.BlockSpec((BLOCK, 2 * KV_DIM), lambda i: (jnp.minimum(i * QB + QB, nkb - 1), Q_DIM // (2 * KV_DIM))),
        pl.BlockSpec((1, 128), lambda i: (0, 0)),
        pl.BlockSpec((1, KV_DIM), lambda i: (0, 0)),
        pl.BlockSpec((N_HEADS, 128), lambda i: (0, 0)),
        pl.BlockSpec((N_HEADS, BLOCK, WIN), lambda i: (0, 0, 0)),
    ]


GROUP_ROWS = 4 * BLOCK


def _half_rstd(x, low):
    x2 = x * x
    z = jnp.zeros_like(x2)
    r0 = lax.rsqrt(jnp.sum(jnp.where(low, x2, z), axis=-1, keepdims=True) * (1.0 / HEAD_DIM) + 1e-6)
    r1 = lax.rsqrt(jnp.sum(jnp.where(low, z, x2), axis=-1, keepdims=True) * (1.0 / HEAD_DIM) + 1e-6)
    return jnp.where(low, r0, r1)


def _kv_windows(kvp_ref, kvc_ref, kvn_ref, kg_ref, low):
    kv = jnp.concatenate([kvp_ref[...], kvc_ref[...], kvn_ref[...]], axis=0)
    k, v = kv[:, :KV_DIM], kv[:, KV_DIM:]
    kn = k * _half_rstd(k, low) * kg_ref[...]
    kr, vr = pltpu.roll(kn, HEAD_DIM, 1), pltpu.roll(v, HEAD_DIM, 1)
    kdup = [_bf(jnp.where(low, kn, kr)), _bf(jnp.where(low, kr, kn))]
    vdup = [_bf(jnp.where(low, v, vr)), _bf(jnp.where(low, vr, v))]
    return kdup, vdup


def _stack_heads(x_ref, t, kh, low):
    rows = slice(t * BLOCK, (t + 1) * BLOCK)
    xa = x_ref[rows, 256 * kh:256 * kh + 128]
    xb = x_ref[rows, 256 * kh + 128:256 * kh + 256]
    z = jnp.zeros_like(xa)
    return jnp.concatenate([jnp.where(low, xa, z), jnp.where(low, z, xa),
                            jnp.where(low, xb, z), jnp.where(low, z, xb)], axis=0)


def _stacked_q(q_ref, qg_ref, t, kh, low):
    qraw = _stack_heads(q_ref, t, kh, low)
    rq = lax.rsqrt(jnp.sum(qraw * qraw, axis=-1, keepdims=True) * (1.0 / HEAD_DIM) + 1e-6)
    return qraw, rq, _bf(qraw * rq * (qg_ref[...] * (HEAD_DIM ** -0.5)))


def _unstack_heads(ov, low):
    return (jnp.where(low, ov[0:128], ov[128:256]), jnp.where(low, ov[256:384], ov[384:512]))


def _edge_bias(i, t, S):
    kpos = i * TQ + (t - 1) * BLOCK + lax.broadcasted_iota(jnp.int32, (1, WIN), 1)
    return jnp.where((kpos < 0) | (kpos >= S), NEG_INF, 0.0)


def _group_exp(lhs, kw, bias_ref, sink_ref, kh, edge):
    s = _dot_nt(lhs, kw) + bias_ref[4 * kh:4 * kh + 4].reshape(GROUP_ROWS, WIN)
    if edge is not None:
        s = s + edge
    sk = jnp.concatenate(
        [jnp.broadcast_to(sink_ref[4 * kh + r:4 * kh + r + 1, 0:1], (BLOCK, 1)) for r in range(4)], axis=0)
    m = jnp.maximum(jnp.max(s, axis=-1, keepdims=True), sk)
    return jnp.exp(s - m), jnp.exp(sk - m)


def _attn_fwd(qkv, qg, kg, sinkb, bias, name, ex=NO_EXCHANGE):
    S = qkv.shape[0]

    def body(q_ref, kvp_ref, kvc_ref, kvn_ref, qg_ref, kg_ref, sink_ref, bias_ref, o_ref):
        i = pl.program_id(0)
        low = lax.broadcasted_iota(jnp.int32, (1, 128), 1) < HEAD_DIM
        ones = jnp.ones((WIN, 128), BF16)
        kdup, vdup = _kv_windows(kvp_ref, kvc_ref, kvn_ref, kg_ref, low)
        for t in range(QB):
            edge = _edge_bias(i, t, S) if t in (0, QB - 1) else None
            rows = slice(t * BLOCK, (t + 1) * BLOCK)
            for kh in range(KV_HEADS):
                _, _, lhs = _stacked_q(q_ref, qg_ref, t, kh, low)
                kw = kdup[kh][t * BLOCK:t * BLOCK + WIN]
                vw = vdup[kh][t * BLOCK:t * BLOCK + WIN]
                e, es = _group_exp(lhs, kw, bias_ref, sink_ref, kh, edge)
                eb = _bf(e)
                ov = _dot(eb, vw) * (1.0 / (_dot(eb, ones) + es))
                oa, ob = _unstack_heads(ov, low)
                o_ref[rows, 256 * kh:256 * kh + 128] = _bf(oa)
                o_ref[rows, 256 * kh + 128:256 * kh + 256] = _bf(ob)

    return _carrier_call(
        body, ex, _grid1_ends(S // TQ), name, out_shape=(jax.ShapeDtypeStruct((S, Q_DIM), BF16),),
        grid=(S // TQ,), in_specs=_attn_in_specs(S),
        out_specs=(pl.BlockSpec((TQ, Q_DIM), lambda i: (i, 0)),), scratch_shapes=[],
        args=(qkv, qkv, qkv, qkv, qg, kg, sinkb, bias))


def _attn_bwd(do, qkv, qg, kg, sinkb, bias, dbias_in, name, ex=NO_EXCHANGE):
    S = qkv.shape[0]
    nkb = S // BLOCK
    nsteps = S // TQ

    def body(do_ref, q_ref, kvp_ref, kvc_ref, kvn_ref, qg_ref, kg_ref, sink_ref, bias_ref, dbin_ref,
             dq_ref, dkp_ref, dvp_ref, dbias_ref, dsink_ref, dqg_ref, dqg_s):
        i = pl.program_id(0)

        @pl.when(i == 0)
        def _():
            dbias_ref[...] = dbin_ref[...]
            dsink_ref[...] = jnp.zeros_like(dsink_ref)
            dqg_s[...] = jnp.zeros_like(dqg_s)

        low = lax.broadcasted_iota(jnp.int32, (1, 128), 1) < HEAD_DIM
        own = ((lax.broadcasted_iota(jnp.int32, (GROUP_ROWS, 128), 1) >> 6) & 1) == (
            (lax.broadcasted_iota(jnp.int32, (GROUP_ROWS, 128), 0) >> 7) & 1)
        gq = qg_ref[...] * (HEAD_DIM ** -0.5)
        ones = jnp.ones((WIN, 128), BF16)
        kdup, vdup = _kv_windows(kvp_ref, kvc_ref, kvn_ref, kg_ref, low)
        for t in range(QB):
            edge = _edge_bias(i, t, S) if t in (0, QB - 1) else None
            rows = slice(t * BLOCK, (t + 1) * BLOCK)
            dk_dup, dv_dup = [], []
            for kh in range(KV_HEADS):
                qraw, rq, lhs = _stacked_q(q_ref, qg_ref, t, kh, low)
                dos = _bf(_stack_heads(do_ref, t, kh, low))
                kw = kdup[kh][t * BLOCK:t * BLOCK + WIN]
                vw = vdup[kh][t * BLOCK:t * BLOCK + WIN]
                e, es = _group_exp(lhs, kw, bias_ref, sink_ref, kh, edge)
                inv = 1.0 / (_dot(_bf(e), ones) + es)
                pr = e * jnp.concatenate([inv] * (WIN // 128), axis=1)
                dpr = _dot_nt(dos, vw)
                delta = jnp.sum(pr * dpr, axis=-1, keepdims=True)
                ds = pr * (dpr - delta)
                dbias_ref[4 * kh:4 * kh + 4] += ds.reshape(4, BLOCK, WIN)
                dsk = es * inv[:, 0:1] * delta
                for r in range(4):
                    dsink_ref[4 * kh + r:4 * kh + r + 1, :] -= jnp.broadcast_to(
                        jnp.sum(dsk[r * BLOCK:(r + 1) * BLOCK], axis=0, keepdims=True), (1, 128))
                dsb = _bf(ds)
                dqs = _dot(dsb, kw)
                qhat = qraw * rq
                dxhat = jnp.where(own, dqs, 0.0) * gq
                dq_st = rq * (dxhat - qhat * (jnp.sum(dxhat * qhat, axis=-1, keepdims=True) * (1.0 / HEAD_DIM)))
                dq_ref[rows, 256 * kh:256 * kh + 128] = dq_st[0:128] + dq_st[128:256]
                dq_ref[rows, 256 * kh + 128:256 * kh + 256] = dq_st[256:384] + dq_st[384:512]
                dqg_s[...] += jnp.sum((dqs * qhat).reshape(GROUP_ROWS // 8, 8, 128), axis=0)
                dkx = _dot_tn(dsb, lhs)
                dvx = _dot_tn(_bf(pr), dos)
                dk_dup.append(dkx + pltpu.roll(dkx, HEAD_DIM, 1))
                dv_dup.append(dvx + pltpu.roll(dvx, HEAD_DIM, 1))
            dkp_ref[t] = jnp.where(low, dk_dup[0], dk_dup[1])
            dvp_ref[t] = jnp.where(low, dv_dup[0], dv_dup[1])

        @pl.when(i == nsteps - 1)
        def _():
            acc = dqg_s[...] * (HEAD_DIM ** -0.5)
            acc = acc + pltpu.roll(acc, HEAD_DIM, 1)
            dqg_ref[...] = jnp.broadcast_to(jnp.sum(acc, axis=0, keepdims=True), (8, 128))

    const2 = lambda shape: pl.BlockSpec(shape, lambda i: (0,) * len(shape))
    part = pl.BlockSpec((QB, WIN, KV_DIM), lambda i: (i, 0, 0))
    return _carrier_call(
        body, ex, _grid1_ends(nsteps), name,
        out_shape=(jax.ShapeDtypeStruct((S, Q_DIM), F32), jax.ShapeDtypeStruct((nkb, WIN, KV_DIM), F32),
                   jax.ShapeDtypeStruct((nkb, WIN, KV_DIM), F32),
                   jax.ShapeDtypeStruct((N_HEADS, BLOCK, WIN), F32), jax.ShapeDtypeStruct((N_HEADS, 128), F32),
                   jax.ShapeDtypeStruct((8, 128), F32)),
        grid=(nsteps,),
        in_specs=[pl.BlockSpec((TQ, Q_DIM), lambda i: (i, 0))] + _attn_in_specs(S)
        + [const2((N_HEADS, BLOCK, WIN))],
        out_specs=(pl.BlockSpec((TQ, Q_DIM), lambda i: (i, 0)), part, part,
                   const2((N_HEADS, BLOCK, WIN)), const2((N_HEADS, 128)), const2((8, 128))),
        scratch_shapes=[pltpu.VMEM((8, 128), F32)],
        args=(do, qkv, qkv, qkv, qkv, qg, kg, sinkb, bias, dbias_in))


def _kv_fold(dkp, dvp, qkv, kg, name):
    nkb = dkp.shape[0]
    S = nkb * BLOCK
    nsteps = S // TQ

    def body(kp_p, kp_c, kp_n, vp_p, vp_c, vp_n, kv_ref, kg_ref, dkv_ref, dkg_ref, dkg_s):
        i = pl.program_id(0)

        @pl.when(i == 0)
        def _():
            dkg_s[...] = jnp.zeros_like(dkg_s)

        def fold(p_ref, c_ref, n_ref):
            blocks = []
            for t in range(QB):
                acc = c_ref[t, BLOCK:2 * BLOCK, :]
                if t > 0:
                    acc = acc + c_ref[t - 1, 2 * BLOCK:, :]
                else:
                    acc = acc + jnp.where(i > 0, p_ref[0, 2 * BLOCK:, :], 0.0)
                if t < QB - 1:
                    acc = acc + c_ref[t + 1, :BLOCK, :]
                else:
                    acc = acc + jnp.where(i < nsteps - 1, n_ref[0, :BLOCK, :], 0.0)
                blocks.append(acc)
            return jnp.concatenate(blocks, axis=0)

        dkn = fold(kp_p, kp_c, kp_n)
        dv = fold(vp_p, vp_c, vp_n)
        k = kv_ref[:, :KV_DIM]
        low = lax.broadcasted_iota(jnp.int32, (1, 128), 1) < HEAD_DIM
        rk = _half_rstd(k, low)
        khat = k * rk
        dxhat = dkn * kg_ref[...]
        prod = dxhat * khat
        z = jnp.zeros_like(prod)
        mean = jnp.where(low, jnp.sum(jnp.where(low, prod, z), axis=-1, keepdims=True),
                         jnp.sum(jnp.where(low, z, prod), axis=-1, keepdims=True)) * (1.0 / HEAD_DIM)
        dkv_ref[:, :KV_DIM] = rk * (dxhat - khat * mean)
        dkv_ref[:, KV_DIM:] = dv
        dkg_s[...] += jnp.sum((dkn * khat).reshape(TQ // 8, 8, KV_DIM), axis=0)

        @pl.when(i == nsteps - 1)
        def _():
            acc = dkg_s[...] + pltpu.roll(dkg_s[...], HEAD_DIM, 1)
            dkg_ref[...] = jnp.broadcast_to(jnp.sum(acc, axis=0, keepdims=True), (8, 128))

    prev = pl.BlockSpec((1, WIN, KV_DIM), lambda i: (jnp.maximum(i * QB - 1, 0), 0, 0))
    cur = pl.BlockSpec((QB, WIN, KV_DIM), lambda i: (i, 0, 0))
    nxt = pl.BlockSpec((1, WIN, KV_DIM), lambda i: (jnp.minimum(i * QB + QB, nkb - 1), 0, 0))
    return pl.pallas_call(
        body, name=name,
        out_shape=(jax.ShapeDtypeStruct((S, 2 * KV_DIM), F32), jax.ShapeDtypeStruct((8, 128), F32)),
        grid=(nsteps,),
        in_specs=[prev, cur, nxt, prev, cur, nxt,
                  pl.BlockSpec((TQ, 2 * KV_DIM), lambda i: (i, Q_DIM // (2 * KV_DIM))),
                  pl.BlockSpec((1, KV_DIM), lambda i: (0, 0))],
        out_specs=(pl.BlockSpec((TQ, 2 * KV_DIM), lambda i: (i, 0)), pl.BlockSpec((8, 128), lambda i: (0, 0))),
        scratch_shapes=[pltpu.VMEM((8, KV_DIM), F32)],
        compiler_params=_params(("arbitrary",)),
    )(dkp, dkp, dkp, dvp, dvp, dvp, qkv, kg)


BIAS_COLS = BLOCK * WIN
BIAS_CHUNK = 6144


def _bias_table(rel_bias_t, onehot, band):
    def body(rb_ref, oh_ref, band_ref, o_ref):
        o_ref[...] = _dot(rb_ref[...], oh_ref[...], HI) + band_ref[...]

    return pl.pallas_call(
        body, name="bias_table", out_shape=jax.ShapeDtypeStruct((N_HEADS, BIAS_COLS), F32),
        grid=(BIAS_COLS // BIAS_CHUNK,),
        in_specs=[pl.BlockSpec((N_HEADS, NUM_BUCKETS), lambda i: (0, 0)),
                  pl.BlockSpec((NUM_BUCKETS, BIAS_CHUNK), lambda i: (0, i)),
                  pl.BlockSpec((1, BIAS_CHUNK), lambda i: (0, i))],
        out_specs=pl.BlockSpec((N_HEADS, BIAS_CHUNK), lambda i: (0, i)),
        compiler_params=_params(("parallel",)),
    )(rel_bias_t, onehot, band)


def _bias_grad(dbias, onehot):
    def body(db_ref, oh_ref, o_ref):
        @pl.when(pl.program_id(0) == 0)
        def _():
            o_ref[...] = jnp.zeros_like(o_ref)

        o_ref[...] += lax.dot_general(db_ref[...], oh_ref[...], (((1,), (1,)), ((), ())),
                                      preferred_element_type=F32, precision=HI)

    return pl.pallas_call(
        body, name="bias_grad", out_shape=jax.ShapeDtypeStruct((N_HEADS, NUM_BUCKETS), F32),
        grid=(BIAS_COLS // BIAS_CHUNK,),
        in_specs=[pl.BlockSpec((N_HEADS, BIAS_CHUNK), lambda i: (0, i)),
                  pl.BlockSpec((NUM_BUCKETS, BIAS_CHUNK), lambda i: (0, i))],
        out_specs=pl.BlockSpec((N_HEADS, NUM_BUCKETS), lambda i: (0, 0)),
        compiler_params=_params(("arbitrary",)),
    )(dbias, onehot)


def _bucket_onehot():
    half = NUM_BUCKETS // 2
    max_exact = half // 2
    rel = jnp.arange(WIN)[None, :] - BLOCK - jnp.arange(BLOCK)[:, None]
    n = jnp.abs(rel)
    ret = jnp.where(rel > 0, half, 0)
    nf = jnp.maximum(n, 1).astype(F32)
    large = max_exact + (jnp.log(nf / max_exact) / np.log(MAX_DISTANCE / max_exact)
                         * (half - max_exact)).astype(jnp.int32)
    large = jnp.minimum(large, half - 1)
    bucket = (ret + jnp.where(n < max_exact, n, large)).reshape(1, BIAS_COLS)
    band = jnp.where(n <= BLOCK, 0.0, NEG_INF).astype(F32).reshape(1, BIAS_COLS)
    return (bucket == jnp.arange(NUM_BUCKETS)[:, None]).astype(F32), band


def _halo_specs(tm, width, S):
    r = tm // HALO
    last = S // HALO - 1
    return [pl.BlockSpec((HALO, width), lambda i: (jnp.maximum(i * r - 1, 0), 0)),
            pl.BlockSpec((tm, width), lambda i: (i, 0)),
            pl.BlockSpec((HALO, width), lambda i: (jnp.minimum(i * r + r, last), 0))]


def _with_halo(p_ref, c_ref, n_ref):
    return jnp.concatenate([p_ref[...], c_ref[...], n_ref[...]], axis=0)


def _row_valid(i, tm, S):
    g = i * tm - HALO + lax.broadcasted_iota(jnp.int32, (tm + 2 * HALO, 1), 0)
    return (g >= 0) & (g < S)


def _shifted(x):
    n = x.shape[0]
    return [x if b == 0 else pltpu.roll(x, n - b, 0) for b in range(8)]


def _tap(sh, off, tm):
    a, b = off // 8, off % 8
    return sh[b][8 * a:8 * a + tm]


def _conv_fwd(cvg, cw, cb, lg, lb, tm, name):
    S = cvg.shape[0]

    def body(p_ref, c_ref, n_ref, cw_ref, cb_ref, lg_ref, lb_ref, act_ref, yc_ref):
        i = pl.program_id(0)
        z = _with_halo(p_ref, c_ref, n_ref)
        glu = jnp.where(_row_valid(i, tm, S), z[:, :CONV_DIM] * _sig(z[:, CONV_DIM:]), 0.0)
        sh = _shifted(glu)
        y = jnp.zeros((tm, CONV_DIM), F32) + cb_ref[...]
        for w in range(CONV_WIDTH):
            y = y + _tap(sh, w + 1, tm) * cw_ref[w:w + 1, :]
        yc_ref[...] = y
        mu = jnp.mean(y, axis=-1, keepdims=True)
        yc = y - mu
        rstd = lax.rsqrt(jnp.mean(yc * yc, axis=-1, keepdims=True) + 1e-5)
        ln = yc * rstd * lg_ref[...] + lb_ref[...]
        act_ref[...] = _bf(ln * _sig(ln))

    vec = pl.BlockSpec((1, CONV_DIM), lambda i: (0, 0))
    row = pl.BlockSpec((tm, CONV_DIM), lambda i: (i, 0))
    return pl.pallas_call(
        body, name=name,
        out_shape=(jax.ShapeDtypeStruct((S, CONV_DIM), BF16), jax.ShapeDtypeStruct((S, CONV_DIM), F32)),
        grid=(S // tm,),
        in_specs=_halo_specs(tm, 2 * CONV_DIM, S) + [pl.BlockSpec((32, CONV_DIM), lambda i: (0, 0)), vec, vec, vec],
        out_specs=(row, row),
        compiler_params=_params(("parallel",)),
    )(cvg, cvg, cvg, cw, cb, lg, lb)


def _conv_bwd(dact, yconv, cvg, cw, lg, lb, tm, name):
    S = cvg.shape[0]
    nsteps = S // tm

    def body(dp, dc, dn, yp, yc_, yn, zp, zc, zn, cw_ref, lg_ref, lb_ref,
             dz_ref, dcw_ref, dvec_ref, dcw_s, dvec_s, shg_s, shd_s):
        i = pl.program_id(0)

        @pl.when(i == 0)
        def _():
            dcw_s[...] = jnp.zeros_like(dcw_s)
            dvec_s[...] = jnp.zeros_like(dvec_s)

        valid = _row_valid(i, tm, S)
        own = (lax.broadcasted_iota(jnp.int32, (tm + 2 * HALO, 1), 0) >= HALO) & (
            lax.broadcasted_iota(jnp.int32, (tm + 2 * HALO, 1), 0) < HALO + tm)
        y = _with_halo(yp, yc_, yn)
        dact_ = _with_halo(dp, dc, dn)
        mu = jnp.mean(y, axis=-1, keepdims=True)
        ycen = y - mu
        rstd = lax.rsqrt(jnp.mean(ycen * ycen, axis=-1, keepdims=True) + 1e-5)
        yhat = ycen * rstd
        ln = yhat * lg_ref[...] + lb_ref[...]
        sg = _sig(ln)
        dln = dact_ * (sg * (1.0 + ln * (1.0 - sg)))
        dyhat = dln * lg_ref[...]
        dy = rstd * (dyhat - jnp.mean(dyhat, axis=-1, keepdims=True)
                     - yhat * jnp.mean(dyhat * yhat, axis=-1, keepdims=True))
        dy = jnp.where(valid, dy, 0.0)
        dln_own = jnp.where(own, dln, 0.0)
        nr = (tm + 2 * HALO) // 8
        dvec_s[0] += jnp.sum(jnp.where(own, dy, 0.0).reshape(nr, 8, CONV_DIM), axis=0)
        dvec_s[1] += jnp.sum((dln_own * yhat).reshape(nr, 8, CONV_DIM), axis=0)
        dvec_s[2] += jnp.sum(dln_own.reshape(nr, 8, CONV_DIM), axis=0)
        z = _with_halo(zp, zc, zn)
        glu = jnp.where(valid, z[:, :CONV_DIM] * _sig(z[:, CONV_DIM:]), 0.0)
        for b, (g_b, d_b) in enumerate(zip(_shifted(glu), _shifted(dy))):
            shg_s[b] = g_b
            shd_s[b] = d_b
        for cb in range(CONV_DIM // 128):
            lanes = slice(128 * cb, 128 * (cb + 1))
            for rb in range(tm // CROWS):
                r0 = rb * CROWS
                dy_own = shd_s[0, HALO + r0:HALO + r0 + CROWS, lanes]
                dglu = jnp.zeros((CROWS, 128), F32)
                for w in range(CONV_WIDTH):
                    a, b = divmod(CONV_WIDTH - w, 8)
                    dglu = dglu + shd_s[b, 8 * a + r0:8 * a + r0 + CROWS, lanes] * cw_ref[w:w + 1, lanes]
                    a, b = divmod(w + 1, 8)
                    prod = dy_own * shg_s[b, 8 * a + r0:8 * a + r0 + CROWS, lanes]
                    dcw_s[w, :, lanes] += jnp.sum(prod.reshape(CROWS // 8, 8, 128), axis=0)
                cv = zc[r0:r0 + CROWS, lanes]
                sg_o = _sig(zc[r0:r0 + CROWS, CONV_DIM + 128 * cb:CONV_DIM + 128 * (cb + 1)])
                dz_ref[r0:r0 + CROWS, lanes] = dglu * sg_o
                dz_ref[r0:r0 + CROWS, CONV_DIM + 128 * cb:CONV_DIM + 128 * (cb + 1)] = (
                    dglu * cv * sg_o * (1.0 - sg_o))

        @pl.when(i == nsteps - 1)
        def _():
            dcw_ref[...] = jnp.sum(dcw_s[...], axis=1)
            dvec_ref[...] = jnp.sum(dvec_s[...], axis=1)

    vec = pl.BlockSpec((1, CONV_DIM), lambda i: (0, 0))
    return pl.pallas_call(
        body, name=name,
        out_shape=(jax.ShapeDtypeStruct((S, 2 * CONV_DIM), F32), jax.ShapeDtypeStruct((32, CONV_DIM), F32),
                   jax.ShapeDtypeStruct((8, CONV_DIM), F32)),
        grid=(nsteps,),
        in_specs=_halo_specs(tm, CONV_DIM, S) + _halo_specs(tm, CONV_DIM, S) + _halo_specs(tm, 2 * CONV_DIM, S)
        + [pl.BlockSpec((32, CONV_DIM), lambda i: (0, 0)), vec, vec],
        out_specs=(pl.BlockSpec((tm, 2 * CONV_DIM), lambda i: (i, 0)),
                   pl.BlockSpec((32, CONV_DIM), lambda i: (0, 0)), pl.BlockSpec((8, CONV_DIM), lambda i: (0, 0))),
        scratch_shapes=[pltpu.VMEM((32, 8, CONV_DIM), F32), pltpu.VMEM((8, 8, CONV_DIM), F32),
                        pltpu.VMEM((8, tm + 2 * HALO, CONV_DIM), F32), pltpu.VMEM((8, tm + 2 * HALO, CONV_DIM), F32)],
        compiler_params=_params(("arbitrary",)),
    )(dact, dact, dact, yconv, yconv, yconv, cvg, cvg, cvg, cw, lg, lb)


def _merge_parts(un, o, cact, wg_ref, wao_ref, wco_ref):
    g = _dot(un, wg_ref[...])
    ga, gc = _sig(g[:, :D_MODEL]), _sig(g[:, D_MODEL:])
    ya = _dot(o, wao_ref[...])
    yc = _dot(cact, wco_ref[...])
    return ga, gc, ya, yc


def _merge_specs(tm):
    row = lambda w: pl.BlockSpec((tm, w), lambda i: (i, 0))
    full = lambda a, b: pl.BlockSpec((a, b), lambda i: (0, 0))
    weights = [full(D_MODEL, 2 * D_MODEL), full(Q_DIM, D_MODEL), full(CONV_DIM, D_MODEL), full(D_MODEL, D_MODEL)]
    return row, weights


def _merge_fwd(h1, un, o, cact, w_g, w_ao, w_co, w_o, tm, name):
    S = h1.shape[0]
    row, weights = _merge_specs(tm)

    def body(h1_ref, un_ref, o_ref, c_ref, wg_ref, wao_ref, wco_ref, wo_ref, h2_ref):
        ga, gc, ya, yc = _merge_parts(un_ref[...], o_ref[...], c_ref[...], wg_ref, wao_ref, wco_ref)
        h2_ref[...] = h1_ref[...] + _dot(_bf(ga * ya + gc * yc), wo_ref[...])

    return pl.pallas_call(
        body, name=name, out_shape=jax.ShapeDtypeStruct((S, D_MODEL), F32),
        grid=(S // tm,),
        in_specs=[row(D_MODEL), row(D_MODEL), row(Q_DIM), row(CONV_DIM)] + weights,
        out_specs=row(D_MODEL),
        compiler_params=_params(("parallel",)),
    )(h1, un, o, cact, w_g, w_ao, w_co, w_o)


def _merge_bwd(dh2, un, o, cact, w_g, w_ao, w_co, w_o, tm, name):
    S = dh2.shape[0]
    row, weights = _merge_specs(tm)

    def body(dh2_ref, un_ref, o_ref, c_ref, wg_ref, wao_ref, wco_ref, wo_ref,
             do_ref, dc_ref, mix_ref, dya_ref, dyc_ref, dgp_ref):
        ga, gc, ya, yc = _merge_parts(un_ref[...], o_ref[...], c_ref[...], wg_ref, wao_ref, wco_ref)
        mix_ref[...] = _bf(ga * ya + gc * yc)
        dmix = _dot_nt(_bf(dh2_ref[...]), wo_ref[...])
        dya = _bf(dmix * ga)
        dyc = _bf(dmix * gc)
        dya_ref[...] = dya
        dyc_ref[...] = dyc
        dgp_ref[:, :D_MODEL] = _bf(dmix * ya * ga * (1.0 - ga))
        dgp_ref[:, D_MODEL:] = _bf(dmix * yc * gc * (1.0 - gc))
        do_ref[...] = _dot_nt(dya, wao_ref[...])
        dc_ref[...] = _dot_nt(dyc, wco_ref[...])

    return pl.pallas_call(
        body, name=name,
        out_shape=(jax.ShapeDtypeStruct((S, Q_DIM), F32), jax.ShapeDtypeStruct((S, CONV_DIM), F32),
                   jax.ShapeDtypeStruct((S, D_MODEL), BF16), jax.ShapeDtypeStruct((S, D_MODEL), BF16),
                   jax.ShapeDtypeStruct((S, D_MODEL), BF16), jax.ShapeDtypeStruct((S, 2 * D_MODEL), BF16)),
        grid=(S // tm,),
        in_specs=[row(D_MODEL), row(D_MODEL), row(Q_DIM), row(CONV_DIM)] + weights,
        out_specs=(row(Q_DIM), row(CONV_DIM), row(D_MODEL), row(D_MODEL), row(D_MODEL), row(2 * D_MODEL)),
        compiler_params=_params(("parallel",)),
    )(dh2, un, o, cact, w_g, w_ao, w_co, w_o)


def _pe_specs(tm, layer):
    row = pl.BlockSpec((tm, D_MODEL), lambda i: (i, 0))
    vec = pl.BlockSpec((1, D_MODEL), lambda i: (0, 0))
    p_s = pl.BlockSpec((None, None, tm, 256), lambda i: (layer, 0, i, 0))
    wpp = pl.BlockSpec((256, D_MODEL), lambda i: (0, 0))
    wpg = pl.BlockSpec((D_MODEL, D_MODEL), lambda i: (0, 0))
    return row, vec, p_s, wpp, wpg


def _pe_fwd(h, gamma, p, layer, w_pp, w_pg, tm, name):
    S = h.shape[0]
    row, vec, p_s, wpp, wpg = _pe_specs(tm, layer)

    def body(h_ref, g_ref, p_ref, wpp_ref, wpg_ref, x_ref, hn_ref):
        hn = _bf(_rms_fwd(h_ref[...], g_ref[...])[0])
        hn_ref[...] = hn
        gate = _sig(_dot(hn, wpg_ref[...]))
        x_ref[...] = h_ref[...] + _dot(_bf(p_ref[...]), wpp_ref[...]) * gate

    return pl.pallas_call(
        body, name=name,
        out_shape=(jax.ShapeDtypeStruct((S, D_MODEL), F32), jax.ShapeDtypeStruct((S, D_MODEL), BF16)),
        grid=(S // tm,), in_specs=[row, vec, p_s, wpp, wpg], out_specs=(row, row),
        compiler_params=_params(("parallel",)),
    )(h, gamma, p, w_pp, w_pg)


def _pe_bwd(dx, h, gamma, hn, p, layer, w_pp, w_pg, tm, name):
    S = h.shape[0]
    row, vec, p_s, wpp, wpg = _pe_specs(tm, layer)

    def body(dx_ref, h_ref, g_ref, hn_ref, p_ref, wpp_ref, wpg_ref, dh_ref, dgp_ref, dpr_ref, dgam_ref):
        @pl.when(pl.program_id(0) == 0)
        def _():
            dgam_ref[...] = jnp.zeros_like(dgam_ref)

        dxv = dx_ref[...]
        gate = _sig(_dot(hn_ref[...], wpg_ref[...]))
        proj = _dot(_bf(p_ref[...]), wpp_ref[...])
        dpr_ref[...] = _bf(dxv * gate)
        dgp = _bf(dxv * proj * gate * (1.0 - gate))
        dgp_ref[...] = dgp
        dxn, dgam = _rms_bwd(_dot_nt(dgp, wpg_ref[...]), h_ref[...], g_ref[...])
        dh_ref[...] = dxv + dxn
        dgam_ref[...] += jnp.sum(dgam, axis=0, keepdims=True)

    return pl.pallas_call(
        body, name=name,
        out_shape=(jax.ShapeDtypeStruct((S, D_MODEL), F32), jax.ShapeDtypeStruct((S, D_MODEL), BF16),
                   jax.ShapeDtypeStruct((S, D_MODEL), BF16), jax.ShapeDtypeStruct((1, D_MODEL), F32)),
        grid=(S // tm,), in_specs=[row, row, vec, row, p_s, wpp, wpg], out_specs=(row, row, row, vec),
        compiler_params=_params(("arbitrary",)),
    )(dx, h, gamma, hn, p, w_pp, w_pg)


def _loss_head(y, target, tm):
    S = y.shape[0]

    def body(y_ref, t_ref, dy_ref, l_ref):
        @pl.when(pl.program_id(0) == 0)
        def _():
            l_ref[...] = jnp.zeros_like(l_ref)

        diff = y_ref[...] - t_ref[...]
        dy_ref[...] = diff * (1.0 / D_MODEL)
        sq = jnp.sum((diff * diff).reshape(tm // 8, 8, D_MODEL), axis=0)
        part = sq[:, 0:128]
        for k in range(1, D_MODEL // 128):
            part = part + sq[:, 128 * k:128 * (k + 1)]
        l_ref[...] += part

    row = pl.BlockSpec((tm, D_MODEL), lambda i: (i, 0))
    return pl.pallas_call(
        body, name="loss_head",
        out_shape=(jax.ShapeDtypeStruct((S, D_MODEL), F32), jax.ShapeDtypeStruct((8, 128), F32)),
        grid=(S // tm,), in_specs=[row, row], out_specs=(row, pl.BlockSpec((8, 128), lambda i: (0, 0))),
        compiler_params=_params(("arbitrary",)),
    )(y, target)


def _adamw(parts, w, m, v, name):
    R, C = w.shape
    tr = R
    for cand in (256, 128, 64, 32, 16):
        if R % cand == 0:
            tr = cand
            break

    def body(p_ref, w_ref, m_ref, v_ref, g_ref, d_ref, nm_ref, nv_ref):
        g = p_ref[0].astype(F32)
        for k in range(1, N_DEV):
            g = g + p_ref[k].astype(F32)
        g_ref[...] = g
        nm = ADAM_B1 * m_ref[...] + (1.0 - ADAM_B1) * g
        nv = ADAM_B2 * v_ref[...] + (1.0 - ADAM_B2) * (g * g)
        nm_ref[...] = nm
        nv_ref[...] = nv
        m_hat = nm / (1.0 - ADAM_B1 ** ADAM_STEP)
        v_hat = nv / (1.0 - ADAM_B2 ** ADAM_STEP)
        d_ref[...] = -ADAM_LR * (m_hat / (jnp.sqrt(v_hat) + ADAM_EPS) + ADAM_WD * w_ref[...])

    blk = pl.BlockSpec((tr, C), lambda i: (i, 0))
    out = jax.ShapeDtypeStruct((R, C), F32)
    return pl.pallas_call(
        body, name=name, out_shape=(out, out, out, out), grid=(R // tr,),
        in_specs=[pl.BlockSpec((N_DEV, tr, C), lambda i: (0, i, 0)), blk, blk, blk],
        out_specs=(blk, blk, blk, blk),
        compiler_params=_params(("parallel",)),
    )(parts, w, m, v)


SHARDED = ("w_ffn1_in", "w_ffn1_out", "w_in", "conv_w", "w_attn_out", "w_conv_out", "w_o",
           "w_ffn2_in", "w_ffn2_out", "w_pe_gate", "w_pe_proj")
COL_SHARDED = ("w_ffn1_in", "w_in", "conv_w", "w_attn_out", "w_conv_out", "w_ffn2_in", "w_pe_proj")
SMALL = ("rel_bias", "norm_ffn1", "norm_mix", "q_norm", "k_norm", "sink", "conv_b", "conv_ln_g", "conv_ln_b",
         "norm_ffn2", "norm_pe")
WEIGHTS = ("rel_bias", "norm_ffn1", "w_ffn1_in", "w_ffn1_out", "norm_mix", "w_in", "q_norm", "k_norm", "sink",
           "conv_w", "conv_b", "conv_ln_g", "conv_ln_b", "w_attn_out", "w_conv_out", "w_o", "norm_ffn2",
           "w_ffn2_in", "w_ffn2_out", "norm_pe", "w_pe_gate", "w_pe_proj")


def _natural(g):
    k, n = g.shape[1], g.shape[2]
    return jnp.transpose(g, (1, 0, 2)).reshape(k, N_DEV * n)


def _blocked(w):
    k, n = w.shape[0], w.shape[1] // N_DEV
    return jnp.transpose(w.reshape(k, N_DEV, n), (1, 0, 2))


def kernel(x, p, rel_bias, norm_ffn1, w_ffn1_in, w_ffn1_out, norm_mix, w_in, q_norm, k_norm, sink, conv_w, conv_b, conv_ln_g, conv_ln_b, w_attn_out, w_conv_out, w_o, norm_ffn2, w_ffn2_in, w_ffn2_out, norm_pe, w_pe_gate, w_pe_proj, loss_target, m_rel_bias, m_norm_ffn1, m_w_ffn1_in, m_w_ffn1_out, m_norm_mix, m_w_in, m_q_norm, m_k_norm, m_sink, m_conv_w, m_conv_b, m_conv_ln_g, m_conv_ln_b, m_w_attn_out, m_w_conv_out, m_w_o, m_norm_ffn2, m_w_ffn2_in, m_w_ffn2_out, m_norm_pe, m_w_pe_gate, m_w_pe_proj, v_rel_bias, v_norm_ffn1, v_w_ffn1_in, v_w_ffn1_out, v_norm_mix, v_w_in, v_q_norm, v_k_norm, v_sink, v_conv_w, v_conv_b, v_conv_ln_g, v_conv_ln_b, v_w_attn_out, v_w_conv_out, v_w_o, v_norm_ffn2, v_w_ffn2_in, v_w_ffn2_out, v_norm_pe, v_w_pe_gate, v_w_pe_proj):
    W = dict(rel_bias=rel_bias, norm_ffn1=norm_ffn1, w_ffn1_in=w_ffn1_in, w_ffn1_out=w_ffn1_out, norm_mix=norm_mix,
             w_in=w_in, q_norm=q_norm, k_norm=k_norm, sink=sink, conv_w=conv_w, conv_b=conv_b, conv_ln_g=conv_ln_g,
             conv_ln_b=conv_ln_b, w_attn_out=w_attn_out, w_conv_out=w_conv_out, w_o=w_o, norm_ffn2=norm_ffn2,
             w_ffn2_in=w_ffn2_in, w_ffn2_out=w_ffn2_out, norm_pe=norm_pe, w_pe_gate=w_pe_gate, w_pe_proj=w_pe_proj)
    M = dict(rel_bias=m_rel_bias, norm_ffn1=m_norm_ffn1, w_ffn1_in=m_w_ffn1_in, w_ffn1_out=m_w_ffn1_out,
             norm_mix=m_norm_mix, w_in=m_w_in, q_norm=m_q_norm, k_norm=m_k_norm, sink=m_sink, conv_w=m_conv_w,
             conv_b=m_conv_b, conv_ln_g=m_conv_ln_g, conv_ln_b=m_conv_ln_b, w_attn_out=m_w_attn_out,
             w_conv_out=m_w_conv_out, w_o=m_w_o, norm_ffn2=m_norm_ffn2, w_ffn2_in=m_w_ffn2_in,
             w_ffn2_out=m_w_ffn2_out, norm_pe=m_norm_pe, w_pe_gate=m_w_pe_gate, w_pe_proj=m_w_pe_proj)
    V = dict(rel_bias=v_rel_bias, norm_ffn1=v_norm_ffn1, w_ffn1_in=v_w_ffn1_in, w_ffn1_out=v_w_ffn1_out,
             norm_mix=v_norm_mix, w_in=v_w_in, q_norm=v_q_norm, k_norm=v_k_norm, sink=v_sink, conv_w=v_conv_w,
             conv_b=v_conv_b, conv_ln_g=v_conv_ln_g, conv_ln_b=v_conv_ln_b, w_attn_out=v_w_attn_out,
             w_conv_out=v_w_conv_out, w_o=v_w_o, norm_ffn2=v_norm_ffn2, w_ffn2_in=v_w_ffn2_in,
             w_ffn2_out=v_w_ffn2_out, norm_pe=v_norm_pe, w_pe_gate=v_w_pe_gate, w_pe_proj=v_w_pe_proj)

    L = w_in.shape[0]
    S = x.shape[1]
    tm = min(512, S)
    xs = x[0]
    target = loss_target[0]
    vec = lambda a: a.reshape(1, -1)

    set_ffn1 = ("w_ffn1_in", "w_ffn1_out")
    set_mid = ("w_in", "conv_w", "w_attn_out", "w_conv_out", "w_o")
    set_late = ("w_ffn2_in", "w_ffn2_out", "w_pe_gate", "w_pe_proj")

    def shards(names, l):
        return [W[n][l] if n == "conv_w" else W[n][l].astype(BF16) for n in names]

    onehot, band = _bucket_onehot()
    bias = _bias_table(rel_bias.T, onehot, band).reshape(N_HEADS, BLOCK, WIN)

    layers, saved = [], []
    h = xs
    ffn1_w = _exchange(shards(set_ffn1, 0), True, "allgather_first")
    for l in range(L):
        sv = dict(x0=h)
        G = dict(wi1=ffn1_w[0], wo1=ffn1_w[1])
        (h1, sv["xn1"], sv["g1"], sv["u1"]), got = _ffn_fwd(
            h, vec(norm_ffn1[l]), G["wi1"], G["wo1"], tm, "ffn1_fwd", _Exchange(shards(set_mid, l), True))
        mid = dict(zip(set_mid, got))
        w_in_n = _natural(mid["w_in"])
        G.update(w_qc=w_in_n[:, :QC_DIM], w_g=w_in_n[:, QC_DIM:],
                 conv_w=jnp.pad(_natural(mid["conv_w"]), ((0, 1), (0, 0))),
                 w_ao=_natural(mid["w_attn_out"]), w_co=_natural(mid["w_conv_out"]),
                 w_o=mid["w_o"].reshape(D_MODEL, D_MODEL))
        sv["h1"] = h1
        sv["un"], sv["qkv"], sv["cvg"] = _mixin_fwd(h1, vec(norm_mix[l]), G["w_qc"], tm, "mixin_fwd")
        sv["qg"] = vec(jnp.tile(q_norm[l], 2))
        sv["kg"] = vec(jnp.tile(k_norm[l], KV_HEADS))
        sv["sinkb"] = jnp.broadcast_to(sink[l][:, None], (N_HEADS, 128))
        (sv["o"],), got = _attn_fwd(sv["qkv"], sv["qg"], sv["kg"], sv["sinkb"], bias, "attn_fwd",
                                    _Exchange(shards(set_late, l), True))
        late = dict(zip(set_late, got))
        G.update(wi2=late["w_ffn2_in"], wo2=late["w_ffn2_out"],
                 w_pg=late["w_pe_gate"].reshape(D_MODEL, D_MODEL), w_pp=_natural(late["w_pe_proj"]))
        sv["cact"], sv["yconv"] = _conv_fwd(sv["cvg"], G["conv_w"], vec(conv_b[l]), vec(conv_ln_g[l]),
                                            vec(conv_ln_b[l]), tm, "conv_fwd")
        h2 = _merge_fwd(h1, sv["un"], sv["o"], sv["cact"], G["w_g"], G["w_ao"], G["w_co"], G["w_o"], tm, "merge_fwd")
        sv["h2"] = h2
        nxt = _Exchange(shards(set_ffn1, l + 1), True) if l + 1 < L else NO_EXCHANGE
        (h3, sv["xn2"], sv["g2"], sv["u2"]), ffn1_w = _ffn_fwd(
            h2, vec(norm_ffn2[l]), G["wi2"], G["wo2"], tm, "ffn2_fwd", nxt)
        sv["h3"] = h3
        h, sv["hn"] = _pe_fwd(h3, vec(norm_pe[l]), p, l, G["w_pp"], G["w_pg"], tm, "pe_fwd")
        layers.append(G)
        saved.append(sv)

    dh, lparts = _loss_head(h, target, tm)
    loss = lax.psum((0.5 / D_MODEL) * jnp.sum(lparts), AXES)

    dbias = jnp.zeros((N_HEADS, BLOCK, WIN), F32)
    small_g = {n: [None] * L for n in SMALL if n != "rel_bias"}
    recv = {n: [None] * L for n in SHARDED}

    def keep(names, l, got):
        for n, r in zip(names, got):
            recv[n][l] = r

    pending = None
    for l in reversed(range(L)):
        G, sv = layers[l], saved[l]
        dh3, dgp_pe, dproj, dg_pe = _pe_bwd(dh, sv["h3"], vec(norm_pe[l]), sv["hn"], p, l, G["w_pp"], G["w_pg"],
                                            tm, "pe_bwd")
        gw_pg = _matmul_tn(sv["hn"][None], dgp_pe[None], 1, "dw_pe_gate")
        gw_pp = _matmul_tn(p[l], dproj[None], 1, "dw_pe_proj")
        (dh2, a2, dgu2, dg_n2), got = _ffn_bwd(
            dh3, sv["h2"], vec(norm_ffn2[l]), sv["g2"], sv["u2"], G["wi2"], G["wo2"], tm, "ffn2_bwd",
            _Exchange(pending, False) if pending else NO_EXCHANGE)
        if pending:
            keep(set_ffn1, l + 1, got)
        gwo2 = _matmul_tn(a2, dh3[None], FF_BLOCKS, "dw_ffn2_out", scale=0.5)
        gwi2 = _matmul_tn(sv["xn2"][None], dgu2.reshape(2 * FF_BLOCKS, S, FF_SHARD), 2 * FF_BLOCKS, "dw_ffn2_in")
        do, dcact, mix, dya, dyc, dgpre = _merge_bwd(dh2, sv["un"], sv["o"], sv["cact"], G["w_g"], G["w_ao"],
                                                     G["w_co"], G["w_o"], tm, "merge_bwd")
        gw_o = _matmul_tn(mix[None], dh2[None], 1, "dw_o")
        gw_ao = _matmul_tn(sv["o"][None], dya[None], 1, "dw_attn_out")
        gw_co = _matmul_tn(sv["cact"][None], dyc[None], 1, "dw_conv_out")
        late_send = [gwi2, gwo2.reshape(N_DEV, FF_SHARD // 2, D_MODEL),
                     gw_pg.reshape(N_DEV, D_MODEL // N_DEV, D_MODEL), _blocked(gw_pp[0])]
        (dq, dkp, dvp, dbias, dsink, dqg), got = _attn_bwd(
            do, sv["qkv"], sv["qg"], sv["kg"], sv["sinkb"], bias, dbias, "attn_bwd", _Exchange(late_send, False))
        keep(set_late, l, got)
        dkv, dkg = _kv_fold(dkp, dvp, sv["qkv"], sv["kg"], "kv_fold")
        dcvg, dcw, dcvec = _conv_bwd(dcact, sv["yconv"], sv["cvg"], G["conv_w"], vec(conv_ln_g[l]),
                                     vec(conv_ln_b[l]), tm, "conv_bwd")
        dh1, dg_mix = _mixin_bwd(dh2, sv["h1"], vec(norm_mix[l]), dq, dkv, dcvg, dgpre, G["w_qc"], G["w_g"],
                                 tm, "mixin_bwd")
        un3 = sv["un"][None]
        gw_in = jnp.concatenate([
            _matmul_tn(un3, dq[None], 1, "dw_in_q")[0], _matmul_tn(un3, dkv[None], 1, "dw_in_kv")[0],
            _matmul_tn(un3, dcvg[None], 1, "dw_in_c")[0], _matmul_tn(un3, dgpre[None], 1, "dw_in_g")[0]], axis=1)
        mid_send = [_blocked(gw_in), _blocked(dcw[:CONV_WIDTH]), _blocked(gw_ao[0]), _blocked(gw_co[0]),
                    gw_o.reshape(N_DEV, D_MODEL // N_DEV, D_MODEL)]
        (dh, a1, dgu1, dg_n1), got = _ffn_bwd(
            dh1, sv["x0"], vec(norm_ffn1[l]), sv["g1"], sv["u1"], G["wi1"], G["wo1"], tm, "ffn1_bwd",
            _Exchange(mid_send, False))
        keep(set_mid, l, got)
        gwo1 = _matmul_tn(a1, dh1[None], FF_BLOCKS, "dw_ffn1_out", scale=0.5)
        gwi1 = _matmul_tn(sv["xn1"][None], dgu1.reshape(2 * FF_BLOCKS, S, FF_SHARD), 2 * FF_BLOCKS, "dw_ffn1_in")
        pending = [gwi1, gwo1.reshape(N_DEV, FF_SHARD // 2, D_MODEL)]
        small_g["norm_ffn1"][l] = dg_n1[0]
        small_g["norm_mix"][l] = dg_mix[0]
        small_g["q_norm"][l] = dqg[0, :HEAD_DIM]
        small_g["k_norm"][l] = dkg[0, :HEAD_DIM]
        small_g["sink"][l] = dsink[:, 0]
        small_g["conv_b"][l] = dcvec[0]
        small_g["conv_ln_g"][l] = dcvec[1]
        small_g["conv_ln_b"][l] = dcvec[2]
        small_g["norm_ffn2"][l] = dg_n2[0]
        small_g["norm_pe"][l] = dg_pe[0]

    keep(set_ffn1, 0, _exchange(pending, False, "grad_exchange_last"))
    grad_x = dh[None]
    drb = _bias_grad(dbias.reshape(N_HEADS, BIAS_COLS), onehot).T

    res = {}
    for n in SHARDED:
        parts = jnp.stack(recv[n], axis=1)
        shp = W[n].shape
        rows, cols = shp[0] * shp[1], shp[2]
        res[n] = [o.reshape(shp) for o in _adamw(
            parts.reshape(N_DEV, rows, cols), W[n].reshape(rows, cols), M[n].reshape(rows, cols),
            V[n].reshape(rows, cols), "adamw_" + n)]

    flat_g = jnp.concatenate([drb.reshape(-1)] + [jnp.stack(small_g[n]).reshape(-1) for n in SMALL[1:]])
    n_small = flat_g.shape[0]
    rows_s = -(-n_small // 1024 // 8) * 8
    pad = lambda a: jnp.pad(a, (0, rows_s * 1024 - n_small)).reshape(rows_s, 1024)
    flat = lambda d: pad(jnp.concatenate([d[n].reshape(-1) for n in SMALL]))
    (parts_s,) = _exchange([pad(flat_g)], True, "small_allgather")
    outs_s = _adamw(parts_s, flat(W), flat(M), flat(V), "adamw_small")
    off = 0
    for n in SMALL:
        size = W[n].size
        res[n] = [o.reshape(-1)[off:off + size].reshape(W[n].shape) for o in outs_s]
        off += size

    out = [loss, grad_x]
    for k in range(4):
        out += [res[n][k] for n in WEIGHTS]
    return tuple(out)
```

```python
import functools

import jax
import jax.numpy as jnp
import numpy as np
from jax import lax
from jax.experimental import pallas as pl
from jax.experimental.pallas import tpu as pltpu

F32 = jnp.float32
BF16 = jnp.bfloat16
MESH_ID = pl.DeviceIdType.MESH
AXES = ("x", "y", "c")
N_DEV = 8

D_MODEL = 1024
N_HEADS = 8
KV_HEADS = 2
HEAD_DIM = 64
Q_DIM = 512
KV_DIM = 128
BLOCK = 128
WIN = 3 * BLOCK
NUM_BUCKETS = 32
MAX_DISTANCE = 128
CONV_DIM = 512
CONV_WIDTH = 31
D_FF = 2816
FF_SHARD = 2 * D_FF // N_DEV
FF_BLOCKS = D_FF // FF_SHARD
QC_DIM = Q_DIM + 2 * KV_DIM + 2 * CONV_DIM
NEG_INF = -1e9
HALO = 16
CROWS = 64

ADAM_LR = 0.001
ADAM_B1 = 0.9
ADAM_B2 = 0.999
ADAM_EPS = 1e-08
ADAM_WD = 0.01
ADAM_STEP = 10

VMEM_LIMIT = 56 * 1024 * 1024
HI = lax.Precision.HIGHEST


def _params(sem):
    return pltpu.CompilerParams(dimension_semantics=sem, vmem_limit_bytes=VMEM_LIMIT)


def _dot(a, b, precision=None):
    return jnp.dot(a, b, preferred_element_type=F32, precision=precision)


def _dot_nt(a, b):
    return lax.dot_general(a, b, (((1,), (1,)), ((), ())), preferred_element_type=F32)


def _dot_tn(a, b):
    return lax.dot_general(a, b, (((0,), (0,)), ((), ())), preferred_element_type=F32)


def _sig(x):
    return 1.0 / (1.0 + jnp.exp(-x))


def _bf(x):
    return x.astype(BF16)


def _rms_fwd(x, gamma):
    r = lax.rsqrt(jnp.mean(x * x, axis=-1, keepdims=True) + 1e-6)
    return x * r * gamma, r


def _rms_bwd(dy, x, gamma):
    r = lax.rsqrt(jnp.mean(x * x, axis=-1, keepdims=True) + 1e-6)
    xhat = x * r
    dxhat = dy * gamma
    dx = r * (dxhat - xhat * jnp.mean(dxhat * xhat, axis=-1, keepdims=True))
    return dx, dy * xhat


class _Exchange:
    def __init__(self, arrs, gather):
        self.arrs, self.gather, self.n = list(arrs), gather, len(arrs)
        n = self.n
        self.out_shape = tuple(
            jax.ShapeDtypeStruct(((N_DEV,) + a.shape) if gather else a.shape, a.dtype) for a in self.arrs)
        self.specs = [pl.BlockSpec(memory_space=pl.ANY)] * n
        self.scratch = [pltpu.SemaphoreType.DMA((7 * n,)), pltpu.SemaphoreType.DMA((7 * n,)),
                        pltpu.SemaphoreType.DMA((n,))] if n else []

    def _copies(self, ins, outs, sems):
        n, gather = self.n, self.gather
        send_sems, recv_sems, local_sems = sems
        x, y, c = lax.axis_index("x"), lax.axis_index("y"), lax.axis_index("c")
        me = 4 * x + 2 * y + c
        copies = [pltpu.make_async_copy(ins[t] if gather else ins[t].at[me], outs[t].at[me], local_sems.at[t])
                  for t in range(n)]
        for d in range(1, N_DEV):
            px = 1 - x if d & 4 else x
            py = 1 - y if d & 2 else y
            pc = 1 - c if d & 1 else c
            peer = 4 * px + 2 * py + pc
            for t in range(n):
                k = (d - 1) * n + t
                copies.append(pltpu.make_async_remote_copy(
                    src_ref=ins[t] if gather else ins[t].at[peer], dst_ref=outs[t].at[me],
                    send_sem=send_sems.at[k], recv_sem=recv_sems.at[k],
                    device_id=(px, py, pc), device_id_type=MESH_ID))
        return copies

    def start(self, ins, outs, sems):
        for cp in self._copies(ins, outs, sems):
            cp.start()

    def wait(self, ins, outs, sems):
        for cp in self._copies(ins, outs, sems):
            cp.wait()


NO_EXCHANGE = _Exchange([], True)


def _exchange(arrs, gather, name):
    ex = _Exchange(arrs, gather)
    n = ex.n

    def body(*refs):
        ins, outs, sems = refs[:n], refs[n:2 * n], refs[2 * n:]
        ex.start(ins, outs, sems)
        ex.wait(ins, outs, sems)

    return pl.pallas_call(
        body, name=name, out_shape=ex.out_shape, in_specs=ex.specs, out_specs=tuple(ex.specs),
        scratch_shapes=ex.scratch,
    )(*ex.arrs)


def _carrier_call(body, ex, first_last, name, out_shape, grid, in_specs, out_specs, scratch_shapes, args):
    n_in, n_out, n_scr, n = len(in_specs), len(out_shape), len(scratch_shapes), ex.n

    def full(*refs):
        a, ci = refs[:n_in], refs[n_in:n_in + n]
        o = refs[n_in + n:n_in + n + n_out]
        co = refs[n_in + n + n_out:n_in + 2 * n + n_out]
        scr = refs[n_in + 2 * n + n_out:n_in + 2 * n + n_out + n_scr]
        sems = refs[n_in + 2 * n + n_out + n_scr:]
        first, last = first_last()
        if n:
            @pl.when(first)
            def _():
                ex.start(ci, co, sems)

        body(*a, *o, *scr)
        if n:
            @pl.when(last)
            def _():
                ex.wait(ci, co, sems)

    outs = pl.pallas_call(
        full, name=name, out_shape=tuple(out_shape) + ex.out_shape, grid=grid,
        in_specs=list(in_specs) + ex.specs, out_specs=tuple(out_specs) + tuple(ex.specs),
        scratch_shapes=list(scratch_shapes) + ex.scratch,
        compiler_params=_params(("arbitrary",) * len(grid)),
    )(*args, *ex.arrs)
    return outs[:n_out], outs[n_out:]


def _matmul_tn(a, b, nb, name, scale=1.0, out_dtype=BF16, ts=1024, ex=None):
    ba, S, K = a.shape
    bb, _, N = b.shape
    ts = min(ts, S)
    tn = N if N <= 1024 else 1024
    assert N % tn == 0 and S % ts == 0
    ns = S // ts

    def body(a_ref, b_ref, o_ref, acc):
        s = pl.program_id(2)

        @pl.when(s == 0)
        def _():
            acc[...] = jnp.zeros_like(acc)

        acc[...] += _dot_tn(_bf(a_ref[...]), _bf(b_ref[...]))

        @pl.when(s == ns - 1)
        def _():
            o_ref[...] = (acc[...] * scale).astype(out_dtype)

    in_specs = [pl.BlockSpec((None, ts, K), (lambda i, j, s: (i, s, 0)) if ba > 1 else (lambda i, j, s: (0, s, 0))),
                pl.BlockSpec((None, ts, tn), (lambda i, j, s: (i, s, j)) if bb > 1 else (lambda i, j, s: (0, s, j)))]
    out_spec = pl.BlockSpec((None, K, tn), lambda i, j, s: (i, 0, j))
    out_shape = jax.ShapeDtypeStruct((nb, K, N), out_dtype)
    scratch = [pltpu.VMEM((K, tn), F32)]
    grid = (nb, N // tn, ns)
    if ex is not None:
        (out,), got = _carrier_call(body, ex, _grid_ends(*grid), name, out_shape=(out_shape,), grid=grid,
                                    in_specs=in_specs, out_specs=(out_spec,), scratch_shapes=scratch, args=(a, b))
        return out, got
    return pl.pallas_call(
        body, name=name, out_shape=out_shape, grid=grid, in_specs=in_specs, out_specs=out_spec,
        scratch_shapes=scratch, compiler_params=_params(("parallel", "parallel", "arbitrary")),
    )(a, b)


def _ffn_specs(tm):
    wg = pl.BlockSpec((None, D_MODEL, FF_SHARD), lambda i, j: (j, 0, 0))
    wu = pl.BlockSpec((None, D_MODEL, FF_SHARD), lambda i, j: (j + FF_BLOCKS, 0, 0))
    wo = pl.BlockSpec((2, FF_SHARD // 2, D_MODEL), lambda i, j: (j, 0, 0))
    row = pl.BlockSpec((tm, D_MODEL), lambda i, j: (i, 0))
    vec = pl.BlockSpec((1, D_MODEL), lambda i, j: (0, 0))
    hid = pl.BlockSpec((None, tm, FF_SHARD), lambda i, j: (j, i, 0))
    return wg, wu, wo, row, vec, hid


def _grid_ends(*grid):
    def first_last():
        first, last = None, None
        for d, n in enumerate(grid):
            i = pl.program_id(d)
            first = (i == 0) if first is None else first & (i == 0)
            last = (i == n - 1) if last is None else last & (i == n - 1)
        return first, last
    return first_last


def _grid2_ends(ni, nj):
    return _grid_ends(ni, nj)


def _grid1_ends(ni):
    return _grid_ends(ni)


def _ffn_fwd(x, gamma, wi, wo, tm, name, ex=NO_EXCHANGE):
    S = x.shape[0]
    wg_s, wu_s, wo_s, row, vec, hid = _ffn_specs(tm)

    def body(x_ref, g_ref, wg_ref, wu_ref, wo_ref, y_ref, xn_ref, gs_ref, us_ref, xn_s, acc):
        j = pl.program_id(1)

        @pl.when(j == 0)
        def _():
            xn = _bf(_rms_fwd(x_ref[...], g_ref[...])[0])
            xn_s[...] = xn
            xn_ref[...] = xn
            acc[...] = jnp.zeros_like(acc)

        xn = xn_s[...]
        g = _dot(xn, wg_ref[...])
        u = _dot(xn, wu_ref[...])
        gs_ref[...] = _bf(g)
        us_ref[...] = _bf(u)
        a = g * _sig(g) * u
        acc[...] += _dot(_bf(a), wo_ref[...].reshape(FF_SHARD, D_MODEL))

        @pl.when(j == FF_BLOCKS - 1)
        def _():
            y_ref[...] = x_ref[...] + 0.5 * acc[...]

    return _carrier_call(
        body, ex, _grid2_ends(S // tm, FF_BLOCKS), name,
        out_shape=(jax.ShapeDtypeStruct((S, D_MODEL), F32), jax.ShapeDtypeStruct((S, D_MODEL), BF16),
                   jax.ShapeDtypeStruct((FF_BLOCKS, S, FF_SHARD), BF16),
                   jax.ShapeDtypeStruct((FF_BLOCKS, S, FF_SHARD), BF16)),
        grid=(S // tm, FF_BLOCKS),
        in_specs=[row, vec, wg_s, wu_s, wo_s],
        out_specs=(row, row, hid, hid),
        scratch_shapes=[pltpu.VMEM((tm, D_MODEL), BF16), pltpu.VMEM((tm, D_MODEL), F32)],
        args=(x, gamma, wi, wi, wo))


def _ffn_bwd(dy, x, gamma, gs, us, wi, wo, tm, name, ex=NO_EXCHANGE):
    S = x.shape[0]
    wg_s, wu_s, wo_s, row, vec, hid = _ffn_specs(tm)
    dgu_s = pl.BlockSpec((2, None, tm, FF_SHARD), lambda i, j: (0, j, i, 0))

    def body(dy_ref, x_ref, g_ref, gs_ref, us_ref, wg_ref, wu_ref, wo_ref,
             dx_ref, a_ref, dgu_ref, dgam_ref, dyh_s, acc):
        i, j = pl.program_id(0), pl.program_id(1)

        @pl.when(j == 0)
        def _():
            dyh_s[...] = _bf(0.5 * dy_ref[...])
            acc[...] = jnp.zeros_like(acc)

        @pl.when((i == 0) & (j == 0))
        def _():
            dgam_ref[...] = jnp.zeros_like(dgam_ref)

        da = _dot_nt(dyh_s[...], wo_ref[...].reshape(FF_SHARD, D_MODEL))
        g = gs_ref[...].astype(F32)
        u = us_ref[...].astype(F32)
        sg = _sig(g)
        sl = g * sg
        a_ref[...] = _bf(sl * u)
        dg = _bf(da * u * (sg * (1.0 + g * (1.0 - sg))))
        du = _bf(da * sl)
        dgu_ref[0] = dg
        dgu_ref[1] = du
        acc[...] += _dot_nt(dg, wg_ref[...]) + _dot_nt(du, wu_ref[...])

        @pl.when(j == FF_BLOCKS - 1)
        def _():
            dx, dgam = _rms_bwd(acc[...], x_ref[...], g_ref[...])
            dx_ref[...] = dy_ref[...] + dx
            dgam_ref[...] += jnp.sum(dgam, axis=0, keepdims=True)

    return _carrier_call(
        body, ex, _grid2_ends(S // tm, FF_BLOCKS), name,
        out_shape=(jax.ShapeDtypeStruct((S, D_MODEL), F32),
                   jax.ShapeDtypeStruct((FF_BLOCKS, S, FF_SHARD), BF16),
                   jax.ShapeDtypeStruct((2, FF_BLOCKS, S, FF_SHARD), BF16),
                   jax.ShapeDtypeStruct((1, D_MODEL), F32)),
        grid=(S // tm, FF_BLOCKS),
        in_specs=[row, row, vec, hid, hid, wg_s, wu_s, wo_s],
        out_specs=(row, hid, dgu_s, vec),
        scratch_shapes=[pltpu.VMEM((tm, D_MODEL), BF16), pltpu.VMEM((tm, D_MODEL), F32)],
        args=(dy, x, gamma, gs, us, wi, wi, wo))


def _mixin_fwd(h, gamma, w_qc, tm, name):
    S = h.shape[0]
    nqkv = Q_DIM + 2 * KV_DIM

    def body(h_ref, g_ref, w_ref, un_ref, qkv_ref, cvg_ref):
        un = _bf(_rms_fwd(h_ref[...], g_ref[...])[0])
        un_ref[...] = un
        z = _dot(un, w_ref[...])
        qkv_ref[...] = z[:, :nqkv]
        cvg_ref[...] = z[:, nqkv:]

    row = lambda w: pl.BlockSpec((tm, w), lambda i: (i, 0))
    return pl.pallas_call(
        body, name=name,
        out_shape=(jax.ShapeDtypeStruct((S, D_MODEL), BF16), jax.ShapeDtypeStruct((S, nqkv), F32),
                   jax.ShapeDtypeStruct((S, 2 * CONV_DIM), F32)),
        grid=(S // tm,),
        in_specs=[row(D_MODEL), pl.BlockSpec((1, D_MODEL), lambda i: (0, 0)),
                  pl.BlockSpec((D_MODEL, QC_DIM), lambda i: (0, 0))],
        out_specs=(row(D_MODEL), row(nqkv), row(2 * CONV_DIM)),
        compiler_params=_params(("parallel",)),
    )(h, gamma, w_qc)


def _mixin_bwd(dh2, h1, gamma, dq, dkv, dcvg, dgpre, w_qc, w_g, tm, name):
    S = h1.shape[0]
    nqkv = Q_DIM + 2 * KV_DIM

    def body(dh2_ref, h1_ref, g_ref, dq_ref, dkv_ref, dcvg_ref, dgp_ref, wqc_ref, wg_ref, dh1_ref, dgam_ref):
        @pl.when(pl.program_id(0) == 0)
        def _():
            dgam_ref[...] = jnp.zeros_like(dgam_ref)

        wqc = wqc_ref[...]
        dun = _dot_nt(_bf(dq_ref[...]), wqc[:, :Q_DIM])
        dun += _dot_nt(_bf(dkv_ref[...]), wqc[:, Q_DIM:nqkv])
        dun += _dot_nt(_bf(dcvg_ref[...]), wqc[:, nqkv:])
        dun += _dot_nt(dgp_ref[...], wg_ref[...])
        dx, dgam = _rms_bwd(dun, h1_ref[...], g_ref[...])
        dh1_ref[...] = dh2_ref[...] + dx
        dgam_ref[...] += jnp.sum(dgam, axis=0, keepdims=True)

    row = lambda w: pl.BlockSpec((tm, w), lambda i: (i, 0))
    vec = pl.BlockSpec((1, D_MODEL), lambda i: (0, 0))
    return pl.pallas_call(
        body, name=name,
        out_shape=(jax.ShapeDtypeStruct((S, D_MODEL), F32), jax.ShapeDtypeStruct((1, D_MODEL), F32)),
        grid=(S // tm,),
        in_specs=[row(D_MODEL), row(D_MODEL), vec, row(Q_DIM), row(2 * KV_DIM), row(2 * CONV_DIM),
                  row(2 * D_MODEL), pl.BlockSpec((D_MODEL, QC_DIM), lambda i: (0, 0)),
                  pl.BlockSpec((D_MODEL, 2 * D_MODEL), lambda i: (0, 0))],
        out_specs=(row(D_MODEL), vec),
        compiler_params=_params(("arbitrary",)),
    )(dh2, h1, gamma, dq, dkv, dcvg, dgpre, w_qc, w_g)


TQ = 512
QB = TQ // BLOCK


def _attn_in_specs(S):
    nkb = S // BLOCK
    return [
        pl.BlockSpec((TQ, Q_DIM), lambda i: (i, 0)),
        pl.BlockSpec((BLOCK, 2 * KV_DIM), lambda i: (jnp.maximum(i * QB - 1, 0), Q_DIM // (2 * KV_DIM))),
        pl.BlockSpec((TQ, 2 * KV_DIM), lambda i: (i, Q_DIM // (2 * KV_DIM))),
        pl.BlockSpec((BLOCK, 2 * KV_DIM), lambda i: (jnp.minimum(i * QB + QB, nkb - 1), Q_DIM // (2 * KV_DIM))),
        pl.BlockSpec((1, 128), lambda i: (0, 0)),
        pl.BlockSpec((1, KV_DIM), lambda i: (0, 0)),
        pl.BlockSpec((N_HEADS, 128), lambda i: (0, 0)),
        pl.BlockSpec((N_HEADS, BLOCK, WIN), lambda i: (0, 0, 0)),
    ]


GROUP_ROWS = 4 * BLOCK


def _half_rstd(x, low):
    x2 = x * x
    z = jnp.zeros_like(x2)
    r0 = lax.rsqrt(jnp.sum(jnp.where(low, x2, z), axis=-1, keepdims=True) * (1.0 / HEAD_DIM) + 1e-6)
    r1 = lax.rsqrt(jnp.sum(jnp.where(low, z, x2), axis=-1, keepdims=True) * (1.0 / HEAD_DIM) + 1e-6)
    return jnp.where(low, r0, r1)


def _kv_windows(kvp_ref, kvc_ref, kvn_ref, kg_ref, low):
    kv = jnp.concatenate([kvp_ref[...], kvc_ref[...], kvn_ref[...]], axis=0)
    k, v = kv[:, :KV_DIM], kv[:, KV_DIM:]
    kn = k * _half_rstd(k, low) * kg_ref[...]
    kr, vr = pltpu.roll(kn, HEAD_DIM, 1), pltpu.roll(v, HEAD_DIM, 1)
    kdup = [_bf(jnp.where(low, kn, kr)), _bf(jnp.where(low, kr, kn))]
    vdup = [_bf(jnp.where(low, v, vr)), _bf(jnp.where(low, vr, v))]
    return kdup, vdup


def _stack_heads(x_ref, t, kh, low):
    rows = slice(t * BLOCK, (t + 1) * BLOCK)
    xa = x_ref[rows, 256 * kh:256 * kh + 128]
    xb = x_ref[rows, 256 * kh + 128:256 * kh + 256]
    z = jnp.zeros_like(xa)
    return jnp.concatenate([jnp.where(low, xa, z), jnp.where(low, z, xa),
                            jnp.where(low, xb, z), jnp.where(low, z, xb)], axis=0)


def _stacked_q(q_ref, qg_ref, t, kh, low):
    qraw = _stack_heads(q_ref, t, kh, low)
    rq = lax.rsqrt(jnp.sum(qraw * qraw, axis=-1, keepdims=True) * (1.0 / HEAD_DIM) + 1e-6)
    return qraw, rq, _bf(qraw * rq * (qg_ref[...] * (HEAD_DIM ** -0.5)))


def _unstack_heads(ov, low):
    return (jnp.where(low, ov[0:128], ov[128:256]), jnp.where(low, ov[256:384], ov[384:512]))


def _edge_bias(i, t, S):
    kpos = i * TQ + (t - 1) * BLOCK + lax.broadcasted_iota(jnp.int32, (1, WIN), 1)
    return jnp.where((kpos < 0) | (kpos >= S), NEG_INF, 0.0)


def _group_exp(lhs, kw, bias_ref, sink_ref, kh, edge):
    s = _dot_nt(lhs, kw) + bias_ref[4 * kh:4 * kh + 4].reshape(GROUP_ROWS, WIN)
    if edge is not None:
        s = s + edge
    sk = jnp.concatenate(
        [jnp.broadcast_to(sink_ref[4 * kh + r:4 * kh + r + 1, 0:1], (BLOCK, 1)) for r in range(4)], axis=0)
    m = jnp.maximum(jnp.max(s, axis=-1, keepdims=True), sk)
    return jnp.exp(s - m), jnp.exp(sk - m)


def _attn_fwd(qkv, qg, kg, sinkb, bias, name, ex=NO_EXCHANGE):
    S = qkv.shape[0]

    def body(q_ref, kvp_ref, kvc_ref, kvn_ref, qg_ref, kg_ref, sink_ref, bias_ref, o_ref):
        i = pl.program_id(0)
        low = lax.broadcasted_iota(jnp.int32, (1, 128), 1) < HEAD_DIM
        ones = jnp.ones((WIN, 128), BF16)
        kdup, vdup = _kv_windows(kvp_ref, kvc_ref, kvn_ref, kg_ref, low)
        for t in range(QB):
            edge = _edge_bias(i, t, S) if t in (0, QB - 1) else None
            rows = slice(t * BLOCK, (t + 1) * BLOCK)
            for kh in range(KV_HEADS):
                _, _, lhs = _stacked_q(q_ref, qg_ref, t, kh, low)
                kw = kdup[kh][t * BLOCK:t * BLOCK + WIN]
                vw = vdup[kh][t * BLOCK:t * BLOCK + WIN]
                e, es = _group_exp(lhs, kw, bias_ref, sink_ref, kh, edge)
                eb = _bf(e)
                ov = _dot(eb, vw) * (1.0 / (_dot(eb, ones) + es))
                oa, ob = _unstack_heads(ov, low)
                o_ref[rows, 256 * kh:256 * kh + 128] = _bf(oa)
                o_ref[rows, 256 * kh + 128:256 * kh + 256] = _bf(ob)

    return _carrier_call(
        body, ex, _grid1_ends(S // TQ), name, out_shape=(jax.ShapeDtypeStruct((S, Q_DIM), BF16),),
        grid=(S // TQ,), in_specs=_attn_in_specs(S),
        out_specs=(pl.BlockSpec((TQ, Q_DIM), lambda i: (i, 0)),), scratch_shapes=[],
        args=(qkv, qkv, qkv, qkv, qg, kg, sinkb, bias))


def _attn_bwd(do, qkv, qg, kg, sinkb, bias, dbias_in, name, ex=NO_EXCHANGE):
    S = qkv.shape[0]
    nkb = S // BLOCK
    nsteps = S // TQ

    def body(do_ref, q_ref, kvp_ref, kvc_ref, kvn_ref, qg_ref, kg_ref, sink_ref, bias_ref, dbin_ref,
             dq_ref, dkp_ref, dvp_ref, dbias_ref, dsink_ref, dqg_ref, dqg_s):
        i = pl.program_id(0)

        @pl.when(i == 0)
        def _():
            dbias_ref[...] = dbin_ref[...]
            dsink_ref[...] = jnp.zeros_like(dsink_ref)
            dqg_s[...] = jnp.zeros_like(dqg_s)

        low = lax.broadcasted_iota(jnp.int32, (1, 128), 1) < HEAD_DIM
        own = ((lax.broadcasted_iota(jnp.int32, (GROUP_ROWS, 128), 1) >> 6) & 1) == (
            (lax.broadcasted_iota(jnp.int32, (GROUP_ROWS, 128), 0) >> 7) & 1)
        gq = qg_ref[...] * (HEAD_DIM ** -0.5)
        ones = jnp.ones((WIN, 128), BF16)
        kdup, vdup = _kv_windows(kvp_ref, kvc_ref, kvn_ref, kg_ref, low)
        for t in range(QB):
            edge = _edge_bias(i, t, S) if t in (0, QB - 1) else None
            rows = slice(t * BLOCK, (t + 1) * BLOCK)
            dk_dup, dv_dup = [], []
            for kh in range(KV_HEADS):
                qraw, rq, lhs = _stacked_q(q_ref, qg_ref, t, kh, low)
                dos = _bf(_stack_heads(do_ref, t, kh, low))
                kw = kdup[kh][t * BLOCK:t * BLOCK + WIN]
                vw = vdup[kh][t * BLOCK:t * BLOCK + WIN]
                e, es = _group_exp(lhs, kw, bias_ref, sink_ref, kh, edge)
                inv = 1.0 / (_dot(_bf(e), ones) + es)
                pr = e * jnp.concatenate([inv] * (WIN // 128), axis=1)
                dpr = _dot_nt(dos, vw)
                delta = jnp.sum(pr * dpr, axis=-1, keepdims=True)
                ds = pr * (dpr - delta)
                dbias_ref[4 * kh:4 * kh + 4] += ds.reshape(4, BLOCK, WIN)
                dsk = es * inv[:, 0:1] * delta
                for r in range(4):
                    dsink_ref[4 * kh + r:4 * kh + r + 1, :] -= jnp.broadcast_to(
                        jnp.sum(dsk[r * BLOCK:(r + 1) * BLOCK], axis=0, keepdims=True), (1, 128))
                dsb = _bf(ds)
                dqs = _dot(dsb, kw)
                qhat = qraw * rq
                dxhat = jnp.where(own, dqs, 0.0) * gq
                dq_st = rq * (dxhat - qhat * (jnp.sum(dxhat * qhat, axis=-1, keepdims=True) * (1.0 / HEAD_DIM)))
                dq_ref[rows, 256 * kh:256 * kh + 128] = dq_st[0:128] + dq_st[128:256]
                dq_ref[rows, 256 * kh + 128:256 * kh + 256] = dq_st[256:384] + dq_st[384:512]
                dqg_s[...] += jnp.sum((dqs * qhat).reshape(GROUP_ROWS // 8, 8, 128), axis=0)
                dkx = _dot_tn(dsb, lhs)
                dvx = _dot_tn(_bf(pr), dos)
                dk_dup.append(dkx + pltpu.roll(dkx, HEAD_DIM, 1))
                dv_dup.append(dvx + pltpu.roll(dvx, HEAD_DIM, 1))
            dkp_ref[t] = jnp.where(low, dk_dup[0], dk_dup[1])
            dvp_ref[t] = jnp.where(low, dv_dup[0], dv_dup[1])

        @pl.when(i == nsteps - 1)
        def _():
            acc = dqg_s[...] * (HEAD_DIM ** -0.5)
            acc = acc + pltpu.roll(acc, HEAD_DIM, 1)
            dqg_ref[...] = jnp.broadcast_to(jnp.sum(acc, axis=0, keepdims=True), (8, 128))

    const2 = lambda shape: pl.BlockSpec(shape, lambda i: (0,) * len(shape))
    part = pl.BlockSpec((QB, WIN, KV_DIM), lambda i: (i, 0, 0))
    return _carrier_call(
        body, ex, _grid1_ends(nsteps), name,
        out_shape=(jax.ShapeDtypeStruct((S, Q_DIM), F32), jax.ShapeDtypeStruct((nkb, WIN, KV_DIM), F32),
                   jax.ShapeDtypeStruct((nkb, WIN, KV_DIM), F32),
                   jax.ShapeDtypeStruct((N_HEADS, BLOCK, WIN), F32), jax.ShapeDtypeStruct((N_HEADS, 128), F32),
                   jax.ShapeDtypeStruct((8, 128), F32)),
        grid=(nsteps,),
        in_specs=[pl.BlockSpec((TQ, Q_DIM), lambda i: (i, 0))] + _attn_in_specs(S)
        + [const2((N_HEADS, BLOCK, WIN))],
        out_specs=(pl.BlockSpec((TQ, Q_DIM), lambda i: (i, 0)), part, part,
                   const2((N_HEADS, BLOCK, WIN)), const2((N_HEADS, 128)), const2((8, 128))),
        scratch_shapes=[pltpu.VMEM((8, 128), F32)],
        args=(do, qkv, qkv, qkv, qkv, qg, kg, sinkb, bias, dbias_in))


def _kv_fold(dkp, dvp, qkv, kg, name):
    nkb = dkp.shape[0]
    S = nkb * BLOCK
    nsteps = S // TQ

    def body(kp_p, kp_c, kp_n, vp_p, vp_c, vp_n, kv_ref, kg_ref, dkv_ref, dkg_ref, dkg_s):
        i = pl.program_id(0)

        @pl.when(i == 0)
        def _():
            dkg_s[...] = jnp.zeros_like(dkg_s)

        def fold(p_ref, c_ref, n_ref):
            blocks = []
            for t in range(QB):
                acc = c_ref[t, BLOCK:2 * BLOCK, :]
                if t > 0:
                    acc = acc + c_ref[t - 1, 2 * BLOCK:, :]
                else:
                    acc = acc + jnp.where(i > 0, p_ref[0, 2 * BLOCK:, :], 0.0)
                if t < QB - 1:
                    acc = acc + c_ref[t + 1, :BLOCK, :]
                else:
                    acc = acc + jnp.where(i < nsteps - 1, n_ref[0, :BLOCK, :], 0.0)
                blocks.append(acc)
            return jnp.concatenate(blocks, axis=0)

        dkn = fold(kp_p, kp_c, kp_n)
        dv = fold(vp_p, vp_c, vp_n)
        k = kv_ref[:, :KV_DIM]
        low = lax.broadcasted_iota(jnp.int32, (1, 128), 1) < HEAD_DIM
        rk = _half_rstd(k, low)
        khat = k * rk
        dxhat = dkn * kg_ref[...]
        prod = dxhat * khat
        z = jnp.zeros_like(prod)
        mean = jnp.where(low, jnp.sum(jnp.where(low, prod, z), axis=-1, keepdims=True),
                         jnp.sum(jnp.where(low, z, prod), axis=-1, keepdims=True)) * (1.0 / HEAD_DIM)
        dkv_ref[:, :KV_DIM] = rk * (dxhat - khat * mean)
        dkv_ref[:, KV_DIM:] = dv
        dkg_s[...] += jnp.sum((dkn * khat).reshape(TQ // 8, 8, KV_DIM), axis=0)

        @pl.when(i == nsteps - 1)
        def _():
            acc = dkg_s[...] + pltpu.roll(dkg_s[...], HEAD_DIM, 1)
            dkg_ref[...] = jnp.broadcast_to(jnp.sum(acc, axis=0, keepdims=True), (8, 128))

    prev = pl.BlockSpec((1, WIN, KV_DIM), lambda i: (jnp.maximum(i * QB - 1, 0), 0, 0))
    cur = pl.BlockSpec((QB, WIN, KV_DIM), lambda i: (i, 0, 0))
    nxt = pl.BlockSpec((1, WIN, KV_DIM), lambda i: (jnp.minimum(i * QB + QB, nkb - 1), 0, 0))
    return pl.pallas_call(
        body, name=name,
        out_shape=(jax.ShapeDtypeStruct((S, 2 * KV_DIM), F32), jax.ShapeDtypeStruct((8, 128), F32)),
        grid=(nsteps,),
        in_specs=[prev, cur, nxt, prev, cur, nxt,
                  pl.BlockSpec((TQ, 2 * KV_DIM), lambda i: (i, Q_DIM // (2 * KV_DIM))),
                  pl.BlockSpec((1, KV_DIM), lambda i: (0, 0))],
        out_specs=(pl.BlockSpec((TQ, 2 * KV_DIM), lambda i: (i, 0)), pl.BlockSpec((8, 128), lambda i: (0, 0))),
        scratch_shapes=[pltpu.VMEM((8, KV_DIM), F32)],
        compiler_params=_params(("arbitrary",)),
    )(dkp, dkp, dkp, dvp, dvp, dvp, qkv, kg)


BIAS_COLS = BLOCK * WIN
BIAS_CHUNK = 6144


def _bias_table(rel_bias_t, onehot, band):
    def body(rb_ref, oh_ref, band_ref, o_ref):
        o_ref[...] = _dot(rb_ref[...], oh_ref[...], HI) + band_ref[...]

    return pl.pallas_call(
        body, name="bias_table", out_shape=jax.ShapeDtypeStruct((N_HEADS, BIAS_COLS), F32),
        grid=(BIAS_COLS // BIAS_CHUNK,),
        in_specs=[pl.BlockSpec((N_HEADS, NUM_BUCKETS), lambda i: (0, 0)),
                  pl.BlockSpec((NUM_BUCKETS, BIAS_CHUNK), lambda i: (0, i)),
                  pl.BlockSpec((1, BIAS_CHUNK), lambda i: (0, i))],
        out_specs=pl.BlockSpec((N_HEADS, BIAS_CHUNK), lambda i: (0, i)),
        compiler_params=_params(("parallel",)),
    )(rel_bias_t, onehot, band)


def _bias_grad(dbias, onehot):
    def body(db_ref, oh_ref, o_ref):
        @pl.when(pl.program_id(0) == 0)
        def _():
            o_ref[...] = jnp.zeros_like(o_ref)

        o_ref[...] += lax.dot_general(db_ref[...], oh_ref[...], (((1,), (1,)), ((), ())),
                                      preferred_element_type=F32, precision=HI)

    return pl.pallas_call(
        body, name="bias_grad", out_shape=jax.ShapeDtypeStruct((N_HEADS, NUM_BUCKETS), F32),
        grid=(BIAS_COLS // BIAS_CHUNK,),
        in_specs=[pl.BlockSpec((N_HEADS, BIAS_CHUNK), lambda i: (0, i)),
                  pl.BlockSpec((NUM_BUCKETS, BIAS_CHUNK), lambda i: (0, i))],
        out_specs=pl.BlockSpec((N_HEADS, NUM_BUCKETS), lambda i: (0, 0)),
        compiler_params=_params(("arbitrary",)),
    )(dbias, onehot)


def _bucket_onehot():
    half = NUM_BUCKETS // 2
    max_exact = half // 2
    rel = jnp.arange(WIN)[None, :] - BLOCK - jnp.arange(BLOCK)[:, None]
    n = jnp.abs(rel)
    ret = jnp.where(rel > 0, half, 0)
    nf = jnp.maximum(n, 1).astype(F32)
    large = max_exact + (jnp.log(nf / max_exact) / np.log(MAX_DISTANCE / max_exact)
                         * (half - max_exact)).astype(jnp.int32)
    large = jnp.minimum(large, half - 1)
    bucket = (ret + jnp.where(n < max_exact, n, large)).reshape(1, BIAS_COLS)
    band = jnp.where(n <= BLOCK, 0.0, NEG_INF).astype(F32).reshape(1, BIAS_COLS)
    return (bucket == jnp.arange(NUM_BUCKETS)[:, None]).astype(F32), band


def _halo_specs(tm, width, S):
    r = tm // HALO
    last = S // HALO - 1
    return [pl.BlockSpec((HALO, width), lambda i: (jnp.maximum(i * r - 1, 0), 0)),
            pl.BlockSpec((tm, width), lambda i: (i, 0)),
            pl.BlockSpec((HALO, width), lambda i: (jnp.minimum(i * r + r, last), 0))]


def _with_halo(p_ref, c_ref, n_ref):
    return jnp.concatenate([p_ref[...], c_ref[...], n_ref[...]], axis=0)


def _row_valid(i, tm, S):
    g = i * tm - HALO + lax.broadcasted_iota(jnp.int32, (tm + 2 * HALO, 1), 0)
    return (g >= 0) & (g < S)


def _shifted(x):
    n = x.shape[0]
    return [x if b == 0 else pltpu.roll(x, n - b, 0) for b in range(8)]


def _tap(sh, off, tm):
    a, b = off // 8, off % 8
    return sh[b][8 * a:8 * a + tm]


def _conv_fwd(cvg, cw, cb, lg, lb, tm, name, ex=NO_EXCHANGE):
    S = cvg.shape[0]

    def body(p_ref, c_ref, n_ref, cw_ref, cb_ref, lg_ref, lb_ref, act_ref, yc_ref):
        i = pl.program_id(0)
        z = _with_halo(p_ref, c_ref, n_ref)
        glu = jnp.where(_row_valid(i, tm, S), z[:, :CONV_DIM] * _sig(z[:, CONV_DIM:]), 0.0)
        sh = _shifted(glu)
        y = jnp.zeros((tm, CONV_DIM), F32) + cb_ref[...]
        for w in range(CONV_WIDTH):
            y = y + _tap(sh, w + 1, tm) * cw_ref[w:w + 1, :]
        yc_ref[...] = y
        mu = jnp.mean(y, axis=-1, keepdims=True)
        yc = y - mu
        rstd = lax.rsqrt(jnp.mean(yc * yc, axis=-1, keepdims=True) + 1e-5)
        ln = yc * rstd * lg_ref[...] + lb_ref[...]
        act_ref[...] = _bf(ln * _sig(ln))

    vec = pl.BlockSpec((1, CONV_DIM), lambda i: (0, 0))
    row = pl.BlockSpec((tm, CONV_DIM), lambda i: (i, 0))
    return _carrier_call(
        body, ex, _grid_ends(S // tm), name,
        out_shape=(jax.ShapeDtypeStruct((S, CONV_DIM), BF16), jax.ShapeDtypeStruct((S, CONV_DIM), F32)),
        grid=(S // tm,),
        in_specs=_halo_specs(tm, 2 * CONV_DIM, S) + [pl.BlockSpec((32, CONV_DIM), lambda i: (0, 0)), vec, vec, vec],
        out_specs=(row, row), scratch_shapes=[], args=(cvg, cvg, cvg, cw, cb, lg, lb))


def _conv_bwd(dact, yconv, cvg, cw, lg, lb, tm, name, ex=NO_EXCHANGE):
    S = cvg.shape[0]
    nsteps = S // tm

    def body(dp, dc, dn, yp, yc_, yn, zp, zc, zn, cw_ref, lg_ref, lb_ref,
             dz_ref, dcw_ref, dvec_ref, dcw_s, dvec_s, shg_s, shd_s):
        i = pl.program_id(0)

        @pl.when(i == 0)
        def _():
            dcw_s[...] = jnp.zeros_like(dcw_s)
            dvec_s[...] = jnp.zeros_like(dvec_s)

        valid = _row_valid(i, tm, S)
        own = (lax.broadcasted_iota(jnp.int32, (tm + 2 * HALO, 1), 0) >= HALO) & (
            lax.broadcasted_iota(jnp.int32, (tm + 2 * HALO, 1), 0) < HALO + tm)
        y = _with_halo(yp, yc_, yn)
        dact_ = _with_halo(dp, dc, dn)
        mu = jnp.mean(y, axis=-1, keepdims=True)
        ycen = y - mu
        rstd = lax.rsqrt(jnp.mean(ycen * ycen, axis=-1, keepdims=True) + 1e-5)
        yhat = ycen * rstd
        ln = yhat * lg_ref[...] + lb_ref[...]
        sg = _sig(ln)
        dln = dact_ * (sg * (1.0 + ln * (1.0 - sg)))
        dyhat = dln * lg_ref[...]
        dy = rstd * (dyhat - jnp.mean(dyhat, axis=-1, keepdims=True)
                     - yhat * jnp.mean(dyhat * yhat, axis=-1, keepdims=True))
        dy = jnp.where(valid, dy, 0.0)
        dln_own = jnp.where(own, dln, 0.0)
        nr = (tm + 2 * HALO) // 8
        dvec_s[0] += jnp.sum(jnp.where(own, dy, 0.0).reshape(nr, 8, CONV_DIM), axis=0)
        dvec_s[1] += jnp.sum((dln_own * yhat).reshape(nr, 8, CONV_DIM), axis=0)
        dvec_s[2] += jnp.sum(dln_own.reshape(nr, 8, CONV_DIM), axis=0)
        z = _with_halo(zp, zc, zn)
        glu = jnp.where(valid, z[:, :CONV_DIM] * _sig(z[:, CONV_DIM:]), 0.0)
        for b, (g_b, d_b) in enumerate(zip(_shifted(glu), _shifted(dy))):
            shg_s[b] = g_b
            shd_s[b] = d_b
        for cb in range(CONV_DIM // 128):
            lanes = slice(128 * cb, 128 * (cb + 1))
            for rb in range(tm // CROWS):
                r0 = rb * CROWS
                dy_own = shd_s[0, HALO + r0:HALO + r0 + CROWS, lanes]
                dglu = jnp.zeros((CROWS, 128), F32)
                for w in range(CONV_WIDTH):
                    a, b = divmod(CONV_WIDTH - w, 8)
                    dglu = dglu + shd_s[b, 8 * a + r0:8 * a + r0 + CROWS, lanes] * cw_ref[w:w + 1, lanes]
                    a, b = divmod(w + 1, 8)
                    prod = dy_own * shg_s[b, 8 * a + r0:8 * a + r0 + CROWS, lanes]
                    dcw_s[w, :, lanes] += jnp.sum(prod.reshape(CROWS // 8, 8, 128), axis=0)
                cv = zc[r0:r0 + CROWS, lanes]
                sg_o = _sig(zc[r0:r0 + CROWS, CONV_DIM + 128 * cb:CONV_DIM + 128 * (cb + 1)])
                dz_ref[r0:r0 + CROWS, lanes] = dglu * sg_o
                dz_ref[r0:r0 + CROWS, CONV_DIM + 128 * cb:CONV_DIM + 128 * (cb + 1)] = (
                    dglu * cv * sg_o * (1.0 - sg_o))

        @pl.when(i == nsteps - 1)
        def _():
            dcw_ref[...] = jnp.sum(dcw_s[...], axis=1)
            dvec_ref[...] = jnp.sum(dvec_s[...], axis=1)

    vec = pl.BlockSpec((1, CONV_DIM), lambda i: (0, 0))
    return _carrier_call(
        body, ex, _grid_ends(nsteps), name,
        out_shape=(jax.ShapeDtypeStruct((S, 2 * CONV_DIM), F32), jax.ShapeDtypeStruct((32, CONV_DIM), F32),
                   jax.ShapeDtypeStruct((8, CONV_DIM), F32)),
        grid=(nsteps,),
        in_specs=_halo_specs(tm, CONV_DIM, S) + _halo_specs(tm, CONV_DIM, S) + _halo_specs(tm, 2 * CONV_DIM, S)
        + [pl.BlockSpec((32, CONV_DIM), lambda i: (0, 0)), vec, vec],
        out_specs=(pl.BlockSpec((tm, 2 * CONV_DIM), lambda i: (i, 0)),
                   pl.BlockSpec((32, CONV_DIM), lambda i: (0, 0)), pl.BlockSpec((8, CONV_DIM), lambda i: (0, 0))),
        scratch_shapes=[pltpu.VMEM((32, 8, CONV_DIM), F32), pltpu.VMEM((8, 8, CONV_DIM), F32),
                        pltpu.VMEM((8, tm + 2 * HALO, CONV_DIM), F32), pltpu.VMEM((8, tm + 2 * HALO, CONV_DIM), F32)],
        args=(dact, dact, dact, yconv, yconv, yconv, cvg, cvg, cvg, cw, lg, lb))


def _merge_parts(un, o, cact, wg_ref, wao_ref, wco_ref):
    g = _dot(un, wg_ref[...])
    ga, gc = _sig(g[:, :D_MODEL]), _sig(g[:, D_MODEL:])
    ya = _dot(o, wao_ref[...])
    yc = _dot(cact, wco_ref[...])
    return ga, gc, ya, yc


def _merge_specs(tm):
    row = lambda w: pl.BlockSpec((tm, w), lambda i: (i, 0))
    full = lambda a, b: pl.BlockSpec((a, b), lambda i: (0, 0))
    weights = [full(D_MODEL, 2 * D_MODEL), full(Q_DIM, D_MODEL), full(CONV_DIM, D_MODEL), full(D_MODEL, D_MODEL)]
    return row, weights


def _merge_fwd(h1, un, o, cact, w_g, w_ao, w_co, w_o, tm, name, ex=NO_EXCHANGE):
    S = h1.shape[0]
    row, weights = _merge_specs(tm)

    def body(h1_ref, un_ref, o_ref, c_ref, wg_ref, wao_ref, wco_ref, wo_ref, h2_ref):
        ga, gc, ya, yc = _merge_parts(un_ref[...], o_ref[...], c_ref[...], wg_ref, wao_ref, wco_ref)
        h2_ref[...] = h1_ref[...] + _dot(_bf(ga * ya + gc * yc), wo_ref[...])

    return _carrier_call(
        body, ex, _grid_ends(S // tm), name, out_shape=(jax.ShapeDtypeStruct((S, D_MODEL), F32),),
        grid=(S // tm,),
        in_specs=[row(D_MODEL), row(D_MODEL), row(Q_DIM), row(CONV_DIM)] + weights,
        out_specs=(row(D_MODEL),), scratch_shapes=[], args=(h1, un, o, cact, w_g, w_ao, w_co, w_o))


def _merge_bwd(dh2, un, o, cact, w_g, w_ao, w_co, w_o, tm, name):
    S = dh2.shape[0]
    row, weights = _merge_specs(tm)

    def body(dh2_ref, un_ref, o_ref, c_ref, wg_ref, wao_ref, wco_ref, wo_ref,
             do_ref, dc_ref, mix_ref, dya_ref, dyc_ref, dgp_ref):
        ga, gc, ya, yc = _merge_parts(un_ref[...], o_ref[...], c_ref[...], wg_ref, wao_ref, wco_ref)
        mix_ref[...] = _bf(ga * ya + gc * yc)
        dmix = _dot_nt(_bf(dh2_ref[...]), wo_ref[...])
        dya = _bf(dmix * ga)
        dyc = _bf(dmix * gc)
        dya_ref[...] = dya
        dyc_ref[...] = dyc
        dgp_ref[:, :D_MODEL] = _bf(dmix * ya * ga * (1.0 - ga))
        dgp_ref[:, D_MODEL:] = _bf(dmix * yc * gc * (1.0 - gc))
        do_ref[...] = _dot_nt(dya, wao_ref[...])
        dc_ref[...] = _dot_nt(dyc, wco_ref[...])

    return pl.pallas_call(
        body, name=name,
        out_shape=(jax.ShapeDtypeStruct((S, Q_DIM), F32), jax.ShapeDtypeStruct((S, CONV_DIM), F32),
                   jax.ShapeDtypeStruct((S, D_MODEL), BF16), jax.ShapeDtypeStruct((S, D_MODEL), BF16),
                   jax.ShapeDtypeStruct((S, D_MODEL), BF16), jax.ShapeDtypeStruct((S, 2 * D_MODEL), BF16)),
        grid=(S // tm,),
        in_specs=[row(D_MODEL), row(D_MODEL), row(Q_DIM), row(CONV_DIM)] + weights,
        out_specs=(row(Q_DIM), row(CONV_DIM), row(D_MODEL), row(D_MODEL), row(D_MODEL), row(2 * D_MODEL)),
        compiler_params=_params(("parallel",)),
    )(dh2, un, o, cact, w_g, w_ao, w_co, w_o)


def _pe_specs(tm, layer):
    row = pl.BlockSpec((tm, D_MODEL), lambda i: (i, 0))
    vec = pl.BlockSpec((1, D_MODEL), lambda i: (0, 0))
    p_s = pl.BlockSpec((None, None, tm, 256), lambda i: (layer, 0, i, 0))
    wpp = pl.BlockSpec((256, D_MODEL), lambda i: (0, 0))
    wpg = pl.BlockSpec((D_MODEL, D_MODEL), lambda i: (0, 0))
    return row, vec, p_s, wpp, wpg


def _pe_fwd(h, gamma, p, layer, w_pp, w_pg, tm, name):
    S = h.shape[0]
    row, vec, p_s, wpp, wpg = _pe_specs(tm, layer)

    def body(h_ref, g_ref, p_ref, wpp_ref, wpg_ref, x_ref, hn_ref):
        hn = _bf(_rms_fwd(h_ref[...], g_ref[...])[0])
        hn_ref[...] = hn
        gate = _sig(_dot(hn, wpg_ref[...]))
        x_ref[...] = h_ref[...] + _dot(_bf(p_ref[...]), wpp_ref[...]) * gate

    return pl.pallas_call(
        body, name=name,
        out_shape=(jax.ShapeDtypeStruct((S, D_MODEL), F32), jax.ShapeDtypeStruct((S, D_MODEL), BF16)),
        grid=(S // tm,), in_specs=[row, vec, p_s, wpp, wpg], out_specs=(row, row),
        compiler_params=_params(("parallel",)),
    )(h, gamma, p, w_pp, w_pg)


def _pe_bwd(dx, h, gamma, hn, p, layer, w_pp, w_pg, tm, name):
    S = h.shape[0]
    row, vec, p_s, wpp, wpg = _pe_specs(tm, layer)

    def body(dx_ref, h_ref, g_ref, hn_ref, p_ref, wpp_ref, wpg_ref, dh_ref, dgp_ref, dpr_ref, dgam_ref):
        @pl.when(pl.program_id(0) == 0)
        def _():
            dgam_ref[...] = jnp.zeros_like(dgam_ref)

        dxv = dx_ref[...]
        gate = _sig(_dot(hn_ref[...], wpg_ref[...]))
        proj = _dot(_bf(p_ref[...]), wpp_ref[...])
        dpr_ref[...] = _bf(dxv * gate)
        dgp = _bf(dxv * proj * gate * (1.0 - gate))
        dgp_ref[...] = dgp
        dxn, dgam = _rms_bwd(_dot_nt(dgp, wpg_ref[...]), h_ref[...], g_ref[...])
        dh_ref[...] = dxv + dxn
        dgam_ref[...] += jnp.sum(dgam, axis=0, keepdims=True)

    return pl.pallas_call(
        body, name=name,
        out_shape=(jax.ShapeDtypeStruct((S, D_MODEL), F32), jax.ShapeDtypeStruct((S, D_MODEL), BF16),
                   jax.ShapeDtypeStruct((S, D_MODEL), BF16), jax.ShapeDtypeStruct((1, D_MODEL), F32)),
        grid=(S // tm,), in_specs=[row, row, vec, row, p_s, wpp, wpg], out_specs=(row, row, row, vec),
        compiler_params=_params(("arbitrary",)),
    )(dx, h, gamma, hn, p, w_pp, w_pg)


def _loss_head(y, target, tm):
    S = y.shape[0]

    def body(y_ref, t_ref, dy_ref, l_ref):
        @pl.when(pl.program_id(0) == 0)
        def _():
            l_ref[...] = jnp.zeros_like(l_ref)

        diff = y_ref[...] - t_ref[...]
        dy_ref[...] = diff * (1.0 / D_MODEL)
        sq = jnp.sum((diff * diff).reshape(tm // 8, 8, D_MODEL), axis=0)
        part = sq[:, 0:128]
        for k in range(1, D_MODEL // 128):
            part = part + sq[:, 128 * k:128 * (k + 1)]
        l_ref[...] += part

    row = pl.BlockSpec((tm, D_MODEL), lambda i: (i, 0))
    return pl.pallas_call(
        body, name="loss_head",
        out_shape=(jax.ShapeDtypeStruct((S, D_MODEL), F32), jax.ShapeDtypeStruct((8, 128), F32)),
        grid=(S // tm,), in_specs=[row, row], out_specs=(row, pl.BlockSpec((8, 128), lambda i: (0, 0))),
        compiler_params=_params(("arbitrary",)),
    )(y, target)


def _adamw(parts, w, m, v, name):
    R, C = w.shape
    tr = R
    for cand in (256, 128, 64, 32, 16):
        if R % cand == 0:
            tr = cand
            break

    def body(p_ref, w_ref, m_ref, v_ref, g_ref, d_ref, nm_ref, nv_ref):
        g = p_ref[0].astype(F32)
        for k in range(1, N_DEV):
            g = g + p_ref[k].astype(F32)
        g_ref[...] = g
        nm = ADAM_B1 * m_ref[...] + (1.0 - ADAM_B1) * g
        nv = ADAM_B2 * v_ref[...] + (1.0 - ADAM_B2) * (g * g)
        nm_ref[...] = nm
        nv_ref[...] = nv
        m_hat = nm / (1.0 - ADAM_B1 ** ADAM_STEP)
        v_hat = nv / (1.0 - ADAM_B2 ** ADAM_STEP)
        d_ref[...] = -ADAM_LR * (m_hat / (jnp.sqrt(v_hat) + ADAM_EPS) + ADAM_WD * w_ref[...])

    blk = pl.BlockSpec((tr, C), lambda i: (i, 0))
    out = jax.ShapeDtypeStruct((R, C), F32)
    return pl.pallas_call(
        body, name=name, out_shape=(out, out, out, out), grid=(R // tr,),
        in_specs=[pl.BlockSpec((N_DEV, tr, C), lambda i: (0, i, 0)), blk, blk, blk],
        out_specs=(blk, blk, blk, blk),
        compiler_params=_params(("parallel",)),
    )(parts, w, m, v)


SHARDED = ("w_ffn1_in", "w_ffn1_out", "w_in", "conv_w", "w_attn_out", "w_conv_out", "w_o",
           "w_ffn2_in", "w_ffn2_out", "w_pe_gate", "w_pe_proj")
COL_SHARDED = ("w_ffn1_in", "w_in", "conv_w", "w_attn_out", "w_conv_out", "w_ffn2_in", "w_pe_proj")
SMALL = ("rel_bias", "norm_ffn1", "norm_mix", "q_norm", "k_norm", "sink", "conv_b", "conv_ln_g", "conv_ln_b",
         "norm_ffn2", "norm_pe")
WEIGHTS = ("rel_bias", "norm_ffn1", "w_ffn1_in", "w_ffn1_out", "norm_mix", "w_in", "q_norm", "k_norm", "sink",
           "conv_w", "conv_b", "conv_ln_g", "conv_ln_b", "w_attn_out", "w_conv_out", "w_o", "norm_ffn2",
           "w_ffn2_in", "w_ffn2_out", "norm_pe", "w_pe_gate", "w_pe_proj")


def _natural(g):
    k, n = g.shape[1], g.shape[2]
    return jnp.transpose(g, (1, 0, 2)).reshape(k, N_DEV * n)


def _blocked(w):
    k, n = w.shape[0], w.shape[1] // N_DEV
    return jnp.transpose(w.reshape(k, N_DEV, n), (1, 0, 2))


def kernel(x, p, rel_bias, norm_ffn1, w_ffn1_in, w_ffn1_out, norm_mix, w_in, q_norm, k_norm, sink, conv_w, conv_b, conv_ln_g, conv_ln_b, w_attn_out, w_conv_out, w_o, norm_ffn2, w_ffn2_in, w_ffn2_out, norm_pe, w_pe_gate, w_pe_proj, loss_target, m_rel_bias, m_norm_ffn1, m_w_ffn1_in, m_w_ffn1_out, m_norm_mix, m_w_in, m_q_norm, m_k_norm, m_sink, m_conv_w, m_conv_b, m_conv_ln_g, m_conv_ln_b, m_w_attn_out, m_w_conv_out, m_w_o, m_norm_ffn2, m_w_ffn2_in, m_w_ffn2_out, m_norm_pe, m_w_pe_gate, m_w_pe_proj, v_rel_bias, v_norm_ffn1, v_w_ffn1_in, v_w_ffn1_out, v_norm_mix, v_w_in, v_q_norm, v_k_norm, v_sink, v_conv_w, v_conv_b, v_conv_ln_g, v_conv_ln_b, v_w_attn_out, v_w_conv_out, v_w_o, v_norm_ffn2, v_w_ffn2_in, v_w_ffn2_out, v_norm_pe, v_w_pe_gate, v_w_pe_proj):
    W = dict(rel_bias=rel_bias, norm_ffn1=norm_ffn1, w_ffn1_in=w_ffn1_in, w_ffn1_out=w_ffn1_out, norm_mix=norm_mix,
             w_in=w_in, q_norm=q_norm, k_norm=k_norm, sink=sink, conv_w=conv_w, conv_b=conv_b, conv_ln_g=conv_ln_g,
             conv_ln_b=conv_ln_b, w_attn_out=w_attn_out, w_conv_out=w_conv_out, w_o=w_o, norm_ffn2=norm_ffn2,
             w_ffn2_in=w_ffn2_in, w_ffn2_out=w_ffn2_out, norm_pe=norm_pe, w_pe_gate=w_pe_gate, w_pe_proj=w_pe_proj)
    M = dict(rel_bias=m_rel_bias, norm_ffn1=m_norm_ffn1, w_ffn1_in=m_w_ffn1_in, w_ffn1_out=m_w_ffn1_out,
             norm_mix=m_norm_mix, w_in=m_w_in, q_norm=m_q_norm, k_norm=m_k_norm, sink=m_sink, conv_w=m_conv_w,
             conv_b=m_conv_b, conv_ln_g=m_conv_ln_g, conv_ln_b=m_conv_ln_b, w_attn_out=m_w_attn_out,
             w_conv_out=m_w_conv_out, w_o=m_w_o, norm_ffn2=m_norm_ffn2, w_ffn2_in=m_w_ffn2_in,
             w_ffn2_out=m_w_ffn2_out, norm_pe=m_norm_pe, w_pe_gate=m_w_pe_gate, w_pe_proj=m_w_pe_proj)
    V = dict(rel_bias=v_rel_bias, norm_ffn1=v_norm_ffn1, w_ffn1_in=v_w_ffn1_in, w_ffn1_out=v_w_ffn1_out,
             norm_mix=v_norm_mix, w_in=v_w_in, q_norm=v_q_norm, k_norm=v_k_norm, sink=v_sink, conv_w=v_conv_w,
             conv_b=v_conv_b, conv_ln_g=v_conv_ln_g, conv_ln_b=v_conv_ln_b, w_attn_out=v_w_attn_out,
             w_conv_out=v_w_conv_out, w_o=v_w_o, norm_ffn2=v_norm_ffn2, w_ffn2_in=v_w_ffn2_in,
             w_ffn2_out=v_w_ffn2_out, norm_pe=v_norm_pe, w_pe_gate=v_w_pe_gate, w_pe_proj=v_w_pe_proj)

    L = w_in.shape[0]
    S = x.shape[1]
    tm = min(512, S)
    xs = x[0]
    target = loss_target[0]
    vec = lambda a: a.reshape(1, -1)

    set_ffn1 = ("w_ffn1_in", "w_ffn1_out")
    set_a = ("w_in", "conv_w", "w_ffn2_in")
    set_b = ("w_ffn2_out",)
    set_c = ("w_attn_out", "w_conv_out", "w_o")
    set_d = ("w_pe_gate", "w_pe_proj")

    def shards(names, l):
        return [W[n][l] if n == "conv_w" else W[n][l].astype(BF16) for n in names]

    onehot, band = _bucket_onehot()
    bias = _bias_table(rel_bias.T, onehot, band).reshape(N_HEADS, BLOCK, WIN)

    layers, saved = [], []
    h = xs
    ffn1_w = _exchange(shards(set_ffn1, 0), True, "allgather_first")
    for l in range(L):
        sv = dict(x0=h)
        G = dict(wi1=ffn1_w[0], wo1=ffn1_w[1])
        (h1, sv["xn1"], sv["g1"], sv["u1"]), got = _ffn_fwd(
            h, vec(norm_ffn1[l]), G["wi1"], G["wo1"], tm, "ffn1_fwd", _Exchange(shards(set_a, l), True))
        w_in_n = _natural(got[0])
        G.update(w_qc=w_in_n[:, :QC_DIM], w_g=w_in_n[:, QC_DIM:],
                 conv_w=jnp.pad(_natural(got[1]), ((0, 1), (0, 0))), wi2=got[2])
        sv["h1"] = h1
        sv["un"], sv["qkv"], sv["cvg"] = _mixin_fwd(h1, vec(norm_mix[l]), G["w_qc"], tm, "mixin_fwd")
        sv["qg"] = vec(jnp.tile(q_norm[l], 2))
        sv["kg"] = vec(jnp.tile(k_norm[l], KV_HEADS))
        sv["sinkb"] = jnp.broadcast_to(sink[l][:, None], (N_HEADS, 128))
        (sv["o"],), got = _attn_fwd(sv["qkv"], sv["qg"], sv["kg"], sv["sinkb"], bias, "attn_fwd",
                                    _Exchange(shards(set_b, l), True))
        G.update(wo2=got[0])
        (sv["cact"], sv["yconv"]), got = _conv_fwd(
            sv["cvg"], G["conv_w"], vec(conv_b[l]), vec(conv_ln_g[l]), vec(conv_ln_b[l]), tm, "conv_fwd",
            _Exchange(shards(set_c, l), True))
        G.update(w_ao=_natural(got[0]), w_co=_natural(got[1]), w_o=got[2].reshape(D_MODEL, D_MODEL))
        (h2,), got = _merge_fwd(h1, sv["un"], sv["o"], sv["cact"], G["w_g"], G["w_ao"], G["w_co"], G["w_o"], tm,
                                "merge_fwd", _Exchange(shards(set_d, l), True))
        G.update(w_pg=got[0].reshape(D_MODEL, D_MODEL), w_pp=_natural(got[1]))
        sv["h2"] = h2
        nxt = _Exchange(shards(set_ffn1, l + 1), True) if l + 1 < L else NO_EXCHANGE
        (h3, sv["xn2"], sv["g2"], sv["u2"]), ffn1_w = _ffn_fwd(
            h2, vec(norm_ffn2[l]), G["wi2"], G["wo2"], tm, "ffn2_fwd", nxt)
        sv["h3"] = h3
        h, sv["hn"] = _pe_fwd(h3, vec(norm_pe[l]), p, l, G["w_pp"], G["w_pg"], tm, "pe_fwd")
        layers.append(G)
        saved.append(sv)

    dh, lparts = _loss_head(h, target, tm)
    loss = lax.psum((0.5 / D_MODEL) * jnp.sum(lparts), AXES)

    dbias = jnp.zeros((N_HEADS, BLOCK, WIN), F32)
    small_g = {n: [None] * L for n in SMALL if n != "rel_bias"}
    recv = {n: [None] * L for n in SHARDED}

    def keep(names, l, got):
        for n, r in zip(names, got):
            recv[n][l] = r

    pending = None
    for l in reversed(range(L)):
        G, sv = layers[l], saved[l]
        dh3, dgp_pe, dproj, dg_pe = _pe_bwd(dh, sv["h3"], vec(norm_pe[l]), sv["hn"], p, l, G["w_pp"], G["w_pg"],
                                            tm, "pe_bwd")
        gw_pg = _matmul_tn(sv["hn"][None], dgp_pe[None], 1, "dw_pe_gate")
        gw_pp = _matmul_tn(p[l], dproj[None], 1, "dw_pe_proj")
        (dh2, a2, dgu2, dg_n2), got = _ffn_bwd(
            dh3, sv["h2"], vec(norm_ffn2[l]), sv["g2"], sv["u2"], G["wi2"], G["wo2"], tm, "ffn2_bwd",
            _Exchange(pending, False) if pending else NO_EXCHANGE)
        if pending:
            keep(("w_ffn1_in",), l + 1, got)
        gwo2 = _matmul_tn(a2, dh3[None], FF_BLOCKS, "dw_ffn2_out", scale=0.5)
        gwi2 = _matmul_tn(sv["xn2"][None], dgu2.reshape(2 * FF_BLOCKS, S, FF_SHARD), 2 * FF_BLOCKS, "dw_ffn2_in")
        do, dcact, mix, dya, dyc, dgpre = _merge_bwd(dh2, sv["un"], sv["o"], sv["cact"], G["w_g"], G["w_ao"],
                                                     G["w_co"], G["w_o"], tm, "merge_bwd")
        gw_o = _matmul_tn(mix[None], dh2[None], 1, "dw_o")
        gw_ao = _matmul_tn(sv["o"][None], dya[None], 1, "dw_attn_out")
        gw_co = _matmul_tn(sv["cact"][None], dyc[None], 1, "dw_conv_out")
        (dq, dkp, dvp, dbias, dsink, dqg), got = _attn_bwd(
            do, sv["qkv"], sv["qg"], sv["kg"], sv["sinkb"], bias, dbias, "attn_bwd",
            _Exchange([gwi2, gw_pg.reshape(N_DEV, D_MODEL // N_DEV, D_MODEL), _blocked(gw_pp[0])], False))
        keep(("w_ffn2_in", "w_pe_gate", "w_pe_proj"), l, got)
        dkv, dkg = _kv_fold(dkp, dvp, sv["qkv"], sv["kg"], "kv_fold")
        (dcvg, dcw, dcvec), got = _conv_bwd(
            dcact, sv["yconv"], sv["cvg"], G["conv_w"], vec(conv_ln_g[l]), vec(conv_ln_b[l]), tm, "conv_bwd",
            _Exchange([gwo2.reshape(N_DEV, FF_SHARD // 2, D_MODEL)], False))
        keep(("w_ffn2_out",), l, got)
        dh1, dg_mix = _mixin_bwd(dh2, sv["h1"], vec(norm_mix[l]), dq, dkv, dcvg, dgpre, G["w_qc"], G["w_g"],
                                 tm, "mixin_bwd")
        un3 = sv["un"][None]
        gw_in = jnp.concatenate([
            _matmul_tn(un3, dq[None], 1, "dw_in_q")[0], _matmul_tn(un3, dkv[None], 1, "dw_in_kv")[0],
            _matmul_tn(un3, dcvg[None], 1, "dw_in_c")[0], _matmul_tn(un3, dgpre[None], 1, "dw_in_g")[0]], axis=1)
        mid_send = [_blocked(gw_in), _blocked(dcw[:CONV_WIDTH]), _blocked(gw_ao[0]), _blocked(gw_co[0]),
                    gw_o.reshape(N_DEV, D_MODEL // N_DEV, D_MODEL)]
        (dh, a1, dgu1, dg_n1), got = _ffn_bwd(
            dh1, sv["x0"], vec(norm_ffn1[l]), sv["g1"], sv["u1"], G["wi1"], G["wo1"], tm, "ffn1_bwd",
            _Exchange(mid_send, False))
        keep(("w_in", "conv_w", "w_attn_out", "w_conv_out", "w_o"), l, got)
        gwo1 = _matmul_tn(a1, dh1[None], FF_BLOCKS, "dw_ffn1_out", scale=0.5)
        gwi1, got = _matmul_tn(sv["xn1"][None], dgu1.reshape(2 * FF_BLOCKS, S, FF_SHARD), 2 * FF_BLOCKS,
                               "dw_ffn1_in", ex=_Exchange([gwo1.reshape(N_DEV, FF_SHARD // 2, D_MODEL)], False))
        keep(("w_ffn1_out",), l, got)
        pending = [gwi1]
        small_g["norm_ffn1"][l] = dg_n1[0]
        small_g["norm_mix"][l] = dg_mix[0]
        small_g["q_norm"][l] = dqg[0, :HEAD_DIM]
        small_g["k_norm"][l] = dkg[0, :HEAD_DIM]
        small_g["sink"][l] = dsink[:, 0]
        small_g["conv_b"][l] = dcvec[0]
        small_g["conv_ln_g"][l] = dcvec[1]
        small_g["conv_ln_b"][l] = dcvec[2]
        small_g["norm_ffn2"][l] = dg_n2[0]
        small_g["norm_pe"][l] = dg_pe[0]

    keep(("w_ffn1_in",), 0, _exchange(pending, False, "grad_exchange_last"))
    grad_x = dh[None]
    drb = _bias_grad(dbias.reshape(N_HEADS, BIAS_COLS), onehot).T

    res = {}
    for n in SHARDED:
        parts = jnp.stack(recv[n], axis=1)
        shp = W[n].shape
        rows, cols = shp[0] * shp[1], shp[2]
        res[n] = [o.reshape(shp) for o in _adamw(
            parts.reshape(N_DEV, rows, cols), W[n].reshape(rows, cols), M[n].reshape(rows, cols),
            V[n].reshape(rows, cols), "adamw_" + n)]

    flat_g = jnp.concatenate([drb.reshape(-1)] + [jnp.stack(small_g[n]).reshape(-1) for n in SMALL[1:]])
    n_small = flat_g.shape[0]
    rows_s = -(-n_small // 1024 // 8) * 8
    pad = lambda a: jnp.pad(a, (0, rows_s * 1024 - n_small)).reshape(rows_s, 1024)
    flat = lambda d: pad(jnp.concatenate([d[n].reshape(-1) for n in SMALL]))
    (parts_s,) = _exchange([pad(flat_g)], True, "small_allgather")
    outs_s = _adamw(parts_s, flat(W), flat(M), flat(V), "adamw_small")
    off = 0
    for n in SMALL:
        size = W[n].size
        res[n] = [o.reshape(-1)[off:off + size].reshape(W[n].shape) for o in outs_s]
        off += size

    out = [loss, grad_x]
    for k in range(4):
        out += [res[n][k] for n in WEIGHTS]
    return tuple(out)
```

```python
import functools

import jax
import jax.numpy as jnp
import numpy as np
from jax import lax
from jax.experimental import pallas as pl
from jax.experimental.pallas import tpu as pltpu

F32 = jnp.float32
BF16 = jnp.bfloat16
MESH_ID = pl.DeviceIdType.MESH
AXES = ("x", "y", "c")
N_DEV = 8

D_MODEL = 1024
N_HEADS = 8
KV_HEADS = 2
HEAD_DIM = 64
Q_DIM = 512
KV_DIM = 128
BLOCK = 128
WIN = 3 * BLOCK
NUM_BUCKETS = 32
MAX_DISTANCE = 128
CONV_DIM = 512
CONV_WIDTH = 31
D_FF = 2816
FF_SHARD = 2 * D_FF // N_DEV
FF_BLOCKS = D_FF // FF_SHARD
QC_DIM = Q_DIM + 2 * KV_DIM + 2 * CONV_DIM
NEG_INF = -1e9
HALO = 16
CROWS = 64
FFN_PART_ROWS = 256

ADAM_LR = 0.001
ADAM_B1 = 0.9
ADAM_B2 = 0.999
ADAM_EPS = 1e-08
ADAM_WD = 0.01
ADAM_STEP = 10

VMEM_LIMIT = 56 * 1024 * 1024
HI = lax.Precision.HIGHEST


def _params(sem):
    return pltpu.CompilerParams(dimension_semantics=sem, vmem_limit_bytes=VMEM_LIMIT)


def _dot(a, b, precision=None):
    return jnp.dot(a, b, preferred_element_type=F32, precision=precision)


def _dot_nt(a, b):
    return lax.dot_general(a, b, (((1,), (1,)), ((), ())), preferred_element_type=F32)


def _dot_tn(a, b):
    return lax.dot_general(a, b, (((0,), (0,)), ((), ())), preferred_element_type=F32)


def _sig(x):
    return 1.0 / (1.0 + jnp.exp(-x))


def _bf(x):
    return x.astype(BF16)


def _rms_fwd(x, gamma):
    r = lax.rsqrt(jnp.mean(x * x, axis=-1, keepdims=True) + 1e-6)
    return x * r * gamma, r


def _rms_bwd(dy, x, gamma):
    r = lax.rsqrt(jnp.mean(x * x, axis=-1, keepdims=True) + 1e-6)
    xhat = x * r
    dxhat = dy * gamma
    dx = r * (dxhat - xhat * jnp.mean(dxhat * xhat, axis=-1, keepdims=True))
    return dx, dy * xhat


class _Exchange:
    def __init__(self, arrs, gather):
        self.arrs, self.gather, self.n = list(arrs), gather, len(arrs)
        n = self.n
        self.out_shape = tuple(
            jax.ShapeDtypeStruct(((N_DEV,) + a.shape) if gather else a.shape, a.dtype) for a in self.arrs)
        self.specs = [pl.BlockSpec(memory_space=pl.ANY)] * n
        self.scratch = [pltpu.SemaphoreType.DMA((7 * n,)), pltpu.SemaphoreType.DMA((7 * n,)),
                        pltpu.SemaphoreType.DMA((n,))] if n else []

    def _copies(self, ins, outs, sems):
        n, gather = self.n, self.gather
        send_sems, recv_sems, local_sems = sems
        x, y, c = lax.axis_index("x"), lax.axis_index("y"), lax.axis_index("c")
        me = 4 * x + 2 * y + c
        copies = [pltpu.make_async_copy(ins[t] if gather else ins[t].at[me], outs[t].at[me], local_sems.at[t])
                  for t in range(n)]
        for d in range(1, N_DEV):
            px = 1 - x if d & 4 else x
            py = 1 - y if d & 2 else y
            pc = 1 - c if d & 1 else c
            peer = 4 * px + 2 * py + pc
            for t in range(n):
                k = (d - 1) * n + t
                copies.append(pltpu.make_async_remote_copy(
                    src_ref=ins[t] if gather else ins[t].at[peer], dst_ref=outs[t].at[me],
                    send_sem=send_sems.at[k], recv_sem=recv_sems.at[k],
                    device_id=(px, py, pc), device_id_type=MESH_ID))
        return copies

    def start(self, ins, outs, sems):
        for cp in self._copies(ins, outs, sems):
            cp.start()

    def wait(self, ins, outs, sems):
        for cp in self._copies(ins, outs, sems):
            cp.wait()


NO_EXCHANGE = _Exchange([], True)


def _exchange(arrs, gather, name):
    ex = _Exchange(arrs, gather)
    n = ex.n

    def body(*refs):
        ins, outs, sems = refs[:n], refs[n:2 * n], refs[2 * n:]
        ex.start(ins, outs, sems)
        ex.wait(ins, outs, sems)

    return pl.pallas_call(
        body, name=name, out_shape=ex.out_shape, in_specs=ex.specs, out_specs=tuple(ex.specs),
        scratch_shapes=ex.scratch,
    )(*ex.arrs)


def _carrier_call(body, ex, first_last, name, out_shape, grid, in_specs, out_specs, scratch_shapes, args):
    n_in, n_out, n_scr, n = len(in_specs), len(out_shape), len(scratch_shapes), ex.n

    def full(*refs):
        a, ci = refs[:n_in], refs[n_in:n_in + n]
        o = refs[n_in + n:n_in + n + n_out]
        co = refs[n_in + n + n_out:n_in + 2 * n + n_out]
        scr = refs[n_in + 2 * n + n_out:n_in + 2 * n + n_out + n_scr]
        sems = refs[n_in + 2 * n + n_out + n_scr:]
        first, last = first_last()
        if n:
            @pl.when(first)
            def _():
                ex.start(ci, co, sems)

        body(*a, *o, *scr)
        if n:
            @pl.when(last)
            def _():
                ex.wait(ci, co, sems)

    outs = pl.pallas_call(
        full, name=name, out_shape=tuple(out_shape) + ex.out_shape, grid=grid,
        in_specs=list(in_specs) + ex.specs, out_specs=tuple(out_specs) + tuple(ex.specs),
        scratch_shapes=list(scratch_shapes) + ex.scratch,
        compiler_params=_params(("arbitrary",) * len(grid)),
    )(*args, *ex.arrs)
    return outs[:n_out], outs[n_out:]


def _matmul_tn(a, b, nb, name, scale=1.0, out_dtype=BF16, ts=1024, ex=None):
    ba, S, K = a.shape
    bb, _, N = b.shape
    ts = min(ts, S)
    tn = N if N <= 1024 else 1024
    assert N % tn == 0 and S % ts == 0
    ns = S // ts

    def body(a_ref, b_ref, o_ref, acc):
        s = pl.program_id(2)

        @pl.when(s == 0)
        def _():
            acc[...] = jnp.zeros_like(acc)

        acc[...] += _dot_tn(_bf(a_ref[...]), _bf(b_ref[...]))

        @pl.when(s == ns - 1)
        def _():
            o_ref[...] = (acc[...] * scale).astype(out_dtype)

    in_specs = [pl.BlockSpec((None, ts, K), (lambda i, j, s: (i, s, 0)) if ba > 1 else (lambda i, j, s: (0, s, 0))),
                pl.BlockSpec((None, ts, tn), (lambda i, j, s: (i, s, j)) if bb > 1 else (lambda i, j, s: (0, s, j)))]
    out_spec = pl.BlockSpec((None, K, tn), lambda i, j, s: (i, 0, j))
    out_shape = jax.ShapeDtypeStruct((nb, K, N), out_dtype)
    scratch = [pltpu.VMEM((K, tn), F32)]
    grid = (nb, N // tn, ns)
    if ex is not None:
        (out,), got = _carrier_call(body, ex, _grid_ends(*grid), name, out_shape=(out_shape,), grid=grid,
                                    in_specs=in_specs, out_specs=(out_spec,), scratch_shapes=scratch, args=(a, b))
        return out, got
    return pl.pallas_call(
        body, name=name, out_shape=out_shape, grid=grid, in_specs=in_specs, out_specs=out_spec,
        scratch_shapes=scratch, compiler_params=_params(("parallel", "parallel", "arbitrary")),
    )(a, b)


def _ffn_specs(tm):
    wg = pl.BlockSpec((None, D_MODEL, FF_SHARD), lambda i, j: (j, 0, 0))
    wu = pl.BlockSpec((None, D_MODEL, FF_SHARD), lambda i, j: (j + FF_BLOCKS, 0, 0))
    wo = pl.BlockSpec((2, FF_SHARD // 2, D_MODEL), lambda i, j: (j, 0, 0))
    row = pl.BlockSpec((tm, D_MODEL), lambda i, j: (i, 0))
    vec = pl.BlockSpec((1, D_MODEL), lambda i, j: (0, 0))
    hid = pl.BlockSpec((None, tm, FF_SHARD), lambda i, j: (j, i, 0))
    return wg, wu, wo, row, vec, hid


def _grid_ends(*grid):
    def first_last():
        first, last = None, None
        for d, n in enumerate(grid):
            i = pl.program_id(d)
            first = (i == 0) if first is None else first & (i == 0)
            last = (i == n - 1) if last is None else last & (i == n - 1)
        return first, last
    return first_last


def _grid2_ends(ni, nj):
    return _grid_ends(ni, nj)


def _grid1_ends(ni):
    return _grid_ends(ni)


def _ffn_fwd(x, gamma, wi, wo, tm, name, ex=NO_EXCHANGE):
    S = x.shape[0]
    wg_s, wu_s, wo_s, row, vec, hid = _ffn_specs(tm)

    def body(x_ref, g_ref, wg_ref, wu_ref, wo_ref, y_ref, xn_ref, gs_ref, us_ref, xn_s, acc):
        j = pl.program_id(1)

        @pl.when(j == 0)
        def _():
            xn = _bf(_rms_fwd(x_ref[...], g_ref[...])[0])
            xn_s[...] = xn
            xn_ref[...] = xn
            acc[...] = jnp.zeros_like(acc)

        wo2 = wo_ref[...].reshape(FF_SHARD, D_MODEL)
        for r in range(tm // FFN_PART_ROWS):
            rows = slice(r * FFN_PART_ROWS, (r + 1) * FFN_PART_ROWS)
            xn = xn_s[rows, :]
            g = _dot(xn, wg_ref[...])
            u = _dot(xn, wu_ref[...])
            gs_ref[rows, :] = _bf(g)
            us_ref[rows, :] = _bf(u)
            a = g * _sig(g) * u
            acc[rows, :] += _dot(_bf(a), wo2)

        @pl.when(j == FF_BLOCKS - 1)
        def _():
            y_ref[...] = x_ref[...] + 0.5 * acc[...]

    return _carrier_call(
        body, ex, _grid2_ends(S // tm, FF_BLOCKS), name,
        out_shape=(jax.ShapeDtypeStruct((S, D_MODEL), F32), jax.ShapeDtypeStruct((S, D_MODEL), BF16),
                   jax.ShapeDtypeStruct((FF_BLOCKS, S, FF_SHARD), BF16),
                   jax.ShapeDtypeStruct((FF_BLOCKS, S, FF_SHARD), BF16)),
        grid=(S // tm, FF_BLOCKS),
        in_specs=[row, vec, wg_s, wu_s, wo_s],
        out_specs=(row, row, hid, hid),
        scratch_shapes=[pltpu.VMEM((tm, D_MODEL), BF16), pltpu.VMEM((tm, D_MODEL), F32)],
        args=(x, gamma, wi, wi, wo))


def _ffn_bwd(dy, x, gamma, gs, us, wi, wo, tm, name, ex=NO_EXCHANGE):
    S = x.shape[0]
    wg_s, wu_s, wo_s, row, vec, hid = _ffn_specs(tm)
    dgu_s = pl.BlockSpec((2, None, tm, FF_SHARD), lambda i, j: (0, j, i, 0))
    row_once = pl.BlockSpec((tm, D_MODEL), lambda i, j: (i, 0), pipeline_mode=pl.Buffered(1))

    def body(dy_ref, x_ref, g_ref, gs_ref, us_ref, wg_ref, wu_ref, wo_ref,
             dx_ref, a_ref, dgu_ref, dgam_ref, dyh_s, acc):
        i, j = pl.program_id(0), pl.program_id(1)

        @pl.when(j == 0)
        def _():
            dyh_s[...] = _bf(0.5 * dy_ref[...])
            acc[...] = jnp.zeros_like(acc)

        @pl.when((i == 0) & (j == 0))
        def _():
            dgam_ref[...] = jnp.zeros_like(dgam_ref)

        wo2 = wo_ref[...].reshape(FF_SHARD, D_MODEL)
        for r in range(tm // FFN_PART_ROWS):
            rows = slice(r * FFN_PART_ROWS, (r + 1) * FFN_PART_ROWS)
            da = _dot_nt(dyh_s[rows, :], wo2)
            g = gs_ref[rows, :].astype(F32)
            u = us_ref[rows, :].astype(F32)
            sg = _sig(g)
            sl = g * sg
            a_ref[rows, :] = _bf(sl * u)
            dg = _bf(da * u * (sg * (1.0 + g * (1.0 - sg))))
            du = _bf(da * sl)
            dgu_ref[0, rows, :] = dg
            dgu_ref[1, rows, :] = du
            acc[rows, :] += _dot_nt(dg, wg_ref[...]) + _dot_nt(du, wu_ref[...])

        @pl.when(j == FF_BLOCKS - 1)
        def _():
            dx, dgam = _rms_bwd(acc[...], x_ref[...], g_ref[...])
            dx_ref[...] = dy_ref[...] + dx
            dgam_ref[...] += jnp.sum(dgam, axis=0, keepdims=True)

    return _carrier_call(
        body, ex, _grid2_ends(S // tm, FF_BLOCKS), name,
        out_shape=(jax.ShapeDtypeStruct((S, D_MODEL), F32),
                   jax.ShapeDtypeStruct((FF_BLOCKS, S, FF_SHARD), BF16),
                   jax.ShapeDtypeStruct((2, FF_BLOCKS, S, FF_SHARD), BF16),
                   jax.ShapeDtypeStruct((1, D_MODEL), F32)),
        grid=(S // tm, FF_BLOCKS),
        in_specs=[row_once, row_once, vec, hid, hid, wg_s, wu_s, wo_s],
        out_specs=(row, hid, dgu_s, vec),
        scratch_shapes=[pltpu.VMEM((tm, D_MODEL), BF16), pltpu.VMEM((tm, D_MODEL), F32)],
        args=(dy, x, gamma, gs, us, wi, wi, wo))


def _mixin_fwd(h, gamma, w_qc, tm, name):
    S = h.shape[0]
    nqkv = Q_DIM + 2 * KV_DIM

    def body(h_ref, g_ref, w_ref, un_ref, qkv_ref, cvg_ref):
        un = _bf(_rms_fwd(h_ref[...], g_ref[...])[0])
        un_ref[...] = un
        z = _dot(un, w_ref[...])
        qkv_ref[...] = z[:, :nqkv]
        cvg_ref[...] = z[:, nqkv:]

    row = lambda w: pl.BlockSpec((tm, w), lambda i: (i, 0))
    return pl.pallas_call(
        body, name=name,
        out_shape=(jax.ShapeDtypeStruct((S, D_MODEL), BF16), jax.ShapeDtypeStruct((S, nqkv), F32),
                   jax.ShapeDtypeStruct((S, 2 * CONV_DIM), F32)),
        grid=(S // tm,),
        in_specs=[row(D_MODEL), pl.BlockSpec((1, D_MODEL), lambda i: (0, 0)),
                  pl.BlockSpec((D_MODEL, QC_DIM), lambda i: (0, 0))],
        out_specs=(row(D_MODEL), row(nqkv), row(2 * CONV_DIM)),
        compiler_params=_params(("parallel",)),
    )(h, gamma, w_qc)


def _mixin_bwd(dh2, h1, gamma, dq, dkv, dcvg, dgpre, w_qc, w_g, tm, name):
    S = h1.shape[0]
    nqkv = Q_DIM + 2 * KV_DIM

    def body(dh2_ref, h1_ref, g_ref, dq_ref, dkv_ref, dcvg_ref, dgp_ref, wqc_ref, wg_ref, dh1_ref, dgam_ref):
        @pl.when(pl.program_id(0) == 0)
        def _():
            dgam_ref[...] = jnp.zeros_like(dgam_ref)

        wqc = wqc_ref[...]
        dun = _dot_nt(_bf(dq_ref[...]), wqc[:, :Q_DIM])
        dun += _dot_nt(_bf(dkv_ref[...]), wqc[:, Q_DIM:nqkv])
        dun += _dot_nt(_bf(dcvg_ref[...]), wqc[:, nqkv:])
        dun += _dot_nt(dgp_ref[...], wg_ref[...])
        dx, dgam = _rms_bwd(dun, h1_ref[...], g_ref[...])
        dh1_ref[...] = dh2_ref[...] + dx
        dgam_ref[...] += jnp.sum(dgam, axis=0, keepdims=True)

    row = lambda w: pl.BlockSpec((tm, w), lambda i: (i, 0))
    vec = pl.BlockSpec((1, D_MODEL), lambda i: (0, 0))
    return pl.pallas_call(
        body, name=name,
        out_shape=(jax.ShapeDtypeStruct((S, D_MODEL), F32), jax.ShapeDtypeStruct((1, D_MODEL), F32)),
        grid=(S // tm,),
        in_specs=[row(D_MODEL), row(D_MODEL), vec, row(Q_DIM), row(2 * KV_DIM), row(2 * CONV_DIM),
                  row(2 * D_MODEL), pl.BlockSpec((D_MODEL, QC_DIM), lambda i: (0, 0)),
                  pl.BlockSpec((D_MODEL, 2 * D_MODEL), lambda i: (0, 0))],
        out_specs=(row(D_MODEL), vec),
        compiler_params=_params(("arbitrary",)),
    )(dh2, h1, gamma, dq, dkv, dcvg, dgpre, w_qc, w_g)


TQ = 512
QB = TQ // BLOCK


def _attn_in_specs(S):
    nkb = S // BLOCK
    return [
        pl.BlockSpec((TQ, Q_DIM), lambda i: (i, 0)),
        pl.BlockSpec((BLOCK, 2 * KV_DIM), lambda i: (jnp.maximum(i * QB - 1, 0), Q_DIM // (2 * KV_DIM))),
        pl.BlockSpec((TQ, 2 * KV_DIM), lambda i: (i, Q_DIM // (2 * KV_DIM))),
        pl.BlockSpec((BLOCK, 2 * KV_DIM), lambda i: (jnp.minimum(i * QB + QB, nkb - 1), Q_DIM // (2 * KV_DIM))),
        pl.BlockSpec((1, 128), lambda i: (0, 0)),
        pl.BlockSpec((1, KV_DIM), lambda i: (0, 0)),
        pl.BlockSpec((N_HEADS, 128), lambda i: (0, 0)),
        pl.BlockSpec((N_HEADS, BLOCK, WIN), lambda i: (0, 0, 0)),
    ]


GROUP_ROWS = 4 * BLOCK


def _half_rstd(x, low):
    x2 = x * x
    z = jnp.zeros_like(x2)
    r0 = lax.rsqrt(jnp.sum(jnp.where(low, x2, z), axis=-1, keepdims=True) * (1.0 / HEAD_DIM) + 1e-6)
    r1 = lax.rsqrt(jnp.sum(jnp.where(low, z, x2), axis=-1, keepdims=True) * (1.0 / HEAD_DIM) + 1e-6)
    return jnp.where(low, r0, r1)


def _kv_windows(kvp_ref, kvc_ref, kvn_ref, kg_ref, low):
    kv = jnp.concatenate([kvp_ref[...], kvc_ref[...], kvn_ref[...]], axis=0)
    k, v = kv[:, :KV_DIM], kv[:, KV_DIM:]
    kn = k * _half_rstd(k, low) * kg_ref[...]
    kr, vr = pltpu.roll(kn, HEAD_DIM, 1), pltpu.roll(v, HEAD_DIM, 1)
    kdup = [_bf(jnp.where(low, kn, kr)), _bf(jnp.where(low, kr, kn))]
    vdup = [_bf(jnp.where(low, v, vr)), _bf(jnp.where(low, vr, v))]
    return kdup, vdup


def _stack_heads(x_ref, t, kh, low):
    rows = slice(t * BLOCK, (t + 1) * BLOCK)
    xa = x_ref[rows, 256 * kh:256 * kh + 128]
    xb = x_ref[rows, 256 * kh + 128:256 * kh + 256]
    z = jnp.zeros_like(xa)
    return jnp.concatenate([jnp.where(low, xa, z), jnp.where(low, z, xa),
                            jnp.where(low, xb, z), jnp.where(low, z, xb)], axis=0)


def _stacked_q(q_ref, qg_ref, t, kh, low):
    qraw = _stack_heads(q_ref, t, kh, low)
    rq = lax.rsqrt(jnp.sum(qraw * qraw, axis=-1, keepdims=True) * (1.0 / HEAD_DIM) + 1e-6)
    return qraw, rq, _bf(qraw * rq * (qg_ref[...] * (HEAD_DIM ** -0.5)))


def _unstack_heads(ov, low):
    return (jnp.where(low, ov[0:128], ov[128:256]), jnp.where(low, ov[256:384], ov[384:512]))


def _edge_bias(i, t, S):
    kpos = i * TQ + (t - 1) * BLOCK + lax.broadcasted_iota(jnp.int32, (1, WIN), 1)
    return jnp.where((kpos < 0) | (kpos >= S), NEG_INF, 0.0)


def _group_exp(lhs, kw, bias_ref, sink_ref, kh, edge):
    s = _dot_nt(lhs, kw) + bias_ref[4 * kh:4 * kh + 4].reshape(GROUP_ROWS, WIN)
    if edge is not None:
        s = s + edge
    sk = jnp.concatenate(
        [jnp.broadcast_to(sink_ref[4 * kh + r:4 * kh + r + 1, 0:1], (BLOCK, 1)) for r in range(4)], axis=0)
    m = jnp.maximum(jnp.max(s, axis=-1, keepdims=True), sk)
    return jnp.exp(s - m), jnp.exp(sk - m)


def _attn_fwd(qkv, qg, kg, sinkb, bias, name, ex=NO_EXCHANGE):
    S = qkv.shape[0]

    def body(q_ref, kvp_ref, kvc_ref, kvn_ref, qg_ref, kg_ref, sink_ref, bias_ref, o_ref):
        i = pl.program_id(0)
        low = lax.broadcasted_iota(jnp.int32, (1, 128), 1) < HEAD_DIM
        ones = jnp.ones((WIN, 128), BF16)
        kdup, vdup = _kv_windows(kvp_ref, kvc_ref, kvn_ref, kg_ref, low)
        for t in range(QB):
            edge = _edge_bias(i, t, S) if t in (0, QB - 1) else None
            rows = slice(t * BLOCK, (t + 1) * BLOCK)
            for kh in range(KV_HEADS):
                _, _, lhs = _stacked_q(q_ref, qg_ref, t, kh, low)
                kw = kdup[kh][t * BLOCK:t * BLOCK + WIN]
                vw = vdup[kh][t * BLOCK:t * BLOCK + WIN]
                e, es = _group_exp(lhs, kw, bias_ref, sink_ref, kh, edge)
                eb = _bf(e)
                ov = _dot(eb, vw) * (1.0 / (_dot(eb, ones) + es))
                oa, ob = _unstack_heads(ov, low)
                o_ref[rows, 256 * kh:256 * kh + 128] = _bf(oa)
                o_ref[rows, 256 * kh + 128:256 * kh + 256] = _bf(ob)

    return _carrier_call(
        body, ex, _grid1_ends(S // TQ), name, out_shape=(jax.ShapeDtypeStruct((S, Q_DIM), BF16),),
        grid=(S // TQ,), in_specs=_attn_in_specs(S),
        out_specs=(pl.BlockSpec((TQ, Q_DIM), lambda i: (i, 0)),), scratch_shapes=[],
        args=(qkv, qkv, qkv, qkv, qg, kg, sinkb, bias))


def _attn_bwd(do, qkv, qg, kg, sinkb, bias, dbias_in, name, ex=NO_EXCHANGE):
    S = qkv.shape[0]
    nkb = S // BLOCK
    nsteps = S // TQ

    def body(do_ref, q_ref, kvp_ref, kvc_ref, kvn_ref, qg_ref, kg_ref, sink_ref, bias_ref, dbin_ref,
             dq_ref, dkp_ref, dvp_ref, dbias_ref, dsink_ref, dqg_ref, dqg_s):
        i = pl.program_id(0)

        @pl.when(i == 0)
        def _():
            dbias_ref[...] = dbin_ref[...]
            dsink_ref[...] = jnp.zeros_like(dsink_ref)
            dqg_s[...] = jnp.zeros_like(dqg_s)

        low = lax.broadcasted_iota(jnp.int32, (1, 128), 1) < HEAD_DIM
        own = ((lax.broadcasted_iota(jnp.int32, (GROUP_ROWS, 128), 1) >> 6) & 1) == (
            (lax.broadcasted_iota(jnp.int32, (GROUP_ROWS, 128), 0) >> 7) & 1)
        gq = qg_ref[...] * (HEAD_DIM ** -0.5)
        ones = jnp.ones((WIN, 128), BF16)
        kdup, vdup = _kv_windows(kvp_ref, kvc_ref, kvn_ref, kg_ref, low)
        for t in range(QB):
            edge = _edge_bias(i, t, S) if t in (0, QB - 1) else None
            rows = slice(t * BLOCK, (t + 1) * BLOCK)
            dk_dup, dv_dup = [], []
            for kh in range(KV_HEADS):
                qraw, rq, lhs = _stacked_q(q_ref, qg_ref, t, kh, low)
                dos = _bf(_stack_heads(do_ref, t, kh, low))
                kw = kdup[kh][t * BLOCK:t * BLOCK + WIN]
                vw = vdup[kh][t * BLOCK:t * BLOCK + WIN]
                e, es = _group_exp(lhs, kw, bias_ref, sink_ref, kh, edge)
                inv = 1.0 / (_dot(_bf(e), ones) + es)
                pr = e * jnp.concatenate([inv] * (WIN // 128), axis=1)
                dpr = _dot_nt(dos, vw)
                delta = jnp.sum(pr * dpr, axis=-1, keepdims=True)
                ds = pr * (dpr - delta)
                dbias_ref[4 * kh:4 * kh + 4] += ds.reshape(4, BLOCK, WIN)
                dsk = es * inv[:, 0:1] * delta
                for r in range(4):
                    dsink_ref[4 * kh + r:4 * kh + r + 1, :] -= jnp.broadcast_to(
                        jnp.sum(dsk[r * BLOCK:(r + 1) * BLOCK], axis=0, keepdims=True), (1, 128))
                dsb = _bf(ds)
                dqs = _dot(dsb, kw)
                qhat = qraw * rq
                dxhat = jnp.where(own, dqs, 0.0) * gq
                dq_st = rq * (dxhat - qhat * (jnp.sum(dxhat * qhat, axis=-1, keepdims=True) * (1.0 / HEAD_DIM)))
                dq_ref[rows, 256 * kh:256 * kh + 128] = dq_st[0:128] + dq_st[128:256]
                dq_ref[rows, 256 * kh + 128:256 * kh + 256] = dq_st[256:384] + dq_st[384:512]
                dqg_s[...] += jnp.sum((dqs * qhat).reshape(GROUP_ROWS // 8, 8, 128), axis=0)
                dkx = _dot_tn(dsb, lhs)
                dvx = _dot_tn(_bf(pr), dos)
                dk_dup.append(dkx + pltpu.roll(dkx, HEAD_DIM, 1))
                dv_dup.append(dvx + pltpu.roll(dvx, HEAD_DIM, 1))
            dkp_ref[t] = jnp.where(low, dk_dup[0], dk_dup[1])
            dvp_ref[t] = jnp.where(low, dv_dup[0], dv_dup[1])

        @pl.when(i == nsteps - 1)
        def _():
            acc = dqg_s[...] * (HEAD_DIM ** -0.5)
            acc = acc + pltpu.roll(acc, HEAD_DIM, 1)
            dqg_ref[...] = jnp.broadcast_to(jnp.sum(acc, axis=0, keepdims=True), (8, 128))

    const2 = lambda shape: pl.BlockSpec(shape, lambda i: (0,) * len(shape))
    part = pl.BlockSpec((QB, WIN, KV_DIM), lambda i: (i, 0, 0))
    return _carrier_call(
        body, ex, _grid1_ends(nsteps), name,
        out_shape=(jax.ShapeDtypeStruct((S, Q_DIM), F32), jax.ShapeDtypeStruct((nkb, WIN, KV_DIM), F32),
                   jax.ShapeDtypeStruct((nkb, WIN, KV_DIM), F32),
                   jax.ShapeDtypeStruct((N_HEADS, BLOCK, WIN), F32), jax.ShapeDtypeStruct((N_HEADS, 128), F32),
                   jax.ShapeDtypeStruct((8, 128), F32)),
        grid=(nsteps,),
        in_specs=[pl.BlockSpec((TQ, Q_DIM), lambda i: (i, 0))] + _attn_in_specs(S)
        + [const2((N_HEADS, BLOCK, WIN))],
        out_specs=(pl.BlockSpec((TQ, Q_DIM), lambda i: (i, 0)), part, part,
                   const2((N_HEADS, BLOCK, WIN)), const2((N_HEADS, 128)), const2((8, 128))),
        scratch_shapes=[pltpu.VMEM((8, 128), F32)],
        args=(do, qkv, qkv, qkv, qkv, qg, kg, sinkb, bias, dbias_in))


def _kv_fold(dkp, dvp, qkv, kg, name):
    nkb = dkp.shape[0]
    S = nkb * BLOCK
    nsteps = S // TQ

    def body(kp_p, kp_c, kp_n, vp_p, vp_c, vp_n, kv_ref, kg_ref, dkv_ref, dkg_ref, dkg_s):
        i = pl.program_id(0)

        @pl.when(i == 0)
        def _():
            dkg_s[...] = jnp.zeros_like(dkg_s)

        def fold(p_ref, c_ref, n_ref):
            blocks = []
            for t in range(QB):
                acc = c_ref[t, BLOCK:2 * BLOCK, :]
                if t > 0:
                    acc = acc + c_ref[t - 1, 2 * BLOCK:, :]
                else:
                    acc = acc + jnp.where(i > 0, p_ref[0, 2 * BLOCK:, :], 0.0)
                if t < QB - 1:
                    acc = acc + c_ref[t + 1, :BLOCK, :]
                else:
                    acc = acc + jnp.where(i < nsteps - 1, n_ref[0, :BLOCK, :], 0.0)
                blocks.append(acc)
            return jnp.concatenate(blocks, axis=0)

        dkn = fold(kp_p, kp_c, kp_n)
        dv = fold(vp_p, vp_c, vp_n)
        k = kv_ref[:, :KV_DIM]
        low = lax.broadcasted_iota(jnp.int32, (1, 128), 1) < HEAD_DIM
        rk = _half_rstd(k, low)
        khat = k * rk
        dxhat = dkn * kg_ref[...]
        prod = dxhat * khat
        z = jnp.zeros_like(prod)
        mean = jnp.where(low, jnp.sum(jnp.where(low, prod, z), axis=-1, keepdims=True),
                         jnp.sum(jnp.where(low, z, prod), axis=-1, keepdims=True)) * (1.0 / HEAD_DIM)
        dkv_ref[:, :KV_DIM] = rk * (dxhat - khat * mean)
        dkv_ref[:, KV_DIM:] = dv
        dkg_s[...] += jnp.sum((dkn * khat).reshape(TQ // 8, 8, KV_DIM), axis=0)

        @pl.when(i == nsteps - 1)
        def _():
            acc = dkg_s[...] + pltpu.roll(dkg_s[...], HEAD_DIM, 1)
            dkg_ref[...] = jnp.broadcast_to(jnp.sum(acc, axis=0, keepdims=True), (8, 128))

    prev = pl.BlockSpec((1, WIN, KV_DIM), lambda i: (jnp.maximum(i * QB - 1, 0), 0, 0))
    cur = pl.BlockSpec((QB, WIN, KV_DIM), lambda i: (i, 0, 0))
    nxt = pl.BlockSpec((1, WIN, KV_DIM), lambda i: (jnp.minimum(i * QB + QB, nkb - 1), 0, 0))
    return pl.pallas_call(
        body, name=name,
        out_shape=(jax.ShapeDtypeStruct((S, 2 * KV_DIM), F32), jax.ShapeDtypeStruct((8, 128), F32)),
        grid=(nsteps,),
        in_specs=[prev, cur, nxt, prev, cur, nxt,
                  pl.BlockSpec((TQ, 2 * KV_DIM), lambda i: (i, Q_DIM // (2 * KV_DIM))),
                  pl.BlockSpec((1, KV_DIM), lambda i: (0, 0))],
        out_specs=(pl.BlockSpec((TQ, 2 * KV_DIM), lambda i: (i, 0)), pl.BlockSpec((8, 128), lambda i: (0, 0))),
        scratch_shapes=[pltpu.VMEM((8, KV_DIM), F32)],
        compiler_params=_params(("arbitrary",)),
    )(dkp, dkp, dkp, dvp, dvp, dvp, qkv, kg)


BIAS_COLS = BLOCK * WIN
BIAS_CHUNK = 6144


def _bias_table(rel_bias_t, onehot, band):
    def body(rb_ref, oh_ref, band_ref, o_ref):
        o_ref[...] = _dot(rb_ref[...], oh_ref[...], HI) + band_ref[...]

    return pl.pallas_call(
        body, name="bias_table", out_shape=jax.ShapeDtypeStruct((N_HEADS, BIAS_COLS), F32),
        grid=(BIAS_COLS // BIAS_CHUNK,),
        in_specs=[pl.BlockSpec((N_HEADS, NUM_BUCKETS), lambda i: (0, 0)),
                  pl.BlockSpec((NUM_BUCKETS, BIAS_CHUNK), lambda i: (0, i)),
                  pl.BlockSpec((1, BIAS_CHUNK), lambda i: (0, i))],
        out_specs=pl.BlockSpec((N_HEADS, BIAS_CHUNK), lambda i: (0, i)),
        compiler_params=_params(("parallel",)),
    )(rel_bias_t, onehot, band)


def _bias_grad(dbias, onehot):
    def body(db_ref, oh_ref, o_ref):
        @pl.when(pl.program_id(0) == 0)
        def _():
            o_ref[...] = jnp.zeros_like(o_ref)

        o_ref[...] += lax.dot_general(db_ref[...], oh_ref[...], (((1,), (1,)), ((), ())),
                                      preferred_element_type=F32, precision=HI)

    return pl.pallas_call(
        body, name="bias_grad", out_shape=jax.ShapeDtypeStruct((N_HEADS, NUM_BUCKETS), F32),
        grid=(BIAS_COLS // BIAS_CHUNK,),
        in_specs=[pl.BlockSpec((N_HEADS, BIAS_CHUNK), lambda i: (0, i)),
                  pl.BlockSpec((NUM_BUCKETS, BIAS_CHUNK), lambda i: (0, i))],
        out_specs=pl.BlockSpec((N_HEADS, NUM_BUCKETS), lambda i: (0, 0)),
        compiler_params=_params(("arbitrary",)),
    )(dbias, onehot)


def _bucket_onehot():
    half = NUM_BUCKETS // 2
    max_exact = half // 2
    rel = jnp.arange(WIN)[None, :] - BLOCK - jnp.arange(BLOCK)[:, None]
    n = jnp.abs(rel)
    ret = jnp.where(rel > 0, half, 0)
    nf = jnp.maximum(n, 1).astype(F32)
    large = max_exact + (jnp.log(nf / max_exact) / np.log(MAX_DISTANCE / max_exact)
                         * (half - max_exact)).astype(jnp.int32)
    large = jnp.minimum(large, half - 1)
    bucket = (ret + jnp.where(n < max_exact, n, large)).reshape(1, BIAS_COLS)
    band = jnp.where(n <= BLOCK, 0.0, NEG_INF).astype(F32).reshape(1, BIAS_COLS)
    return (bucket == jnp.arange(NUM_BUCKETS)[:, None]).astype(F32), band


def _halo_specs(tm, width, S):
    r = tm // HALO
    last = S // HALO - 1
    return [pl.BlockSpec((HALO, width), lambda i: (jnp.maximum(i * r - 1, 0), 0)),
            pl.BlockSpec((tm, width), lambda i: (i, 0)),
            pl.BlockSpec((HALO, width), lambda i: (jnp.minimum(i * r + r, last), 0))]


def _with_halo(p_ref, c_ref, n_ref):
    return jnp.concatenate([p_ref[...], c_ref[...], n_ref[...]], axis=0)


def _row_valid(i, tm, S):
    g = i * tm - HALO + lax.broadcasted_iota(jnp.int32, (tm + 2 * HALO, 1), 0)
    return (g >= 0) & (g < S)


def _shifted(x):
    n = x.shape[0]
    return [x if b == 0 else pltpu.roll(x, n - b, 0) for b in range(8)]


def _tap(sh, off, tm):
    a, b = off // 8, off % 8
    return sh[b][8 * a:8 * a + tm]


def _conv_fwd(cvg, cw, cb, lg, lb, tm, name, ex=NO_EXCHANGE):
    S = cvg.shape[0]

    def body(p_ref, c_ref, n_ref, cw_ref, cb_ref, lg_ref, lb_ref, act_ref, yc_ref):
        i = pl.program_id(0)
        z = _with_halo(p_ref, c_ref, n_ref)
        glu = jnp.where(_row_valid(i, tm, S), z[:, :CONV_DIM] * _sig(z[:, CONV_DIM:]), 0.0)
        sh = _shifted(glu)
        y = jnp.zeros((tm, CONV_DIM), F32) + cb_ref[...]
        for w in range(CONV_WIDTH):
            y = y + _tap(sh, w + 1, tm) * cw_ref[w:w + 1, :]
        yc_ref[...] = y
        mu = jnp.mean(y, axis=-1, keepdims=True)
        yc = y - mu
        rstd = lax.rsqrt(jnp.mean(yc * yc, axis=-1, keepdims=True) + 1e-5)
        ln = yc * rstd * lg_ref[...] + lb_ref[...]
        act_ref[...] = _bf(ln * _sig(ln))

    vec = pl.BlockSpec((1, CONV_DIM), lambda i: (0, 0))
    row = pl.BlockSpec((tm, CONV_DIM), lambda i: (i, 0))
    return _carrier_call(
        body, ex, _grid_ends(S // tm), name,
        out_shape=(jax.ShapeDtypeStruct((S, CONV_DIM), BF16), jax.ShapeDtypeStruct((S, CONV_DIM), F32)),
        grid=(S // tm,),
        in_specs=_halo_specs(tm, 2 * CONV_DIM, S) + [pl.BlockSpec((32, CONV_DIM), lambda i: (0, 0)), vec, vec, vec],
        out_specs=(row, row), scratch_shapes=[], args=(cvg, cvg, cvg, cw, cb, lg, lb))


def _conv_bwd(dact, yconv, cvg, cw, lg, lb, tm, name, ex=NO_EXCHANGE):
    S = cvg.shape[0]
    nsteps = S // tm

    def body(dp, dc, dn, yp, yc_, yn, zp, zc, zn, cw_ref, lg_ref, lb_ref,
             dz_ref, dcw_ref, dvec_ref, dcw_s, dvec_s, shg_s, shd_s):
        i = pl.program_id(0)

        @pl.when(i == 0)
        def _():
            dcw_s[...] = jnp.zeros_like(dcw_s)
            dvec_s[...] = jnp.zeros_like(dvec_s)

        valid = _row_valid(i, tm, S)
        own = (lax.broadcasted_iota(jnp.int32, (tm + 2 * HALO, 1), 0) >= HALO) & (
            lax.broadcasted_iota(jnp.int32, (tm + 2 * HALO, 1), 0) < HALO + tm)
        y = _with_halo(yp, yc_, yn)
        dact_ = _with_halo(dp, dc, dn)
        mu = jnp.mean(y, axis=-1, keepdims=True)
        ycen = y - mu
        rstd = lax.rsqrt(jnp.mean(ycen * ycen, axis=-1, keepdims=True) + 1e-5)
        yhat = ycen * rstd
        ln = yhat * lg_ref[...] + lb_ref[...]
        sg = _sig(ln)
        dln = dact_ * (sg * (1.0 + ln * (1.0 - sg)))
        dyhat = dln * lg_ref[...]
        dy = rstd * (dyhat - jnp.mean(dyhat, axis=-1, keepdims=True)
                     - yhat * jnp.mean(dyhat * yhat, axis=-1, keepdims=True))
        dy = jnp.where(valid, dy, 0.0)
        dln_own = jnp.where(own, dln, 0.0)
        nr = (tm + 2 * HALO) // 8
        dvec_s[0] += jnp.sum(jnp.where(own, dy, 0.0).reshape(nr, 8, CONV_DIM), axis=0)
        dvec_s[1] += jnp.sum((dln_own * yhat).reshape(nr, 8, CONV_DIM), axis=0)
        dvec_s[2] += jnp.sum(dln_own.reshape(nr, 8, CONV_DIM), axis=0)
        z = _with_halo(zp, zc, zn)
        glu = jnp.where(valid, z[:, :CONV_DIM] * _sig(z[:, CONV_DIM:]), 0.0)
        for b, (g_b, d_b) in enumerate(zip(_shifted(glu), _shifted(dy))):
            shg_s[b] = g_b
            shd_s[b] = d_b
        for cb in range(CONV_DIM // 128):
            lanes = slice(128 * cb, 128 * (cb + 1))
            for rb in range(tm // CROWS):
                r0 = rb * CROWS
                dy_own = shd_s[0, HALO + r0:HALO + r0 + CROWS, lanes]
                dglu = jnp.zeros((CROWS, 128), F32)
                for w in range(CONV_WIDTH):
                    a, b = divmod(CONV_WIDTH - w, 8)
                    dglu = dglu + shd_s[b, 8 * a + r0:8 * a + r0 + CROWS, lanes] * cw_ref[w:w + 1, lanes]
                    a, b = divmod(w + 1, 8)
                    prod = dy_own * shg_s[b, 8 * a + r0:8 * a + r0 + CROWS, lanes]
                    dcw_s[w, :, lanes] += jnp.sum(prod.reshape(CROWS // 8, 8, 128), axis=0)
                cv = zc[r0:r0 + CROWS, lanes]
                sg_o = _sig(zc[r0:r0 + CROWS, CONV_DIM + 128 * cb:CONV_DIM + 128 * (cb + 1)])
                dz_ref[r0:r0 + CROWS, lanes] = dglu * sg_o
                dz_ref[r0:r0 + CROWS, CONV_DIM + 128 * cb:CONV_DIM + 128 * (cb + 1)] = (
                    dglu * cv * sg_o * (1.0 - sg_o))

        @pl.when(i == nsteps - 1)
        def _():
            dcw_ref[...] = jnp.sum(dcw_s[...], axis=1)
            dvec_ref[...] = jnp.sum(dvec_s[...], axis=1)

    vec = pl.BlockSpec((1, CONV_DIM), lambda i: (0, 0))
    return _carrier_call(
        body, ex, _grid_ends(nsteps), name,
        out_shape=(jax.ShapeDtypeStruct((S, 2 * CONV_DIM), F32), jax.ShapeDtypeStruct((32, CONV_DIM), F32),
                   jax.ShapeDtypeStruct((8, CONV_DIM), F32)),
        grid=(nsteps,),
        in_specs=_halo_specs(tm, CONV_DIM, S) + _halo_specs(tm, CONV_DIM, S) + _halo_specs(tm, 2 * CONV_DIM, S)
        + [pl.BlockSpec((32, CONV_DIM), lambda i: (0, 0)), vec, vec],
        out_specs=(pl.BlockSpec((tm, 2 * CONV_DIM), lambda i: (i, 0)),
                   pl.BlockSpec((32, CONV_DIM), lambda i: (0, 0)), pl.BlockSpec((8, CONV_DIM), lambda i: (0, 0))),
        scratch_shapes=[pltpu.VMEM((32, 8, CONV_DIM), F32), pltpu.VMEM((8, 8, CONV_DIM), F32),
                        pltpu.VMEM((8, tm + 2 * HALO, CONV_DIM), F32), pltpu.VMEM((8, tm + 2 * HALO, CONV_DIM), F32)],
        args=(dact, dact, dact, yconv, yconv, yconv, cvg, cvg, cvg, cw, lg, lb))


def _merge_parts(un, o, cact, wg_ref, wao_ref, wco_ref):
    g = _dot(un, wg_ref[...])
    ga, gc = _sig(g[:, :D_MODEL]), _sig(g[:, D_MODEL:])
    ya = _dot(o, wao_ref[...])
    yc = _dot(cact, wco_ref[...])
    return ga, gc, ya, yc


def _merge_specs(tm):
    row = lambda w: pl.BlockSpec((tm, w), lambda i: (i, 0))
    full = lambda a, b: pl.BlockSpec((a, b), lambda i: (0, 0))
    weights = [full(D_MODEL, 2 * D_MODEL), full(Q_DIM, D_MODEL), full(CONV_DIM, D_MODEL), full(D_MODEL, D_MODEL)]
    return row, weights


def _merge_fwd(h1, un, o, cact, w_g, w_ao, w_co, w_o, tm, name, ex=NO_EXCHANGE):
    S = h1.shape[0]
    row, weights = _merge_specs(tm)

    def body(h1_ref, un_ref, o_ref, c_ref, wg_ref, wao_ref, wco_ref, wo_ref, h2_ref):
        ga, gc, ya, yc = _merge_parts(un_ref[...], o_ref[...], c_ref[...], wg_ref, wao_ref, wco_ref)
        h2_ref[...] = h1_ref[...] + _dot(_bf(ga * ya + gc * yc), wo_ref[...])

    return _carrier_call(
        body, ex, _grid_ends(S // tm), name, out_shape=(jax.ShapeDtypeStruct((S, D_MODEL), F32),),
        grid=(S // tm,),
        in_specs=[row(D_MODEL), row(D_MODEL), row(Q_DIM), row(CONV_DIM)] + weights,
        out_specs=(row(D_MODEL),), scratch_shapes=[], args=(h1, un, o, cact, w_g, w_ao, w_co, w_o))


def _merge_bwd(dh2, un, o, cact, w_g, w_ao, w_co, w_o, tm, name):
    S = dh2.shape[0]
    row, weights = _merge_specs(tm)

    def body(dh2_ref, un_ref, o_ref, c_ref, wg_ref, wao_ref, wco_ref, wo_ref,
             do_ref, dc_ref, mix_ref, dya_ref, dyc_ref, dgp_ref):
        ga, gc, ya, yc = _merge_parts(un_ref[...], o_ref[...], c_ref[...], wg_ref, wao_ref, wco_ref)
        mix_ref[...] = _bf(ga * ya + gc * yc)
        dmix = _dot_nt(_bf(dh2_ref[...]), wo_ref[...])
        dya = _bf(dmix * ga)
        dyc = _bf(dmix * gc)
        dya_ref[...] = dya
        dyc_ref[...] = dyc
        dgp_ref[:, :D_MODEL] = _bf(dmix * ya * ga * (1.0 - ga))
        dgp_ref[:, D_MODEL:] = _bf(dmix * yc * gc * (1.0 - gc))
        do_ref[...] = _dot_nt(dya, wao_ref[...])
        dc_ref[...] = _dot_nt(dyc, wco_ref[...])

    return pl.pallas_call(
        body, name=name,
        out_shape=(jax.ShapeDtypeStruct((S, Q_DIM), F32), jax.ShapeDtypeStruct((S, CONV_DIM), F32),
                   jax.ShapeDtypeStruct((S, D_MODEL), BF16), jax.ShapeDtypeStruct((S, D_MODEL), BF16),
                   jax.ShapeDtypeStruct((S, D_MODEL), BF16), jax.ShapeDtypeStruct((S, 2 * D_MODEL), BF16)),
        grid=(S // tm,),
        in_specs=[row(D_MODEL), row(D_MODEL), row(Q_DIM), row(CONV_DIM)] + weights,
        out_specs=(row(Q_DIM), row(CONV_DIM), row(D_MODEL), row(D_MODEL), row(D_MODEL), row(2 * D_MODEL)),
        compiler_params=_params(("parallel",)),
    )(dh2, un, o, cact, w_g, w_ao, w_co, w_o)


def _pe_specs(tm, layer):
    row = pl.BlockSpec((tm, D_MODEL), lambda i: (i, 0))
    vec = pl.BlockSpec((1, D_MODEL), lambda i: (0, 0))
    p_s = pl.BlockSpec((None, None, tm, 256), lambda i: (layer, 0, i, 0))
    wpp = pl.BlockSpec((256, D_MODEL), lambda i: (0, 0))
    wpg = pl.BlockSpec((D_MODEL, D_MODEL), lambda i: (0, 0))
    return row, vec, p_s, wpp, wpg


def _pe_fwd(h, gamma, p, layer, w_pp, w_pg, tm, name):
    S = h.shape[0]
    row, vec, p_s, wpp, wpg = _pe_specs(tm, layer)

    def body(h_ref, g_ref, p_ref, wpp_ref, wpg_ref, x_ref, hn_ref):
        hn = _bf(_rms_fwd(h_ref[...], g_ref[...])[0])
        hn_ref[...] = hn
        gate = _sig(_dot(hn, wpg_ref[...]))
        x_ref[...] = h_ref[...] + _dot(_bf(p_ref[...]), wpp_ref[...]) * gate

    return pl.pallas_call(
        body, name=name,
        out_shape=(jax.ShapeDtypeStruct((S, D_MODEL), F32), jax.ShapeDtypeStruct((S, D_MODEL), BF16)),
        grid=(S // tm,), in_specs=[row, vec, p_s, wpp, wpg], out_specs=(row, row),
        compiler_params=_params(("parallel",)),
    )(h, gamma, p, w_pp, w_pg)


def _pe_bwd(dx, h, gamma, hn, p, layer, w_pp, w_pg, tm, name):
    S = h.shape[0]
    row, vec, p_s, wpp, wpg = _pe_specs(tm, layer)

    def body(dx_ref, h_ref, g_ref, hn_ref, p_ref, wpp_ref, wpg_ref, dh_ref, dgp_ref, dpr_ref, dgam_ref):
        @pl.when(pl.program_id(0) == 0)
        def _():
            dgam_ref[...] = jnp.zeros_like(dgam_ref)

        dxv = dx_ref[...]
        gate = _sig(_dot(hn_ref[...], wpg_ref[...]))
        proj = _dot(_bf(p_ref[...]), wpp_ref[...])
        dpr_ref[...] = _bf(dxv * gate)
        dgp = _bf(dxv * proj * gate * (1.0 - gate))
        dgp_ref[...] = dgp
        dxn, dgam = _rms_bwd(_dot_nt(dgp, wpg_ref[...]), h_ref[...], g_ref[...])
        dh_ref[...] = dxv + dxn
        dgam_ref[...] += jnp.sum(dgam, axis=0, keepdims=True)

    return pl.pallas_call(
        body, name=name,
        out_shape=(jax.ShapeDtypeStruct((S, D_MODEL), F32), jax.ShapeDtypeStruct((S, D_MODEL), BF16),
                   jax.ShapeDtypeStruct((S, D_MODEL), BF16), jax.ShapeDtypeStruct((1, D_MODEL), F32)),
        grid=(S // tm,), in_specs=[row, row, vec, row, p_s, wpp, wpg], out_specs=(row, row, row, vec),
        compiler_params=_params(("arbitrary",)),
    )(dx, h, gamma, hn, p, w_pp, w_pg)


def _loss_head(y, target, tm):
    S = y.shape[0]

    def body(y_ref, t_ref, dy_ref, l_ref):
        @pl.when(pl.program_id(0) == 0)
        def _():
            l_ref[...] = jnp.zeros_like(l_ref)

        diff = y_ref[...] - t_ref[...]
        dy_ref[...] = diff * (1.0 / D_MODEL)
        sq = jnp.sum((diff * diff).reshape(tm // 8, 8, D_MODEL), axis=0)
        part = sq[:, 0:128]
        for k in range(1, D_MODEL // 128):
            part = part + sq[:, 128 * k:128 * (k + 1)]
        l_ref[...] += part

    row = pl.BlockSpec((tm, D_MODEL), lambda i: (i, 0))
    return pl.pallas_call(
        body, name="loss_head",
        out_shape=(jax.ShapeDtypeStruct((S, D_MODEL), F32), jax.ShapeDtypeStruct((8, 128), F32)),
        grid=(S // tm,), in_specs=[row, row], out_specs=(row, pl.BlockSpec((8, 128), lambda i: (0, 0))),
        compiler_params=_params(("arbitrary",)),
    )(y, target)


def _adamw(parts, w, m, v, name):
    R, C = w.shape
    tr = R
    for cand in (256, 128, 64, 32, 16):
        if R % cand == 0:
            tr = cand
            break

    def body(p_ref, w_ref, m_ref, v_ref, g_ref, d_ref, nm_ref, nv_ref):
        g = p_ref[0].astype(F32)
        for k in range(1, N_DEV):
            g = g + p_ref[k].astype(F32)
        g_ref[...] = g
        nm = ADAM_B1 * m_ref[...] + (1.0 - ADAM_B1) * g
        nv = ADAM_B2 * v_ref[...] + (1.0 - ADAM_B2) * (g * g)
        nm_ref[...] = nm
        nv_ref[...] = nv
        m_hat = nm / (1.0 - ADAM_B1 ** ADAM_STEP)
        v_hat = nv / (1.0 - ADAM_B2 ** ADAM_STEP)
        d_ref[...] = -ADAM_LR * (m_hat / (jnp.sqrt(v_hat) + ADAM_EPS) + ADAM_WD * w_ref[...])

    blk = pl.BlockSpec((tr, C), lambda i: (i, 0))
    out = jax.ShapeDtypeStruct((R, C), F32)
    return pl.pallas_call(
        body, name=name, out_shape=(out, out, out, out), grid=(R // tr,),
        in_specs=[pl.BlockSpec((N_DEV, tr, C), lambda i: (0, i, 0)), blk, blk, blk],
        out_specs=(blk, blk, blk, blk),
        compiler_params=_params(("parallel",)),
    )(parts, w, m, v)


SHARDED = ("w_ffn1_in", "w_ffn1_out", "w_in", "conv_w", "w_attn_out", "w_conv_out", "w_o",
           "w_ffn2_in", "w_ffn2_out", "w_pe_gate", "w_pe_proj")
COL_SHARDED = ("w_ffn1_in", "w_in", "conv_w", "w_attn_out", "w_conv_out", "w_ffn2_in", "w_pe_proj")
SMALL = ("rel_bias", "norm_ffn1", "norm_mix", "q_norm", "k_norm", "sink", "conv_b", "conv_ln_g", "conv_ln_b",
         "norm_ffn2", "norm_pe")
WEIGHTS = ("rel_bias", "norm_ffn1", "w_ffn1_in", "w_ffn1_out", "norm_mix", "w_in", "q_norm", "k_norm", "sink",
           "conv_w", "conv_b", "conv_ln_g", "conv_ln_b", "w_attn_out", "w_conv_out", "w_o", "norm_ffn2",
           "w_ffn2_in", "w_ffn2_out", "norm_pe", "w_pe_gate", "w_pe_proj")


def _natural(g):
    k, n = g.shape[1], g.shape[2]
    return jnp.transpose(g, (1, 0, 2)).reshape(k, N_DEV * n)


def _blocked(w):
    k, n = w.shape[0], w.shape[1] // N_DEV
    return jnp.transpose(w.reshape(k, N_DEV, n), (1, 0, 2))


def kernel(x, p, rel_bias, norm_ffn1, w_ffn1_in, w_ffn1_out, norm_mix, w_in, q_norm, k_norm, sink, conv_w, conv_b, conv_ln_g, conv_ln_b, w_attn_out, w_conv_out, w_o, norm_ffn2, w_ffn2_in, w_ffn2_out, norm_pe, w_pe_gate, w_pe_proj, loss_target, m_rel_bias, m_norm_ffn1, m_w_ffn1_in, m_w_ffn1_out, m_norm_mix, m_w_in, m_q_norm, m_k_norm, m_sink, m_conv_w, m_conv_b, m_conv_ln_g, m_conv_ln_b, m_w_attn_out, m_w_conv_out, m_w_o, m_norm_ffn2, m_w_ffn2_in, m_w_ffn2_out, m_norm_pe, m_w_pe_gate, m_w_pe_proj, v_rel_bias, v_norm_ffn1, v_w_ffn1_in, v_w_ffn1_out, v_norm_mix, v_w_in, v_q_norm, v_k_norm, v_sink, v_conv_w, v_conv_b, v_conv_ln_g, v_conv_ln_b, v_w_attn_out, v_w_conv_out, v_w_o, v_norm_ffn2, v_w_ffn2_in, v_w_ffn2_out, v_norm_pe, v_w_pe_gate, v_w_pe_proj):
    W = dict(rel_bias=rel_bias, norm_ffn1=norm_ffn1, w_ffn1_in=w_ffn1_in, w_ffn1_out=w_ffn1_out, norm_mix=norm_mix,
             w_in=w_in, q_norm=q_norm, k_norm=k_norm, sink=sink, conv_w=conv_w, conv_b=conv_b, conv_ln_g=conv_ln_g,
             conv_ln_b=conv_ln_b, w_attn_out=w_attn_out, w_conv_out=w_conv_out, w_o=w_o, norm_ffn2=norm_ffn2,
             w_ffn2_in=w_ffn2_in, w_ffn2_out=w_ffn2_out, norm_pe=norm_pe, w_pe_gate=w_pe_gate, w_pe_proj=w_pe_proj)
    M = dict(rel_bias=m_rel_bias, norm_ffn1=m_norm_ffn1, w_ffn1_in=m_w_ffn1_in, w_ffn1_out=m_w_ffn1_out,
             norm_mix=m_norm_mix, w_in=m_w_in, q_norm=m_q_norm, k_norm=m_k_norm, sink=m_sink, conv_w=m_conv_w,
             conv_b=m_conv_b, conv_ln_g=m_conv_ln_g, conv_ln_b=m_conv_ln_b, w_attn_out=m_w_attn_out,
             w_conv_out=m_w_conv_out, w_o=m_w_o, norm_ffn2=m_norm_ffn2, w_ffn2_in=m_w_ffn2_in,
             w_ffn2_out=m_w_ffn2_out, norm_pe=m_norm_pe, w_pe_gate=m_w_pe_gate, w_pe_proj=m_w_pe_proj)
    V = dict(rel_bias=v_rel_bias, norm_ffn1=v_norm_ffn1, w_ffn1_in=v_w_ffn1_in, w_ffn1_out=v_w_ffn1_out,
             norm_mix=v_norm_mix, w_in=v_w_in, q_norm=v_q_norm, k_norm=v_k_norm, sink=v_sink, conv_w=v_conv_w,
             conv_b=v_conv_b, conv_ln_g=v_conv_ln_g, conv_ln_b=v_conv_ln_b, w_attn_out=v_w_attn_out,
             w_conv_out=v_w_conv_out, w_o=v_w_o, norm_ffn2=v_norm_ffn2, w_ffn2_in=v_w_ffn2_in,
             w_ffn2_out=v_w_ffn2_out, norm_pe=v_norm_pe, w_pe_gate=v_w_pe_gate, w_pe_proj=v_w_pe_proj)

    L = w_in.shape[0]
    S = x.shape[1]
    tm = min(512, S)
    tm_ffn = min(1024, S)
    xs = x[0]
    target = loss_target[0]
    vec = lambda a: a.reshape(1, -1)

    set_ffn1 = ("w_ffn1_in", "w_ffn1_out")
    set_a = ("w_in", "conv_w", "w_ffn2_in")
    set_b = ("w_ffn2_out",)
    set_c = ("w_attn_out", "w_conv_out", "w_o")
    set_d = ("w_pe_gate", "w_pe_proj")

    def shards(names, l):
        return [W[n][l] if n == "conv_w" else W[n][l].astype(BF16) for n in names]

    onehot, band = _bucket_onehot()
    bias = _bias_table(rel_bias.T, onehot, band).reshape(N_HEADS, BLOCK, WIN)

    layers, saved = [], []
    h = xs
    ffn1_w = _exchange(shards(set_ffn1, 0), True, "allgather_first")
    for l in range(L):
        sv = dict(x0=h)
        G = dict(wi1=ffn1_w[0], wo1=ffn1_w[1])
        (h1, sv["xn1"], sv["g1"], sv["u1"]), got = _ffn_fwd(
            h, vec(norm_ffn1[l]), G["wi1"], G["wo1"], tm_ffn, "ffn1_fwd", _Exchange(shards(set_a, l), True))
        w_in_n = _natural(got[0])
        G.update(w_qc=w_in_n[:, :QC_DIM], w_g=w_in_n[:, QC_DIM:],
                 conv_w=jnp.pad(_natural(got[1]), ((0, 1), (0, 0))), wi2=got[2])
        sv["h1"] = h1
        sv["un"], sv["qkv"], sv["cvg"] = _mixin_fwd(h1, vec(norm_mix[l]), G["w_qc"], tm, "mixin_fwd")
        sv["qg"] = vec(jnp.tile(q_norm[l], 2))
        sv["kg"] = vec(jnp.tile(k_norm[l], KV_HEADS))
        sv["sinkb"] = jnp.broadcast_to(sink[l][:, None], (N_HEADS, 128))
        (sv["o"],), got = _attn_fwd(sv["qkv"], sv["qg"], sv["kg"], sv["sinkb"], bias, "attn_fwd",
                                    _Exchange(shards(set_b, l), True))
        G.update(wo2=got[0])
        (sv["cact"], sv["yconv"]), got = _conv_fwd(
            sv["cvg"], G["conv_w"], vec(conv_b[l]), vec(conv_ln_g[l]), vec(conv_ln_b[l]), tm, "conv_fwd",
            _Exchange(shards(set_c, l), True))
        G.update(w_ao=_natural(got[0]), w_co=_natural(got[1]), w_o=got[2].reshape(D_MODEL, D_MODEL))
        (h2,), got = _merge_fwd(h1, sv["un"], sv["o"], sv["cact"], G["w_g"], G["w_ao"], G["w_co"], G["w_o"], tm,
                                "merge_fwd", _Exchange(shards(set_d, l), True))
        G.update(w_pg=got[0].reshape(D_MODEL, D_MODEL), w_pp=_natural(got[1]))
        sv["h2"] = h2
        nxt = _Exchange(shards(set_ffn1, l + 1), True) if l + 1 < L else NO_EXCHANGE
        (h3, sv["xn2"], sv["g2"], sv["u2"]), ffn1_w = _ffn_fwd(
            h2, vec(norm_ffn2[l]), G["wi2"], G["wo2"], tm_ffn, "ffn2_fwd", nxt)
        sv["h3"] = h3
        h, sv["hn"] = _pe_fwd(h3, vec(norm_pe[l]), p, l, G["w_pp"], G["w_pg"], tm, "pe_fwd")
        layers.append(G)
        saved.append(sv)

    dh, lparts = _loss_head(h, target, tm)
    loss = lax.psum((0.5 / D_MODEL) * jnp.sum(lparts), AXES)

    dbias = jnp.zeros((N_HEADS, BLOCK, WIN), F32)
    small_g = {n: [None] * L for n in SMALL if n != "rel_bias"}
    recv = {n: [None] * L for n in SHARDED}

    def keep(names, l, got):
        for n, r in zip(names, got):
            recv[n][l] = r

    pending = None
    for l in reversed(range(L)):
        G, sv = layers[l], saved[l]
        dh3, dgp_pe, dproj, dg_pe = _pe_bwd(dh, sv["h3"], vec(norm_pe[l]), sv["hn"], p, l, G["w_pp"], G["w_pg"],
                                            tm, "pe_bwd")
        gw_pg = _matmul_tn(sv["hn"][None], dgp_pe[None], 1, "dw_pe_gate")
        gw_pp = _matmul_tn(p[l], dproj[None], 1, "dw_pe_proj")
        (dh2, a2, dgu2, dg_n2), got = _ffn_bwd(
            dh3, sv["h2"], vec(norm_ffn2[l]), sv["g2"], sv["u2"], G["wi2"], G["wo2"], tm_ffn, "ffn2_bwd",
            _Exchange(pending, False) if pending else NO_EXCHANGE)
        if pending:
            keep(("w_ffn1_in",), l + 1, got)
        gwo2 = _matmul_tn(a2, dh3[None], FF_BLOCKS, "dw_ffn2_out", scale=0.5)
        gwi2 = _matmul_tn(sv["xn2"][None], dgu2.reshape(2 * FF_BLOCKS, S, FF_SHARD), 2 * FF_BLOCKS, "dw_ffn2_in")
        do, dcact, mix, dya, dyc, dgpre = _merge_bwd(dh2, sv["un"], sv["o"], sv["cact"], G["w_g"], G["w_ao"],
                                                     G["w_co"], G["w_o"], tm, "merge_bwd")
        gw_o = _matmul_tn(mix[None], dh2[None], 1, "dw_o")
        gw_ao = _matmul_tn(sv["o"][None], dya[None], 1, "dw_attn_out")
        gw_co = _matmul_tn(sv["cact"][None], dyc[None], 1, "dw_conv_out")
        (dq, dkp, dvp, dbias, dsink, dqg), got = _attn_bwd(
            do, sv["qkv"], sv["qg"], sv["kg"], sv["sinkb"], bias, dbias, "attn_bwd",
            _Exchange([gwi2, gw_pg.reshape(N_DEV, D_MODEL // N_DEV, D_MODEL), _blocked(gw_pp[0])], False))
        keep(("w_ffn2_in", "w_pe_gate", "w_pe_proj"), l, got)
        dkv, dkg = _kv_fold(dkp, dvp, sv["qkv"], sv["kg"], "kv_fold")
        (dcvg, dcw, dcvec), got = _conv_bwd(
            dcact, sv["yconv"], sv["cvg"], G["conv_w"], vec(conv_ln_g[l]), vec(conv_ln_b[l]), tm, "conv_bwd",
            _Exchange([gwo2.reshape(N_DEV, FF_SHARD // 2, D_MODEL)], False))
        keep(("w_ffn2_out",), l, got)
        dh1, dg_mix = _mixin_bwd(dh2, sv["h1"], vec(norm_mix[l]), dq, dkv, dcvg, dgpre, G["w_qc"], G["w_g"],
                                 tm, "mixin_bwd")
        un3 = sv["un"][None]
        gw_in = jnp.concatenate([
            _matmul_tn(un3, dq[None], 1, "dw_in_q")[0], _matmul_tn(un3, dkv[None], 1, "dw_in_kv")[0],
            _matmul_tn(un3, dcvg[None], 1, "dw_in_c")[0], _matmul_tn(un3, dgpre[None], 1, "dw_in_g")[0]], axis=1)
        mid_send = [_blocked(gw_in), _blocked(dcw[:CONV_WIDTH]), _blocked(gw_ao[0]), _blocked(gw_co[0]),
                    gw_o.reshape(N_DEV, D_MODEL // N_DEV, D_MODEL)]
        (dh, a1, dgu1, dg_n1), got = _ffn_bwd(
            dh1, sv["x0"], vec(norm_ffn1[l]), sv["g1"], sv["u1"], G["wi1"], G["wo1"], tm_ffn, "ffn1_bwd",
            _Exchange(mid_send, False))
        keep(("w_in", "conv_w", "w_attn_out", "w_conv_out", "w_o"), l, got)
        gwo1 = _matmul_tn(a1, dh1[None], FF_BLOCKS, "dw_ffn1_out", scale=0.5)
        gwi1, got = _matmul_tn(sv["xn1"][None], dgu1.reshape(2 * FF_BLOCKS, S, FF_SHARD), 2 * FF_BLOCKS,
                               "dw_ffn1_in", ex=_Exchange([gwo1.reshape(N_DEV, FF_SHARD // 2, D_MODEL)], False))
        keep(("w_ffn1_out",), l, got)
        pending = [gwi1]
        small_g["norm_ffn1"][l] = dg_n1[0]
        small_g["norm_mix"][l] = dg_mix[0]
        small_g["q_norm"][l] = dqg[0, :HEAD_DIM]
        small_g["k_norm"][l] = dkg[0, :HEAD_DIM]
        small_g["sink"][l] = dsink[:, 0]
        small_g["conv_b"][l] = dcvec[0]
        small_g["conv_ln_g"][l] = dcvec[1]
        small_g["conv_ln_b"][l] = dcvec[2]
        small_g["norm_ffn2"][l] = dg_n2[0]
        small_g["norm_pe"][l] = dg_pe[0]

    keep(("w_ffn1_in",), 0, _exchange(pending, False, "grad_exchange_last"))
    grad_x = dh[None]
    drb = _bias_grad(dbias.reshape(N_HEADS, BIAS_COLS), onehot).T

    res = {}
    for n in SHARDED:
        parts = jnp.stack(recv[n], axis=1)
        shp = W[n].shape
        rows, cols = shp[0] * shp[1], shp[2]
        res[n] = [o.reshape(shp) for o in _adamw(
            parts.reshape(N_DEV, rows, cols), W[n].reshape(rows, cols), M[n].reshape(rows, cols),
            V[n].reshape(rows, cols), "adamw_" + n)]

    flat_g = jnp.concatenate([drb.reshape(-1)] + [jnp.stack(small_g[n]).reshape(-1) for n in SMALL[1:]])
    n_small = flat_g.shape[0]
    rows_s = -(-n_small // 1024 // 8) * 8
    pad = lambda a: jnp.pad(a, (0, rows_s * 1024 - n_small)).reshape(rows_s, 1024)
    flat = lambda d: pad(jnp.concatenate([d[n].reshape(-1) for n in SMALL]))
    (parts_s,) = _exchange([pad(flat_g)], True, "small_allgather")
    outs_s = _adamw(parts_s, flat(W), flat(M), flat(V), "adamw_small")
    off = 0
    for n in SMALL:
        size = W[n].size
        res[n] = [o.reshape(-1)[off:off + size].reshape(W[n].shape) for o in outs_s]
        off += size

    out = [loss, grad_x]
    for k in range(4):
        out += [res[n][k] for n in WEIGHTS]
    return tuple(out)
```

```python
import functools

import jax
import jax.numpy as jnp
import numpy as np
from jax import lax
from jax.experimental import pallas as pl
from jax.experimental.pallas import tpu as pltpu

F32 = jnp.float32
BF16 = jnp.bfloat16
MESH_ID = pl.DeviceIdType.MESH
AXES = ("x", "y", "c")
N_DEV = 8

D_MODEL = 1024
N_HEADS = 8
KV_HEADS = 2
HEAD_DIM = 64
Q_DIM = 512
KV_DIM = 128
BLOCK = 128
WIN = 3 * BLOCK
NUM_BUCKETS = 32
MAX_DISTANCE = 128
CONV_DIM = 512
CONV_WIDTH = 31
D_FF = 2816
FF_SHARD = 2 * D_FF // N_DEV
FF_BLOCKS = D_FF // FF_SHARD
QC_DIM = Q_DIM + 2 * KV_DIM + 2 * CONV_DIM
NEG_INF = -1e9
HALO = 16
CROWS = 64
FFN_PART_ROWS = 256

ADAM_LR = 0.001
ADAM_B1 = 0.9
ADAM_B2 = 0.999
ADAM_EPS = 1e-08
ADAM_WD = 0.01
ADAM_STEP = 10

VMEM_LIMIT = 56 * 1024 * 1024
HI = lax.Precision.HIGHEST


def _params(sem):
    return pltpu.CompilerParams(dimension_semantics=sem, vmem_limit_bytes=VMEM_LIMIT)


def _dot(a, b, precision=None):
    return jnp.dot(a, b, preferred_element_type=F32, precision=precision)


def _dot_nt(a, b):
    return lax.dot_general(a, b, (((1,), (1,)), ((), ())), preferred_element_type=F32)


def _dot_tn(a, b):
    return lax.dot_general(a, b, (((0,), (0,)), ((), ())), preferred_element_type=F32)


def _sig(x):
    return 1.0 / (1.0 + jnp.exp(-x))


def _bf(x):
    return x.astype(BF16)


def _rms_fwd(x, gamma):
    r = lax.rsqrt(jnp.mean(x * x, axis=-1, keepdims=True) + 1e-6)
    return x * r * gamma, r


def _rms_bwd(dy, x, gamma):
    r = lax.rsqrt(jnp.mean(x * x, axis=-1, keepdims=True) + 1e-6)
    xhat = x * r
    dxhat = dy * gamma
    dx = r * (dxhat - xhat * jnp.mean(dxhat * xhat, axis=-1, keepdims=True))
    return dx, dy * xhat


SAME_CORE = (2, 4, 6)


class _Exchange:
    def __init__(self, arrs=(), gather=True, kinds=None):
        self.arrs = list(arrs)
        self.n = n = len(self.arrs)
        self.kinds = list(kinds) if kinds is not None else ["gather" if gather else "scatter"] * n
        self.out_shape = tuple(
            jax.ShapeDtypeStruct(((N_DEV,) + a.shape) if k in ("gather", "stage1") else a.shape, a.dtype)
            for a, k in zip(self.arrs, self.kinds))
        self.aliases = {t: t for t, k in enumerate(self.kinds) if k == "stage2"}
        self.specs = [pl.BlockSpec(memory_space=pl.ANY)] * n
        self.scratch = [pltpu.SemaphoreType.DMA((7 * n,)), pltpu.SemaphoreType.DMA((7 * n,)),
                        pltpu.SemaphoreType.DMA((n,))] if n else []

    def __add__(self, other):
        return _Exchange(self.arrs + other.arrs, kinds=self.kinds + other.kinds)

    def _copies(self, ins, outs, sems):
        n = self.n
        send_sems, recv_sems, local_sems = sems
        x, y, c = lax.axis_index("x"), lax.axis_index("y"), lax.axis_index("c")
        me = 4 * x + 2 * y + c
        copies = []
        for t, kind in enumerate(self.kinds):
            if kind != "stage2":
                copies.append(pltpu.make_async_copy(
                    ins[t].at[me] if kind == "scatter" else ins[t], outs[t].at[me], local_sems.at[t]))
            offsets = {"gather": range(1, N_DEV), "scatter": range(1, N_DEV),
                       "stage1": (1,) + SAME_CORE, "stage2": SAME_CORE}[kind]
            for d in offsets:
                px = 1 - x if d & 4 else x
                py = 1 - y if d & 2 else y
                pc = 1 - c if d & 1 else c
                peer = 4 * px + 2 * py + pc
                if kind == "stage2":
                    src, dst, to = ins[t].at[peer], outs[t].at[peer], (x, y, 1 - c)
                else:
                    src, dst, to = (ins[t].at[peer] if kind == "scatter" else ins[t]), outs[t].at[me], (px, py, pc)
                k = (d - 1) * n + t
                copies.append(pltpu.make_async_remote_copy(
                    src_ref=src, dst_ref=dst, send_sem=send_sems.at[k], recv_sem=recv_sems.at[k],
                    device_id=to, device_id_type=MESH_ID))
        return copies

    def start(self, ins, outs, sems):
        for cp in self._copies(ins, outs, sems):
            cp.start()

    def wait(self, ins, outs, sems):
        for cp in self._copies(ins, outs, sems):
            cp.wait()


NO_EXCHANGE = _Exchange()


def _stage1(arrs):
    return _Exchange(arrs, kinds=["stage1"] * len(arrs))


def _stage2(arrs):
    return _Exchange(arrs, kinds=["stage2"] * len(arrs))


def _exchange(ex, name):
    n = ex.n

    def body(*refs):
        ins, outs, sems = refs[:n], refs[n:2 * n], refs[2 * n:]
        ex.start(ins, outs, sems)
        ex.wait(ins, outs, sems)

    return pl.pallas_call(
        body, name=name, out_shape=ex.out_shape, in_specs=ex.specs, out_specs=tuple(ex.specs),
        scratch_shapes=ex.scratch, input_output_aliases=ex.aliases,
    )(*ex.arrs)


def _carrier_call(body, ex, first_last, name, out_shape, grid, in_specs, out_specs, scratch_shapes, args):
    n_in, n_out, n_scr, n = len(in_specs), len(out_shape), len(scratch_shapes), ex.n

    def full(*refs):
        a, ci = refs[:n_in], refs[n_in:n_in + n]
        o = refs[n_in + n:n_in + n + n_out]
        co = refs[n_in + n + n_out:n_in + 2 * n + n_out]
        scr = refs[n_in + 2 * n + n_out:n_in + 2 * n + n_out + n_scr]
        sems = refs[n_in + 2 * n + n_out + n_scr:]
        first, last = first_last()
        if n:
            @pl.when(first)
            def _():
                ex.start(ci, co, sems)

        body(*a, *o, *scr)
        if n:
            @pl.when(last)
            def _():
                ex.wait(ci, co, sems)

    outs = pl.pallas_call(
        full, name=name, out_shape=tuple(out_shape) + ex.out_shape, grid=grid,
        in_specs=list(in_specs) + ex.specs, out_specs=tuple(out_specs) + tuple(ex.specs),
        scratch_shapes=list(scratch_shapes) + ex.scratch,
        input_output_aliases={n_in + t: n_out + u for t, u in ex.aliases.items()},
        compiler_params=_params(("arbitrary",) * len(grid)),
    )(*args, *ex.arrs)
    return outs[:n_out], outs[n_out:]


def _matmul_tn(a, b, nb, name, scale=1.0, out_dtype=BF16, ts=1024, ex=None):
    ba, S, K = a.shape
    bb, _, N = b.shape
    ts = min(ts, S)
    tn = N if N <= 1024 else next(c for c in (1280, 1024, 768, 512, 256) if N % c == 0)
    assert S % ts == 0
    ns = S // ts

    def body(a_ref, b_ref, o_ref, acc):
        s = pl.program_id(2)

        @pl.when(s == 0)
        def _():
            acc[...] = jnp.zeros_like(acc)

        acc[...] += _dot_tn(_bf(a_ref[...]), _bf(b_ref[...]))

        @pl.when(s == ns - 1)
        def _():
            o_ref[...] = (acc[...] * scale).astype(out_dtype)

    in_specs = [pl.BlockSpec((None, ts, K), (lambda i, j, s: (i, s, 0)) if ba > 1 else (lambda i, j, s: (0, s, 0))),
                pl.BlockSpec((None, ts, tn), (lambda i, j, s: (i, s, j)) if bb > 1 else (lambda i, j, s: (0, s, j)))]
    out_spec = pl.BlockSpec((None, K, tn), lambda i, j, s: (i, 0, j))
    out_shape = jax.ShapeDtypeStruct((nb, K, N), out_dtype)
    scratch = [pltpu.VMEM((K, tn), F32)]
    grid = (nb, N // tn, ns)
    if ex is not None:
        (out,), got = _carrier_call(body, ex, _grid_ends(*grid), name, out_shape=(out_shape,), grid=grid,
                                    in_specs=in_specs, out_specs=(out_spec,), scratch_shapes=scratch, args=(a, b))
        return out, got
    return pl.pallas_call(
        body, name=name, out_shape=out_shape, grid=grid, in_specs=in_specs, out_specs=out_spec,
        scratch_shapes=scratch, compiler_params=_params(("parallel", "parallel", "arbitrary")),
    )(a, b)


def _ffn_specs(tm):
    wg = pl.BlockSpec((None, D_MODEL, FF_SHARD), lambda i, j: (j, 0, 0))
    wu = pl.BlockSpec((None, D_MODEL, FF_SHARD), lambda i, j: (j + FF_BLOCKS, 0, 0))
    wo = pl.BlockSpec((2, FF_SHARD // 2, D_MODEL), lambda i, j: (j, 0, 0))
    row = pl.BlockSpec((tm, D_MODEL), lambda i, j: (i, 0))
    vec = pl.BlockSpec((1, D_MODEL), lambda i, j: (0, 0))
    hid = pl.BlockSpec((None, tm, FF_SHARD), lambda i, j: (j, i, 0))
    return wg, wu, wo, row, vec, hid


def _grid_ends(*grid):
    def first_last():
        first, last = None, None
        for d, n in enumerate(grid):
            i = pl.program_id(d)
            first = (i == 0) if first is None else first & (i == 0)
            last = (i == n - 1) if last is None else last & (i == n - 1)
        return first, last
    return first_last


def _grid2_ends(ni, nj):
    return _grid_ends(ni, nj)


def _grid1_ends(ni):
    return _grid_ends(ni)


def _ffn_fwd(x, gamma, wi, wo, tm, name, ex=NO_EXCHANGE):
    S = x.shape[0]
    wg_s, wu_s, wo_s, row, vec, hid = _ffn_specs(tm)

    def body(x_ref, g_ref, wg_ref, wu_ref, wo_ref, y_ref, xn_ref, gs_ref, us_ref, xn_s, acc):
        j = pl.program_id(1)

        @pl.when(j == 0)
        def _():
            xn = _bf(_rms_fwd(x_ref[...], g_ref[...])[0])
            xn_s[...] = xn
            xn_ref[...] = xn
            acc[...] = jnp.zeros_like(acc)

        wo2 = wo_ref[...].reshape(FF_SHARD, D_MODEL)
        for r in range(tm // FFN_PART_ROWS):
            rows = slice(r * FFN_PART_ROWS, (r + 1) * FFN_PART_ROWS)
            xn = xn_s[rows, :]
            g = _dot(xn, wg_ref[...])
            u = _dot(xn, wu_ref[...])
            gs_ref[rows, :] = _bf(g)
            us_ref[rows, :] = _bf(u)
            a = g * _sig(g) * u
            acc[rows, :] += _dot(_bf(a), wo2)

        @pl.when(j == FF_BLOCKS - 1)
        def _():
            y_ref[...] = x_ref[...] + 0.5 * acc[...]

    return _carrier_call(
        body, ex, _grid2_ends(S // tm, FF_BLOCKS), name,
        out_shape=(jax.ShapeDtypeStruct((S, D_MODEL), F32), jax.ShapeDtypeStruct((S, D_MODEL), BF16),
                   jax.ShapeDtypeStruct((FF_BLOCKS, S, FF_SHARD), BF16),
                   jax.ShapeDtypeStruct((FF_BLOCKS, S, FF_SHARD), BF16)),
        grid=(S // tm, FF_BLOCKS),
        in_specs=[row, vec, wg_s, wu_s, wo_s],
        out_specs=(row, row, hid, hid),
        scratch_shapes=[pltpu.VMEM((tm, D_MODEL), BF16), pltpu.VMEM((tm, D_MODEL), F32)],
        args=(x, gamma, wi, wi, wo))


def _ffn_bwd(dy, x, gamma, gs, us, wi, wo, tm, name, ex=NO_EXCHANGE):
    S = x.shape[0]
    wg_s, wu_s, wo_s, row, vec, hid = _ffn_specs(tm)
    dgu_s = pl.BlockSpec((2, None, tm, FF_SHARD), lambda i, j: (0, j, i, 0))

    def body(dy_ref, x_hbm, g_ref, gs_ref, us_ref, wg_ref, wu_ref, wo_ref,
             dx_ref, a_ref, dgu_ref, dgam_ref, dyh_s, x_buf, x_sem):
        i, j = pl.program_id(0), pl.program_id(1)
        acc = dx_ref
        x_copy = pltpu.make_async_copy(x_hbm.at[pl.ds(pl.multiple_of(i * tm, tm), tm), :], x_buf, x_sem)

        @pl.when(j == 0)
        def _():
            x_copy.start()
            dyh_s[...] = _bf(0.5 * dy_ref[...])
            acc[...] = jnp.zeros_like(acc)

        @pl.when((i == 0) & (j == 0))
        def _():
            dgam_ref[...] = jnp.zeros_like(dgam_ref)

        wo2 = wo_ref[...].reshape(FF_SHARD, D_MODEL)
        for r in range(tm // FFN_PART_ROWS):
            rows = slice(r * FFN_PART_ROWS, (r + 1) * FFN_PART_ROWS)
            da = _dot_nt(dyh_s[rows, :], wo2)
            g = gs_ref[rows, :].astype(F32)
            u = us_ref[rows, :].astype(F32)
            sg = _sig(g)
            sl = g * sg
            a_ref[rows, :] = _bf(sl * u)
            dg = _bf(da * u * (sg * (1.0 + g * (1.0 - sg))))
            du = _bf(da * sl)
            dgu_ref[0, rows, :] = dg
            dgu_ref[1, rows, :] = du
            acc[rows, :] += _dot_nt(dg, wg_ref[...]) + _dot_nt(du, wu_ref[...])

        @pl.when(j == FF_BLOCKS - 1)
        def _():
            x_copy.wait()
            dx, dgam = _rms_bwd(acc[...], x_buf[...], g_ref[...])
            dx_ref[...] = dy_ref[...] + dx
            dgam_ref[...] += jnp.sum(dgam, axis=0, keepdims=True)

    return _carrier_call(
        body, ex, _grid2_ends(S // tm, FF_BLOCKS), name,
        out_shape=(jax.ShapeDtypeStruct((S, D_MODEL), F32),
                   jax.ShapeDtypeStruct((FF_BLOCKS, S, FF_SHARD), BF16),
                   jax.ShapeDtypeStruct((2, FF_BLOCKS, S, FF_SHARD), BF16),
                   jax.ShapeDtypeStruct((1, D_MODEL), F32)),
        grid=(S // tm, FF_BLOCKS),
        in_specs=[row, pl.BlockSpec(memory_space=pl.ANY), vec, hid, hid, wg_s, wu_s, wo_s],
        out_specs=(row, hid, dgu_s, vec),
        scratch_shapes=[pltpu.VMEM((tm, D_MODEL), BF16), pltpu.VMEM((tm, D_MODEL), F32),
                        pltpu.SemaphoreType.DMA(())],
        args=(dy, x, gamma, gs, us, wi, wi, wo))


def _mixin_fwd(h, gamma, w_qc, tm, name):
    S = h.shape[0]
    nqkv = Q_DIM + 2 * KV_DIM

    def body(h_ref, g_ref, w_ref, un_ref, qkv_ref, cvg_ref):
        un = _bf(_rms_fwd(h_ref[...], g_ref[...])[0])
        un_ref[...] = un
        z = _dot(un, w_ref[...])
        qkv_ref[...] = z[:, :nqkv]
        cvg_ref[...] = z[:, nqkv:]

    row = lambda w: pl.BlockSpec((tm, w), lambda i: (i, 0))
    return pl.pallas_call(
        body, name=name,
        out_shape=(jax.ShapeDtypeStruct((S, D_MODEL), BF16), jax.ShapeDtypeStruct((S, nqkv), F32),
                   jax.ShapeDtypeStruct((S, 2 * CONV_DIM), F32)),
        grid=(S // tm,),
        in_specs=[row(D_MODEL), pl.BlockSpec((1, D_MODEL), lambda i: (0, 0)),
                  pl.BlockSpec((D_MODEL, QC_DIM), lambda i: (0, 0))],
        out_specs=(row(D_MODEL), row(nqkv), row(2 * CONV_DIM)),
        compiler_params=_params(("parallel",)),
    )(h, gamma, w_qc)


def _mixin_bwd(dh2, h1, gamma, dq, dkv, dcvg, dgpre, w_qc, w_g, tm, name):
    S = h1.shape[0]
    nqkv = Q_DIM + 2 * KV_DIM
    n_in = QC_DIM + 2 * D_MODEL

    def body(dh2_ref, h1_ref, g_ref, dq_ref, dkv_ref, dcvg_ref, dgp_ref, wqc_ref, wg_ref,
             dh1_ref, dgam_ref, dz_ref):
        @pl.when(pl.program_id(0) == 0)
        def _():
            dgam_ref[...] = jnp.zeros_like(dgam_ref)

        wqc = wqc_ref[...]
        dq, dkv, dcvg = _bf(dq_ref[...]), _bf(dkv_ref[...]), _bf(dcvg_ref[...])
        dz_ref[:, :Q_DIM] = dq
        dz_ref[:, Q_DIM:nqkv] = dkv
        dz_ref[:, nqkv:QC_DIM] = dcvg
        dz_ref[:, QC_DIM:] = dgp_ref[...]
        dun = _dot_nt(dq, wqc[:, :Q_DIM])
        dun += _dot_nt(dkv, wqc[:, Q_DIM:nqkv])
        dun += _dot_nt(dcvg, wqc[:, nqkv:])
        dun += _dot_nt(dgp_ref[...], wg_ref[...])
        dx, dgam = _rms_bwd(dun, h1_ref[...], g_ref[...])
        dh1_ref[...] = dh2_ref[...] + dx
        dgam_ref[...] += jnp.sum(dgam, axis=0, keepdims=True)

    row = lambda w: pl.BlockSpec((tm, w), lambda i: (i, 0))
    vec = pl.BlockSpec((1, D_MODEL), lambda i: (0, 0))
    return pl.pallas_call(
        body, name=name,
        out_shape=(jax.ShapeDtypeStruct((S, D_MODEL), F32), jax.ShapeDtypeStruct((1, D_MODEL), F32),
                   jax.ShapeDtypeStruct((S, n_in), BF16)),
        grid=(S // tm,),
        in_specs=[row(D_MODEL), row(D_MODEL), vec, row(Q_DIM), row(2 * KV_DIM), row(2 * CONV_DIM),
                  row(2 * D_MODEL), pl.BlockSpec((D_MODEL, QC_DIM), lambda i: (0, 0)),
                  pl.BlockSpec((D_MODEL, 2 * D_MODEL), lambda i: (0, 0))],
        out_specs=(row(D_MODEL), vec, row(n_in)),
        compiler_params=_params(("arbitrary",)),
    )(dh2, h1, gamma, dq, dkv, dcvg, dgpre, w_qc, w_g)


TQ = 512
QB = TQ // BLOCK


def _attn_in_specs(S):
    nkb = S // BLOCK
    return [
        pl.BlockSpec((TQ, Q_DIM), lambda i: (i, 0)),
        pl.BlockSpec((BLOCK, 2 * KV_DIM), lambda i: (jnp.maximum(i * QB - 1, 0), Q_DIM // (2 * KV_DIM))),
        pl.BlockSpec((TQ, 2 * KV_DIM), lambda i: (i, Q_DIM // (2 * KV_DIM))),
        pl.BlockSpec((BLOCK, 2 * KV_DIM), lambda i: (jnp.minimum(i * QB + QB, nkb - 1), Q_DIM // (2 * KV_DIM))),
        pl.BlockSpec((1, 128), lambda i: (0, 0)),
        pl.BlockSpec((1, KV_DIM), lambda i: (0, 0)),
        pl.BlockSpec((N_HEADS, 128), lambda i: (0, 0)),
        pl.BlockSpec((N_HEADS, BLOCK, WIN), lambda i: (0, 0, 0)),
    ]


GROUP_ROWS = 4 * BLOCK


def _half_rstd(x, low):
    x2 = x * x
    z = jnp.zeros_like(x2)
    r0 = lax.rsqrt(jnp.sum(jnp.where(low, x2, z), axis=-1, keepdims=True) * (1.0 / HEAD_DIM) + 1e-6)
    r1 = lax.rsqrt(jnp.sum(jnp.where(low, z, x2), axis=-1, keepdims=True) * (1.0 / HEAD_DIM) + 1e-6)
    return jnp.where(low, r0, r1)


def _kv_windows(kvp_ref, kvc_ref, kvn_ref, kg_ref, low):
    kv = jnp.concatenate([kvp_ref[...], kvc_ref[...], kvn_ref[...]], axis=0)
    k, v = kv[:, :KV_DIM], kv[:, KV_DIM:]
    kn = k * _half_rstd(k, low) * kg_ref[...]
    kr, vr = pltpu.roll(kn, HEAD_DIM, 1), pltpu.roll(v, HEAD_DIM, 1)
    kdup = [_bf(jnp.where(low, kn, kr)), _bf(jnp.where(low, kr, kn))]
    vdup = [_bf(jnp.where(low, v, vr)), _bf(jnp.where(low, vr, v))]
    return kdup, vdup


def _stack_heads(x_ref, t, kh, low):
    rows = slice(t * BLOCK, (t + 1) * BLOCK)
    xa = x_ref[rows, 256 * kh:256 * kh + 128]
    xb = x_ref[rows, 256 * kh + 128:256 * kh + 256]
    z = jnp.zeros_like(xa)
    return jnp.concatenate([jnp.where(low, xa, z), jnp.where(low, z, xa),
                            jnp.where(low, xb, z), jnp.where(low, z, xb)], axis=0)


def _stacked_q(q_ref, qg_ref, t, kh, low):
    qraw = _stack_heads(q_ref, t, kh, low)
    rq = lax.rsqrt(jnp.sum(qraw * qraw, axis=-1, keepdims=True) * (1.0 / HEAD_DIM) + 1e-6)
    return qraw, rq, _bf(qraw * rq * (qg_ref[...] * (HEAD_DIM ** -0.5)))


def _unstack_heads(ov, low):
    return (jnp.where(low, ov[0:128], ov[128:256]), jnp.where(low, ov[256:384], ov[384:512]))


def _edge_bias(i, t, S):
    kpos = i * TQ + (t - 1) * BLOCK + lax.broadcasted_iota(jnp.int32, (1, WIN), 1)
    return jnp.where((kpos < 0) | (kpos >= S), NEG_INF, 0.0)


def _group_exp(lhs, kw, bias_ref, sink_ref, kh, edge):
    s = _dot_nt(lhs, kw) + bias_ref[4 * kh:4 * kh + 4].reshape(GROUP_ROWS, WIN)
    if edge is not None:
        s = s + edge
    sk = jnp.concatenate(
        [jnp.broadcast_to(sink_ref[4 * kh + r:4 * kh + r + 1, 0:1], (BLOCK, 1)) for r in range(4)], axis=0)
    m = jnp.maximum(jnp.max(s, axis=-1, keepdims=True), sk)
    return jnp.exp(s - m), jnp.exp(sk - m)


def _attn_fwd(qkv, qg, kg, sinkb, bias, name, ex=NO_EXCHANGE):
    S = qkv.shape[0]

    def body(q_ref, kvp_ref, kvc_ref, kvn_ref, qg_ref, kg_ref, sink_ref, bias_ref, o_ref):
        i = pl.program_id(0)
        low = lax.broadcasted_iota(jnp.int32, (1, 128), 1) < HEAD_DIM
        ones = jnp.ones((WIN, 128), BF16)
        kdup, vdup = _kv_windows(kvp_ref, kvc_ref, kvn_ref, kg_ref, low)
        for t in range(QB):
            edge = _edge_bias(i, t, S) if t in (0, QB - 1) else None
            rows = slice(t * BLOCK, (t + 1) * BLOCK)
            for kh in range(KV_HEADS):
                _, _, lhs = _stacked_q(q_ref, qg_ref, t, kh, low)
                kw = kdup[kh][t * BLOCK:t * BLOCK + WIN]
                vw = vdup[kh][t * BLOCK:t * BLOCK + WIN]
                e, es = _group_exp(lhs, kw, bias_ref, sink_ref, kh, edge)
                eb = _bf(e)
                ov = _dot(eb, vw) * (1.0 / (_dot(eb, ones) + es))
                oa, ob = _unstack_heads(ov, low)
                o_ref[rows, 256 * kh:256 * kh + 128] = _bf(oa)
                o_ref[rows, 256 * kh + 128:256 * kh + 256] = _bf(ob)

    return _carrier_call(
        body, ex, _grid1_ends(S // TQ), name, out_shape=(jax.ShapeDtypeStruct((S, Q_DIM), BF16),),
        grid=(S // TQ,), in_specs=_attn_in_specs(S),
        out_specs=(pl.BlockSpec((TQ, Q_DIM), lambda i: (i, 0)),), scratch_shapes=[],
        args=(qkv, qkv, qkv, qkv, qg, kg, sinkb, bias))


def _attn_bwd(do, qkv, qg, kg, sinkb, bias, dbias_in, name, ex=NO_EXCHANGE):
    S = qkv.shape[0]
    nkb = S // BLOCK
    nsteps = S // TQ

    def body(do_ref, q_ref, kvp_ref, kvc_ref, kvn_ref, qg_ref, kg_ref, sink_ref, bias_ref, dbin_ref,
             dq_ref, dkp_ref, dvp_ref, dbias_ref, dsink_ref, dqg_ref, dqg_s):
        i = pl.program_id(0)

        @pl.when(i == 0)
        def _():
            dbias_ref[...] = dbin_ref[...]
            dsink_ref[...] = jnp.zeros_like(dsink_ref)
            dqg_s[...] = jnp.zeros_like(dqg_s)

        low = lax.broadcasted_iota(jnp.int32, (1, 128), 1) < HEAD_DIM
        own = ((lax.broadcasted_iota(jnp.int32, (GROUP_ROWS, 128), 1) >> 6) & 1) == (
            (lax.broadcasted_iota(jnp.int32, (GROUP_ROWS, 128), 0) >> 7) & 1)
        gq = qg_ref[...] * (HEAD_DIM ** -0.5)
        ones = jnp.ones((WIN, 128), BF16)
        kdup, vdup = _kv_windows(kvp_ref, kvc_ref, kvn_ref, kg_ref, low)
        for t in range(QB):
            edge = _edge_bias(i, t, S) if t in (0, QB - 1) else None
            rows = slice(t * BLOCK, (t + 1) * BLOCK)
            dk_dup, dv_dup = [], []
            for kh in range(KV_HEADS):
                qraw, rq, lhs = _stacked_q(q_ref, qg_ref, t, kh, low)
                dos = _bf(_stack_heads(do_ref, t, kh, low))
                kw = kdup[kh][t * BLOCK:t * BLOCK + WIN]
                vw = vdup[kh][t * BLOCK:t * BLOCK + WIN]
                e, es = _group_exp(lhs, kw, bias_ref, sink_ref, kh, edge)
                inv = 1.0 / (_dot(_bf(e), ones) + es)
                pr = e * jnp.concatenate([inv] * (WIN // 128), axis=1)
                dpr = _dot_nt(dos, vw)
                delta = jnp.sum(pr * dpr, axis=-1, keepdims=True)
                ds = pr * (dpr - delta)
                dbias_ref[4 * kh:4 * kh + 4] += ds.reshape(4, BLOCK, WIN)
                dsk = es * inv[:, 0:1] * delta
                for r in range(4):
                    dsink_ref[4 * kh + r:4 * kh + r + 1, :] -= jnp.broadcast_to(
                        jnp.sum(dsk[r * BLOCK:(r + 1) * BLOCK], axis=0, keepdims=True), (1, 128))
                dsb = _bf(ds)
                dqs = _dot(dsb, kw)
                qhat = qraw * rq
                dxhat = jnp.where(own, dqs, 0.0) * gq
                dq_st = rq * (dxhat - qhat * (jnp.sum(dxhat * qhat, axis=-1, keepdims=True) * (1.0 / HEAD_DIM)))
                dq_ref[rows, 256 * kh:256 * kh + 128] = dq_st[0:128] + dq_st[128:256]
                dq_ref[rows, 256 * kh + 128:256 * kh + 256] = dq_st[256:384] + dq_st[384:512]
                dqg_s[...] += jnp.sum((dqs * qhat).reshape(GROUP_ROWS // 8, 8, 128), axis=0)
                dkx = _dot_tn(dsb, lhs)
                dvx = _dot_tn(_bf(pr), dos)
                dk_dup.append(dkx + pltpu.roll(dkx, HEAD_DIM, 1))
                dv_dup.append(dvx + pltpu.roll(dvx, HEAD_DIM, 1))
            dkp_ref[t] = jnp.where(low, dk_dup[0], dk_dup[1])
            dvp_ref[t] = jnp.where(low, dv_dup[0], dv_dup[1])

        @pl.when(i == nsteps - 1)
        def _():
            acc = dqg_s[...] * (HEAD_DIM ** -0.5)
            acc = acc + pltpu.roll(acc, HEAD_DIM, 1)
            dqg_ref[...] = jnp.broadcast_to(jnp.sum(acc, axis=0, keepdims=True), (8, 128))

    const2 = lambda shape: pl.BlockSpec(shape, lambda i: (0,) * len(shape))
    part = pl.BlockSpec((QB, WIN, KV_DIM), lambda i: (i, 0, 0))
    return _carrier_call(
        body, ex, _grid1_ends(nsteps), name,
        out_shape=(jax.ShapeDtypeStruct((S, Q_DIM), F32), jax.ShapeDtypeStruct((nkb, WIN, KV_DIM), F32),
                   jax.ShapeDtypeStruct((nkb, WIN, KV_DIM), F32),
                   jax.ShapeDtypeStruct((N_HEADS, BLOCK, WIN), F32), jax.ShapeDtypeStruct((N_HEADS, 128), F32),
                   jax.ShapeDtypeStruct((8, 128), F32)),
        grid=(nsteps,),
        in_specs=[pl.BlockSpec((TQ, Q_DIM), lambda i: (i, 0))] + _attn_in_specs(S)
        + [const2((N_HEADS, BLOCK, WIN))],
        out_specs=(pl.BlockSpec((TQ, Q_DIM), lambda i: (i, 0)), part, part,
                   const2((N_HEADS, BLOCK, WIN)), const2((N_HEADS, 128)), const2((8, 128))),
        scratch_shapes=[pltpu.VMEM((8, 128), F32)],
        args=(do, qkv, qkv, qkv, qkv, qg, kg, sinkb, bias, dbias_in))


def _kv_fold(dkp, dvp, qkv, kg, name):
    nkb = dkp.shape[0]
    S = nkb * BLOCK
    nsteps = S // TQ

    def body(kp_p, kp_c, kp_n, vp_p, vp_c, vp_n, kv_ref, kg_ref, dkv_ref, dkg_ref, dkg_s):
        i = pl.program_id(0)

        @pl.when(i == 0)
        def _():
            dkg_s[...] = jnp.zeros_like(dkg_s)

        def fold(p_ref, c_ref, n_ref):
            blocks = []
            for t in range(QB):
                acc = c_ref[t, BLOCK:2 * BLOCK, :]
                if t > 0:
                    acc = acc + c_ref[t - 1, 2 * BLOCK:, :]
                else:
                    acc = acc + jnp.where(i > 0, p_ref[0, 2 * BLOCK:, :], 0.0)
                if t < QB - 1:
                    acc = acc + c_ref[t + 1, :BLOCK, :]
                else:
                    acc = acc + jnp.where(i < nsteps - 1, n_ref[0, :BLOCK, :], 0.0)
                blocks.append(acc)
            return jnp.concatenate(blocks, axis=0)

        dkn = fold(kp_p, kp_c, kp_n)
        dv = fold(vp_p, vp_c, vp_n)
        k = kv_ref[:, :KV_DIM]
        low = lax.broadcasted_iota(jnp.int32, (1, 128), 1) < HEAD_DIM
        rk = _half_rstd(k, low)
        khat = k * rk
        dxhat = dkn * kg_ref[...]
        prod = dxhat * khat
        z = jnp.zeros_like(prod)
        mean = jnp.where(low, jnp.sum(jnp.where(low, prod, z), axis=-1, keepdims=True),
                         jnp.sum(jnp.where(low, z, prod), axis=-1, keepdims=True)) * (1.0 / HEAD_DIM)
        dkv_ref[:, :KV_DIM] = rk * (dxhat - khat * mean)
        dkv_ref[:, KV_DIM:] = dv
        dkg_s[...] += jnp.sum((dkn * khat).reshape(TQ // 8, 8, KV_DIM), axis=0)

        @pl.when(i == nsteps - 1)
        def _():
            acc = dkg_s[...] + pltpu.roll(dkg_s[...], HEAD_DIM, 1)
            dkg_ref[...] = jnp.broadcast_to(jnp.sum(acc, axis=0, keepdims=True), (8, 128))

    prev = pl.BlockSpec((1, WIN, KV_DIM), lambda i: (jnp.maximum(i * QB - 1, 0), 0, 0))
    cur = pl.BlockSpec((QB, WIN, KV_DIM), lambda i: (i, 0, 0))
    nxt = pl.BlockSpec((1, WIN, KV_DIM), lambda i: (jnp.minimum(i * QB + QB, nkb - 1), 0, 0))
    return pl.pallas_call(
        body, name=name,
        out_shape=(jax.ShapeDtypeStruct((S, 2 * KV_DIM), F32), jax.ShapeDtypeStruct((8, 128), F32)),
        grid=(nsteps,),
        in_specs=[prev, cur, nxt, prev, cur, nxt,
                  pl.BlockSpec((TQ, 2 * KV_DIM), lambda i: (i, Q_DIM // (2 * KV_DIM))),
                  pl.BlockSpec((1, KV_DIM), lambda i: (0, 0))],
        out_specs=(pl.BlockSpec((TQ, 2 * KV_DIM), lambda i: (i, 0)), pl.BlockSpec((8, 128), lambda i: (0, 0))),
        scratch_shapes=[pltpu.VMEM((8, KV_DIM), F32)],
        compiler_params=_params(("arbitrary",)),
    )(dkp, dkp, dkp, dvp, dvp, dvp, qkv, kg)


BIAS_COLS = BLOCK * WIN
BIAS_CHUNK = 6144


def _bias_table(rel_bias_t, onehot, band):
    def body(rb_ref, oh_ref, band_ref, o_ref):
        o_ref[...] = _dot(rb_ref[...], oh_ref[...], HI) + band_ref[...]

    return pl.pallas_call(
        body, name="bias_table", out_shape=jax.ShapeDtypeStruct((N_HEADS, BIAS_COLS), F32),
        grid=(BIAS_COLS // BIAS_CHUNK,),
        in_specs=[pl.BlockSpec((N_HEADS, NUM_BUCKETS), lambda i: (0, 0)),
                  pl.BlockSpec((NUM_BUCKETS, BIAS_CHUNK), lambda i: (0, i)),
                  pl.BlockSpec((1, BIAS_CHUNK), lambda i: (0, i))],
        out_specs=pl.BlockSpec((N_HEADS, BIAS_CHUNK), lambda i: (0, i)),
        compiler_params=_params(("parallel",)),
    )(rel_bias_t, onehot, band)


def _bias_grad(dbias, onehot):
    def body(db_ref, oh_ref, o_ref):
        @pl.when(pl.program_id(0) == 0)
        def _():
            o_ref[...] = jnp.zeros_like(o_ref)

        o_ref[...] += lax.dot_general(db_ref[...], oh_ref[...], (((1,), (1,)), ((), ())),
                                      preferred_element_type=F32, precision=HI)

    return pl.pallas_call(
        body, name="bias_grad", out_shape=jax.ShapeDtypeStruct((N_HEADS, NUM_BUCKETS), F32),
        grid=(BIAS_COLS // BIAS_CHUNK,),
        in_specs=[pl.BlockSpec((N_HEADS, BIAS_CHUNK), lambda i: (0, i)),
                  pl.BlockSpec((NUM_BUCKETS, BIAS_CHUNK), lambda i: (0, i))],
        out_specs=pl.BlockSpec((N_HEADS, NUM_BUCKETS), lambda i: (0, 0)),
        compiler_params=_params(("arbitrary",)),
    )(dbias, onehot)


def _bucket_onehot():
    half = NUM_BUCKETS // 2
    max_exact = half // 2
    rel = jnp.arange(WIN)[None, :] - BLOCK - jnp.arange(BLOCK)[:, None]
    n = jnp.abs(rel)
    ret = jnp.where(rel > 0, half, 0)
    nf = jnp.maximum(n, 1).astype(F32)
    large = max_exact + (jnp.log(nf / max_exact) / np.log(MAX_DISTANCE / max_exact)
                         * (half - max_exact)).astype(jnp.int32)
    large = jnp.minimum(large, half - 1)
    bucket = (ret + jnp.where(n < max_exact, n, large)).reshape(1, BIAS_COLS)
    band = jnp.where(n <= BLOCK, 0.0, NEG_INF).astype(F32).reshape(1, BIAS_COLS)
    return (bucket == jnp.arange(NUM_BUCKETS)[:, None]).astype(F32), band


def _halo_specs(tm, width, S):
    r = tm // HALO
    last = S // HALO - 1
    return [pl.BlockSpec((HALO, width), lambda i: (jnp.maximum(i * r - 1, 0), 0)),
            pl.BlockSpec((tm, width), lambda i: (i, 0)),
            pl.BlockSpec((HALO, width), lambda i: (jnp.minimum(i * r + r, last), 0))]


def _with_halo(p_ref, c_ref, n_ref):
    return jnp.concatenate([p_ref[...], c_ref[...], n_ref[...]], axis=0)


def _row_valid(i, tm, S):
    g = i * tm - HALO + lax.broadcasted_iota(jnp.int32, (tm + 2 * HALO, 1), 0)
    return (g >= 0) & (g < S)


def _shifted(x):
    n = x.shape[0]
    return [x if b == 0 else pltpu.roll(x, n - b, 0) for b in range(8)]


def _tap(sh, off, tm):
    a, b = off // 8, off % 8
    return sh[b][8 * a:8 * a + tm]


def _conv_fwd(cvg, cw, cb, lg, lb, tm, name, ex=NO_EXCHANGE):
    S = cvg.shape[0]

    def body(p_ref, c_ref, n_ref, cw_ref, cb_ref, lg_ref, lb_ref, act_ref, yc_ref):
        i = pl.program_id(0)
        z = _with_halo(p_ref, c_ref, n_ref)
        glu = jnp.where(_row_valid(i, tm, S), z[:, :CONV_DIM] * _sig(z[:, CONV_DIM:]), 0.0)
        sh = _shifted(glu)
        y = jnp.zeros((tm, CONV_DIM), F32) + cb_ref[...]
        for w in range(CONV_WIDTH):
            y = y + _tap(sh, w + 1, tm) * cw_ref[w:w + 1, :]
        yc_ref[...] = y
        mu = jnp.mean(y, axis=-1, keepdims=True)
        yc = y - mu
        rstd = lax.rsqrt(jnp.mean(yc * yc, axis=-1, keepdims=True) + 1e-5)
        ln = yc * rstd * lg_ref[...] + lb_ref[...]
        act_ref[...] = _bf(ln * _sig(ln))

    vec = pl.BlockSpec((1, CONV_DIM), lambda i: (0, 0))
    row = pl.BlockSpec((tm, CONV_DIM), lambda i: (i, 0))
    return _carrier_call(
        body, ex, _grid_ends(S // tm), name,
        out_shape=(jax.ShapeDtypeStruct((S, CONV_DIM), BF16), jax.ShapeDtypeStruct((S, CONV_DIM), F32)),
        grid=(S // tm,),
        in_specs=_halo_specs(tm, 2 * CONV_DIM, S) + [pl.BlockSpec((32, CONV_DIM), lambda i: (0, 0)), vec, vec, vec],
        out_specs=(row, row), scratch_shapes=[], args=(cvg, cvg, cvg, cw, cb, lg, lb))


def _conv_bwd(dact, yconv, cvg, cw, lg, lb, tm, name, ex=NO_EXCHANGE):
    S = cvg.shape[0]
    nsteps = S // tm

    def body(dp, dc, dn, yp, yc_, yn, zp, zc, zn, cw_ref, lg_ref, lb_ref,
             dz_ref, dcw_ref, dvec_ref, dcw_s, dvec_s, shg_s, shd_s):
        i = pl.program_id(0)

        @pl.when(i == 0)
        def _():
            dcw_s[...] = jnp.zeros_like(dcw_s)
            dvec_s[...] = jnp.zeros_like(dvec_s)

        valid = _row_valid(i, tm, S)
        own = (lax.broadcasted_iota(jnp.int32, (tm + 2 * HALO, 1), 0) >= HALO) & (
            lax.broadcasted_iota(jnp.int32, (tm + 2 * HALO, 1), 0) < HALO + tm)
        y = _with_halo(yp, yc_, yn)
        dact_ = _with_halo(dp, dc, dn)
        mu = jnp.mean(y, axis=-1, keepdims=True)
        ycen = y - mu
        rstd = lax.rsqrt(jnp.mean(ycen * ycen, axis=-1, keepdims=True) + 1e-5)
        yhat = ycen * rstd
        ln = yhat * lg_ref[...] + lb_ref[...]
        sg = _sig(ln)
        dln = dact_ * (sg * (1.0 + ln * (1.0 - sg)))
        dyhat = dln * lg_ref[...]
        dy = rstd * (dyhat - jnp.mean(dyhat, axis=-1, keepdims=True)
                     - yhat * jnp.mean(dyhat * yhat, axis=-1, keepdims=True))
        dy = jnp.where(valid, dy, 0.0)
        dln_own = jnp.where(own, dln, 0.0)
        nr = (tm + 2 * HALO) // 8
        dvec_s[0] += jnp.sum(jnp.where(own, dy, 0.0).reshape(nr, 8, CONV_DIM), axis=0)
        dvec_s[1] += jnp.sum((dln_own * yhat).reshape(nr, 8, CONV_DIM), axis=0)
        dvec_s[2] += jnp.sum(dln_own.reshape(nr, 8, CONV_DIM), axis=0)
        z = _with_halo(zp, zc, zn)
        glu = jnp.where(valid, z[:, :CONV_DIM] * _sig(z[:, CONV_DIM:]), 0.0)
        for b, (g_b, d_b) in enumerate(zip(_shifted(glu), _shifted(dy))):
            shg_s[b] = g_b
            shd_s[b] = d_b
        for cb in range(CONV_DIM // 128):
            lanes = slice(128 * cb, 128 * (cb + 1))
            for rb in range(tm // CROWS):
                r0 = rb * CROWS
                dy_own = shd_s[0, HALO + r0:HALO + r0 + CROWS, lanes]
                dglu = jnp.zeros((CROWS, 128), F32)
                for w in range(CONV_WIDTH):
                    a, b = divmod(CONV_WIDTH - w, 8)
                    dglu = dglu + shd_s[b, 8 * a + r0:8 * a + r0 + CROWS, lanes] * cw_ref[w:w + 1, lanes]
                    a, b = divmod(w + 1, 8)
                    prod = dy_own * shg_s[b, 8 * a + r0:8 * a + r0 + CROWS, lanes]
                    dcw_s[w, :, lanes] += jnp.sum(prod.reshape(CROWS // 8, 8, 128), axis=0)
                cv = zc[r0:r0 + CROWS, lanes]
                sg_o = _sig(zc[r0:r0 + CROWS, CONV_DIM + 128 * cb:CONV_DIM + 128 * (cb + 1)])
                dz_ref[r0:r0 + CROWS, lanes] = dglu * sg_o
                dz_ref[r0:r0 + CROWS, CONV_DIM + 128 * cb:CONV_DIM + 128 * (cb + 1)] = (
                    dglu * cv * sg_o * (1.0 - sg_o))

        @pl.when(i == nsteps - 1)
        def _():
            dcw_ref[...] = jnp.sum(dcw_s[...], axis=1)
            dvec_ref[...] = jnp.sum(dvec_s[...], axis=1)

    vec = pl.BlockSpec((1, CONV_DIM), lambda i: (0, 0))
    return _carrier_call(
        body, ex, _grid_ends(nsteps), name,
        out_shape=(jax.ShapeDtypeStruct((S, 2 * CONV_DIM), F32), jax.ShapeDtypeStruct((32, CONV_DIM), F32),
                   jax.ShapeDtypeStruct((8, CONV_DIM), F32)),
        grid=(nsteps,),
        in_specs=_halo_specs(tm, CONV_DIM, S) + _halo_specs(tm, CONV_DIM, S) + _halo_specs(tm, 2 * CONV_DIM, S)
        + [pl.BlockSpec((32, CONV_DIM), lambda i: (0, 0)), vec, vec],
        out_specs=(pl.BlockSpec((tm, 2 * CONV_DIM), lambda i: (i, 0)),
                   pl.BlockSpec((32, CONV_DIM), lambda i: (0, 0)), pl.BlockSpec((8, CONV_DIM), lambda i: (0, 0))),
        scratch_shapes=[pltpu.VMEM((32, 8, CONV_DIM), F32), pltpu.VMEM((8, 8, CONV_DIM), F32),
                        pltpu.VMEM((8, tm + 2 * HALO, CONV_DIM), F32), pltpu.VMEM((8, tm + 2 * HALO, CONV_DIM), F32)],
        args=(dact, dact, dact, yconv, yconv, yconv, cvg, cvg, cvg, cw, lg, lb))


def _merge_parts(un, o, cact, wg_ref, wao_ref, wco_ref):
    g = _dot(un, wg_ref[...])
    ga, gc = _sig(g[:, :D_MODEL]), _sig(g[:, D_MODEL:])
    ya = _dot(o, wao_ref[...])
    yc = _dot(cact, wco_ref[...])
    return ga, gc, ya, yc


def _merge_specs(tm):
    row = lambda w: pl.BlockSpec((tm, w), lambda i: (i, 0))
    full = lambda a, b: pl.BlockSpec((a, b), lambda i: (0, 0))
    weights = [full(D_MODEL, 2 * D_MODEL), full(Q_DIM, D_MODEL), full(CONV_DIM, D_MODEL), full(D_MODEL, D_MODEL)]
    return row, weights


def _merge_fwd(h1, un, o, cact, w_g, w_ao, w_co, w_o, tm, name, ex=NO_EXCHANGE):
    S = h1.shape[0]
    row, weights = _merge_specs(tm)

    def body(h1_ref, un_ref, o_ref, c_ref, wg_ref, wao_ref, wco_ref, wo_ref, h2_ref):
        ga, gc, ya, yc = _merge_parts(un_ref[...], o_ref[...], c_ref[...], wg_ref, wao_ref, wco_ref)
        h2_ref[...] = h1_ref[...] + _dot(_bf(ga * ya + gc * yc), wo_ref[...])

    return _carrier_call(
        body, ex, _grid_ends(S // tm), name, out_shape=(jax.ShapeDtypeStruct((S, D_MODEL), F32),),
        grid=(S // tm,),
        in_specs=[row(D_MODEL), row(D_MODEL), row(Q_DIM), row(CONV_DIM)] + weights,
        out_specs=(row(D_MODEL),), scratch_shapes=[], args=(h1, un, o, cact, w_g, w_ao, w_co, w_o))


def _merge_bwd(dh2, un, o, cact, w_g, w_ao, w_co, w_o, tm, name):
    S = dh2.shape[0]
    row, weights = _merge_specs(tm)

    def body(dh2_ref, un_ref, o_ref, c_ref, wg_ref, wao_ref, wco_ref, wo_ref,
             do_ref, dc_ref, mix_ref, dya_ref, dyc_ref, dgp_ref):
        ga, gc, ya, yc = _merge_parts(un_ref[...], o_ref[...], c_ref[...], wg_ref, wao_ref, wco_ref)
        mix_ref[...] = _bf(ga * ya + gc * yc)
        dmix = _dot_nt(_bf(dh2_ref[...]), wo_ref[...])
        dya = _bf(dmix * ga)
        dyc = _bf(dmix * gc)
        dya_ref[...] = dya
        dyc_ref[...] = dyc
        dgp_ref[:, :D_MODEL] = _bf(dmix * ya * ga * (1.0 - ga))
        dgp_ref[:, D_MODEL:] = _bf(dmix * yc * gc * (1.0 - gc))
        do_ref[...] = _dot_nt(dya, wao_ref[...])
        dc_ref[...] = _dot_nt(dyc, wco_ref[...])

    return pl.pallas_call(
        body, name=name,
        out_shape=(jax.ShapeDtypeStruct((S, Q_DIM), F32), jax.ShapeDtypeStruct((S, CONV_DIM), F32),
                   jax.ShapeDtypeStruct((S, D_MODEL), BF16), jax.ShapeDtypeStruct((S, D_MODEL), BF16),
                   jax.ShapeDtypeStruct((S, D_MODEL), BF16), jax.ShapeDtypeStruct((S, 2 * D_MODEL), BF16)),
        grid=(S // tm,),
        in_specs=[row(D_MODEL), row(D_MODEL), row(Q_DIM), row(CONV_DIM)] + weights,
        out_specs=(row(Q_DIM), row(CONV_DIM), row(D_MODEL), row(D_MODEL), row(D_MODEL), row(2 * D_MODEL)),
        compiler_params=_params(("parallel",)),
    )(dh2, un, o, cact, w_g, w_ao, w_co, w_o)


def _pe_specs(tm, layer):
    row = pl.BlockSpec((tm, D_MODEL), lambda i: (i, 0))
    vec = pl.BlockSpec((1, D_MODEL), lambda i: (0, 0))
    p_s = pl.BlockSpec((None, None, tm, 256), lambda i: (layer, 0, i, 0))
    wpp = pl.BlockSpec((256, D_MODEL), lambda i: (0, 0))
    wpg = pl.BlockSpec((D_MODEL, D_MODEL), lambda i: (0, 0))
    return row, vec, p_s, wpp, wpg


def _pe_fwd(h, gamma, p, layer, w_pp, w_pg, tm, name):
    S = h.shape[0]
    row, vec, p_s, wpp, wpg = _pe_specs(tm, layer)

    def body(h_ref, g_ref, p_ref, wpp_ref, wpg_ref, x_ref, hn_ref):
        hn = _bf(_rms_fwd(h_ref[...], g_ref[...])[0])
        hn_ref[...] = hn
        gate = _sig(_dot(hn, wpg_ref[...]))
        x_ref[...] = h_ref[...] + _dot(_bf(p_ref[...]), wpp_ref[...]) * gate

    return pl.pallas_call(
        body, name=name,
        out_shape=(jax.ShapeDtypeStruct((S, D_MODEL), F32), jax.ShapeDtypeStruct((S, D_MODEL), BF16)),
        grid=(S // tm,), in_specs=[row, vec, p_s, wpp, wpg], out_specs=(row, row),
        compiler_params=_params(("parallel",)),
    )(h, gamma, p, w_pp, w_pg)


def _pe_bwd(dx, h, gamma, hn, p, layer, w_pp, w_pg, tm, name):
    S = h.shape[0]
    row, vec, p_s, wpp, wpg = _pe_specs(tm, layer)

    def body(dx_ref, h_ref, g_ref, hn_ref, p_ref, wpp_ref, wpg_ref, dh_ref, dgp_ref, dpr_ref, dgam_ref):
        @pl.when(pl.program_id(0) == 0)
        def _():
            dgam_ref[...] = jnp.zeros_like(dgam_ref)

        dxv = dx_ref[...]
        gate = _sig(_dot(hn_ref[...], wpg_ref[...]))
        proj = _dot(_bf(p_ref[...]), wpp_ref[...])
        dpr_ref[...] = _bf(dxv * gate)
        dgp = _bf(dxv * proj * gate * (1.0 - gate))
        dgp_ref[...] = dgp
        dxn, dgam = _rms_bwd(_dot_nt(dgp, wpg_ref[...]), h_ref[...], g_ref[...])
        dh_ref[...] = dxv + dxn
        dgam_ref[...] += jnp.sum(dgam, axis=0, keepdims=True)

    return pl.pallas_call(
        body, name=name,
        out_shape=(jax.ShapeDtypeStruct((S, D_MODEL), F32), jax.ShapeDtypeStruct((S, D_MODEL), BF16),
                   jax.ShapeDtypeStruct((S, D_MODEL), BF16), jax.ShapeDtypeStruct((1, D_MODEL), F32)),
        grid=(S // tm,), in_specs=[row, row, vec, row, p_s, wpp, wpg], out_specs=(row, row, row, vec),
        compiler_params=_params(("arbitrary",)),
    )(dx, h, gamma, hn, p, w_pp, w_pg)


def _loss_head(y, target, tm):
    S = y.shape[0]

    def body(y_ref, t_ref, dy_ref, l_ref):
        @pl.when(pl.program_id(0) == 0)
        def _():
            l_ref[...] = jnp.zeros_like(l_ref)

        diff = y_ref[...] - t_ref[...]
        dy_ref[...] = diff * (1.0 / D_MODEL)
        sq = jnp.sum((diff * diff).reshape(tm // 8, 8, D_MODEL), axis=0)
        part = sq[:, 0:128]
        for k in range(1, D_MODEL // 128):
            part = part + sq[:, 128 * k:128 * (k + 1)]
        l_ref[...] += part

    row = pl.BlockSpec((tm, D_MODEL), lambda i: (i, 0))
    return pl.pallas_call(
        body, name="loss_head",
        out_shape=(jax.ShapeDtypeStruct((S, D_MODEL), F32), jax.ShapeDtypeStruct((8, 128), F32)),
        grid=(S // tm,), in_specs=[row, row], out_specs=(row, pl.BlockSpec((8, 128), lambda i: (0, 0))),
        compiler_params=_params(("arbitrary",)),
    )(y, target)


def _adamw(parts, w, m, v, name):
    R, C = w.shape
    tr = R
    for cand in (256, 128, 64, 32, 16):
        if R % cand == 0:
            tr = cand
            break

    def body(p_ref, w_ref, m_ref, v_ref, g_ref, d_ref, nm_ref, nv_ref):
        g = p_ref[0].astype(F32)
        for k in range(1, N_DEV):
            g = g + p_ref[k].astype(F32)
        g_ref[...] = g
        nm = ADAM_B1 * m_ref[...] + (1.0 - ADAM_B1) * g
        nv = ADAM_B2 * v_ref[...] + (1.0 - ADAM_B2) * (g * g)
        nm_ref[...] = nm
        nv_ref[...] = nv
        m_hat = nm / (1.0 - ADAM_B1 ** ADAM_STEP)
        v_hat = nv / (1.0 - ADAM_B2 ** ADAM_STEP)
        d_ref[...] = -ADAM_LR * (m_hat / (jnp.sqrt(v_hat) + ADAM_EPS) + ADAM_WD * w_ref[...])

    blk = pl.BlockSpec((tr, C), lambda i: (i, 0))
    out = jax.ShapeDtypeStruct((R, C), F32)
    return pl.pallas_call(
        body, name=name, out_shape=(out, out, out, out), grid=(R // tr,),
        in_specs=[pl.BlockSpec((N_DEV, tr, C), lambda i: (0, i, 0)), blk, blk, blk],
        out_specs=(blk, blk, blk, blk),
        compiler_params=_params(("parallel",)),
    )(parts, w, m, v)


SHARDED = ("w_ffn1_in", "w_ffn1_out", "w_in", "conv_w", "w_attn_out", "w_conv_out", "w_o",
           "w_ffn2_in", "w_ffn2_out", "w_pe_gate", "w_pe_proj")
COL_SHARDED = ("w_ffn1_in", "w_in", "conv_w", "w_attn_out", "w_conv_out", "w_ffn2_in", "w_pe_proj")
SMALL = ("rel_bias", "norm_ffn1", "norm_mix", "q_norm", "k_norm", "sink", "conv_b", "conv_ln_g", "conv_ln_b",
         "norm_ffn2", "norm_pe")
WEIGHTS = ("rel_bias", "norm_ffn1", "w_ffn1_in", "w_ffn1_out", "norm_mix", "w_in", "q_norm", "k_norm", "sink",
           "conv_w", "conv_b", "conv_ln_g", "conv_ln_b", "w_attn_out", "w_conv_out", "w_o", "norm_ffn2",
           "w_ffn2_in", "w_ffn2_out", "norm_pe", "w_pe_gate", "w_pe_proj")


def _natural(g):
    k, n = g.shape[1], g.shape[2]
    return jnp.transpose(g, (1, 0, 2)).reshape(k, N_DEV * n)


def _blocked(w):
    k, n = w.shape[0], w.shape[1] // N_DEV
    return jnp.transpose(w.reshape(k, N_DEV, n), (1, 0, 2))


def kernel(x, p, rel_bias, norm_ffn1, w_ffn1_in, w_ffn1_out, norm_mix, w_in, q_norm, k_norm, sink, conv_w, conv_b, conv_ln_g, conv_ln_b, w_attn_out, w_conv_out, w_o, norm_ffn2, w_ffn2_in, w_ffn2_out, norm_pe, w_pe_gate, w_pe_proj, loss_target, m_rel_bias, m_norm_ffn1, m_w_ffn1_in, m_w_ffn1_out, m_norm_mix, m_w_in, m_q_norm, m_k_norm, m_sink, m_conv_w, m_conv_b, m_conv_ln_g, m_conv_ln_b, m_w_attn_out, m_w_conv_out, m_w_o, m_norm_ffn2, m_w_ffn2_in, m_w_ffn2_out, m_norm_pe, m_w_pe_gate, m_w_pe_proj, v_rel_bias, v_norm_ffn1, v_w_ffn1_in, v_w_ffn1_out, v_norm_mix, v_w_in, v_q_norm, v_k_norm, v_sink, v_conv_w, v_conv_b, v_conv_ln_g, v_conv_ln_b, v_w_attn_out, v_w_conv_out, v_w_o, v_norm_ffn2, v_w_ffn2_in, v_w_ffn2_out, v_norm_pe, v_w_pe_gate, v_w_pe_proj):
    W = dict(rel_bias=rel_bias, norm_ffn1=norm_ffn1, w_ffn1_in=w_ffn1_in, w_ffn1_out=w_ffn1_out, norm_mix=norm_mix,
             w_in=w_in, q_norm=q_norm, k_norm=k_norm, sink=sink, conv_w=conv_w, conv_b=conv_b, conv_ln_g=conv_ln_g,
             conv_ln_b=conv_ln_b, w_attn_out=w_attn_out, w_conv_out=w_conv_out, w_o=w_o, norm_ffn2=norm_ffn2,
             w_ffn2_in=w_ffn2_in, w_ffn2_out=w_ffn2_out, norm_pe=norm_pe, w_pe_gate=w_pe_gate, w_pe_proj=w_pe_proj)
    M = dict(rel_bias=m_rel_bias, norm_ffn1=m_norm_ffn1, w_ffn1_in=m_w_ffn1_in, w_ffn1_out=m_w_ffn1_out,
             norm_mix=m_norm_mix, w_in=m_w_in, q_norm=m_q_norm, k_norm=m_k_norm, sink=m_sink, conv_w=m_conv_w,
             conv_b=m_conv_b, conv_ln_g=m_conv_ln_g, conv_ln_b=m_conv_ln_b, w_attn_out=m_w_attn_out,
             w_conv_out=m_w_conv_out, w_o=m_w_o, norm_ffn2=m_norm_ffn2, w_ffn2_in=m_w_ffn2_in,
             w_ffn2_out=m_w_ffn2_out, norm_pe=m_norm_pe, w_pe_gate=m_w_pe_gate, w_pe_proj=m_w_pe_proj)
    V = dict(rel_bias=v_rel_bias, norm_ffn1=v_norm_ffn1, w_ffn1_in=v_w_ffn1_in, w_ffn1_out=v_w_ffn1_out,
             norm_mix=v_norm_mix, w_in=v_w_in, q_norm=v_q_norm, k_norm=v_k_norm, sink=v_sink, conv_w=v_conv_w,
             conv_b=v_conv_b, conv_ln_g=v_conv_ln_g, conv_ln_b=v_conv_ln_b, w_attn_out=v_w_attn_out,
             w_conv_out=v_w_conv_out, w_o=v_w_o, norm_ffn2=v_norm_ffn2, w_ffn2_in=v_w_ffn2_in,
             w_ffn2_out=v_w_ffn2_out, norm_pe=v_norm_pe, w_pe_gate=v_w_pe_gate, w_pe_proj=v_w_pe_proj)

    L = w_in.shape[0]
    S = x.shape[1]
    tm = min(512, S)
    tm_ffn = min(1024, S)
    xs = x[0]
    target = loss_target[0]
    vec = lambda a: a.reshape(1, -1)

    half, full = {}, {}

    def carried(stage1_items, stage2_items):
        s1 = [it for it in stage1_items if it[1] < L]
        s2 = [it for it in stage2_items if it[1] < L]
        ex = _stage1([W[n][l] if n == "conv_w" else W[n][l].astype(BF16) for n, l in s1]) + _stage2(
            [half.pop(it) for it in s2])
        return ex, s1, s2

    def landed(got, s1, s2):
        half.update(zip(s1, got[:len(s1)]))
        full.update(zip(s2, got[len(s1):]))

    onehot, band = _bucket_onehot()
    bias = _bias_table(rel_bias.T, onehot, band).reshape(N_HEADS, BLOCK, WIN)

    layers, saved = [], []
    h = xs
    ex, s1, s2 = carried([("w_ffn1_in", 0), ("w_ffn1_out", 0), ("w_in", 0), ("conv_w", 0)], [])
    landed(_exchange(ex, "allgather_first"), s1, s2)
    ex, s1, s2 = carried([], [("w_ffn1_in", 0), ("w_ffn1_out", 0)])
    landed(_exchange(ex, "allgather_relay"), s1, s2)
    for l in range(L):
        sv = dict(x0=h)
        G = dict(wi1=full.pop(("w_ffn1_in", l)), wo1=full.pop(("w_ffn1_out", l)))
        ex, s1, s2 = carried(
            [(n, l) for n in ("w_ffn2_in", "w_ffn2_out", "w_attn_out", "w_conv_out", "w_o")],
            [("w_in", l), ("conv_w", l)])
        (h1, sv["xn1"], sv["g1"], sv["u1"]), got = _ffn_fwd(
            h, vec(norm_ffn1[l]), G["wi1"], G["wo1"], tm_ffn, "ffn1_fwd", ex)
        landed(got, s1, s2)
        w_in_n = _natural(full.pop(("w_in", l)))
        G.update(w_qc=w_in_n[:, :QC_DIM], w_g=w_in_n[:, QC_DIM:],
                 conv_w=jnp.pad(_natural(full.pop(("conv_w", l))), ((0, 1), (0, 0))))
        sv["h1"] = h1
        sv["un"], sv["qkv"], sv["cvg"] = _mixin_fwd(h1, vec(norm_mix[l]), G["w_qc"], tm, "mixin_fwd")
        sv["qg"] = vec(jnp.tile(q_norm[l], 2))
        sv["kg"] = vec(jnp.tile(k_norm[l], KV_HEADS))
        sv["sinkb"] = jnp.broadcast_to(sink[l][:, None], (N_HEADS, 128))
        ex, s1, s2 = carried(
            [("w_pe_gate", l), ("w_pe_proj", l), ("w_ffn1_in", l + 1)],
            [(n, l) for n in ("w_ffn2_in", "w_ffn2_out", "w_attn_out", "w_conv_out", "w_o")])
        (sv["o"],), got = _attn_fwd(sv["qkv"], sv["qg"], sv["kg"], sv["sinkb"], bias, "attn_fwd", ex)
        landed(got, s1, s2)
        G.update(wi2=full.pop(("w_ffn2_in", l)), wo2=full.pop(("w_ffn2_out", l)),
                 w_ao=_natural(full.pop(("w_attn_out", l))), w_co=_natural(full.pop(("w_conv_out", l))),
                 w_o=full.pop(("w_o", l)).reshape(D_MODEL, D_MODEL))
        ex, s1, s2 = carried([("w_ffn1_out", l + 1)],
                             [("w_pe_gate", l), ("w_pe_proj", l), ("w_ffn1_in", l + 1)])
        (sv["cact"], sv["yconv"]), got = _conv_fwd(
            sv["cvg"], G["conv_w"], vec(conv_b[l]), vec(conv_ln_g[l]), vec(conv_ln_b[l]), tm, "conv_fwd", ex)
        landed(got, s1, s2)
        G.update(w_pg=full.pop(("w_pe_gate", l)).reshape(D_MODEL, D_MODEL),
                 w_pp=_natural(full.pop(("w_pe_proj", l))))
        ex, s1, s2 = carried([("w_in", l + 1), ("conv_w", l + 1)], [("w_ffn1_out", l + 1)])
        (h2,), got = _merge_fwd(h1, sv["un"], sv["o"], sv["cact"], G["w_g"], G["w_ao"], G["w_co"], G["w_o"], tm,
                                "merge_fwd", ex)
        landed(got, s1, s2)
        sv["h2"] = h2
        (h3, sv["xn2"], sv["g2"], sv["u2"]), _ = _ffn_fwd(
            h2, vec(norm_ffn2[l]), G["wi2"], G["wo2"], tm_ffn, "ffn2_fwd")
        sv["h3"] = h3
        h, sv["hn"] = _pe_fwd(h3, vec(norm_pe[l]), p, l, G["w_pp"], G["w_pg"], tm, "pe_fwd")
        layers.append(G)
        saved.append(sv)

    dh, lparts = _loss_head(h, target, tm)
    loss = lax.psum((0.5 / D_MODEL) * jnp.sum(lparts), AXES)

    dbias = jnp.zeros((N_HEADS, BLOCK, WIN), F32)
    small_g = {n: [None] * L for n in SMALL if n != "rel_bias"}
    recv = {n: [None] * L for n in SHARDED}

    def keep(names, l, got):
        for n, r in zip(names, got):
            recv[n][l] = r

    pending = None
    for l in reversed(range(L)):
        G, sv = layers[l], saved[l]
        dh3, dgp_pe, dproj, dg_pe = _pe_bwd(dh, sv["h3"], vec(norm_pe[l]), sv["hn"], p, l, G["w_pp"], G["w_pg"],
                                            tm, "pe_bwd")
        gw_pg = _matmul_tn(sv["hn"][None], dgp_pe[None], 1, "dw_pe_gate")
        gw_pp = _matmul_tn(p[l], dproj[None], 1, "dw_pe_proj")
        (dh2, a2, dgu2, dg_n2), got = _ffn_bwd(
            dh3, sv["h2"], vec(norm_ffn2[l]), sv["g2"], sv["u2"], G["wi2"], G["wo2"], tm_ffn, "ffn2_bwd",
            _Exchange(pending, False) if pending else NO_EXCHANGE)
        if pending:
            keep(("w_ffn1_in",), l + 1, got)
        gwo2 = _matmul_tn(a2, dh3[None], FF_BLOCKS, "dw_ffn2_out", scale=0.5)
        gwi2 = _matmul_tn(sv["xn2"][None], dgu2.reshape(2 * FF_BLOCKS, S, FF_SHARD), 2 * FF_BLOCKS, "dw_ffn2_in")
        do, dcact, mix, dya, dyc, dgpre = _merge_bwd(dh2, sv["un"], sv["o"], sv["cact"], G["w_g"], G["w_ao"],
                                                     G["w_co"], G["w_o"], tm, "merge_bwd")
        gw_o = _matmul_tn(mix[None], dh2[None], 1, "dw_o")
        gw_ao = _matmul_tn(sv["o"][None], dya[None], 1, "dw_attn_out")
        gw_co = _matmul_tn(sv["cact"][None], dyc[None], 1, "dw_conv_out")
        (dq, dkp, dvp, dbias, dsink, dqg), got = _attn_bwd(
            do, sv["qkv"], sv["qg"], sv["kg"], sv["sinkb"], bias, dbias, "attn_bwd",
            _Exchange([gwi2, gw_pg.reshape(N_DEV, D_MODEL // N_DEV, D_MODEL), _blocked(gw_pp[0])], False))
        keep(("w_ffn2_in", "w_pe_gate", "w_pe_proj"), l, got)
        dkv, dkg = _kv_fold(dkp, dvp, sv["qkv"], sv["kg"], "kv_fold")
        (dcvg, dcw, dcvec), got = _conv_bwd(
            dcact, sv["yconv"], sv["cvg"], G["conv_w"], vec(conv_ln_g[l]), vec(conv_ln_b[l]), tm, "conv_bwd",
            _Exchange([gwo2.reshape(N_DEV, FF_SHARD // 2, D_MODEL)], False))
        keep(("w_ffn2_out",), l, got)
        dh1, dg_mix, dz = _mixin_bwd(dh2, sv["h1"], vec(norm_mix[l]), dq, dkv, dcvg, dgpre, G["w_qc"], G["w_g"],
                                     tm, "mixin_bwd")
        gw_in = _matmul_tn(sv["un"][None], dz[None], 1, "dw_in")[0]
        mid_send = [_blocked(gw_in), _blocked(dcw[:CONV_WIDTH]), _blocked(gw_ao[0]), _blocked(gw_co[0]),
                    gw_o.reshape(N_DEV, D_MODEL // N_DEV, D_MODEL)]
        (dh, a1, dgu1, dg_n1), got = _ffn_bwd(
            dh1, sv["x0"], vec(norm_ffn1[l]), sv["g1"], sv["u1"], G["wi1"], G["wo1"], tm_ffn, "ffn1_bwd",
            _Exchange(mid_send, False))
        keep(("w_in", "conv_w", "w_attn_out", "w_conv_out", "w_o"), l, got)
        gwo1 = _matmul_tn(a1, dh1[None], FF_BLOCKS, "dw_ffn1_out", scale=0.5)
        gwi1, got = _matmul_tn(sv["xn1"][None], dgu1.reshape(2 * FF_BLOCKS, S, FF_SHARD), 2 * FF_BLOCKS,
                               "dw_ffn1_in", ex=_Exchange([gwo1.reshape(N_DEV, FF_SHARD // 2, D_MODEL)], False))
        keep(("w_ffn1_out",), l, got)
        pending = [gwi1]
        small_g["norm_ffn1"][l] = dg_n1[0]
        small_g["norm_mix"][l] = dg_mix[0]
        small_g["q_norm"][l] = dqg[0, :HEAD_DIM]
        small_g["k_norm"][l] = dkg[0, :HEAD_DIM]
        small_g["sink"][l] = dsink[:, 0]
        small_g["conv_b"][l] = dcvec[0]
        small_g["conv_ln_g"][l] = dcvec[1]
        small_g["conv_ln_b"][l] = dcvec[2]
        small_g["norm_ffn2"][l] = dg_n2[0]
        small_g["norm_pe"][l] = dg_pe[0]

    keep(("w_ffn1_in",), 0, _exchange(_Exchange(pending, False), "grad_exchange_last"))
    grad_x = dh[None]
    drb = _bias_grad(dbias.reshape(N_HEADS, BIAS_COLS), onehot).T

    res = {}
    for n in SHARDED:
        parts = jnp.stack(recv[n], axis=1)
        shp = W[n].shape
        rows, cols = shp[0] * shp[1], shp[2]
        res[n] = [o.reshape(shp) for o in _adamw(
            parts.reshape(N_DEV, rows, cols), W[n].reshape(rows, cols), M[n].reshape(rows, cols),
            V[n].reshape(rows, cols), "adamw_" + n)]

    flat_g = jnp.concatenate([drb.reshape(-1)] + [jnp.stack(small_g[n]).reshape(-1) for n in SMALL[1:]])
    n_small = flat_g.shape[0]
    rows_s = -(-n_small // 1024 // 8) * 8
    pad = lambda a: jnp.pad(a, (0, rows_s * 1024 - n_small)).reshape(rows_s, 1024)
    flat = lambda d: pad(jnp.concatenate([d[n].reshape(-1) for n in SMALL]))
    (parts_s,) = _exchange(_Exchange([pad(flat_g)], True), "small_allgather")
    outs_s = _adamw(parts_s, flat(W), flat(M), flat(V), "adamw_small")
    off = 0
    for n in SMALL:
        size = W[n].size
        res[n] = [o.reshape(-1)[off:off + size].reshape(W[n].shape) for o in outs_s]
        off += size

    out = [loss, grad_x]
    for k in range(4):
        out += [res[n][k] for n in WEIGHTS]
    return tuple(out)
```

```python
import functools

import jax
import jax.numpy as jnp
import numpy as np
from jax import lax
from jax.experimental import pallas as pl
from jax.experimental.pallas import tpu as pltpu

F32 = jnp.float32
BF16 = jnp.bfloat16
MESH_ID = pl.DeviceIdType.MESH
AXES = ("x", "y", "c")
N_DEV = 8

D_MODEL = 1024
N_HEADS = 8
KV_HEADS = 2
HEAD_DIM = 64
Q_DIM = 512
KV_DIM = 128
BLOCK = 128
WIN = 3 * BLOCK
NUM_BUCKETS = 32
MAX_DISTANCE = 128
CONV_DIM = 512
CONV_WIDTH = 31
D_FF = 2816
FF_SHARD = 2 * D_FF // N_DEV
FF_BLOCKS = D_FF // FF_SHARD
QC_DIM = Q_DIM + 2 * KV_DIM + 2 * CONV_DIM
NEG_INF = -1e9
HALO = 16
CROWS = 64
FFN_PART_ROWS = 256

ADAM_LR = 0.001
ADAM_B1 = 0.9
ADAM_B2 = 0.999
ADAM_EPS = 1e-08
ADAM_WD = 0.01
ADAM_STEP = 10

VMEM_LIMIT = 56 * 1024 * 1024
HI = lax.Precision.HIGHEST


def _params(sem):
    return pltpu.CompilerParams(dimension_semantics=sem, vmem_limit_bytes=VMEM_LIMIT)


def _dot(a, b, precision=None):
    return jnp.dot(a, b, preferred_element_type=F32, precision=precision)


def _dot_nt(a, b):
    return lax.dot_general(a, b, (((1,), (1,)), ((), ())), preferred_element_type=F32)


def _dot_tn(a, b):
    return lax.dot_general(a, b, (((0,), (0,)), ((), ())), preferred_element_type=F32)


def _sig(x):
    return 1.0 / (1.0 + jnp.exp(-x))


def _bf(x):
    return x.astype(BF16)


def _rms_fwd(x, gamma):
    r = lax.rsqrt(jnp.mean(x * x, axis=-1, keepdims=True) + 1e-6)
    return x * r * gamma, r


def _rms_bwd(dy, x, gamma):
    r = lax.rsqrt(jnp.mean(x * x, axis=-1, keepdims=True) + 1e-6)
    xhat = x * r
    dxhat = dy * gamma
    dx = r * (dxhat - xhat * jnp.mean(dxhat * xhat, axis=-1, keepdims=True))
    return dx, dy * xhat


SAME_CORE = (2, 4, 6)


class _Exchange:
    def __init__(self, arrs=(), gather=True, kinds=None):
        self.arrs = list(arrs)
        self.n = n = len(self.arrs)
        self.kinds = list(kinds) if kinds is not None else ["gather" if gather else "scatter"] * n
        self.out_shape = tuple(
            jax.ShapeDtypeStruct(((N_DEV,) + a.shape) if k in ("gather", "stage1") else a.shape, a.dtype)
            for a, k in zip(self.arrs, self.kinds))
        self.aliases = {t: t for t, k in enumerate(self.kinds) if k == "stage2"}
        self.specs = [pl.BlockSpec(memory_space=pl.ANY)] * n
        self.scratch = [pltpu.SemaphoreType.DMA((7 * n,)), pltpu.SemaphoreType.DMA((7 * n,)),
                        pltpu.SemaphoreType.DMA((n,))] if n else []

    def __add__(self, other):
        return _Exchange(self.arrs + other.arrs, kinds=self.kinds + other.kinds)

    def _copies(self, ins, outs, sems):
        n = self.n
        send_sems, recv_sems, local_sems = sems
        x, y, c = lax.axis_index("x"), lax.axis_index("y"), lax.axis_index("c")
        me = 4 * x + 2 * y + c
        copies = []
        for t, kind in enumerate(self.kinds):
            if kind != "stage2":
                copies.append(pltpu.make_async_copy(
                    ins[t].at[me] if kind == "scatter" else ins[t], outs[t].at[me], local_sems.at[t]))
            offsets = {"gather": range(1, N_DEV), "scatter": range(1, N_DEV),
                       "stage1": (1,) + SAME_CORE, "stage2": SAME_CORE}[kind]
            for d in offsets:
                px = 1 - x if d & 4 else x
                py = 1 - y if d & 2 else y
                pc = 1 - c if d & 1 else c
                peer = 4 * px + 2 * py + pc
                if kind == "stage2":
                    src, dst, to = ins[t].at[peer], outs[t].at[peer], (x, y, 1 - c)
                else:
                    src, dst, to = (ins[t].at[peer] if kind == "scatter" else ins[t]), outs[t].at[me], (px, py, pc)
                k = (d - 1) * n + t
                copies.append(pltpu.make_async_remote_copy(
                    src_ref=src, dst_ref=dst, send_sem=send_sems.at[k], recv_sem=recv_sems.at[k],
                    device_id=to, device_id_type=MESH_ID))
        return copies

    def start(self, ins, outs, sems):
        for cp in self._copies(ins, outs, sems):
            cp.start()

    def wait(self, ins, outs, sems):
        for cp in self._copies(ins, outs, sems):
            cp.wait()


NO_EXCHANGE = _Exchange()


def _stage1(arrs):
    return _Exchange(arrs, kinds=["stage1"] * len(arrs))


def _stage2(arrs):
    return _Exchange(arrs, kinds=["stage2"] * len(arrs))


def _exchange(ex, name):
    n = ex.n

    def body(*refs):
        ins, outs, sems = refs[:n], refs[n:2 * n], refs[2 * n:]
        ex.start(ins, outs, sems)
        ex.wait(ins, outs, sems)

    return pl.pallas_call(
        body, name=name, out_shape=ex.out_shape, in_specs=ex.specs, out_specs=tuple(ex.specs),
        scratch_shapes=ex.scratch, input_output_aliases=ex.aliases,
    )(*ex.arrs)


def _carrier_call(body, ex, first_last, name, out_shape, grid, in_specs, out_specs, scratch_shapes, args):
    n_in, n_out, n_scr, n = len(in_specs), len(out_shape), len(scratch_shapes), ex.n

    def full(*refs):
        a, ci = refs[:n_in], refs[n_in:n_in + n]
        o = refs[n_in + n:n_in + n + n_out]
        co = refs[n_in + n + n_out:n_in + 2 * n + n_out]
        scr = refs[n_in + 2 * n + n_out:n_in + 2 * n + n_out + n_scr]
        sems = refs[n_in + 2 * n + n_out + n_scr:]
        first, last = first_last()
        if n:
            @pl.when(first)
            def _():
                ex.start(ci, co, sems)

        body(*a, *o, *scr)
        if n:
            @pl.when(last)
            def _():
                ex.wait(ci, co, sems)

    outs = pl.pallas_call(
        full, name=name, out_shape=tuple(out_shape) + ex.out_shape, grid=grid,
        in_specs=list(in_specs) + ex.specs, out_specs=tuple(out_specs) + tuple(ex.specs),
        scratch_shapes=list(scratch_shapes) + ex.scratch,
        input_output_aliases={n_in + t: n_out + u for t, u in ex.aliases.items()},
        compiler_params=_params(("arbitrary",) * len(grid)),
    )(*args, *ex.arrs)
    return outs[:n_out], outs[n_out:]


def _matmul_tn(a, b, nb, name, scale=1.0, out_dtype=BF16, ts=2048, ex=None):
    ba, S, K = a.shape
    bb, _, N = b.shape
    ts = min(ts, S)
    tn = N if N <= 1024 else next(c for c in (1280, 1024, 768, 512, 256) if N % c == 0)
    assert S % ts == 0
    ns = S // ts

    def body(a_ref, b_ref, o_ref, acc):
        s = pl.program_id(2)

        @pl.when(s == 0)
        def _():
            acc[...] = jnp.zeros_like(acc)

        acc[...] += _dot_tn(_bf(a_ref[...]), _bf(b_ref[...]))

        @pl.when(s == ns - 1)
        def _():
            o_ref[...] = (acc[...] * scale).astype(out_dtype)

    in_specs = [pl.BlockSpec((None, ts, K), (lambda i, j, s: (i, s, 0)) if ba > 1 else (lambda i, j, s: (0, s, 0))),
                pl.BlockSpec((None, ts, tn), (lambda i, j, s: (i, s, j)) if bb > 1 else (lambda i, j, s: (0, s, j)))]
    out_spec = pl.BlockSpec((None, K, tn), lambda i, j, s: (i, 0, j))
    out_shape = jax.ShapeDtypeStruct((nb, K, N), out_dtype)
    scratch = [pltpu.VMEM((K, tn), F32)]
    grid = (nb, N // tn, ns)
    if ex is not None:
        (out,), got = _carrier_call(body, ex, _grid_ends(*grid), name, out_shape=(out_shape,), grid=grid,
                                    in_specs=in_specs, out_specs=(out_spec,), scratch_shapes=scratch, args=(a, b))
        return out, got
    return pl.pallas_call(
        body, name=name, out_shape=out_shape, grid=grid, in_specs=in_specs, out_specs=out_spec,
        scratch_shapes=scratch, compiler_params=_params(("parallel", "parallel", "arbitrary")),
    )(a, b)


def _ffn_specs(tm):
    wg = pl.BlockSpec((None, D_MODEL, FF_SHARD), lambda i, j: (j, 0, 0))
    wu = pl.BlockSpec((None, D_MODEL, FF_SHARD), lambda i, j: (j + FF_BLOCKS, 0, 0))
    wo = pl.BlockSpec((2, FF_SHARD // 2, D_MODEL), lambda i, j: (j, 0, 0))
    row = pl.BlockSpec((tm, D_MODEL), lambda i, j: (i, 0))
    vec = pl.BlockSpec((1, D_MODEL), lambda i, j: (0, 0))
    hid = pl.BlockSpec((None, tm, FF_SHARD), lambda i, j: (j, i, 0))
    return wg, wu, wo, row, vec, hid


def _grid_ends(*grid):
    def first_last():
        first, last = None, None
        for d, n in enumerate(grid):
            i = pl.program_id(d)
            first = (i == 0) if first is None else first & (i == 0)
            last = (i == n - 1) if last is None else last & (i == n - 1)
        return first, last
    return first_last


def _grid2_ends(ni, nj):
    return _grid_ends(ni, nj)


def _grid1_ends(ni):
    return _grid_ends(ni)


def _ffn_fwd(x, gamma, wi, wo, tm, name, ex=NO_EXCHANGE):
    S = x.shape[0]
    wg_s, wu_s, wo_s, row, vec, hid = _ffn_specs(tm)

    def body(x_ref, g_ref, wg_ref, wu_ref, wo_ref, y_ref, xn_ref, gs_ref, us_ref, xn_s, acc):
        j = pl.program_id(1)

        @pl.when(j == 0)
        def _():
            xn = _bf(_rms_fwd(x_ref[...], g_ref[...])[0])
            xn_s[...] = xn
            xn_ref[...] = xn
            acc[...] = jnp.zeros_like(acc)

        wo2 = wo_ref[...].reshape(FF_SHARD, D_MODEL)
        for r in range(tm // FFN_PART_ROWS):
            rows = slice(r * FFN_PART_ROWS, (r + 1) * FFN_PART_ROWS)
            xn = xn_s[rows, :]
            g = _dot(xn, wg_ref[...])
            u = _dot(xn, wu_ref[...])
            gs_ref[rows, :] = _bf(g)
            us_ref[rows, :] = _bf(u)
            a = g * _sig(g) * u
            acc[rows, :] += _dot(_bf(a), wo2)

        @pl.when(j == FF_BLOCKS - 1)
        def _():
            y_ref[...] = x_ref[...] + 0.5 * acc[...]

    return _carrier_call(
        body, ex, _grid2_ends(S // tm, FF_BLOCKS), name,
        out_shape=(jax.ShapeDtypeStruct((S, D_MODEL), F32), jax.ShapeDtypeStruct((S, D_MODEL), BF16),
                   jax.ShapeDtypeStruct((FF_BLOCKS, S, FF_SHARD), BF16),
                   jax.ShapeDtypeStruct((FF_BLOCKS, S, FF_SHARD), BF16)),
        grid=(S // tm, FF_BLOCKS),
        in_specs=[row, vec, wg_s, wu_s, wo_s],
        out_specs=(row, row, hid, hid),
        scratch_shapes=[pltpu.VMEM((tm, D_MODEL), BF16), pltpu.VMEM((tm, D_MODEL), F32)],
        args=(x, gamma, wi, wi, wo))


def _ffn_bwd(dy, x, gamma, gs, us, wi, wo, tm, name, ex=NO_EXCHANGE):
    S = x.shape[0]
    wg_s, wu_s, wo_s, row, vec, hid = _ffn_specs(tm)
    dgu_s = pl.BlockSpec((2, None, tm, FF_SHARD), lambda i, j: (0, j, i, 0))

    def body(dy_ref, x_hbm, g_ref, gs_ref, us_ref, wg_ref, wu_ref, wo_ref,
             dx_ref, a_ref, dgu_ref, dgam_ref, dyh_s, x_buf, x_sem):
        i, j = pl.program_id(0), pl.program_id(1)
        acc = dx_ref
        x_copy = pltpu.make_async_copy(x_hbm.at[pl.ds(pl.multiple_of(i * tm, tm), tm), :], x_buf, x_sem)

        @pl.when(j == 0)
        def _():
            x_copy.start()
            dyh_s[...] = _bf(0.5 * dy_ref[...])
            acc[...] = jnp.zeros_like(acc)

        @pl.when((i == 0) & (j == 0))
        def _():
            dgam_ref[...] = jnp.zeros_like(dgam_ref)

        wo2 = wo_ref[...].reshape(FF_SHARD, D_MODEL)
        for r in range(tm // FFN_PART_ROWS):
            rows = slice(r * FFN_PART_ROWS, (r + 1) * FFN_PART_ROWS)
            da = _dot_nt(dyh_s[rows, :], wo2)
            g = gs_ref[rows, :].astype(F32)
            u = us_ref[rows, :].astype(F32)
            sg = _sig(g)
            sl = g * sg
            a_ref[rows, :] = _bf(sl * u)
            dg = _bf(da * u * (sg * (1.0 + g * (1.0 - sg))))
            du = _bf(da * sl)
            dgu_ref[0, rows, :] = dg
            dgu_ref[1, rows, :] = du
            acc[rows, :] += _dot_nt(dg, wg_ref[...]) + _dot_nt(du, wu_ref[...])

        @pl.when(j == FF_BLOCKS - 1)
        def _():
            x_copy.wait()
            dx, dgam = _rms_bwd(acc[...], x_buf[...], g_ref[...])
            dx_ref[...] = dy_ref[...] + dx
            dgam_ref[...] += jnp.sum(dgam, axis=0, keepdims=True)

    return _carrier_call(
        body, ex, _grid2_ends(S // tm, FF_BLOCKS), name,
        out_shape=(jax.ShapeDtypeStruct((S, D_MODEL), F32),
                   jax.ShapeDtypeStruct((FF_BLOCKS, S, FF_SHARD), BF16),
                   jax.ShapeDtypeStruct((2, FF_BLOCKS, S, FF_SHARD), BF16),
                   jax.ShapeDtypeStruct((1, D_MODEL), F32)),
        grid=(S // tm, FF_BLOCKS),
        in_specs=[row, pl.BlockSpec(memory_space=pl.ANY), vec, hid, hid, wg_s, wu_s, wo_s],
        out_specs=(row, hid, dgu_s, vec),
        scratch_shapes=[pltpu.VMEM((tm, D_MODEL), BF16), pltpu.VMEM((tm, D_MODEL), F32),
                        pltpu.SemaphoreType.DMA(())],
        args=(dy, x, gamma, gs, us, wi, wi, wo))


def _mixin_fwd(h, gamma, w_qc, tm, name):
    S = h.shape[0]
    nqkv = Q_DIM + 2 * KV_DIM

    def body(h_ref, g_ref, w_ref, un_ref, qkv_ref, cvg_ref):
        un = _bf(_rms_fwd(h_ref[...], g_ref[...])[0])
        un_ref[...] = un
        z = _dot(un, w_ref[...])
        qkv_ref[...] = z[:, :nqkv]
        cvg_ref[...] = z[:, nqkv:]

    row = lambda w: pl.BlockSpec((tm, w), lambda i: (i, 0))
    return pl.pallas_call(
        body, name=name,
        out_shape=(jax.ShapeDtypeStruct((S, D_MODEL), BF16), jax.ShapeDtypeStruct((S, nqkv), F32),
                   jax.ShapeDtypeStruct((S, 2 * CONV_DIM), F32)),
        grid=(S // tm,),
        in_specs=[row(D_MODEL), pl.BlockSpec((1, D_MODEL), lambda i: (0, 0)),
                  pl.BlockSpec((D_MODEL, QC_DIM), lambda i: (0, 0))],
        out_specs=(row(D_MODEL), row(nqkv), row(2 * CONV_DIM)),
        compiler_params=_params(("parallel",)),
    )(h, gamma, w_qc)


def _mixin_bwd(dh2, h1, gamma, dq, dkv, dcvg, dgpre, w_qc, w_g, tm, name):
    S = h1.shape[0]
    nqkv = Q_DIM + 2 * KV_DIM
    n_in = QC_DIM + 2 * D_MODEL

    def body(dh2_ref, h1_ref, g_ref, dq_ref, dkv_ref, dcvg_ref, dgp_ref, wqc_ref, wg_ref,
             dh1_ref, dgam_ref, dz_ref):
        @pl.when(pl.program_id(0) == 0)
        def _():
            dgam_ref[...] = jnp.zeros_like(dgam_ref)

        wqc = wqc_ref[...]
        dq, dkv, dcvg = _bf(dq_ref[...]), _bf(dkv_ref[...]), _bf(dcvg_ref[...])
        dz_ref[:, :Q_DIM] = dq
        dz_ref[:, Q_DIM:nqkv] = dkv
        dz_ref[:, nqkv:QC_DIM] = dcvg
        dz_ref[:, QC_DIM:] = dgp_ref[...]
        dun = _dot_nt(dq, wqc[:, :Q_DIM])
        dun += _dot_nt(dkv, wqc[:, Q_DIM:nqkv])
        dun += _dot_nt(dcvg, wqc[:, nqkv:])
        dun += _dot_nt(dgp_ref[...], wg_ref[...])
        dx, dgam = _rms_bwd(dun, h1_ref[...], g_ref[...])
        dh1_ref[...] = dh2_ref[...] + dx
        dgam_ref[...] += jnp.sum(dgam, axis=0, keepdims=True)

    row = lambda w: pl.BlockSpec((tm, w), lambda i: (i, 0))
    vec = pl.BlockSpec((1, D_MODEL), lambda i: (0, 0))
    return pl.pallas_call(
        body, name=name,
        out_shape=(jax.ShapeDtypeStruct((S, D_MODEL), F32), jax.ShapeDtypeStruct((1, D_MODEL), F32),
                   jax.ShapeDtypeStruct((S, n_in), BF16)),
        grid=(S // tm,),
        in_specs=[row(D_MODEL), row(D_MODEL), vec, row(Q_DIM), row(2 * KV_DIM), row(2 * CONV_DIM),
                  row(2 * D_MODEL), pl.BlockSpec((D_MODEL, QC_DIM), lambda i: (0, 0)),
                  pl.BlockSpec((D_MODEL, 2 * D_MODEL), lambda i: (0, 0))],
        out_specs=(row(D_MODEL), vec, row(n_in)),
        compiler_params=_params(("arbitrary",)),
    )(dh2, h1, gamma, dq, dkv, dcvg, dgpre, w_qc, w_g)


TQ = 512
QB = TQ // BLOCK


def _attn_in_specs(S):
    nkb = S // BLOCK
    return [
        pl.BlockSpec((TQ, Q_DIM), lambda i: (i, 0)),
        pl.BlockSpec((BLOCK, 2 * KV_DIM), lambda i: (jnp.maximum(i * QB - 1, 0), Q_DIM // (2 * KV_DIM))),
        pl.BlockSpec((TQ, 2 * KV_DIM), lambda i: (i, Q_DIM // (2 * KV_DIM))),
        pl.BlockSpec((BLOCK, 2 * KV_DIM), lambda i: (jnp.minimum(i * QB + QB, nkb - 1), Q_DIM // (2 * KV_DIM))),
        pl.BlockSpec((1, 128), lambda i: (0, 0)),
        pl.BlockSpec((1, KV_DIM), lambda i: (0, 0)),
        pl.BlockSpec((N_HEADS, 128), lambda i: (0, 0)),
        pl.BlockSpec((N_HEADS, BLOCK, WIN), lambda i: (0, 0, 0)),
    ]


GROUP_ROWS = 4 * BLOCK


def _half_rstd(x, low):
    x2 = x * x
    z = jnp.zeros_like(x2)
    r0 = lax.rsqrt(jnp.sum(jnp.where(low, x2, z), axis=-1, keepdims=True) * (1.0 / HEAD_DIM) + 1e-6)
    r1 = lax.rsqrt(jnp.sum(jnp.where(low, z, x2), axis=-1, keepdims=True) * (1.0 / HEAD_DIM) + 1e-6)
    return jnp.where(low, r0, r1)


def _kv_windows(kvp_ref, kvc_ref, kvn_ref, kg_ref, low):
    kv = jnp.concatenate([kvp_ref[...], kvc_ref[...], kvn_ref[...]], axis=0)
    k, v = kv[:, :KV_DIM], kv[:, KV_DIM:]
    kn = k * _half_rstd(k, low) * kg_ref[...]
    kr, vr = pltpu.roll(kn, HEAD_DIM, 1), pltpu.roll(v, HEAD_DIM, 1)
    kdup = [_bf(jnp.where(low, kn, kr)), _bf(jnp.where(low, kr, kn))]
    vdup = [_bf(jnp.where(low, v, vr)), _bf(jnp.where(low, vr, v))]
    return kdup, vdup


def _stack_heads(x_ref, t, kh, low):
    rows = slice(t * BLOCK, (t + 1) * BLOCK)
    xa = x_ref[rows, 256 * kh:256 * kh + 128]
    xb = x_ref[rows, 256 * kh + 128:256 * kh + 256]
    z = jnp.zeros_like(xa)
    return jnp.concatenate([jnp.where(low, xa, z), jnp.where(low, z, xa),
                            jnp.where(low, xb, z), jnp.where(low, z, xb)], axis=0)


def _stacked_q(q_ref, qg_ref, t, kh, low):
    qraw = _stack_heads(q_ref, t, kh, low)
    rq = lax.rsqrt(jnp.sum(qraw * qraw, axis=-1, keepdims=True) * (1.0 / HEAD_DIM) + 1e-6)
    return qraw, rq, _bf(qraw * rq * (qg_ref[...] * (HEAD_DIM ** -0.5)))


def _unstack_heads(ov, low):
    return (jnp.where(low, ov[0:128], ov[128:256]), jnp.where(low, ov[256:384], ov[384:512]))


def _edge_bias(i, t, S):
    kpos = i * TQ + (t - 1) * BLOCK + lax.broadcasted_iota(jnp.int32, (1, WIN), 1)
    return jnp.where((kpos < 0) | (kpos >= S), NEG_INF, 0.0)


def _group_exp(lhs, kw, bias_ref, sink_ref, kh, edge):
    s = _dot_nt(lhs, kw) + bias_ref[4 * kh:4 * kh + 4].reshape(GROUP_ROWS, WIN)
    if edge is not None:
        s = s + edge
    sk = jnp.concatenate(
        [jnp.broadcast_to(sink_ref[4 * kh + r:4 * kh + r + 1, 0:1], (BLOCK, 1)) for r in range(4)], axis=0)
    m = jnp.maximum(jnp.max(s, axis=-1, keepdims=True), sk)
    return jnp.exp(s - m), jnp.exp(sk - m)


def _attn_fwd(qkv, qg, kg, sinkb, bias, name, ex=NO_EXCHANGE):
    S = qkv.shape[0]

    def body(q_ref, kvp_ref, kvc_ref, kvn_ref, qg_ref, kg_ref, sink_ref, bias_ref, o_ref):
        i = pl.program_id(0)
        low = lax.broadcasted_iota(jnp.int32, (1, 128), 1) < HEAD_DIM
        ones = jnp.ones((WIN, 128), BF16)
        kdup, vdup = _kv_windows(kvp_ref, kvc_ref, kvn_ref, kg_ref, low)
        for t in range(QB):
            edge = _edge_bias(i, t, S) if t in (0, QB - 1) else None
            rows = slice(t * BLOCK, (t + 1) * BLOCK)
            for kh in range(KV_HEADS):
                _, _, lhs = _stacked_q(q_ref, qg_ref, t, kh, low)
                kw = kdup[kh][t * BLOCK:t * BLOCK + WIN]
                vw = vdup[kh][t * BLOCK:t * BLOCK + WIN]
                e, es = _group_exp(lhs, kw, bias_ref, sink_ref, kh, edge)
                eb = _bf(e)
                ov = _dot(eb, vw) * (1.0 / (_dot(eb, ones) + es))
                oa, ob = _unstack_heads(ov, low)
                o_ref[rows, 256 * kh:256 * kh + 128] = _bf(oa)
                o_ref[rows, 256 * kh + 128:256 * kh + 256] = _bf(ob)

    return _carrier_call(
        body, ex, _grid1_ends(S // TQ), name, out_shape=(jax.ShapeDtypeStruct((S, Q_DIM), BF16),),
        grid=(S // TQ,), in_specs=_attn_in_specs(S),
        out_specs=(pl.BlockSpec((TQ, Q_DIM), lambda i: (i, 0)),), scratch_shapes=[],
        args=(qkv, qkv, qkv, qkv, qg, kg, sinkb, bias))


def _attn_bwd(do, qkv, qg, kg, sinkb, bias, dbias_in, name, ex=NO_EXCHANGE):
    S = qkv.shape[0]
    nkb = S // BLOCK
    nsteps = S // TQ

    def body(do_ref, q_ref, kvp_ref, kvc_ref, kvn_ref, qg_ref, kg_ref, sink_ref, bias_ref, dbin_ref,
             dq_ref, dkp_ref, dvp_ref, dbias_ref, dsink_ref, dqg_ref, dqg_s):
        i = pl.program_id(0)

        @pl.when(i == 0)
        def _():
            dbias_ref[...] = dbin_ref[...]
            dsink_ref[...] = jnp.zeros_like(dsink_ref)
            dqg_s[...] = jnp.zeros_like(dqg_s)

        low = lax.broadcasted_iota(jnp.int32, (1, 128), 1) < HEAD_DIM
        own = ((lax.broadcasted_iota(jnp.int32, (GROUP_ROWS, 128), 1) >> 6) & 1) == (
            (lax.broadcasted_iota(jnp.int32, (GROUP_ROWS, 128), 0) >> 7) & 1)
        gq = qg_ref[...] * (HEAD_DIM ** -0.5)
        ones = jnp.ones((WIN, 128), BF16)
        kdup, vdup = _kv_windows(kvp_ref, kvc_ref, kvn_ref, kg_ref, low)
        for t in range(QB):
            edge = _edge_bias(i, t, S) if t in (0, QB - 1) else None
            rows = slice(t * BLOCK, (t + 1) * BLOCK)
            dk_dup, dv_dup = [], []
            for kh in range(KV_HEADS):
                qraw, rq, lhs = _stacked_q(q_ref, qg_ref, t, kh, low)
                dos = _bf(_stack_heads(do_ref, t, kh, low))
                kw = kdup[kh][t * BLOCK:t * BLOCK + WIN]
                vw = vdup[kh][t * BLOCK:t * BLOCK + WIN]
                e, es = _group_exp(lhs, kw, bias_ref, sink_ref, kh, edge)
                inv = 1.0 / (_dot(_bf(e), ones) + es)
                pr = e * jnp.concatenate([inv] * (WIN // 128), axis=1)
                dpr = _dot_nt(dos, vw)
                delta = jnp.sum(pr * dpr, axis=-1, keepdims=True)
                ds = pr * (dpr - delta)
                dbias_ref[4 * kh:4 * kh + 4] += ds.reshape(4, BLOCK, WIN)
                dsk = es * inv[:, 0:1] * delta
                for r in range(4):
                    dsink_ref[4 * kh + r:4 * kh + r + 1, :] -= jnp.broadcast_to(
                        jnp.sum(dsk[r * BLOCK:(r + 1) * BLOCK], axis=0, keepdims=True), (1, 128))
                dsb = _bf(ds)
                dqs = _dot(dsb, kw)
                qhat = qraw * rq
                dxhat = jnp.where(own, dqs, 0.0) * gq
                dq_st = rq * (dxhat - qhat * (jnp.sum(dxhat * qhat, axis=-1, keepdims=True) * (1.0 / HEAD_DIM)))
                dq_ref[rows, 256 * kh:256 * kh + 128] = dq_st[0:128] + dq_st[128:256]
                dq_ref[rows, 256 * kh + 128:256 * kh + 256] = dq_st[256:384] + dq_st[384:512]
                dqg_s[...] += jnp.sum((dqs * qhat).reshape(GROUP_ROWS // 8, 8, 128), axis=0)
                dkx = _dot_tn(dsb, lhs)
                dvx = _dot_tn(_bf(pr), dos)
                dk_dup.append(dkx + pltpu.roll(dkx, HEAD_DIM, 1))
                dv_dup.append(dvx + pltpu.roll(dvx, HEAD_DIM, 1))
            dkp_ref[t] = jnp.where(low, dk_dup[0], dk_dup[1])
            dvp_ref[t] = jnp.where(low, dv_dup[0], dv_dup[1])

        @pl.when(i == nsteps - 1)
        def _():
            acc = dqg_s[...] * (HEAD_DIM ** -0.5)
            acc = acc + pltpu.roll(acc, HEAD_DIM, 1)
            dqg_ref[...] = jnp.broadcast_to(jnp.sum(acc, axis=0, keepdims=True), (8, 128))

    const2 = lambda shape: pl.BlockSpec(shape, lambda i: (0,) * len(shape))
    part = pl.BlockSpec((QB, WIN, KV_DIM), lambda i: (i, 0, 0))
    return _carrier_call(
        body, ex, _grid1_ends(nsteps), name,
        out_shape=(jax.ShapeDtypeStruct((S, Q_DIM), F32), jax.ShapeDtypeStruct((nkb, WIN, KV_DIM), F32),
                   jax.ShapeDtypeStruct((nkb, WIN, KV_DIM), F32),
                   jax.ShapeDtypeStruct((N_HEADS, BLOCK, WIN), F32), jax.ShapeDtypeStruct((N_HEADS, 128), F32),
                   jax.ShapeDtypeStruct((8, 128), F32)),
        grid=(nsteps,),
        in_specs=[pl.BlockSpec((TQ, Q_DIM), lambda i: (i, 0))] + _attn_in_specs(S)
        + [const2((N_HEADS, BLOCK, WIN))],
        out_specs=(pl.BlockSpec((TQ, Q_DIM), lambda i: (i, 0)), part, part,
                   const2((N_HEADS, BLOCK, WIN)), const2((N_HEADS, 128)), const2((8, 128))),
        scratch_shapes=[pltpu.VMEM((8, 128), F32)],
        args=(do, qkv, qkv, qkv, qkv, qg, kg, sinkb, bias, dbias_in))


def _kv_fold(dkp, dvp, qkv, kg, name):
    nkb = dkp.shape[0]
    S = nkb * BLOCK
    nsteps = S // TQ

    def body(kp_p, kp_c, kp_n, vp_p, vp_c, vp_n, kv_ref, kg_ref, dkv_ref, dkg_ref, dkg_s):
        i = pl.program_id(0)

        @pl.when(i == 0)
        def _():
            dkg_s[...] = jnp.zeros_like(dkg_s)

        def fold(p_ref, c_ref, n_ref):
            blocks = []
            for t in range(QB):
                acc = c_ref[t, BLOCK:2 * BLOCK, :]
                if t > 0:
                    acc = acc + c_ref[t - 1, 2 * BLOCK:, :]
                else:
                    acc = acc + jnp.where(i > 0, p_ref[0, 2 * BLOCK:, :], 0.0)
                if t < QB - 1:
                    acc = acc + c_ref[t + 1, :BLOCK, :]
                else:
                    acc = acc + jnp.where(i < nsteps - 1, n_ref[0, :BLOCK, :], 0.0)
                blocks.append(acc)
            return jnp.concatenate(blocks, axis=0)

        dkn = fold(kp_p, kp_c, kp_n)
        dv = fold(vp_p, vp_c, vp_n)
        k = kv_ref[:, :KV_DIM]
        low = lax.broadcasted_iota(jnp.int32, (1, 128), 1) < HEAD_DIM
        rk = _half_rstd(k, low)
        khat = k * rk
        dxhat = dkn * kg_ref[...]
        prod = dxhat * khat
        z = jnp.zeros_like(prod)
        mean = jnp.where(low, jnp.sum(jnp.where(low, prod, z), axis=-1, keepdims=True),
                         jnp.sum(jnp.where(low, z, prod), axis=-1, keepdims=True)) * (1.0 / HEAD_DIM)
        dkv_ref[:, :KV_DIM] = rk * (dxhat - khat * mean)
        dkv_ref[:, KV_DIM:] = dv
        dkg_s[...] += jnp.sum((dkn * khat).reshape(TQ // 8, 8, KV_DIM), axis=0)

        @pl.when(i == nsteps - 1)
        def _():
            acc = dkg_s[...] + pltpu.roll(dkg_s[...], HEAD_DIM, 1)
            dkg_ref[...] = jnp.broadcast_to(jnp.sum(acc, axis=0, keepdims=True), (8, 128))

    prev = pl.BlockSpec((1, WIN, KV_DIM), lambda i: (jnp.maximum(i * QB - 1, 0), 0, 0))
    cur = pl.BlockSpec((QB, WIN, KV_DIM), lambda i: (i, 0, 0))
    nxt = pl.BlockSpec((1, WIN, KV_DIM), lambda i: (jnp.minimum(i * QB + QB, nkb - 1), 0, 0))
    return pl.pallas_call(
        body, name=name,
        out_shape=(jax.ShapeDtypeStruct((S, 2 * KV_DIM), F32), jax.ShapeDtypeStruct((8, 128), F32)),
        grid=(nsteps,),
        in_specs=[prev, cur, nxt, prev, cur, nxt,
                  pl.BlockSpec((TQ, 2 * KV_DIM), lambda i: (i, Q_DIM // (2 * KV_DIM))),
                  pl.BlockSpec((1, KV_DIM), lambda i: (0, 0))],
        out_specs=(pl.BlockSpec((TQ, 2 * KV_DIM), lambda i: (i, 0)), pl.BlockSpec((8, 128), lambda i: (0, 0))),
        scratch_shapes=[pltpu.VMEM((8, KV_DIM), F32)],
        compiler_params=_params(("arbitrary",)),
    )(dkp, dkp, dkp, dvp, dvp, dvp, qkv, kg)


BIAS_COLS = BLOCK * WIN
BIAS_CHUNK = 6144


def _bias_table(rel_bias_t, onehot, band):
    def body(rb_ref, oh_ref, band_ref, o_ref):
        o_ref[...] = _dot(rb_ref[...], oh_ref[...], HI) + band_ref[...]

    return pl.pallas_call(
        body, name="bias_table", out_shape=jax.ShapeDtypeStruct((N_HEADS, BIAS_COLS), F32),
        grid=(BIAS_COLS // BIAS_CHUNK,),
        in_specs=[pl.BlockSpec((N_HEADS, NUM_BUCKETS), lambda i: (0, 0)),
                  pl.BlockSpec((NUM_BUCKETS, BIAS_CHUNK), lambda i: (0, i)),
                  pl.BlockSpec((1, BIAS_CHUNK), lambda i: (0, i))],
        out_specs=pl.BlockSpec((N_HEADS, BIAS_CHUNK), lambda i: (0, i)),
        compiler_params=_params(("parallel",)),
    )(rel_bias_t, onehot, band)


def _bias_grad(dbias, onehot):
    def body(db_ref, oh_ref, o_ref):
        @pl.when(pl.program_id(0) == 0)
        def _():
            o_ref[...] = jnp.zeros_like(o_ref)

        o_ref[...] += lax.dot_general(db_ref[...], oh_ref[...], (((1,), (1,)), ((), ())),
                                      preferred_element_type=F32, precision=HI)

    return pl.pallas_call(
        body, name="bias_grad", out_shape=jax.ShapeDtypeStruct((N_HEADS, NUM_BUCKETS), F32),
        grid=(BIAS_COLS // BIAS_CHUNK,),
        in_specs=[pl.BlockSpec((N_HEADS, BIAS_CHUNK), lambda i: (0, i)),
                  pl.BlockSpec((NUM_BUCKETS, BIAS_CHUNK), lambda i: (0, i))],
        out_specs=pl.BlockSpec((N_HEADS, NUM_BUCKETS), lambda i: (0, 0)),
        compiler_params=_params(("arbitrary",)),
    )(dbias, onehot)


def _bucket_onehot():
    half = NUM_BUCKETS // 2
    max_exact = half // 2
    rel = jnp.arange(WIN)[None, :] - BLOCK - jnp.arange(BLOCK)[:, None]
    n = jnp.abs(rel)
    ret = jnp.where(rel > 0, half, 0)
    nf = jnp.maximum(n, 1).astype(F32)
    large = max_exact + (jnp.log(nf / max_exact) / np.log(MAX_DISTANCE / max_exact)
                         * (half - max_exact)).astype(jnp.int32)
    large = jnp.minimum(large, half - 1)
    bucket = (ret + jnp.where(n < max_exact, n, large)).reshape(1, BIAS_COLS)
    band = jnp.where(n <= BLOCK, 0.0, NEG_INF).astype(F32).reshape(1, BIAS_COLS)
    return (bucket == jnp.arange(NUM_BUCKETS)[:, None]).astype(F32), band


def _halo_specs(tm, width, S):
    r = tm // HALO
    last = S // HALO - 1
    return [pl.BlockSpec((HALO, width), lambda i: (jnp.maximum(i * r - 1, 0), 0)),
            pl.BlockSpec((tm, width), lambda i: (i, 0)),
            pl.BlockSpec((HALO, width), lambda i: (jnp.minimum(i * r + r, last), 0))]


def _with_halo(p_ref, c_ref, n_ref):
    return jnp.concatenate([p_ref[...], c_ref[...], n_ref[...]], axis=0)


def _row_valid(i, tm, S):
    g = i * tm - HALO + lax.broadcasted_iota(jnp.int32, (tm + 2 * HALO, 1), 0)
    return (g >= 0) & (g < S)


def _shifted(x):
    n = x.shape[0]
    return [x if b == 0 else pltpu.roll(x, n - b, 0) for b in range(8)]


def _tap(sh, off, tm):
    a, b = off // 8, off % 8
    return sh[b][8 * a:8 * a + tm]


def _conv_fwd(cvg, cw, cb, lg, lb, tm, name, ex=NO_EXCHANGE):
    S = cvg.shape[0]

    def body(p_ref, c_ref, n_ref, cw_ref, cb_ref, lg_ref, lb_ref, act_ref, yc_ref):
        i = pl.program_id(0)
        z = _with_halo(p_ref, c_ref, n_ref)
        glu = jnp.where(_row_valid(i, tm, S), z[:, :CONV_DIM] * _sig(z[:, CONV_DIM:]), 0.0)
        sh = _shifted(glu)
        y = jnp.zeros((tm, CONV_DIM), F32) + cb_ref[...]
        for w in range(CONV_WIDTH):
            y = y + _tap(sh, w + 1, tm) * cw_ref[w:w + 1, :]
        yc_ref[...] = y
        mu = jnp.mean(y, axis=-1, keepdims=True)
        yc = y - mu
        rstd = lax.rsqrt(jnp.mean(yc * yc, axis=-1, keepdims=True) + 1e-5)
        ln = yc * rstd * lg_ref[...] + lb_ref[...]
        act_ref[...] = _bf(ln * _sig(ln))

    vec = pl.BlockSpec((1, CONV_DIM), lambda i: (0, 0))
    row = pl.BlockSpec((tm, CONV_DIM), lambda i: (i, 0))
    return _carrier_call(
        body, ex, _grid_ends(S // tm), name,
        out_shape=(jax.ShapeDtypeStruct((S, CONV_DIM), BF16), jax.ShapeDtypeStruct((S, CONV_DIM), F32)),
        grid=(S // tm,),
        in_specs=_halo_specs(tm, 2 * CONV_DIM, S) + [pl.BlockSpec((32, CONV_DIM), lambda i: (0, 0)), vec, vec, vec],
        out_specs=(row, row), scratch_shapes=[], args=(cvg, cvg, cvg, cw, cb, lg, lb))


def _conv_bwd(dact, yconv, cvg, cw, lg, lb, tm, name, ex=NO_EXCHANGE):
    S = cvg.shape[0]
    nsteps = S // tm

    def body(dp, dc, dn, yp, yc_, yn, zp, zc, zn, cw_ref, lg_ref, lb_ref,
             dz_ref, dcw_ref, dvec_ref, dcw_s, dvec_s, shg_s, shd_s):
        i = pl.program_id(0)

        @pl.when(i == 0)
        def _():
            dcw_s[...] = jnp.zeros_like(dcw_s)
            dvec_s[...] = jnp.zeros_like(dvec_s)

        valid = _row_valid(i, tm, S)
        own = (lax.broadcasted_iota(jnp.int32, (tm + 2 * HALO, 1), 0) >= HALO) & (
            lax.broadcasted_iota(jnp.int32, (tm + 2 * HALO, 1), 0) < HALO + tm)
        y = _with_halo(yp, yc_, yn)
        dact_ = _with_halo(dp, dc, dn)
        mu = jnp.mean(y, axis=-1, keepdims=True)
        ycen = y - mu
        rstd = lax.rsqrt(jnp.mean(ycen * ycen, axis=-1, keepdims=True) + 1e-5)
        yhat = ycen * rstd
        ln = yhat * lg_ref[...] + lb_ref[...]
        sg = _sig(ln)
        dln = dact_ * (sg * (1.0 + ln * (1.0 - sg)))
        dyhat = dln * lg_ref[...]
        dy = rstd * (dyhat - jnp.mean(dyhat, axis=-1, keepdims=True)
                     - yhat * jnp.mean(dyhat * yhat, axis=-1, keepdims=True))
        dy = jnp.where(valid, dy, 0.0)
        dln_own = jnp.where(own, dln, 0.0)
        nr = (tm + 2 * HALO) // 8
        dvec_s[0] += jnp.sum(jnp.where(own, dy, 0.0).reshape(nr, 8, CONV_DIM), axis=0)
        dvec_s[1] += jnp.sum((dln_own * yhat).reshape(nr, 8, CONV_DIM), axis=0)
        dvec_s[2] += jnp.sum(dln_own.reshape(nr, 8, CONV_DIM), axis=0)
        z = _with_halo(zp, zc, zn)
        glu = jnp.where(valid, z[:, :CONV_DIM] * _sig(z[:, CONV_DIM:]), 0.0)
        for b, (g_b, d_b) in enumerate(zip(_shifted(glu), _shifted(dy))):
            shg_s[b] = g_b
            shd_s[b] = d_b
        for cb in range(CONV_DIM // 128):
            lanes = slice(128 * cb, 128 * (cb + 1))
            for rb in range(tm // CROWS):
                r0 = rb * CROWS
                dy_own = shd_s[0, HALO + r0:HALO + r0 + CROWS, lanes]
                dglu = jnp.zeros((CROWS, 128), F32)
                for w in range(CONV_WIDTH):
                    a, b = divmod(CONV_WIDTH - w, 8)
                    dglu = dglu + shd_s[b, 8 * a + r0:8 * a + r0 + CROWS, lanes] * cw_ref[w:w + 1, lanes]
                    a, b = divmod(w + 1, 8)
                    prod = dy_own * shg_s[b, 8 * a + r0:8 * a + r0 + CROWS, lanes]
                    dcw_s[w, :, lanes] += jnp.sum(prod.reshape(CROWS // 8, 8, 128), axis=0)
                cv = zc[r0:r0 + CROWS, lanes]
                sg_o = _sig(zc[r0:r0 + CROWS, CONV_DIM + 128 * cb:CONV_DIM + 128 * (cb + 1)])
                dz_ref[r0:r0 + CROWS, lanes] = dglu * sg_o
                dz_ref[r0:r0 + CROWS, CONV_DIM + 128 * cb:CONV_DIM + 128 * (cb + 1)] = (
                    dglu * cv * sg_o * (1.0 - sg_o))

        @pl.when(i == nsteps - 1)
        def _():
            dcw_ref[...] = jnp.sum(dcw_s[...], axis=1)
            dvec_ref[...] = jnp.sum(dvec_s[...], axis=1)

    vec = pl.BlockSpec((1, CONV_DIM), lambda i: (0, 0))
    return _carrier_call(
        body, ex, _grid_ends(nsteps), name,
        out_shape=(jax.ShapeDtypeStruct((S, 2 * CONV_DIM), F32), jax.ShapeDtypeStruct((32, CONV_DIM), F32),
                   jax.ShapeDtypeStruct((8, CONV_DIM), F32)),
        grid=(nsteps,),
        in_specs=_halo_specs(tm, CONV_DIM, S) + _halo_specs(tm, CONV_DIM, S) + _halo_specs(tm, 2 * CONV_DIM, S)
        + [pl.BlockSpec((32, CONV_DIM), lambda i: (0, 0)), vec, vec],
        out_specs=(pl.BlockSpec((tm, 2 * CONV_DIM), lambda i: (i, 0)),
                   pl.BlockSpec((32, CONV_DIM), lambda i: (0, 0)), pl.BlockSpec((8, CONV_DIM), lambda i: (0, 0))),
        scratch_shapes=[pltpu.VMEM((32, 8, CONV_DIM), F32), pltpu.VMEM((8, 8, CONV_DIM), F32),
                        pltpu.VMEM((8, tm + 2 * HALO, CONV_DIM), F32), pltpu.VMEM((8, tm + 2 * HALO, CONV_DIM), F32)],
        args=(dact, dact, dact, yconv, yconv, yconv, cvg, cvg, cvg, cw, lg, lb))


def _merge_parts(un, o, cact, wg_ref, wao_ref, wco_ref):
    g = _dot(un, wg_ref[...])
    ga, gc = _sig(g[:, :D_MODEL]), _sig(g[:, D_MODEL:])
    ya = _dot(o, wao_ref[...])
    yc = _dot(cact, wco_ref[...])
    return ga, gc, ya, yc


def _merge_specs(tm):
    row = lambda w: pl.BlockSpec((tm, w), lambda i: (i, 0))
    full = lambda a, b: pl.BlockSpec((a, b), lambda i: (0, 0))
    weights = [full(D_MODEL, 2 * D_MODEL), full(Q_DIM, D_MODEL), full(CONV_DIM, D_MODEL), full(D_MODEL, D_MODEL)]
    return row, weights


def _merge_fwd(h1, un, o, cact, w_g, w_ao, w_co, w_o, tm, name, ex=NO_EXCHANGE):
    S = h1.shape[0]
    row, weights = _merge_specs(tm)

    def body(h1_ref, un_ref, o_ref, c_ref, wg_ref, wao_ref, wco_ref, wo_ref, h2_ref):
        ga, gc, ya, yc = _merge_parts(un_ref[...], o_ref[...], c_ref[...], wg_ref, wao_ref, wco_ref)
        h2_ref[...] = h1_ref[...] + _dot(_bf(ga * ya + gc * yc), wo_ref[...])

    return _carrier_call(
        body, ex, _grid_ends(S // tm), name, out_shape=(jax.ShapeDtypeStruct((S, D_MODEL), F32),),
        grid=(S // tm,),
        in_specs=[row(D_MODEL), row(D_MODEL), row(Q_DIM), row(CONV_DIM)] + weights,
        out_specs=(row(D_MODEL),), scratch_shapes=[], args=(h1, un, o, cact, w_g, w_ao, w_co, w_o))


def _merge_bwd(dh2, un, o, cact, w_g, w_ao, w_co, w_o, tm, name):
    S = dh2.shape[0]
    row, weights = _merge_specs(tm)

    def body(dh2_ref, un_ref, o_ref, c_ref, wg_ref, wao_ref, wco_ref, wo_ref,
             do_ref, dc_ref, mix_ref, dya_ref, dyc_ref, dgp_ref):
        ga, gc, ya, yc = _merge_parts(un_ref[...], o_ref[...], c_ref[...], wg_ref, wao_ref, wco_ref)
        mix_ref[...] = _bf(ga * ya + gc * yc)
        dmix = _dot_nt(_bf(dh2_ref[...]), wo_ref[...])
        dya = _bf(dmix * ga)
        dyc = _bf(dmix * gc)
        dya_ref[...] = dya
        dyc_ref[...] = dyc
        dgp_ref[:, :D_MODEL] = _bf(dmix * ya * ga * (1.0 - ga))
        dgp_ref[:, D_MODEL:] = _bf(dmix * yc * gc * (1.0 - gc))
        do_ref[...] = _dot_nt(dya, wao_ref[...])
        dc_ref[...] = _dot_nt(dyc, wco_ref[...])

    return pl.pallas_call(
        body, name=name,
        out_shape=(jax.ShapeDtypeStruct((S, Q_DIM), F32), jax.ShapeDtypeStruct((S, CONV_DIM), F32),
                   jax.ShapeDtypeStruct((S, D_MODEL), BF16), jax.ShapeDtypeStruct((S, D_MODEL), BF16),
                   jax.ShapeDtypeStruct((S, D_MODEL), BF16), jax.ShapeDtypeStruct((S, 2 * D_MODEL), BF16)),
        grid=(S // tm,),
        in_specs=[row(D_MODEL), row(D_MODEL), row(Q_DIM), row(CONV_DIM)] + weights,
        out_specs=(row(Q_DIM), row(CONV_DIM), row(D_MODEL), row(D_MODEL), row(D_MODEL), row(2 * D_MODEL)),
        compiler_params=_params(("parallel",)),
    )(dh2, un, o, cact, w_g, w_ao, w_co, w_o)


def _pe_specs(tm, layer):
    row = pl.BlockSpec((tm, D_MODEL), lambda i: (i, 0))
    vec = pl.BlockSpec((1, D_MODEL), lambda i: (0, 0))
    p_s = pl.BlockSpec((None, None, tm, 256), lambda i: (layer, 0, i, 0))
    wpp = pl.BlockSpec((256, D_MODEL), lambda i: (0, 0))
    wpg = pl.BlockSpec((D_MODEL, D_MODEL), lambda i: (0, 0))
    return row, vec, p_s, wpp, wpg


def _pe_fwd(h, gamma, p, layer, w_pp, w_pg, tm, name):
    S = h.shape[0]
    row, vec, p_s, wpp, wpg = _pe_specs(tm, layer)

    def body(h_ref, g_ref, p_ref, wpp_ref, wpg_ref, x_ref, hn_ref):
        hn = _bf(_rms_fwd(h_ref[...], g_ref[...])[0])
        hn_ref[...] = hn
        gate = _sig(_dot(hn, wpg_ref[...]))
        x_ref[...] = h_ref[...] + _dot(_bf(p_ref[...]), wpp_ref[...]) * gate

    return pl.pallas_call(
        body, name=name,
        out_shape=(jax.ShapeDtypeStruct((S, D_MODEL), F32), jax.ShapeDtypeStruct((S, D_MODEL), BF16)),
        grid=(S // tm,), in_specs=[row, vec, p_s, wpp, wpg], out_specs=(row, row),
        compiler_params=_params(("parallel",)),
    )(h, gamma, p, w_pp, w_pg)


def _pe_bwd(dx, h, gamma, hn, p, layer, w_pp, w_pg, tm, name):
    S = h.shape[0]
    row, vec, p_s, wpp, wpg = _pe_specs(tm, layer)

    def body(dx_ref, h_ref, g_ref, hn_ref, p_ref, wpp_ref, wpg_ref, dh_ref, dgp_ref, dpr_ref, dgam_ref):
        @pl.when(pl.program_id(0) == 0)
        def _():
            dgam_ref[...] = jnp.zeros_like(dgam_ref)

        dxv = dx_ref[...]
        gate = _sig(_dot(hn_ref[...], wpg_ref[...]))
        proj = _dot(_bf(p_ref[...]), wpp_ref[...])
        dpr_ref[...] = _bf(dxv * gate)
        dgp = _bf(dxv * proj * gate * (1.0 - gate))
        dgp_ref[...] = dgp
        dxn, dgam = _rms_bwd(_dot_nt(dgp, wpg_ref[...]), h_ref[...], g_ref[...])
        dh_ref[...] = dxv + dxn
        dgam_ref[...] += jnp.sum(dgam, axis=0, keepdims=True)

    return pl.pallas_call(
        body, name=name,
        out_shape=(jax.ShapeDtypeStruct((S, D_MODEL), F32), jax.ShapeDtypeStruct((S, D_MODEL), BF16),
                   jax.ShapeDtypeStruct((S, D_MODEL), BF16), jax.ShapeDtypeStruct((1, D_MODEL), F32)),
        grid=(S // tm,), in_specs=[row, row, vec, row, p_s, wpp, wpg], out_specs=(row, row, row, vec),
        compiler_params=_params(("arbitrary",)),
    )(dx, h, gamma, hn, p, w_pp, w_pg)


def _loss_head(y, target, tm):
    S = y.shape[0]

    def body(y_ref, t_ref, dy_ref, l_ref):
        @pl.when(pl.program_id(0) == 0)
        def _():
            l_ref[...] = jnp.zeros_like(l_ref)

        diff = y_ref[...] - t_ref[...]
        dy_ref[...] = diff * (1.0 / D_MODEL)
        sq = jnp.sum((diff * diff).reshape(tm // 8, 8, D_MODEL), axis=0)
        part = sq[:, 0:128]
        for k in range(1, D_MODEL // 128):
            part = part + sq[:, 128 * k:128 * (k + 1)]
        l_ref[...] += part

    row = pl.BlockSpec((tm, D_MODEL), lambda i: (i, 0))
    return pl.pallas_call(
        body, name="loss_head",
        out_shape=(jax.ShapeDtypeStruct((S, D_MODEL), F32), jax.ShapeDtypeStruct((8, 128), F32)),
        grid=(S // tm,), in_specs=[row, row], out_specs=(row, pl.BlockSpec((8, 128), lambda i: (0, 0))),
        compiler_params=_params(("arbitrary",)),
    )(y, target)


def _adamw(parts, w, m, v, name):
    nl = len(parts)
    R, C = w.shape
    K = R // nl
    tr = K
    for cand in (256, 128, 64, 32, 16):
        if K % cand == 0:
            tr = cand
            break
    nk = K // tr

    def body(*refs):
        p_refs = refs[:nl]
        w_ref, m_ref, v_ref, g_ref, d_ref, nm_ref, nv_ref = refs[nl:]
        for lyr in range(nl):
            @pl.when(pl.program_id(0) == lyr)
            def _(p_ref=p_refs[lyr]):
                g = p_ref[0].astype(F32)
                for k in range(1, N_DEV):
                    g = g + p_ref[k].astype(F32)
                g_ref[...] = g
                nm = ADAM_B1 * m_ref[...] + (1.0 - ADAM_B1) * g
                nv = ADAM_B2 * v_ref[...] + (1.0 - ADAM_B2) * (g * g)
                nm_ref[...] = nm
                nv_ref[...] = nv
                m_hat = nm / (1.0 - ADAM_B1 ** ADAM_STEP)
                v_hat = nv / (1.0 - ADAM_B2 ** ADAM_STEP)
                d_ref[...] = -ADAM_LR * (m_hat / (jnp.sqrt(v_hat) + ADAM_EPS) + ADAM_WD * w_ref[...])

    def part_spec(lyr):
        return pl.BlockSpec((N_DEV, tr, C), lambda l, i: (0, jnp.where(l == lyr, i, jnp.where(l < lyr, 0, nk - 1)), 0))

    blk = pl.BlockSpec((tr, C), lambda l, i: (l * nk + i, 0))
    out = jax.ShapeDtypeStruct((R, C), F32)
    return pl.pallas_call(
        body, name=name, out_shape=(out, out, out, out), grid=(nl, nk),
        in_specs=[part_spec(lyr) for lyr in range(nl)] + [blk, blk, blk],
        out_specs=(blk, blk, blk, blk),
        compiler_params=_params(("arbitrary", "arbitrary")),
    )(*parts, w, m, v)


SHARDED = ("w_ffn1_in", "w_ffn1_out", "w_in", "conv_w", "w_attn_out", "w_conv_out", "w_o",
           "w_ffn2_in", "w_ffn2_out", "w_pe_gate", "w_pe_proj")
COL_SHARDED = ("w_ffn1_in", "w_in", "conv_w", "w_attn_out", "w_conv_out", "w_ffn2_in", "w_pe_proj")
SMALL = ("rel_bias", "norm_ffn1", "norm_mix", "q_norm", "k_norm", "sink", "conv_b", "conv_ln_g", "conv_ln_b",
         "norm_ffn2", "norm_pe")
WEIGHTS = ("rel_bias", "norm_ffn1", "w_ffn1_in", "w_ffn1_out", "norm_mix", "w_in", "q_norm", "k_norm", "sink",
           "conv_w", "conv_b", "conv_ln_g", "conv_ln_b", "w_attn_out", "w_conv_out", "w_o", "norm_ffn2",
           "w_ffn2_in", "w_ffn2_out", "norm_pe", "w_pe_gate", "w_pe_proj")


def _natural(g):
    k, n = g.shape[1], g.shape[2]
    return jnp.transpose(g, (1, 0, 2)).reshape(k, N_DEV * n)


def _blocked(w):
    k, n = w.shape[0], w.shape[1] // N_DEV
    return jnp.transpose(w.reshape(k, N_DEV, n), (1, 0, 2))


def kernel(x, p, rel_bias, norm_ffn1, w_ffn1_in, w_ffn1_out, norm_mix, w_in, q_norm, k_norm, sink, conv_w, conv_b, conv_ln_g, conv_ln_b, w_attn_out, w_conv_out, w_o, norm_ffn2, w_ffn2_in, w_ffn2_out, norm_pe, w_pe_gate, w_pe_proj, loss_target, m_rel_bias, m_norm_ffn1, m_w_ffn1_in, m_w_ffn1_out, m_norm_mix, m_w_in, m_q_norm, m_k_norm, m_sink, m_conv_w, m_conv_b, m_conv_ln_g, m_conv_ln_b, m_w_attn_out, m_w_conv_out, m_w_o, m_norm_ffn2, m_w_ffn2_in, m_w_ffn2_out, m_norm_pe, m_w_pe_gate, m_w_pe_proj, v_rel_bias, v_norm_ffn1, v_w_ffn1_in, v_w_ffn1_out, v_norm_mix, v_w_in, v_q_norm, v_k_norm, v_sink, v_conv_w, v_conv_b, v_conv_ln_g, v_conv_ln_b, v_w_attn_out, v_w_conv_out, v_w_o, v_norm_ffn2, v_w_ffn2_in, v_w_ffn2_out, v_norm_pe, v_w_pe_gate, v_w_pe_proj):
    W = dict(rel_bias=rel_bias, norm_ffn1=norm_ffn1, w_ffn1_in=w_ffn1_in, w_ffn1_out=w_ffn1_out, norm_mix=norm_mix,
             w_in=w_in, q_norm=q_norm, k_norm=k_norm, sink=sink, conv_w=conv_w, conv_b=conv_b, conv_ln_g=conv_ln_g,
             conv_ln_b=conv_ln_b, w_attn_out=w_attn_out, w_conv_out=w_conv_out, w_o=w_o, norm_ffn2=norm_ffn2,
             w_ffn2_in=w_ffn2_in, w_ffn2_out=w_ffn2_out, norm_pe=norm_pe, w_pe_gate=w_pe_gate, w_pe_proj=w_pe_proj)
    M = dict(rel_bias=m_rel_bias, norm_ffn1=m_norm_ffn1, w_ffn1_in=m_w_ffn1_in, w_ffn1_out=m_w_ffn1_out,
             norm_mix=m_norm_mix, w_in=m_w_in, q_norm=m_q_norm, k_norm=m_k_norm, sink=m_sink, conv_w=m_conv_w,
             conv_b=m_conv_b, conv_ln_g=m_conv_ln_g, conv_ln_b=m_conv_ln_b, w_attn_out=m_w_attn_out,
             w_conv_out=m_w_conv_out, w_o=m_w_o, norm_ffn2=m_norm_ffn2, w_ffn2_in=m_w_ffn2_in,
             w_ffn2_out=m_w_ffn2_out, norm_pe=m_norm_pe, w_pe_gate=m_w_pe_gate, w_pe_proj=m_w_pe_proj)
    V = dict(rel_bias=v_rel_bias, norm_ffn1=v_norm_ffn1, w_ffn1_in=v_w_ffn1_in, w_ffn1_out=v_w_ffn1_out,
             norm_mix=v_norm_mix, w_in=v_w_in, q_norm=v_q_norm, k_norm=v_k_norm, sink=v_sink, conv_w=v_conv_w,
             conv_b=v_conv_b, conv_ln_g=v_conv_ln_g, conv_ln_b=v_conv_ln_b, w_attn_out=v_w_attn_out,
             w_conv_out=v_w_conv_out, w_o=v_w_o, norm_ffn2=v_norm_ffn2, w_ffn2_in=v_w_ffn2_in,
             w_ffn2_out=v_w_ffn2_out, norm_pe=v_norm_pe, w_pe_gate=v_w_pe_gate, w_pe_proj=v_w_pe_proj)

    L = w_in.shape[0]
    S = x.shape[1]
    tm = min(512, S)
    tm_ffn = min(1024, S)
    xs = x[0]
    target = loss_target[0]
    vec = lambda a: a.reshape(1, -1)

    half, full = {}, {}

    def carried(stage1_items, stage2_items):
        s1 = [it for it in stage1_items if it[1] < L]
        s2 = [it for it in stage2_items if it[1] < L]
        ex = _stage1([W[n][l] if n == "conv_w" else W[n][l].astype(BF16) for n, l in s1]) + _stage2(
            [half.pop(it) for it in s2])
        return ex, s1, s2

    def landed(got, s1, s2):
        half.update(zip(s1, got[:len(s1)]))
        full.update(zip(s2, got[len(s1):]))

    onehot, band = _bucket_onehot()
    bias = _bias_table(rel_bias.T, onehot, band).reshape(N_HEADS, BLOCK, WIN)

    layers, saved = [], []
    h = xs
    ex, s1, s2 = carried([("w_ffn1_in", 0), ("w_ffn1_out", 0), ("w_in", 0), ("conv_w", 0)], [])
    landed(_exchange(ex, "allgather_first"), s1, s2)
    ex, s1, s2 = carried([], [("w_ffn1_in", 0), ("w_ffn1_out", 0)])
    landed(_exchange(ex, "allgather_relay"), s1, s2)
    for l in range(L):
        sv = dict(x0=h)
        G = dict(wi1=full.pop(("w_ffn1_in", l)), wo1=full.pop(("w_ffn1_out", l)))
        ex, s1, s2 = carried(
            [(n, l) for n in ("w_ffn2_in", "w_ffn2_out", "w_attn_out", "w_conv_out", "w_o")],
            [("w_in", l), ("conv_w", l)])
        (h1, sv["xn1"], sv["g1"], sv["u1"]), got = _ffn_fwd(
            h, vec(norm_ffn1[l]), G["wi1"], G["wo1"], tm_ffn, "ffn1_fwd", ex)
        landed(got, s1, s2)
        w_in_n = _natural(full.pop(("w_in", l)))
        G.update(w_qc=w_in_n[:, :QC_DIM], w_g=w_in_n[:, QC_DIM:],
                 conv_w=jnp.pad(_natural(full.pop(("conv_w", l))), ((0, 1), (0, 0))))
        sv["h1"] = h1
        sv["un"], sv["qkv"], sv["cvg"] = _mixin_fwd(h1, vec(norm_mix[l]), G["w_qc"], tm, "mixin_fwd")
        sv["qg"] = vec(jnp.tile(q_norm[l], 2))
        sv["kg"] = vec(jnp.tile(k_norm[l], KV_HEADS))
        sv["sinkb"] = jnp.broadcast_to(sink[l][:, None], (N_HEADS, 128))
        ex, s1, s2 = carried(
            [("w_pe_gate", l), ("w_pe_proj", l), ("w_ffn1_in", l + 1)],
            [(n, l) for n in ("w_ffn2_in", "w_ffn2_out", "w_attn_out", "w_conv_out", "w_o")])
        (sv["o"],), got = _attn_fwd(sv["qkv"], sv["qg"], sv["kg"], sv["sinkb"], bias, "attn_fwd", ex)
        landed(got, s1, s2)
        G.update(wi2=full.pop(("w_ffn2_in", l)), wo2=full.pop(("w_ffn2_out", l)),
                 w_ao=_natural(full.pop(("w_attn_out", l))), w_co=_natural(full.pop(("w_conv_out", l))),
                 w_o=full.pop(("w_o", l)).reshape(D_MODEL, D_MODEL))
        ex, s1, s2 = carried([("w_ffn1_out", l + 1)],
                             [("w_pe_gate", l), ("w_pe_proj", l), ("w_ffn1_in", l + 1)])
        (sv["cact"], sv["yconv"]), got = _conv_fwd(
            sv["cvg"], G["conv_w"], vec(conv_b[l]), vec(conv_ln_g[l]), vec(conv_ln_b[l]), tm, "conv_fwd", ex)
        landed(got, s1, s2)
        G.update(w_pg=full.pop(("w_pe_gate", l)).reshape(D_MODEL, D_MODEL),
                 w_pp=_natural(full.pop(("w_pe_proj", l))))
        ex, s1, s2 = carried([("w_in", l + 1), ("conv_w", l + 1)], [("w_ffn1_out", l + 1)])
        (h2,), got = _merge_fwd(h1, sv["un"], sv["o"], sv["cact"], G["w_g"], G["w_ao"], G["w_co"], G["w_o"], tm,
                                "merge_fwd", ex)
        landed(got, s1, s2)
        sv["h2"] = h2
        (h3, sv["xn2"], sv["g2"], sv["u2"]), _ = _ffn_fwd(
            h2, vec(norm_ffn2[l]), G["wi2"], G["wo2"], tm_ffn, "ffn2_fwd")
        sv["h3"] = h3
        h, sv["hn"] = _pe_fwd(h3, vec(norm_pe[l]), p, l, G["w_pp"], G["w_pg"], tm, "pe_fwd")
        layers.append(G)
        saved.append(sv)

    dh, lparts = _loss_head(h, target, tm)
    loss = lax.psum((0.5 / D_MODEL) * jnp.sum(lparts), AXES)

    dbias = jnp.zeros((N_HEADS, BLOCK, WIN), F32)
    small_g = {n: [None] * L for n in SMALL if n != "rel_bias"}
    recv = {n: [None] * L for n in SHARDED}

    def keep(names, l, got):
        for n, r in zip(names, got):
            recv[n][l] = r

    pending = None
    for l in reversed(range(L)):
        G, sv = layers[l], saved[l]
        dh3, dgp_pe, dproj, dg_pe = _pe_bwd(dh, sv["h3"], vec(norm_pe[l]), sv["hn"], p, l, G["w_pp"], G["w_pg"],
                                            tm, "pe_bwd")
        gw_pg = _matmul_tn(sv["hn"][None], dgp_pe[None], 1, "dw_pe_gate")
        gw_pp = _matmul_tn(p[l], dproj[None], 1, "dw_pe_proj")
        (dh2, a2, dgu2, dg_n2), got = _ffn_bwd(
            dh3, sv["h2"], vec(norm_ffn2[l]), sv["g2"], sv["u2"], G["wi2"], G["wo2"], tm_ffn, "ffn2_bwd",
            _Exchange(pending, False) if pending else NO_EXCHANGE)
        if pending:
            keep(("w_ffn1_in",), l + 1, got)
        gwo2 = _matmul_tn(a2, dh3[None], FF_BLOCKS, "dw_ffn2_out", scale=0.5)
        gwi2 = _matmul_tn(sv["xn2"][None], dgu2.reshape(2 * FF_BLOCKS, S, FF_SHARD), 2 * FF_BLOCKS, "dw_ffn2_in")
        do, dcact, mix, dya, dyc, dgpre = _merge_bwd(dh2, sv["un"], sv["o"], sv["cact"], G["w_g"], G["w_ao"],
                                                     G["w_co"], G["w_o"], tm, "merge_bwd")
        gw_o = _matmul_tn(mix[None], dh2[None], 1, "dw_o")
        gw_ao = _matmul_tn(sv["o"][None], dya[None], 1, "dw_attn_out")
        gw_co = _matmul_tn(sv["cact"][None], dyc[None], 1, "dw_conv_out")
        (dq, dkp, dvp, dbias, dsink, dqg), got = _attn_bwd(
            do, sv["qkv"], sv["qg"], sv["kg"], sv["sinkb"], bias, dbias, "attn_bwd",
            _Exchange([gwi2, gw_pg.reshape(N_DEV, D_MODEL // N_DEV, D_MODEL), _blocked(gw_pp[0])], False))
        keep(("w_ffn2_in", "w_pe_gate", "w_pe_proj"), l, got)
        dkv, dkg = _kv_fold(dkp, dvp, sv["qkv"], sv["kg"], "kv_fold")
        (dcvg, dcw, dcvec), got = _conv_bwd(
            dcact, sv["yconv"], sv["cvg"], G["conv_w"], vec(conv_ln_g[l]), vec(conv_ln_b[l]), tm, "conv_bwd",
            _Exchange([gwo2.reshape(N_DEV, FF_SHARD // 2, D_MODEL)], False))
        keep(("w_ffn2_out",), l, got)
        dh1, dg_mix, dz = _mixin_bwd(dh2, sv["h1"], vec(norm_mix[l]), dq, dkv, dcvg, dgpre, G["w_qc"], G["w_g"],
                                     tm, "mixin_bwd")
        gw_in = _matmul_tn(sv["un"][None], dz[None], 1, "dw_in")[0]
        mid_send = [_blocked(gw_in), _blocked(dcw[:CONV_WIDTH]), _blocked(gw_ao[0]), _blocked(gw_co[0]),
                    gw_o.reshape(N_DEV, D_MODEL // N_DEV, D_MODEL)]
        (dh, a1, dgu1, dg_n1), got = _ffn_bwd(
            dh1, sv["x0"], vec(norm_ffn1[l]), sv["g1"], sv["u1"], G["wi1"], G["wo1"], tm_ffn, "ffn1_bwd",
            _Exchange(mid_send, False))
        keep(("w_in", "conv_w", "w_attn_out", "w_conv_out", "w_o"), l, got)
        gwo1 = _matmul_tn(a1, dh1[None], FF_BLOCKS, "dw_ffn1_out", scale=0.5)
        gwi1, got = _matmul_tn(sv["xn1"][None], dgu1.reshape(2 * FF_BLOCKS, S, FF_SHARD), 2 * FF_BLOCKS,
                               "dw_ffn1_in", ex=_Exchange([gwo1.reshape(N_DEV, FF_SHARD // 2, D_MODEL)], False))
        keep(("w_ffn1_out",), l, got)
        pending = [gwi1]
        small_g["norm_ffn1"][l] = dg_n1[0]
        small_g["norm_mix"][l] = dg_mix[0]
        small_g["q_norm"][l] = dqg[0, :HEAD_DIM]
        small_g["k_norm"][l] = dkg[0, :HEAD_DIM]
        small_g["sink"][l] = dsink[:, 0]
        small_g["conv_b"][l] = dcvec[0]
        small_g["conv_ln_g"][l] = dcvec[1]
        small_g["conv_ln_b"][l] = dcvec[2]
        small_g["norm_ffn2"][l] = dg_n2[0]
        small_g["norm_pe"][l] = dg_pe[0]

    keep(("w_ffn1_in",), 0, _exchange(_Exchange(pending, False), "grad_exchange_last"))
    grad_x = dh[None]
    drb = _bias_grad(dbias.reshape(N_HEADS, BIAS_COLS), onehot).T

    res = {}
    for n in SHARDED:
        shp = W[n].shape
        rows, cols = shp[0] * shp[1], shp[2]
        parts = recv[n]
        if shp[1] % 8:
            parts = [jnp.stack(recv[n], axis=1).reshape(N_DEV, rows, cols)]
        res[n] = [o.reshape(shp) for o in _adamw(
            parts, W[n].reshape(rows, cols), M[n].reshape(rows, cols), V[n].reshape(rows, cols), "adamw_" + n)]

    flat_g = jnp.concatenate([drb.reshape(-1)] + [jnp.stack(small_g[n]).reshape(-1) for n in SMALL[1:]])
    n_small = flat_g.shape[0]
    rows_s = -(-n_small // 1024 // 8) * 8
    pad = lambda a: jnp.pad(a, (0, rows_s * 1024 - n_small)).reshape(rows_s, 1024)
    flat = lambda d: pad(jnp.concatenate([d[n].reshape(-1) for n in SMALL]))
    (parts_s,) = _exchange(_Exchange([pad(flat_g)], True), "small_allgather")
    outs_s = _adamw([parts_s], flat(W), flat(M), flat(V), "adamw_small")
    off = 0
    for n in SMALL:
        size = W[n].size
        res[n] = [o.reshape(-1)[off:off + size].reshape(W[n].shape) for o in outs_s]
        off += size

    out = [loss, grad_x]
    for k in range(4):
        out += [res[n][k] for n in WEIGHTS]
    return tuple(out)
```

```python
import functools

import jax
import jax.numpy as jnp
import numpy as np
from jax import lax
from jax.experimental import pallas as pl
from jax.experimental.pallas import tpu as pltpu

F32 = jnp.float32
BF16 = jnp.bfloat16
MESH_ID = pl.DeviceIdType.MESH
AXES = ("x", "y", "c")
N_DEV = 8

D_MODEL = 1024
N_HEADS = 8
KV_HEADS = 2
HEAD_DIM = 64
Q_DIM = 512
KV_DIM = 128
BLOCK = 128
WIN = 3 * BLOCK
NUM_BUCKETS = 32
MAX_DISTANCE = 128
CONV_DIM = 512
CONV_WIDTH = 31
D_FF = 2816
FF_SHARD = 2 * D_FF // N_DEV
FF_BLOCKS = D_FF // FF_SHARD
QC_DIM = Q_DIM + 2 * KV_DIM + 2 * CONV_DIM
IN_DIM = QC_DIM + 2 * D_MODEL
NEG_INF = -1e9
HALO = 16
CROWS = 64
FFN_PART_ROWS = 256

ADAM_LR = 0.001
ADAM_B1 = 0.9
ADAM_B2 = 0.999
ADAM_EPS = 1e-08
ADAM_WD = 0.01
ADAM_STEP = 10

VMEM_LIMIT = 56 * 1024 * 1024
HI = lax.Precision.HIGHEST


def _params(sem):
    return pltpu.CompilerParams(dimension_semantics=sem, vmem_limit_bytes=VMEM_LIMIT)


def _dot(a, b, precision=None):
    return jnp.dot(a, b, preferred_element_type=F32, precision=precision)


def _dot_nt(a, b):
    return lax.dot_general(a, b, (((1,), (1,)), ((), ())), preferred_element_type=F32)


def _dot_tn(a, b):
    return lax.dot_general(a, b, (((0,), (0,)), ((), ())), preferred_element_type=F32)


def _sig(x):
    return 1.0 / (1.0 + jnp.exp(-x))


def _bf(x):
    return x.astype(BF16)


def _rms_fwd(x, gamma):
    r = lax.rsqrt(jnp.mean(x * x, axis=-1, keepdims=True) + 1e-6)
    return x * r * gamma, r


def _rms_bwd(dy, x, gamma):
    r = lax.rsqrt(jnp.mean(x * x, axis=-1, keepdims=True) + 1e-6)
    xhat = x * r
    dxhat = dy * gamma
    dx = r * (dxhat - xhat * jnp.mean(dxhat * xhat, axis=-1, keepdims=True))
    return dx, dy * xhat


SAME_CORE = (2, 4, 6)


class _Exchange:
    def __init__(self, arrs=(), gather=True, kinds=None):
        self.arrs = list(arrs)
        self.n = n = len(self.arrs)
        self.kinds = list(kinds) if kinds is not None else ["gather" if gather else "scatter"] * n
        self.out_shape = tuple(
            jax.ShapeDtypeStruct(((N_DEV,) + a.shape) if k in ("gather", "stage1") else a.shape, a.dtype)
            for a, k in zip(self.arrs, self.kinds))
        self.aliases = {t: t for t, k in enumerate(self.kinds) if k == "stage2"}
        self.specs = [pl.BlockSpec(memory_space=pl.ANY)] * n
        self.scratch = [pltpu.SemaphoreType.DMA((7 * n,)), pltpu.SemaphoreType.DMA((7 * n,)),
                        pltpu.SemaphoreType.DMA((n,))] if n else []

    def __add__(self, other):
        return _Exchange(self.arrs + other.arrs, kinds=self.kinds + other.kinds)

    def _copies(self, ins, outs, sems):
        n = self.n
        send_sems, recv_sems, local_sems = sems
        x, y, c = lax.axis_index("x"), lax.axis_index("y"), lax.axis_index("c")
        me = 4 * x + 2 * y + c
        copies = []
        for t, kind in enumerate(self.kinds):
            if kind != "stage2":
                copies.append(pltpu.make_async_copy(
                    ins[t].at[me] if kind == "scatter" else ins[t], outs[t].at[me], local_sems.at[t]))
            offsets = {"gather": range(1, N_DEV), "scatter": range(1, N_DEV),
                       "stage1": (1,) + SAME_CORE, "stage2": SAME_CORE}[kind]
            for d in offsets:
                px = 1 - x if d & 4 else x
                py = 1 - y if d & 2 else y
                pc = 1 - c if d & 1 else c
                peer = 4 * px + 2 * py + pc
                if kind == "stage2":
                    src, dst, to = ins[t].at[peer], outs[t].at[peer], (x, y, 1 - c)
                else:
                    src, dst, to = (ins[t].at[peer] if kind == "scatter" else ins[t]), outs[t].at[me], (px, py, pc)
                k = (d - 1) * n + t
                copies.append(pltpu.make_async_remote_copy(
                    src_ref=src, dst_ref=dst, send_sem=send_sems.at[k], recv_sem=recv_sems.at[k],
                    device_id=to, device_id_type=MESH_ID))
        return copies

    def start(self, ins, outs, sems):
        for cp in self._copies(ins, outs, sems):
            cp.start()

    def wait(self, ins, outs, sems):
        for cp in self._copies(ins, outs, sems):
            cp.wait()


NO_EXCHANGE = _Exchange()


def _stage1(arrs):
    return _Exchange(arrs, kinds=["stage1"] * len(arrs))


def _stage2(arrs):
    return _Exchange(arrs, kinds=["stage2"] * len(arrs))


def _exchange(ex, name):
    n = ex.n

    def body(*refs):
        ins, outs, sems = refs[:n], refs[n:2 * n], refs[2 * n:]
        ex.start(ins, outs, sems)
        ex.wait(ins, outs, sems)

    return pl.pallas_call(
        body, name=name, out_shape=ex.out_shape, in_specs=ex.specs, out_specs=tuple(ex.specs),
        scratch_shapes=ex.scratch, input_output_aliases=ex.aliases,
    )(*ex.arrs)


def _carrier_call(body, ex, first_last, name, out_shape, grid, in_specs, out_specs, scratch_shapes, args):
    n_in, n_out, n_scr, n = len(in_specs), len(out_shape), len(scratch_shapes), ex.n

    def full(*refs):
        a, ci = refs[:n_in], refs[n_in:n_in + n]
        o = refs[n_in + n:n_in + n + n_out]
        co = refs[n_in + n + n_out:n_in + 2 * n + n_out]
        scr = refs[n_in + 2 * n + n_out:n_in + 2 * n + n_out + n_scr]
        sems = refs[n_in + 2 * n + n_out + n_scr:]
        first, last = first_last()
        if n:
            @pl.when(first)
            def _():
                ex.start(ci, co, sems)

        body(*a, *o, *scr)
        if n:
            @pl.when(last)
            def _():
                ex.wait(ci, co, sems)

    outs = pl.pallas_call(
        full, name=name, out_shape=tuple(out_shape) + ex.out_shape, grid=grid,
        in_specs=list(in_specs) + ex.specs, out_specs=tuple(out_specs) + tuple(ex.specs),
        scratch_shapes=list(scratch_shapes) + ex.scratch,
        input_output_aliases={n_in + t: n_out + u for t, u in ex.aliases.items()},
        compiler_params=_params(("arbitrary",) * len(grid)),
    )(*args, *ex.arrs)
    return outs[:n_out], outs[n_out:]


def _matmul_tn(a, b, nb, name, scale=1.0, out_dtype=BF16, ts=2048, ex=None, a_index=None):
    ba, S, K = a.shape
    bb, _, N = b.shape
    ts = min(ts, S)
    tn = N if N <= 1024 else next(c for c in (1280, 1024, 768, 512, 256) if N % c == 0)
    assert S % ts == 0
    ns = S // ts

    def body(a_ref, b_ref, o_ref, acc):
        s = pl.program_id(2)

        @pl.when(s == 0)
        def _():
            acc[...] = jnp.zeros_like(acc)

        acc[...] += _dot_tn(_bf(a_ref[...]), _bf(b_ref[...]))

        @pl.when(s == ns - 1)
        def _():
            o_ref[...] = (acc[...] * scale).astype(out_dtype)

    a_map = (lambda i, j, s: (a_index, s, 0)) if a_index is not None else (
        (lambda i, j, s: (i, s, 0)) if ba > 1 else (lambda i, j, s: (0, s, 0)))
    in_specs = [pl.BlockSpec((None, ts, K), a_map),
                pl.BlockSpec((None, ts, tn), (lambda i, j, s: (i, s, j)) if bb > 1 else (lambda i, j, s: (0, s, j)))]
    out_spec = pl.BlockSpec((None, K, tn), lambda i, j, s: (i, 0, j))
    out_shape = jax.ShapeDtypeStruct((nb, K, N), out_dtype)
    scratch = [pltpu.VMEM((K, tn), F32)]
    grid = (nb, N // tn, ns)
    if ex is not None:
        (out,), got = _carrier_call(body, ex, _grid_ends(*grid), name, out_shape=(out_shape,), grid=grid,
                                    in_specs=in_specs, out_specs=(out_spec,), scratch_shapes=scratch, args=(a, b))
        return out, got
    return pl.pallas_call(
        body, name=name, out_shape=out_shape, grid=grid, in_specs=in_specs, out_specs=out_spec,
        scratch_shapes=scratch, compiler_params=_params(("parallel", "parallel", "arbitrary")),
    )(a, b)


def _ffn_specs(tm):
    wg = pl.BlockSpec((None, D_MODEL, FF_SHARD), lambda i, j: (j, 0, 0))
    wu = pl.BlockSpec((None, D_MODEL, FF_SHARD), lambda i, j: (j + FF_BLOCKS, 0, 0))
    wo = pl.BlockSpec((2, FF_SHARD // 2, D_MODEL), lambda i, j: (j, 0, 0))
    row = pl.BlockSpec((tm, D_MODEL), lambda i, j: (i, 0))
    vec = pl.BlockSpec((1, D_MODEL), lambda i, j: (0, 0))
    hid = pl.BlockSpec((None, tm, FF_SHARD), lambda i, j: (j, i, 0))
    return wg, wu, wo, row, vec, hid


def _grid_ends(*grid):
    def first_last():
        first, last = None, None
        for d, n in enumerate(grid):
            i = pl.program_id(d)
            first = (i == 0) if first is None else first & (i == 0)
            last = (i == n - 1) if last is None else last & (i == n - 1)
        return first, last
    return first_last


def _grid2_ends(ni, nj):
    return _grid_ends(ni, nj)


def _grid1_ends(ni):
    return _grid_ends(ni)


def _ffn_fwd(x, gamma, wi, wo, tm, name, ex=NO_EXCHANGE):
    S = x.shape[0]
    wg_s, wu_s, wo_s, row, vec, hid = _ffn_specs(tm)

    def body(x_ref, g_ref, wg_ref, wu_ref, wo_ref, y_ref, xn_ref, gs_ref, us_ref, xn_s, acc):
        j = pl.program_id(1)

        @pl.when(j == 0)
        def _():
            xn = _bf(_rms_fwd(x_ref[...], g_ref[...])[0])
            xn_s[...] = xn
            xn_ref[...] = xn
            acc[...] = jnp.zeros_like(acc)

        wo2 = wo_ref[...].reshape(FF_SHARD, D_MODEL)
        for r in range(tm // FFN_PART_ROWS):
            rows = slice(r * FFN_PART_ROWS, (r + 1) * FFN_PART_ROWS)
            xn = xn_s[rows, :]
            g = _dot(xn, wg_ref[...])
            u = _dot(xn, wu_ref[...])
            gs_ref[rows, :] = _bf(g)
            us_ref[rows, :] = _bf(u)
            a = g * _sig(g) * u
            acc[rows, :] += _dot(_bf(a), wo2)

        @pl.when(j == FF_BLOCKS - 1)
        def _():
            y_ref[...] = x_ref[...] + 0.5 * acc[...]

    return _carrier_call(
        body, ex, _grid2_ends(S // tm, FF_BLOCKS), name,
        out_shape=(jax.ShapeDtypeStruct((S, D_MODEL), F32), jax.ShapeDtypeStruct((S, D_MODEL), BF16),
                   jax.ShapeDtypeStruct((FF_BLOCKS, S, FF_SHARD), BF16),
                   jax.ShapeDtypeStruct((FF_BLOCKS, S, FF_SHARD), BF16)),
        grid=(S // tm, FF_BLOCKS),
        in_specs=[row, vec, wg_s, wu_s, wo_s],
        out_specs=(row, row, hid, hid),
        scratch_shapes=[pltpu.VMEM((tm, D_MODEL), BF16), pltpu.VMEM((tm, D_MODEL), F32)],
        args=(x, gamma, wi, wi, wo))


def _ffn_bwd(dy, x, gamma, gs, us, wi, wo, tm, name, ex=NO_EXCHANGE):
    S = x.shape[0]
    wg_s, wu_s, wo_s, row, vec, hid = _ffn_specs(tm)
    dgu_s = pl.BlockSpec((2, None, tm, FF_SHARD), lambda i, j: (0, j, i, 0))

    def body(dy_ref, x_hbm, g_ref, gs_ref, us_ref, wg_ref, wu_ref, wo_ref,
             dx_ref, a_ref, dgu_ref, dgam_ref, dyh_s, x_buf, x_sem):
        i, j = pl.program_id(0), pl.program_id(1)
        acc = dx_ref
        x_copy = pltpu.make_async_copy(x_hbm.at[pl.ds(pl.multiple_of(i * tm, tm), tm), :], x_buf, x_sem)

        @pl.when(j == 0)
        def _():
            x_copy.start()
            dyh_s[...] = _bf(0.5 * dy_ref[...])
            acc[...] = jnp.zeros_like(acc)

        @pl.when((i == 0) & (j == 0))
        def _():
            dgam_ref[...] = jnp.zeros_like(dgam_ref)

        wo2 = wo_ref[...].reshape(FF_SHARD, D_MODEL)
        for r in range(tm // FFN_PART_ROWS):
            rows = slice(r * FFN_PART_ROWS, (r + 1) * FFN_PART_ROWS)
            da = _dot_nt(dyh_s[rows, :], wo2)
            g = gs_ref[rows, :].astype(F32)
            u = us_ref[rows, :].astype(F32)
            sg = _sig(g)
            sl = g * sg
            a_ref[rows, :] = _bf(sl * u)
            dg = _bf(da * u * (sg * (1.0 + g * (1.0 - sg))))
            du = _bf(da * sl)
            dgu_ref[0, rows, :] = dg
            dgu_ref[1, rows, :] = du
            acc[rows, :] += _dot_nt(dg, wg_ref[...]) + _dot_nt(du, wu_ref[...])

        @pl.when(j == FF_BLOCKS - 1)
        def _():
            x_copy.wait()
            dx, dgam = _rms_bwd(acc[...], x_buf[...], g_ref[...])
            dx_ref[...] = dy_ref[...] + dx
            dgam_ref[...] += jnp.sum(dgam, axis=0, keepdims=True)

    return _carrier_call(
        body, ex, _grid2_ends(S // tm, FF_BLOCKS), name,
        out_shape=(jax.ShapeDtypeStruct((S, D_MODEL), F32),
                   jax.ShapeDtypeStruct((FF_BLOCKS, S, FF_SHARD), BF16),
                   jax.ShapeDtypeStruct((2, FF_BLOCKS, S, FF_SHARD), BF16),
                   jax.ShapeDtypeStruct((1, D_MODEL), F32)),
        grid=(S // tm, FF_BLOCKS),
        in_specs=[row, pl.BlockSpec(memory_space=pl.ANY), vec, hid, hid, wg_s, wu_s, wo_s],
        out_specs=(row, hid, dgu_s, vec),
        scratch_shapes=[pltpu.VMEM((tm, D_MODEL), BF16), pltpu.VMEM((tm, D_MODEL), F32),
                        pltpu.SemaphoreType.DMA(())],
        args=(dy, x, gamma, gs, us, wi, wi, wo))


def _mixin_fwd(h, gamma, w_qc, tm, name):
    S = h.shape[0]
    nqkv = Q_DIM + 2 * KV_DIM

    def body(h_ref, g_ref, w_ref, un_ref, qkv_ref, cvg_ref):
        un = _bf(_rms_fwd(h_ref[...], g_ref[...])[0])
        un_ref[...] = un
        z = _dot(un, w_ref[:, :QC_DIM])
        qkv_ref[...] = z[:, :nqkv]
        cvg_ref[...] = z[:, nqkv:]

    row = lambda w: pl.BlockSpec((tm, w), lambda i: (i, 0))
    return pl.pallas_call(
        body, name=name,
        out_shape=(jax.ShapeDtypeStruct((S, D_MODEL), BF16), jax.ShapeDtypeStruct((S, nqkv), F32),
                   jax.ShapeDtypeStruct((S, 2 * CONV_DIM), F32)),
        grid=(S // tm,),
        in_specs=[row(D_MODEL), pl.BlockSpec((1, D_MODEL), lambda i: (0, 0)),
                  pl.BlockSpec((D_MODEL, IN_DIM), lambda i: (0, 0))],
        out_specs=(row(D_MODEL), row(nqkv), row(2 * CONV_DIM)),
        compiler_params=_params(("parallel",)),
    )(h, gamma, w_qc)


def _mixin_bwd(dh2, h1, gamma, dq, dkv, dcvg, dgpre, w_in, tm, name):
    S = h1.shape[0]
    nqkv = Q_DIM + 2 * KV_DIM
    n_in = QC_DIM + 2 * D_MODEL

    def body(dh2_ref, h1_ref, g_ref, dq_ref, dkv_ref, dcvg_ref, dgp_ref, win_ref,
             dh1_ref, dgam_ref, dz_ref):
        @pl.when(pl.program_id(0) == 0)
        def _():
            dgam_ref[...] = jnp.zeros_like(dgam_ref)

        wqc = win_ref[:, :QC_DIM]
        dq, dkv, dcvg = _bf(dq_ref[...]), _bf(dkv_ref[...]), _bf(dcvg_ref[...])
        dz_ref[:, :Q_DIM] = dq
        dz_ref[:, Q_DIM:nqkv] = dkv
        dz_ref[:, nqkv:QC_DIM] = dcvg
        dz_ref[:, QC_DIM:] = dgp_ref[...]
        dun = _dot_nt(dq, wqc[:, :Q_DIM])
        dun += _dot_nt(dkv, wqc[:, Q_DIM:nqkv])
        dun += _dot_nt(dcvg, wqc[:, nqkv:])
        dun += _dot_nt(dgp_ref[...], win_ref[:, QC_DIM:])
        dx, dgam = _rms_bwd(dun, h1_ref[...], g_ref[...])
        dh1_ref[...] = dh2_ref[...] + dx
        dgam_ref[...] += jnp.sum(dgam, axis=0, keepdims=True)

    row = lambda w: pl.BlockSpec((tm, w), lambda i: (i, 0))
    vec = pl.BlockSpec((1, D_MODEL), lambda i: (0, 0))
    return pl.pallas_call(
        body, name=name,
        out_shape=(jax.ShapeDtypeStruct((S, D_MODEL), F32), jax.ShapeDtypeStruct((1, D_MODEL), F32),
                   jax.ShapeDtypeStruct((S, n_in), BF16)),
        grid=(S // tm,),
        in_specs=[row(D_MODEL), row(D_MODEL), vec, row(Q_DIM), row(2 * KV_DIM), row(2 * CONV_DIM),
                  row(2 * D_MODEL), pl.BlockSpec((D_MODEL, IN_DIM), lambda i: (0, 0))],
        out_specs=(row(D_MODEL), vec, row(n_in)),
        compiler_params=_params(("arbitrary",)),
    )(dh2, h1, gamma, dq, dkv, dcvg, dgpre, w_in)


TQ = 512
QB = TQ // BLOCK


def _attn_in_specs(S):
    nkb = S // BLOCK
    return [
        pl.BlockSpec((TQ, Q_DIM), lambda i: (i, 0)),
        pl.BlockSpec((BLOCK, 2 * KV_DIM), lambda i: (jnp.maximum(i * QB - 1, 0), Q_DIM // (2 * KV_DIM))),
        pl.BlockSpec((TQ, 2 * KV_DIM), lambda i: (i, Q_DIM // (2 * KV_DIM))),
        pl.BlockSpec((BLOCK, 2 * KV_DIM), lambda i: (jnp.minimum(i * QB + QB, nkb - 1), Q_DIM // (2 * KV_DIM))),
        pl.BlockSpec((1, 128), lambda i: (0, 0)),
        pl.BlockSpec((1, KV_DIM), lambda i: (0, 0)),
        pl.BlockSpec((N_HEADS, 128), lambda i: (0, 0)),
        pl.BlockSpec((N_HEADS, BLOCK, WIN), lambda i: (0, 0, 0)),
    ]


GROUP_ROWS = 4 * BLOCK


def _half_rstd(x, low):
    x2 = x * x
    z = jnp.zeros_like(x2)
    r0 = lax.rsqrt(jnp.sum(jnp.where(low, x2, z), axis=-1, keepdims=True) * (1.0 / HEAD_DIM) + 1e-6)
    r1 = lax.rsqrt(jnp.sum(jnp.where(low, z, x2), axis=-1, keepdims=True) * (1.0 / HEAD_DIM) + 1e-6)
    return jnp.where(low, r0, r1)


def _kv_windows(kvp_ref, kvc_ref, kvn_ref, kg_ref, low):
    kv = jnp.concatenate([kvp_ref[...], kvc_ref[...], kvn_ref[...]], axis=0)
    k, v = kv[:, :KV_DIM], kv[:, KV_DIM:]
    kn = k * _half_rstd(k, low) * kg_ref[...]
    kr, vr = pltpu.roll(kn, HEAD_DIM, 1), pltpu.roll(v, HEAD_DIM, 1)
    kdup = [_bf(jnp.where(low, kn, kr)), _bf(jnp.where(low, kr, kn))]
    vdup = [_bf(jnp.where(low, v, vr)), _bf(jnp.where(low, vr, v))]
    return kdup, vdup


def _stack_heads(x_ref, t, kh, low):
    rows = slice(t * BLOCK, (t + 1) * BLOCK)
    xa = x_ref[rows, 256 * kh:256 * kh + 128]
    xb = x_ref[rows, 256 * kh + 128:256 * kh + 256]
    z = jnp.zeros_like(xa)
    return jnp.concatenate([jnp.where(low, xa, z), jnp.where(low, z, xa),
                            jnp.where(low, xb, z), jnp.where(low, z, xb)], axis=0)


def _stacked_q(q_ref, qg_ref, t, kh, low):
    qraw = _stack_heads(q_ref, t, kh, low)
    rq = lax.rsqrt(jnp.sum(qraw * qraw, axis=-1, keepdims=True) * (1.0 / HEAD_DIM) + 1e-6)
    return qraw, rq, _bf(qraw * rq * (qg_ref[...] * (HEAD_DIM ** -0.5)))


def _unstack_heads(ov, low):
    return (jnp.where(low, ov[0:128], ov[128:256]), jnp.where(low, ov[256:384], ov[384:512]))


def _edge_bias(i, t, S):
    kpos = i * TQ + (t - 1) * BLOCK + lax.broadcasted_iota(jnp.int32, (1, WIN), 1)
    return jnp.where((kpos < 0) | (kpos >= S), NEG_INF, 0.0)


def _group_exp(lhs, kw, bias_ref, sink_ref, kh, edge):
    s = _dot_nt(lhs, kw) + bias_ref[4 * kh:4 * kh + 4].reshape(GROUP_ROWS, WIN)
    if edge is not None:
        s = s + edge
    sk = jnp.concatenate(
        [jnp.broadcast_to(sink_ref[4 * kh + r:4 * kh + r + 1, 0:1], (BLOCK, 1)) for r in range(4)], axis=0)
    m = jnp.maximum(jnp.max(s, axis=-1, keepdims=True), sk)
    return jnp.exp(s - m), jnp.exp(sk - m)


def _attn_fwd(qkv, qg, kg, sinkb, bias, name, ex=NO_EXCHANGE):
    S = qkv.shape[0]

    def body(q_ref, kvp_ref, kvc_ref, kvn_ref, qg_ref, kg_ref, sink_ref, bias_ref, o_ref):
        i = pl.program_id(0)
        low = lax.broadcasted_iota(jnp.int32, (1, 128), 1) < HEAD_DIM
        ones = jnp.ones((WIN, 128), BF16)
        kdup, vdup = _kv_windows(kvp_ref, kvc_ref, kvn_ref, kg_ref, low)
        for t in range(QB):
            edge = _edge_bias(i, t, S) if t in (0, QB - 1) else None
            rows = slice(t * BLOCK, (t + 1) * BLOCK)
            for kh in range(KV_HEADS):
                _, _, lhs = _stacked_q(q_ref, qg_ref, t, kh, low)
                kw = kdup[kh][t * BLOCK:t * BLOCK + WIN]
                vw = vdup[kh][t * BLOCK:t * BLOCK + WIN]
                e, es = _group_exp(lhs, kw, bias_ref, sink_ref, kh, edge)
                eb = _bf(e)
                ov = _dot(eb, vw) * (1.0 / (_dot(eb, ones) + es))
                oa, ob = _unstack_heads(ov, low)
                o_ref[rows, 256 * kh:256 * kh + 128] = _bf(oa)
                o_ref[rows, 256 * kh + 128:256 * kh + 256] = _bf(ob)

    return _carrier_call(
        body, ex, _grid1_ends(S // TQ), name, out_shape=(jax.ShapeDtypeStruct((S, Q_DIM), BF16),),
        grid=(S // TQ,), in_specs=_attn_in_specs(S),
        out_specs=(pl.BlockSpec((TQ, Q_DIM), lambda i: (i, 0)),), scratch_shapes=[],
        args=(qkv, qkv, qkv, qkv, qg, kg, sinkb, bias))


def _attn_bwd(do, qkv, qg, kg, sinkb, bias, dbias_in, name, ex=NO_EXCHANGE):
    S = qkv.shape[0]
    nkb = S // BLOCK
    nsteps = S // TQ

    def body(do_ref, q_ref, kvp_ref, kvc_ref, kvn_ref, qg_ref, kg_ref, sink_ref, bias_ref, dbin_ref,
             dq_ref, dkp_ref, dvp_ref, dbias_ref, dsink_ref, dqg_ref, dqg_s):
        i = pl.program_id(0)

        @pl.when(i == 0)
        def _():
            dbias_ref[...] = dbin_ref[...]
            dsink_ref[...] = jnp.zeros_like(dsink_ref)
            dqg_s[...] = jnp.zeros_like(dqg_s)

        low = lax.broadcasted_iota(jnp.int32, (1, 128), 1) < HEAD_DIM
        own = ((lax.broadcasted_iota(jnp.int32, (GROUP_ROWS, 128), 1) >> 6) & 1) == (
            (lax.broadcasted_iota(jnp.int32, (GROUP_ROWS, 128), 0) >> 7) & 1)
        gq = qg_ref[...] * (HEAD_DIM ** -0.5)
        ones = jnp.ones((WIN, 128), BF16)
        kdup, vdup = _kv_windows(kvp_ref, kvc_ref, kvn_ref, kg_ref, low)
        for t in range(QB):
            edge = _edge_bias(i, t, S) if t in (0, QB - 1) else None
            rows = slice(t * BLOCK, (t + 1) * BLOCK)
            dk_dup, dv_dup = [], []
            for kh in range(KV_HEADS):
                qraw, rq, lhs = _stacked_q(q_ref, qg_ref, t, kh, low)
                dos = _bf(_stack_heads(do_ref, t, kh, low))
                kw = kdup[kh][t * BLOCK:t * BLOCK + WIN]
                vw = vdup[kh][t * BLOCK:t * BLOCK + WIN]
                e, es = _group_exp(lhs, kw, bias_ref, sink_ref, kh, edge)
                inv = 1.0 / (_dot(_bf(e), ones) + es)
                pr = e * jnp.concatenate([inv] * (WIN // 128), axis=1)
                dpr = _dot_nt(dos, vw)
                delta = jnp.sum(pr * dpr, axis=-1, keepdims=True)
                ds = pr * (dpr - delta)
                dbias_ref[4 * kh:4 * kh + 4] += ds.reshape(4, BLOCK, WIN)
                dsk = es * inv[:, 0:1] * delta
                for r in range(4):
                    dsink_ref[4 * kh + r:4 * kh + r + 1, :] -= jnp.broadcast_to(
                        jnp.sum(dsk[r * BLOCK:(r + 1) * BLOCK], axis=0, keepdims=True), (1, 128))
                dsb = _bf(ds)
                dqs = _dot(dsb, kw)
                qhat = qraw * rq
                dxhat = jnp.where(own, dqs, 0.0) * gq
                dq_st = rq * (dxhat - qhat * (jnp.sum(dxhat * qhat, axis=-1, keepdims=True) * (1.0 / HEAD_DIM)))
                dq_ref[rows, 256 * kh:256 * kh + 128] = dq_st[0:128] + dq_st[128:256]
                dq_ref[rows, 256 * kh + 128:256 * kh + 256] = dq_st[256:384] + dq_st[384:512]
                dqg_s[...] += jnp.sum((dqs * qhat).reshape(GROUP_ROWS // 8, 8, 128), axis=0)
                dkx = _dot_tn(dsb, lhs)
                dvx = _dot_tn(_bf(pr), dos)
                dk_dup.append(dkx + pltpu.roll(dkx, HEAD_DIM, 1))
                dv_dup.append(dvx + pltpu.roll(dvx, HEAD_DIM, 1))
            dkp_ref[t] = jnp.where(low, dk_dup[0], dk_dup[1])
            dvp_ref[t] = jnp.where(low, dv_dup[0], dv_dup[1])

        @pl.when(i == nsteps - 1)
        def _():
            acc = dqg_s[...] * (HEAD_DIM ** -0.5)
            acc = acc + pltpu.roll(acc, HEAD_DIM, 1)
            dqg_ref[...] = jnp.broadcast_to(jnp.sum(acc, axis=0, keepdims=True), (8, 128))

    const2 = lambda shape: pl.BlockSpec(shape, lambda i: (0,) * len(shape))
    part = pl.BlockSpec((QB, WIN, KV_DIM), lambda i: (i, 0, 0))
    return _carrier_call(
        body, ex, _grid1_ends(nsteps), name,
        out_shape=(jax.ShapeDtypeStruct((S, Q_DIM), F32), jax.ShapeDtypeStruct((nkb, WIN, KV_DIM), F32),
                   jax.ShapeDtypeStruct((nkb, WIN, KV_DIM), F32),
                   jax.ShapeDtypeStruct((N_HEADS, BLOCK, WIN), F32), jax.ShapeDtypeStruct((N_HEADS, 128), F32),
                   jax.ShapeDtypeStruct((8, 128), F32)),
        grid=(nsteps,),
        in_specs=[pl.BlockSpec((TQ, Q_DIM), lambda i: (i, 0))] + _attn_in_specs(S)
        + [const2((N_HEADS, BLOCK, WIN))],
        out_specs=(pl.BlockSpec((TQ, Q_DIM), lambda i: (i, 0)), part, part,
                   const2((N_HEADS, BLOCK, WIN)), const2((N_HEADS, 128)), const2((8, 128))),
        scratch_shapes=[pltpu.VMEM((8, 128), F32)],
        args=(do, qkv, qkv, qkv, qkv, qg, kg, sinkb, bias, dbias_in))


def _kv_fold(dkp, dvp, qkv, kg, name):
    nkb = dkp.shape[0]
    S = nkb * BLOCK
    nsteps = S // TQ

    def body(kp_p, kp_c, kp_n, vp_p, vp_c, vp_n, kv_ref, kg_ref, dkv_ref, dkg_ref, dkg_s):
        i = pl.program_id(0)

        @pl.when(i == 0)
        def _():
            dkg_s[...] = jnp.zeros_like(dkg_s)

        def fold(p_ref, c_ref, n_ref):
            blocks = []
            for t in range(QB):
                acc = c_ref[t, BLOCK:2 * BLOCK, :]
                if t > 0:
                    acc = acc + c_ref[t - 1, 2 * BLOCK:, :]
                else:
                    acc = acc + jnp.where(i > 0, p_ref[0, 2 * BLOCK:, :], 0.0)
                if t < QB - 1:
                    acc = acc + c_ref[t + 1, :BLOCK, :]
                else:
                    acc = acc + jnp.where(i < nsteps - 1, n_ref[0, :BLOCK, :], 0.0)
                blocks.append(acc)
            return jnp.concatenate(blocks, axis=0)

        dkn = fold(kp_p, kp_c, kp_n)
        dv = fold(vp_p, vp_c, vp_n)
        k = kv_ref[:, :KV_DIM]
        low = lax.broadcasted_iota(jnp.int32, (1, 128), 1) < HEAD_DIM
        rk = _half_rstd(k, low)
        khat = k * rk
        dxhat = dkn * kg_ref[...]
        prod = dxhat * khat
        z = jnp.zeros_like(prod)
        mean = jnp.where(low, jnp.sum(jnp.where(low, prod, z), axis=-1, keepdims=True),
                         jnp.sum(jnp.where(low, z, prod), axis=-1, keepdims=True)) * (1.0 / HEAD_DIM)
        dkv_ref[:, :KV_DIM] = rk * (dxhat - khat * mean)
        dkv_ref[:, KV_DIM:] = dv
        dkg_s[...] += jnp.sum((dkn * khat).reshape(TQ // 8, 8, KV_DIM), axis=0)

        @pl.when(i == nsteps - 1)
        def _():
            acc = dkg_s[...] + pltpu.roll(dkg_s[...], HEAD_DIM, 1)
            dkg_ref[...] = jnp.broadcast_to(jnp.sum(acc, axis=0, keepdims=True), (8, 128))

    prev = pl.BlockSpec((1, WIN, KV_DIM), lambda i: (jnp.maximum(i * QB - 1, 0), 0, 0))
    cur = pl.BlockSpec((QB, WIN, KV_DIM), lambda i: (i, 0, 0))
    nxt = pl.BlockSpec((1, WIN, KV_DIM), lambda i: (jnp.minimum(i * QB + QB, nkb - 1), 0, 0))
    return pl.pallas_call(
        body, name=name,
        out_shape=(jax.ShapeDtypeStruct((S, 2 * KV_DIM), F32), jax.ShapeDtypeStruct((8, 128), F32)),
        grid=(nsteps,),
        in_specs=[prev, cur, nxt, prev, cur, nxt,
                  pl.BlockSpec((TQ, 2 * KV_DIM), lambda i: (i, Q_DIM // (2 * KV_DIM))),
                  pl.BlockSpec((1, KV_DIM), lambda i: (0, 0))],
        out_specs=(pl.BlockSpec((TQ, 2 * KV_DIM), lambda i: (i, 0)), pl.BlockSpec((8, 128), lambda i: (0, 0))),
        scratch_shapes=[pltpu.VMEM((8, KV_DIM), F32)],
        compiler_params=_params(("arbitrary",)),
    )(dkp, dkp, dkp, dvp, dvp, dvp, qkv, kg)


BIAS_COLS = BLOCK * WIN
BIAS_CHUNK = 6144


def _bias_table(rel_bias_t, onehot, band):
    def body(rb_ref, oh_ref, band_ref, o_ref):
        o_ref[...] = _dot(rb_ref[...], oh_ref[...], HI) + band_ref[...]

    return pl.pallas_call(
        body, name="bias_table", out_shape=jax.ShapeDtypeStruct((N_HEADS, BIAS_COLS), F32),
        grid=(BIAS_COLS // BIAS_CHUNK,),
        in_specs=[pl.BlockSpec((N_HEADS, NUM_BUCKETS), lambda i: (0, 0)),
                  pl.BlockSpec((NUM_BUCKETS, BIAS_CHUNK), lambda i: (0, i)),
                  pl.BlockSpec((1, BIAS_CHUNK), lambda i: (0, i))],
        out_specs=pl.BlockSpec((N_HEADS, BIAS_CHUNK), lambda i: (0, i)),
        compiler_params=_params(("parallel",)),
    )(rel_bias_t, onehot, band)


def _bias_grad(dbias, onehot):
    def body(db_ref, oh_ref, o_ref):
        @pl.when(pl.program_id(0) == 0)
        def _():
            o_ref[...] = jnp.zeros_like(o_ref)

        o_ref[...] += lax.dot_general(db_ref[...], oh_ref[...], (((1,), (1,)), ((), ())),
                                      preferred_element_type=F32, precision=HI)

    return pl.pallas_call(
        body, name="bias_grad", out_shape=jax.ShapeDtypeStruct((N_HEADS, NUM_BUCKETS), F32),
        grid=(BIAS_COLS // BIAS_CHUNK,),
        in_specs=[pl.BlockSpec((N_HEADS, BIAS_CHUNK), lambda i: (0, i)),
                  pl.BlockSpec((NUM_BUCKETS, BIAS_CHUNK), lambda i: (0, i))],
        out_specs=pl.BlockSpec((N_HEADS, NUM_BUCKETS), lambda i: (0, 0)),
        compiler_params=_params(("arbitrary",)),
    )(dbias, onehot)


def _bucket_onehot():
    half = NUM_BUCKETS // 2
    max_exact = half // 2
    rel = jnp.arange(WIN)[None, :] - BLOCK - jnp.arange(BLOCK)[:, None]
    n = jnp.abs(rel)
    ret = jnp.where(rel > 0, half, 0)
    nf = jnp.maximum(n, 1).astype(F32)
    large = max_exact + (jnp.log(nf / max_exact) / np.log(MAX_DISTANCE / max_exact)
                         * (half - max_exact)).astype(jnp.int32)
    large = jnp.minimum(large, half - 1)
    bucket = (ret + jnp.where(n < max_exact, n, large)).reshape(1, BIAS_COLS)
    band = jnp.where(n <= BLOCK, 0.0, NEG_INF).astype(F32).reshape(1, BIAS_COLS)
    return (bucket == jnp.arange(NUM_BUCKETS)[:, None]).astype(F32), band


def _halo_specs(tm, width, S):
    r = tm // HALO
    last = S // HALO - 1
    return [pl.BlockSpec((HALO, width), lambda i: (jnp.maximum(i * r - 1, 0), 0)),
            pl.BlockSpec((tm, width), lambda i: (i, 0)),
            pl.BlockSpec((HALO, width), lambda i: (jnp.minimum(i * r + r, last), 0))]


def _with_halo(p_ref, c_ref, n_ref):
    return jnp.concatenate([p_ref[...], c_ref[...], n_ref[...]], axis=0)


def _row_valid(i, tm, S):
    g = i * tm - HALO + lax.broadcasted_iota(jnp.int32, (tm + 2 * HALO, 1), 0)
    return (g >= 0) & (g < S)


def _shifted(x):
    n = x.shape[0]
    return [x if b == 0 else pltpu.roll(x, n - b, 0) for b in range(8)]


def _tap(sh, off, tm):
    a, b = off // 8, off % 8
    return sh[b][8 * a:8 * a + tm]


def _conv_fwd(cvg, cw, cb, lg, lb, tm, name, ex=NO_EXCHANGE):
    S = cvg.shape[0]

    def body(p_ref, c_ref, n_ref, cw_ref, cb_ref, lg_ref, lb_ref, act_ref, yc_ref):
        i = pl.program_id(0)
        z = _with_halo(p_ref, c_ref, n_ref)
        glu = jnp.where(_row_valid(i, tm, S), z[:, :CONV_DIM] * _sig(z[:, CONV_DIM:]), 0.0)
        sh = _shifted(glu)
        y = jnp.zeros((tm, CONV_DIM), F32) + cb_ref[...]
        for w in range(CONV_WIDTH):
            y = y + _tap(sh, w + 1, tm) * cw_ref[w:w + 1, :]
        yc_ref[...] = y
        mu = jnp.mean(y, axis=-1, keepdims=True)
        yc = y - mu
        rstd = lax.rsqrt(jnp.mean(yc * yc, axis=-1, keepdims=True) + 1e-5)
        ln = yc * rstd * lg_ref[...] + lb_ref[...]
        act_ref[...] = _bf(ln * _sig(ln))

    vec = pl.BlockSpec((1, CONV_DIM), lambda i: (0, 0))
    row = pl.BlockSpec((tm, CONV_DIM), lambda i: (i, 0))
    return _carrier_call(
        body, ex, _grid_ends(S // tm), name,
        out_shape=(jax.ShapeDtypeStruct((S, CONV_DIM), BF16), jax.ShapeDtypeStruct((S, CONV_DIM), F32)),
        grid=(S // tm,),
        in_specs=_halo_specs(tm, 2 * CONV_DIM, S) + [pl.BlockSpec((32, CONV_DIM), lambda i: (0, 0)), vec, vec, vec],
        out_specs=(row, row), scratch_shapes=[], args=(cvg, cvg, cvg, cw, cb, lg, lb))


def _conv_bwd(dact, yconv, cvg, cw, lg, lb, tm, name, ex=NO_EXCHANGE):
    S = cvg.shape[0]
    nsteps = S // tm

    def body(dp, dc, dn, yp, yc_, yn, zp, zc, zn, cw_ref, lg_ref, lb_ref,
             dz_ref, dcw_ref, dvec_ref, dcw_s, dvec_s, shg_s, shd_s):
        i = pl.program_id(0)

        @pl.when(i == 0)
        def _():
            dcw_s[...] = jnp.zeros_like(dcw_s)
            dvec_s[...] = jnp.zeros_like(dvec_s)

        valid = _row_valid(i, tm, S)
        own = (lax.broadcasted_iota(jnp.int32, (tm + 2 * HALO, 1), 0) >= HALO) & (
            lax.broadcasted_iota(jnp.int32, (tm + 2 * HALO, 1), 0) < HALO + tm)
        y = _with_halo(yp, yc_, yn)
        dact_ = _with_halo(dp, dc, dn)
        mu = jnp.mean(y, axis=-1, keepdims=True)
        ycen = y - mu
        rstd = lax.rsqrt(jnp.mean(ycen * ycen, axis=-1, keepdims=True) + 1e-5)
        yhat = ycen * rstd
        ln = yhat * lg_ref[...] + lb_ref[...]
        sg = _sig(ln)
        dln = dact_ * (sg * (1.0 + ln * (1.0 - sg)))
        dyhat = dln * lg_ref[...]
        dy = rstd * (dyhat - jnp.mean(dyhat, axis=-1, keepdims=True)
                     - yhat * jnp.mean(dyhat * yhat, axis=-1, keepdims=True))
        dy = jnp.where(valid, dy, 0.0)
        dln_own = jnp.where(own, dln, 0.0)
        nr = (tm + 2 * HALO) // 8
        dvec_s[0] += jnp.sum(jnp.where(own, dy, 0.0).reshape(nr, 8, CONV_DIM), axis=0)
        dvec_s[1] += jnp.sum((dln_own * yhat).reshape(nr, 8, CONV_DIM), axis=0)
        dvec_s[2] += jnp.sum(dln_own.reshape(nr, 8, CONV_DIM), axis=0)
        z = _with_halo(zp, zc, zn)
        glu = jnp.where(valid, z[:, :CONV_DIM] * _sig(z[:, CONV_DIM:]), 0.0)
        for b, (g_b, d_b) in enumerate(zip(_shifted(glu), _shifted(dy))):
            shg_s[b] = g_b
            shd_s[b] = d_b
        for cb in range(CONV_DIM // 128):
            lanes = slice(128 * cb, 128 * (cb + 1))
            for rb in range(tm // CROWS):
                r0 = rb * CROWS
                dy_own = shd_s[0, HALO + r0:HALO + r0 + CROWS, lanes]
                dglu = jnp.zeros((CROWS, 128), F32)
                for w in range(CONV_WIDTH):
                    a, b = divmod(CONV_WIDTH - w, 8)
                    dglu = dglu + shd_s[b, 8 * a + r0:8 * a + r0 + CROWS, lanes] * cw_ref[w:w + 1, lanes]
                    a, b = divmod(w + 1, 8)
                    prod = dy_own * shg_s[b, 8 * a + r0:8 * a + r0 + CROWS, lanes]
                    dcw_s[w, :, lanes] += jnp.sum(prod.reshape(CROWS // 8, 8, 128), axis=0)
                cv = zc[r0:r0 + CROWS, lanes]
                sg_o = _sig(zc[r0:r0 + CROWS, CONV_DIM + 128 * cb:CONV_DIM + 128 * (cb + 1)])
                dz_ref[r0:r0 + CROWS, lanes] = dglu * sg_o
                dz_ref[r0:r0 + CROWS, CONV_DIM + 128 * cb:CONV_DIM + 128 * (cb + 1)] = (
                    dglu * cv * sg_o * (1.0 - sg_o))

        @pl.when(i == nsteps - 1)
        def _():
            dcw_ref[...] = jnp.sum(dcw_s[...], axis=1)
            dvec_ref[...] = jnp.sum(dvec_s[...], axis=1)

    vec = pl.BlockSpec((1, CONV_DIM), lambda i: (0, 0))
    return _carrier_call(
        body, ex, _grid_ends(nsteps), name,
        out_shape=(jax.ShapeDtypeStruct((S, 2 * CONV_DIM), F32), jax.ShapeDtypeStruct((32, CONV_DIM), F32),
                   jax.ShapeDtypeStruct((8, CONV_DIM), F32)),
        grid=(nsteps,),
        in_specs=_halo_specs(tm, CONV_DIM, S) + _halo_specs(tm, CONV_DIM, S) + _halo_specs(tm, 2 * CONV_DIM, S)
        + [pl.BlockSpec((32, CONV_DIM), lambda i: (0, 0)), vec, vec],
        out_specs=(pl.BlockSpec((tm, 2 * CONV_DIM), lambda i: (i, 0)),
                   pl.BlockSpec((32, CONV_DIM), lambda i: (0, 0)), pl.BlockSpec((8, CONV_DIM), lambda i: (0, 0))),
        scratch_shapes=[pltpu.VMEM((32, 8, CONV_DIM), F32), pltpu.VMEM((8, 8, CONV_DIM), F32),
                        pltpu.VMEM((8, tm + 2 * HALO, CONV_DIM), F32), pltpu.VMEM((8, tm + 2 * HALO, CONV_DIM), F32)],
        args=(dact, dact, dact, yconv, yconv, yconv, cvg, cvg, cvg, cw, lg, lb))


def _merge_parts(un, o, cact, win_ref, wao_ref, wco_ref):
    g = _dot(un, win_ref[:, QC_DIM:])
    ga, gc = _sig(g[:, :D_MODEL]), _sig(g[:, D_MODEL:])
    ya = _dot(o, wao_ref[...])
    yc = _dot(cact, wco_ref[...])
    return ga, gc, ya, yc


def _merge_specs(tm):
    row = lambda w: pl.BlockSpec((tm, w), lambda i: (i, 0))
    full = lambda a, b: pl.BlockSpec((a, b), lambda i: (0, 0))
    weights = [full(D_MODEL, IN_DIM), full(Q_DIM, D_MODEL), full(CONV_DIM, D_MODEL), full(D_MODEL, D_MODEL)]
    return row, weights


def _merge_fwd(h1, un, o, cact, w_g, w_ao, w_co, w_o, tm, name, ex=NO_EXCHANGE):
    S = h1.shape[0]
    row, weights = _merge_specs(tm)

    def body(h1_ref, un_ref, o_ref, c_ref, wg_ref, wao_ref, wco_ref, wo_ref, h2_ref):
        ga, gc, ya, yc = _merge_parts(un_ref[...], o_ref[...], c_ref[...], wg_ref, wao_ref, wco_ref)
        h2_ref[...] = h1_ref[...] + _dot(_bf(ga * ya + gc * yc), wo_ref[...])

    return _carrier_call(
        body, ex, _grid_ends(S // tm), name, out_shape=(jax.ShapeDtypeStruct((S, D_MODEL), F32),),
        grid=(S // tm,),
        in_specs=[row(D_MODEL), row(D_MODEL), row(Q_DIM), row(CONV_DIM)] + weights,
        out_specs=(row(D_MODEL),), scratch_shapes=[], args=(h1, un, o, cact, w_g, w_ao, w_co, w_o))


def _merge_bwd(dh2, un, o, cact, w_g, w_ao, w_co, w_o, tm, name):
    S = dh2.shape[0]
    row, weights = _merge_specs(tm)

    def body(dh2_ref, un_ref, o_ref, c_ref, wg_ref, wao_ref, wco_ref, wo_ref,
             do_ref, dc_ref, mix_ref, dya_ref, dyc_ref, dgp_ref):
        ga, gc, ya, yc = _merge_parts(un_ref[...], o_ref[...], c_ref[...], wg_ref, wao_ref, wco_ref)
        mix_ref[...] = _bf(ga * ya + gc * yc)
        dmix = _dot_nt(_bf(dh2_ref[...]), wo_ref[...])
        dya = _bf(dmix * ga)
        dyc = _bf(dmix * gc)
        dya_ref[...] = dya
        dyc_ref[...] = dyc
        dgp_ref[:, :D_MODEL] = _bf(dmix * ya * ga * (1.0 - ga))
        dgp_ref[:, D_MODEL:] = _bf(dmix * yc * gc * (1.0 - gc))
        do_ref[...] = _dot_nt(dya, wao_ref[...])
        dc_ref[...] = _dot_nt(dyc, wco_ref[...])

    return pl.pallas_call(
        body, name=name,
        out_shape=(jax.ShapeDtypeStruct((S, Q_DIM), F32), jax.ShapeDtypeStruct((S, CONV_DIM), F32),
                   jax.ShapeDtypeStruct((S, D_MODEL), BF16), jax.ShapeDtypeStruct((S, D_MODEL), BF16),
                   jax.ShapeDtypeStruct((S, D_MODEL), BF16), jax.ShapeDtypeStruct((S, 2 * D_MODEL), BF16)),
        grid=(S // tm,),
        in_specs=[row(D_MODEL), row(D_MODEL), row(Q_DIM), row(CONV_DIM)] + weights,
        out_specs=(row(Q_DIM), row(CONV_DIM), row(D_MODEL), row(D_MODEL), row(D_MODEL), row(2 * D_MODEL)),
        compiler_params=_params(("parallel",)),
    )(dh2, un, o, cact, w_g, w_ao, w_co, w_o)


def _pe_specs(tm, layer):
    row = pl.BlockSpec((tm, D_MODEL), lambda i: (i, 0))
    vec = pl.BlockSpec((1, D_MODEL), lambda i: (0, 0))
    p_s = pl.BlockSpec((None, None, tm, 256), lambda i: (layer, 0, i, 0))
    wpp = pl.BlockSpec((256, D_MODEL), lambda i: (0, 0))
    wpg = pl.BlockSpec((D_MODEL, D_MODEL), lambda i: (0, 0))
    return row, vec, p_s, wpp, wpg


def _pe_fwd(h, gamma, p, layer, w_pp, w_pg, tm, name):
    S = h.shape[0]
    row, vec, p_s, wpp, wpg = _pe_specs(tm, layer)

    def body(h_ref, g_ref, p_ref, wpp_ref, wpg_ref, x_ref, hn_ref):
        hn = _bf(_rms_fwd(h_ref[...], g_ref[...])[0])
        hn_ref[...] = hn
        gate = _sig(_dot(hn, wpg_ref[...]))
        x_ref[...] = h_ref[...] + _dot(_bf(p_ref[...]), wpp_ref[...]) * gate

    return pl.pallas_call(
        body, name=name,
        out_shape=(jax.ShapeDtypeStruct((S, D_MODEL), F32), jax.ShapeDtypeStruct((S, D_MODEL), BF16)),
        grid=(S // tm,), in_specs=[row, vec, p_s, wpp, wpg], out_specs=(row, row),
        compiler_params=_params(("parallel",)),
    )(h, gamma, p, w_pp, w_pg)


def _pe_bwd(dx, h, gamma, hn, p, layer, w_pp, w_pg, tm, name):
    S = h.shape[0]
    row, vec, p_s, wpp, wpg = _pe_specs(tm, layer)

    def body(dx_ref, h_ref, g_ref, hn_ref, p_ref, wpp_ref, wpg_ref, dh_ref, dgp_ref, dpr_ref, dgam_ref):
        @pl.when(pl.program_id(0) == 0)
        def _():
            dgam_ref[...] = jnp.zeros_like(dgam_ref)

        dxv = dx_ref[...]
        gate = _sig(_dot(hn_ref[...], wpg_ref[...]))
        proj = _dot(_bf(p_ref[...]), wpp_ref[...])
        dpr_ref[...] = _bf(dxv * gate)
        dgp = _bf(dxv * proj * gate * (1.0 - gate))
        dgp_ref[...] = dgp
        dxn, dgam = _rms_bwd(_dot_nt(dgp, wpg_ref[...]), h_ref[...], g_ref[...])
        dh_ref[...] = dxv + dxn
        dgam_ref[...] += jnp.sum(dgam, axis=0, keepdims=True)

    return pl.pallas_call(
        body, name=name,
        out_shape=(jax.ShapeDtypeStruct((S, D_MODEL), F32), jax.ShapeDtypeStruct((S, D_MODEL), BF16),
                   jax.ShapeDtypeStruct((S, D_MODEL), BF16), jax.ShapeDtypeStruct((1, D_MODEL), F32)),
        grid=(S // tm,), in_specs=[row, row, vec, row, p_s, wpp, wpg], out_specs=(row, row, row, vec),
        compiler_params=_params(("arbitrary",)),
    )(dx, h, gamma, hn, p, w_pp, w_pg)


def _loss_head(y, target, tm):
    S = y.shape[0]

    def body(y_ref, t_ref, dy_ref, l_ref):
        @pl.when(pl.program_id(0) == 0)
        def _():
            l_ref[...] = jnp.zeros_like(l_ref)

        diff = y_ref[...] - t_ref[...]
        dy_ref[...] = diff * (1.0 / D_MODEL)
        sq = jnp.sum((diff * diff).reshape(tm // 8, 8, D_MODEL), axis=0)
        part = sq[:, 0:128]
        for k in range(1, D_MODEL // 128):
            part = part + sq[:, 128 * k:128 * (k + 1)]
        l_ref[...] += part

    row = pl.BlockSpec((tm, D_MODEL), lambda i: (i, 0))
    return pl.pallas_call(
        body, name="loss_head",
        out_shape=(jax.ShapeDtypeStruct((S, D_MODEL), F32), jax.ShapeDtypeStruct((8, 128), F32)),
        grid=(S // tm,), in_specs=[row, row], out_specs=(row, pl.BlockSpec((8, 128), lambda i: (0, 0))),
        compiler_params=_params(("arbitrary",)),
    )(y, target)


def _adamw(parts, w, m, v, name, ex=NO_EXCHANGE):
    nl = len(parts)
    R, C = w.shape
    K = R // nl
    tr = next((c for c in range(min(K, 256) // 16 * 16, 15, -16) if K % c == 0), K)
    nk = K // tr

    def body(*refs):
        p_refs = refs[:nl]
        w_ref, m_ref, v_ref, g_ref, d_ref, nm_ref, nv_ref = refs[nl:]
        for lyr in range(nl):
            @pl.when(pl.program_id(0) == lyr)
            def _(p_ref=p_refs[lyr]):
                g = p_ref[0].astype(F32)
                for k in range(1, N_DEV):
                    g = g + p_ref[k].astype(F32)
                g_ref[...] = g
                nm = ADAM_B1 * m_ref[...] + (1.0 - ADAM_B1) * g
                nv = ADAM_B2 * v_ref[...] + (1.0 - ADAM_B2) * (g * g)
                nm_ref[...] = nm
                nv_ref[...] = nv
                m_hat = nm / (1.0 - ADAM_B1 ** ADAM_STEP)
                v_hat = nv / (1.0 - ADAM_B2 ** ADAM_STEP)
                d_ref[...] = -ADAM_LR * (m_hat / (jnp.sqrt(v_hat) + ADAM_EPS) + ADAM_WD * w_ref[...])

    def part_spec(lyr):
        return pl.BlockSpec((N_DEV, tr, C), lambda l, i: (0, jnp.where(l == lyr, i, jnp.where(l < lyr, 0, nk - 1)), 0))

    blk = pl.BlockSpec((tr, C), lambda l, i: (l * nk + i, 0))
    out = jax.ShapeDtypeStruct((R, C), F32)
    return _carrier_call(
        body, ex, _grid_ends(nl, nk), name, out_shape=(out, out, out, out), grid=(nl, nk),
        in_specs=[part_spec(lyr) for lyr in range(nl)] + [blk, blk, blk],
        out_specs=(blk, blk, blk, blk), scratch_shapes=[], args=(*parts, w, m, v))


SHARDED = ("w_ffn1_in", "w_ffn1_out", "w_in", "conv_w", "w_attn_out", "w_conv_out", "w_o",
           "w_ffn2_in", "w_ffn2_out", "w_pe_gate", "w_pe_proj")
COL_SHARDED = ("w_ffn1_in", "w_in", "conv_w", "w_attn_out", "w_conv_out", "w_ffn2_in", "w_pe_proj")
SMALL = ("rel_bias", "norm_ffn1", "norm_mix", "q_norm", "k_norm", "sink", "conv_b", "conv_ln_g", "conv_ln_b",
         "norm_ffn2", "norm_pe")
WEIGHTS = ("rel_bias", "norm_ffn1", "w_ffn1_in", "w_ffn1_out", "norm_mix", "w_in", "q_norm", "k_norm", "sink",
           "conv_w", "conv_b", "conv_ln_g", "conv_ln_b", "w_attn_out", "w_conv_out", "w_o", "norm_ffn2",
           "w_ffn2_in", "w_ffn2_out", "norm_pe", "w_pe_gate", "w_pe_proj")


def _natural(g):
    k, n = g.shape[1], g.shape[2]
    return jnp.transpose(g, (1, 0, 2)).reshape(k, N_DEV * n)


def _blocked(w):
    k, n = w.shape[0], w.shape[1] // N_DEV
    return jnp.transpose(w.reshape(k, N_DEV, n), (1, 0, 2))


def kernel(x, p, rel_bias, norm_ffn1, w_ffn1_in, w_ffn1_out, norm_mix, w_in, q_norm, k_norm, sink, conv_w, conv_b, conv_ln_g, conv_ln_b, w_attn_out, w_conv_out, w_o, norm_ffn2, w_ffn2_in, w_ffn2_out, norm_pe, w_pe_gate, w_pe_proj, loss_target, m_rel_bias, m_norm_ffn1, m_w_ffn1_in, m_w_ffn1_out, m_norm_mix, m_w_in, m_q_norm, m_k_norm, m_sink, m_conv_w, m_conv_b, m_conv_ln_g, m_conv_ln_b, m_w_attn_out, m_w_conv_out, m_w_o, m_norm_ffn2, m_w_ffn2_in, m_w_ffn2_out, m_norm_pe, m_w_pe_gate, m_w_pe_proj, v_rel_bias, v_norm_ffn1, v_w_ffn1_in, v_w_ffn1_out, v_norm_mix, v_w_in, v_q_norm, v_k_norm, v_sink, v_conv_w, v_conv_b, v_conv_ln_g, v_conv_ln_b, v_w_attn_out, v_w_conv_out, v_w_o, v_norm_ffn2, v_w_ffn2_in, v_w_ffn2_out, v_norm_pe, v_w_pe_gate, v_w_pe_proj):
    W = dict(rel_bias=rel_bias, norm_ffn1=norm_ffn1, w_ffn1_in=w_ffn1_in, w_ffn1_out=w_ffn1_out, norm_mix=norm_mix,
             w_in=w_in, q_norm=q_norm, k_norm=k_norm, sink=sink, conv_w=conv_w, conv_b=conv_b, conv_ln_g=conv_ln_g,
             conv_ln_b=conv_ln_b, w_attn_out=w_attn_out, w_conv_out=w_conv_out, w_o=w_o, norm_ffn2=norm_ffn2,
             w_ffn2_in=w_ffn2_in, w_ffn2_out=w_ffn2_out, norm_pe=norm_pe, w_pe_gate=w_pe_gate, w_pe_proj=w_pe_proj)
    M = dict(rel_bias=m_rel_bias, norm_ffn1=m_norm_ffn1, w_ffn1_in=m_w_ffn1_in, w_ffn1_out=m_w_ffn1_out,
             norm_mix=m_norm_mix, w_in=m_w_in, q_norm=m_q_norm, k_norm=m_k_norm, sink=m_sink, conv_w=m_conv_w,
             conv_b=m_conv_b, conv_ln_g=m_conv_ln_g, conv_ln_b=m_conv_ln_b, w_attn_out=m_w_attn_out,
             w_conv_out=m_w_conv_out, w_o=m_w_o, norm_ffn2=m_norm_ffn2, w_ffn2_in=m_w_ffn2_in,
             w_ffn2_out=m_w_ffn2_out, norm_pe=m_norm_pe, w_pe_gate=m_w_pe_gate, w_pe_proj=m_w_pe_proj)
    V = dict(rel_bias=v_rel_bias, norm_ffn1=v_norm_ffn1, w_ffn1_in=v_w_ffn1_in, w_ffn1_out=v_w_ffn1_out,
             norm_mix=v_norm_mix, w_in=v_w_in, q_norm=v_q_norm, k_norm=v_k_norm, sink=v_sink, conv_w=v_conv_w,
             conv_b=v_conv_b, conv_ln_g=v_conv_ln_g, conv_ln_b=v_conv_ln_b, w_attn_out=v_w_attn_out,
             w_conv_out=v_w_conv_out, w_o=v_w_o, norm_ffn2=v_norm_ffn2, w_ffn2_in=v_w_ffn2_in,
             w_ffn2_out=v_w_ffn2_out, norm_pe=v_norm_pe, w_pe_gate=v_w_pe_gate, w_pe_proj=v_w_pe_proj)

    L = w_in.shape[0]
    S = x.shape[1]
    tm = min(512, S)
    tm_ffn = min(1024, S)
    xs = x[0]
    target = loss_target[0]
    vec = lambda a: a.reshape(1, -1)

    half, full = {}, {}

    def carried(stage1_items, stage2_items):
        s1 = [it for it in stage1_items if it[1] < L]
        s2 = [it for it in stage2_items if it[1] < L]
        ex = _stage1([W[n][l] if n == "conv_w" else W[n][l].astype(BF16) for n, l in s1]) + _stage2(
            [half.pop(it) for it in s2])
        return ex, s1, s2

    def landed(got, s1, s2):
        half.update(zip(s1, got[:len(s1)]))
        full.update(zip(s2, got[len(s1):]))

    onehot, band = _bucket_onehot()
    bias = _bias_table(rel_bias.T, onehot, band).reshape(N_HEADS, BLOCK, WIN)

    layers, saved = [], []
    h = xs
    ex, s1, s2 = carried([("w_ffn1_in", 0), ("w_ffn1_out", 0), ("w_in", 0), ("conv_w", 0)], [])
    landed(_exchange(ex, "allgather_first"), s1, s2)
    ex, s1, s2 = carried([], [("w_ffn1_in", 0), ("w_ffn1_out", 0)])
    landed(_exchange(ex, "allgather_relay"), s1, s2)
    for l in range(L):
        sv = dict(x0=h)
        G = dict(wi1=full.pop(("w_ffn1_in", l)), wo1=full.pop(("w_ffn1_out", l)))
        ex, s1, s2 = carried(
            [(n, l) for n in ("w_ffn2_in", "w_ffn2_out", "w_attn_out", "w_conv_out", "w_o")],
            [("w_in", l), ("conv_w", l)])
        (h1, sv["xn1"], sv["g1"], sv["u1"]), got = _ffn_fwd(
            h, vec(norm_ffn1[l]), G["wi1"], G["wo1"], tm_ffn, "ffn1_fwd", ex)
        landed(got, s1, s2)
        G.update(w_in=_natural(full.pop(("w_in", l))),
                 conv_w=jnp.pad(_natural(full.pop(("conv_w", l))), ((0, 1), (0, 0))))
        sv["h1"] = h1
        sv["un"], sv["qkv"], sv["cvg"] = _mixin_fwd(h1, vec(norm_mix[l]), G["w_in"], tm, "mixin_fwd")
        sv["qg"] = vec(jnp.tile(q_norm[l], 2))
        sv["kg"] = vec(jnp.tile(k_norm[l], KV_HEADS))
        sv["sinkb"] = jnp.broadcast_to(sink[l][:, None], (N_HEADS, 128))
        ex, s1, s2 = carried(
            [("w_pe_gate", l), ("w_pe_proj", l), ("w_ffn1_in", l + 1)],
            [(n, l) for n in ("w_ffn2_in", "w_ffn2_out", "w_attn_out", "w_conv_out", "w_o")])
        (sv["o"],), got = _attn_fwd(sv["qkv"], sv["qg"], sv["kg"], sv["sinkb"], bias, "attn_fwd", ex)
        landed(got, s1, s2)
        G.update(wi2=full.pop(("w_ffn2_in", l)), wo2=full.pop(("w_ffn2_out", l)),
                 w_ao=_natural(full.pop(("w_attn_out", l))), w_co=_natural(full.pop(("w_conv_out", l))),
                 w_o=full.pop(("w_o", l)).reshape(D_MODEL, D_MODEL))
        ex, s1, s2 = carried([("w_ffn1_out", l + 1)],
                             [("w_pe_gate", l), ("w_pe_proj", l), ("w_ffn1_in", l + 1)])
        (sv["cact"], sv["yconv"]), got = _conv_fwd(
            sv["cvg"], G["conv_w"], vec(conv_b[l]), vec(conv_ln_g[l]), vec(conv_ln_b[l]), tm, "conv_fwd", ex)
        landed(got, s1, s2)
        G.update(w_pg=full.pop(("w_pe_gate", l)).reshape(D_MODEL, D_MODEL),
                 w_pp=_natural(full.pop(("w_pe_proj", l))))
        ex, s1, s2 = carried([("w_in", l + 1), ("conv_w", l + 1)], [("w_ffn1_out", l + 1)])
        (h2,), got = _merge_fwd(h1, sv["un"], sv["o"], sv["cact"], G["w_in"], G["w_ao"], G["w_co"], G["w_o"], tm,
                                "merge_fwd", ex)
        landed(got, s1, s2)
        sv["h2"] = h2
        (h3, sv["xn2"], sv["g2"], sv["u2"]), _ = _ffn_fwd(
            h2, vec(norm_ffn2[l]), G["wi2"], G["wo2"], tm_ffn, "ffn2_fwd")
        sv["h3"] = h3
        h, sv["hn"] = _pe_fwd(h3, vec(norm_pe[l]), p, l, G["w_pp"], G["w_pg"], tm, "pe_fwd")
        layers.append(G)
        saved.append(sv)

    dh, lparts = _loss_head(h, target, tm)
    loss = lax.psum((0.5 / D_MODEL) * jnp.sum(lparts), AXES)

    dbias = jnp.zeros((N_HEADS, BLOCK, WIN), F32)
    small_g = {n: [None] * L for n in SMALL if n != "rel_bias"}
    recv = {n: [None] * L for n in SHARDED}

    def keep(names, l, got):
        for n, r in zip(names, got):
            recv[n][l] = r

    pending = None
    for l in reversed(range(L)):
        G, sv = layers[l], saved[l]
        dh3, dgp_pe, dproj, dg_pe = _pe_bwd(dh, sv["h3"], vec(norm_pe[l]), sv["hn"], p, l, G["w_pp"], G["w_pg"],
                                            tm, "pe_bwd")
        gw_pg = _matmul_tn(sv["hn"][None], dgp_pe[None], 1, "dw_pe_gate")
        gw_pp = _matmul_tn(p.reshape(L, S, p.shape[-1]), dproj[None], 1, "dw_pe_proj", a_index=l)
        (dh2, a2, dgu2, dg_n2), got = _ffn_bwd(
            dh3, sv["h2"], vec(norm_ffn2[l]), sv["g2"], sv["u2"], G["wi2"], G["wo2"], tm_ffn, "ffn2_bwd",
            _Exchange(pending, False) if pending else NO_EXCHANGE)
        if pending:
            keep(("w_ffn1_in",), l + 1, got)
        gwo2 = _matmul_tn(a2, dh3[None], FF_BLOCKS, "dw_ffn2_out", scale=0.5)
        gwi2 = _matmul_tn(sv["xn2"][None], dgu2.reshape(2 * FF_BLOCKS, S, FF_SHARD), 2 * FF_BLOCKS, "dw_ffn2_in")
        do, dcact, mix, dya, dyc, dgpre = _merge_bwd(dh2, sv["un"], sv["o"], sv["cact"], G["w_in"], G["w_ao"],
                                                     G["w_co"], G["w_o"], tm, "merge_bwd")
        gw_o = _matmul_tn(mix[None], dh2[None], 1, "dw_o")
        gw_ao = _matmul_tn(sv["o"][None], dya[None], 1, "dw_attn_out")
        gw_co = _matmul_tn(sv["cact"][None], dyc[None], 1, "dw_conv_out")
        (dq, dkp, dvp, dbias, dsink, dqg), got = _attn_bwd(
            do, sv["qkv"], sv["qg"], sv["kg"], sv["sinkb"], bias, dbias, "attn_bwd",
            _Exchange([gwi2, gw_pg.reshape(N_DEV, D_MODEL // N_DEV, D_MODEL), _blocked(gw_pp[0])], False))
        keep(("w_ffn2_in", "w_pe_gate", "w_pe_proj"), l, got)
        dkv, dkg = _kv_fold(dkp, dvp, sv["qkv"], sv["kg"], "kv_fold")
        (dcvg, dcw, dcvec), got = _conv_bwd(
            dcact, sv["yconv"], sv["cvg"], G["conv_w"], vec(conv_ln_g[l]), vec(conv_ln_b[l]), tm, "conv_bwd",
            _Exchange([gwo2.reshape(N_DEV, FF_SHARD // 2, D_MODEL)], False))
        keep(("w_ffn2_out",), l, got)
        dh1, dg_mix, dz = _mixin_bwd(dh2, sv["h1"], vec(norm_mix[l]), dq, dkv, dcvg, dgpre, G["w_in"],
                                     tm, "mixin_bwd")
        gw_in = _matmul_tn(sv["un"][None], dz[None], 1, "dw_in")[0]
        mid_send = [_blocked(gw_in), _blocked(dcw[:CONV_WIDTH]), _blocked(gw_ao[0]), _blocked(gw_co[0]),
                    gw_o.reshape(N_DEV, D_MODEL // N_DEV, D_MODEL)]
        (dh, a1, dgu1, dg_n1), got = _ffn_bwd(
            dh1, sv["x0"], vec(norm_ffn1[l]), sv["g1"], sv["u1"], G["wi1"], G["wo1"], tm_ffn, "ffn1_bwd",
            _Exchange(mid_send, False))
        keep(("w_in", "conv_w", "w_attn_out", "w_conv_out", "w_o"), l, got)
        gwo1 = _matmul_tn(a1, dh1[None], FF_BLOCKS, "dw_ffn1_out", scale=0.5)
        gwi1, got = _matmul_tn(sv["xn1"][None], dgu1.reshape(2 * FF_BLOCKS, S, FF_SHARD), 2 * FF_BLOCKS,
                               "dw_ffn1_in", ex=_Exchange([gwo1.reshape(N_DEV, FF_SHARD // 2, D_MODEL)], False))
        keep(("w_ffn1_out",), l, got)
        pending = [gwi1]
        small_g["norm_ffn1"][l] = dg_n1[0]
        small_g["norm_mix"][l] = dg_mix[0]
        small_g["q_norm"][l] = dqg[0, :HEAD_DIM]
        small_g["k_norm"][l] = dkg[0, :HEAD_DIM]
        small_g["sink"][l] = dsink[:, 0]
        small_g["conv_b"][l] = dcvec[0]
        small_g["conv_ln_g"][l] = dcvec[1]
        small_g["conv_ln_b"][l] = dcvec[2]
        small_g["norm_ffn2"][l] = dg_n2[0]
        small_g["norm_pe"][l] = dg_pe[0]

    keep(("w_ffn1_in",), 0, _exchange(_Exchange(pending, False), "grad_exchange_last"))
    grad_x = dh[None]
    drb = _bias_grad(dbias.reshape(N_HEADS, BIAS_COLS), onehot).T

    res = {}
    for n in SHARDED:
        shp = W[n].shape
        rows, cols = shp[0] * shp[1], shp[2]
        parts = recv[n]
        if shp[1] % 8:
            parts = [jnp.stack(recv[n], axis=1).reshape(N_DEV, rows, cols)]
        res[n] = [o.reshape(shp) for o in _adamw(
            parts, W[n].reshape(rows, cols), M[n].reshape(rows, cols), V[n].reshape(rows, cols), "adamw_" + n)[0]]

    flat_g = jnp.concatenate([drb.reshape(-1)] + [jnp.stack(small_g[n]).reshape(-1) for n in SMALL[1:]])
    n_small = flat_g.shape[0]
    rows_s = -(-n_small // 1024 // 8) * 8
    pad = lambda a: jnp.pad(a, (0, rows_s * 1024 - n_small)).reshape(rows_s, 1024)
    flat = lambda d: pad(jnp.concatenate([d[n].reshape(-1) for n in SMALL]))
    (parts_s,) = _exchange(_Exchange([pad(flat_g)], True), "small_allgather")
    outs_s = _adamw([parts_s], flat(W), flat(M), flat(V), "adamw_small")[0]
    off = 0
    for n in SMALL:
        size = W[n].size
        res[n] = [o.reshape(-1)[off:off + size].reshape(W[n].shape) for o in outs_s]
        off += size

    out = [loss, grad_x]
    for k in range(4):
        out += [res[n][k] for n in WEIGHTS]
    return tuple(out)
```

```python
import functools

import jax
import jax.numpy as jnp
import numpy as np
from jax import lax
from jax.experimental import pallas as pl
from jax.experimental.pallas import tpu as pltpu

F32 = jnp.float32
BF16 = jnp.bfloat16
MESH_ID = pl.DeviceIdType.MESH
AXES = ("x", "y", "c")
N_DEV = 8

D_MODEL = 1024
N_HEADS = 8
KV_HEADS = 2
HEAD_DIM = 64
Q_DIM = 512
KV_DIM = 128
BLOCK = 128
WIN = 3 * BLOCK
NUM_BUCKETS = 32
MAX_DISTANCE = 128
CONV_DIM = 512
CONV_WIDTH = 31
D_FF = 2816
FF_SHARD = 2 * D_FF // N_DEV
FF_BLOCKS = D_FF // FF_SHARD
QC_DIM = Q_DIM + 2 * KV_DIM + 2 * CONV_DIM
IN_DIM = QC_DIM + 2 * D_MODEL
NEG_INF = -1e9
HALO = 16
CROWS = 64
FFN_PART_ROWS = 256
TS_BF16 = 4096

ADAM_LR = 0.001
ADAM_B1 = 0.9
ADAM_B2 = 0.999
ADAM_EPS = 1e-08
ADAM_WD = 0.01
ADAM_STEP = 10

VMEM_LIMIT = 56 * 1024 * 1024
HI = lax.Precision.HIGHEST


def _params(sem):
    return pltpu.CompilerParams(dimension_semantics=sem, vmem_limit_bytes=VMEM_LIMIT)


def _dot(a, b, precision=None):
    return jnp.dot(a, b, preferred_element_type=F32, precision=precision)


def _dot_nt(a, b):
    return lax.dot_general(a, b, (((1,), (1,)), ((), ())), preferred_element_type=F32)


def _dot_tn(a, b):
    return lax.dot_general(a, b, (((0,), (0,)), ((), ())), preferred_element_type=F32)


def _sig(x):
    return 1.0 / (1.0 + jnp.exp(-x))


def _bf(x):
    return x.astype(BF16)


def _rms_fwd(x, gamma):
    r = lax.rsqrt(jnp.mean(x * x, axis=-1, keepdims=True) + 1e-6)
    return x * r * gamma, r


def _rms_bwd(dy, x, gamma):
    r = lax.rsqrt(jnp.mean(x * x, axis=-1, keepdims=True) + 1e-6)
    xhat = x * r
    dxhat = dy * gamma
    dx = r * (dxhat - xhat * jnp.mean(dxhat * xhat, axis=-1, keepdims=True))
    return dx, dy * xhat


SAME_CORE = (2, 4, 6)


class _Exchange:
    def __init__(self, arrs=(), gather=True, kinds=None):
        self.arrs = list(arrs)
        self.n = n = len(self.arrs)
        self.kinds = list(kinds) if kinds is not None else ["gather" if gather else "scatter"] * n
        self.out_shape = tuple(
            jax.ShapeDtypeStruct(((N_DEV,) + a.shape) if k in ("gather", "stage1") else a.shape, a.dtype)
            for a, k in zip(self.arrs, self.kinds))
        self.aliases = {t: t for t, k in enumerate(self.kinds) if k == "stage2"}
        self.specs = [pl.BlockSpec(memory_space=pl.ANY)] * n
        self.scratch = [pltpu.SemaphoreType.DMA((7 * n,)), pltpu.SemaphoreType.DMA((7 * n,)),
                        pltpu.SemaphoreType.DMA((n,))] if n else []

    def __add__(self, other):
        return _Exchange(self.arrs + other.arrs, kinds=self.kinds + other.kinds)

    def _copies(self, ins, outs, sems):
        n = self.n
        send_sems, recv_sems, local_sems = sems
        x, y, c = lax.axis_index("x"), lax.axis_index("y"), lax.axis_index("c")
        me = 4 * x + 2 * y + c
        copies = []
        for t, kind in enumerate(self.kinds):
            if kind != "stage2":
                copies.append(pltpu.make_async_copy(
                    ins[t].at[me] if kind == "scatter" else ins[t], outs[t].at[me], local_sems.at[t]))
            offsets = {"gather": range(1, N_DEV), "scatter": range(1, N_DEV),
                       "stage1": (1,) + SAME_CORE, "stage2": SAME_CORE}[kind]
            for d in offsets:
                px = 1 - x if d & 4 else x
                py = 1 - y if d & 2 else y
                pc = 1 - c if d & 1 else c
                peer = 4 * px + 2 * py + pc
                if kind == "stage2":
                    src, dst, to = ins[t].at[peer], outs[t].at[peer], (x, y, 1 - c)
                else:
                    src, dst, to = (ins[t].at[peer] if kind == "scatter" else ins[t]), outs[t].at[me], (px, py, pc)
                k = (d - 1) * n + t
                copies.append(pltpu.make_async_remote_copy(
                    src_ref=src, dst_ref=dst, send_sem=send_sems.at[k], recv_sem=recv_sems.at[k],
                    device_id=to, device_id_type=MESH_ID))
        return copies

    def start(self, ins, outs, sems):
        for cp in self._copies(ins, outs, sems):
            cp.start()

    def wait(self, ins, outs, sems):
        for cp in self._copies(ins, outs, sems):
            cp.wait()


NO_EXCHANGE = _Exchange()


def _stage1(arrs):
    return _Exchange(arrs, kinds=["stage1"] * len(arrs))


def _stage2(arrs):
    return _Exchange(arrs, kinds=["stage2"] * len(arrs))


def _exchange(ex, name):
    n = ex.n

    def body(*refs):
        ins, outs, sems = refs[:n], refs[n:2 * n], refs[2 * n:]
        ex.start(ins, outs, sems)
        ex.wait(ins, outs, sems)

    return pl.pallas_call(
        body, name=name, out_shape=ex.out_shape, in_specs=ex.specs, out_specs=tuple(ex.specs),
        scratch_shapes=ex.scratch, input_output_aliases=ex.aliases,
    )(*ex.arrs)


def _carrier_call(body, ex, first_last, name, out_shape, grid, in_specs, out_specs, scratch_shapes, args):
    n_in, n_out, n_scr, n = len(in_specs), len(out_shape), len(scratch_shapes), ex.n

    def full(*refs):
        a, ci = refs[:n_in], refs[n_in:n_in + n]
        o = refs[n_in + n:n_in + n + n_out]
        co = refs[n_in + n + n_out:n_in + 2 * n + n_out]
        scr = refs[n_in + 2 * n + n_out:n_in + 2 * n + n_out + n_scr]
        sems = refs[n_in + 2 * n + n_out + n_scr:]
        first, last = first_last()
        if n:
            @pl.when(first)
            def _():
                ex.start(ci, co, sems)

        body(*a, *o, *scr)
        if n:
            @pl.when(last)
            def _():
                ex.wait(ci, co, sems)

    outs = pl.pallas_call(
        full, name=name, out_shape=tuple(out_shape) + ex.out_shape, grid=grid,
        in_specs=list(in_specs) + ex.specs, out_specs=tuple(out_specs) + tuple(ex.specs),
        scratch_shapes=list(scratch_shapes) + ex.scratch,
        input_output_aliases={n_in + t: n_out + u for t, u in ex.aliases.items()},
        compiler_params=_params(("arbitrary",) * len(grid)),
    )(*args, *ex.arrs)
    return outs[:n_out], outs[n_out:]


def _matmul_tn(a, b, nb, name, scale=1.0, out_dtype=BF16, ts=2048, ex=None, a_index=None):
    ba, S, K = a.shape
    bb, _, N = b.shape
    ts = min(ts, S)
    tn = N if N <= 1024 else next(c for c in (1280, 1024, 768, 512, 256) if N % c == 0)
    assert S % ts == 0
    ns = S // ts

    def body(a_ref, b_ref, o_ref, acc):
        s = pl.program_id(2)

        @pl.when(s == 0)
        def _():
            acc[...] = jnp.zeros_like(acc)

        acc[...] += _dot_tn(_bf(a_ref[...]), _bf(b_ref[...]))

        @pl.when(s == ns - 1)
        def _():
            o_ref[...] = (acc[...] * scale).astype(out_dtype)

    a_map = (lambda i, j, s: (a_index, s, 0)) if a_index is not None else (
        (lambda i, j, s: (i, s, 0)) if ba > 1 else (lambda i, j, s: (0, s, 0)))
    in_specs = [pl.BlockSpec((None, ts, K), a_map),
                pl.BlockSpec((None, ts, tn), (lambda i, j, s: (i, s, j)) if bb > 1 else (lambda i, j, s: (0, s, j)))]
    out_spec = pl.BlockSpec((None, K, tn), lambda i, j, s: (i, 0, j))
    out_shape = jax.ShapeDtypeStruct((nb, K, N), out_dtype)
    scratch = [pltpu.VMEM((K, tn), F32)]
    grid = (nb, N // tn, ns)
    if ex is not None:
        (out,), got = _carrier_call(body, ex, _grid_ends(*grid), name, out_shape=(out_shape,), grid=grid,
                                    in_specs=in_specs, out_specs=(out_spec,), scratch_shapes=scratch, args=(a, b))
        return out, got
    return pl.pallas_call(
        body, name=name, out_shape=out_shape, grid=grid, in_specs=in_specs, out_specs=out_spec,
        scratch_shapes=scratch, compiler_params=_params(("parallel", "parallel", "arbitrary")),
    )(a, b)


def _ffn_specs(tm):
    wg = pl.BlockSpec((None, D_MODEL, FF_SHARD), lambda i, j: (j, 0, 0))
    wu = pl.BlockSpec((None, D_MODEL, FF_SHARD), lambda i, j: (j + FF_BLOCKS, 0, 0))
    wo = pl.BlockSpec((2, FF_SHARD // 2, D_MODEL), lambda i, j: (j, 0, 0))
    row = pl.BlockSpec((tm, D_MODEL), lambda i, j: (i, 0))
    vec = pl.BlockSpec((1, D_MODEL), lambda i, j: (0, 0))
    hid = pl.BlockSpec((None, tm, FF_SHARD), lambda i, j: (j, i, 0))
    return wg, wu, wo, row, vec, hid


def _grid_ends(*grid):
    def first_last():
        first, last = None, None
        for d, n in enumerate(grid):
            i = pl.program_id(d)
            first = (i == 0) if first is None else first & (i == 0)
            last = (i == n - 1) if last is None else last & (i == n - 1)
        return first, last
    return first_last


def _grid2_ends(ni, nj):
    return _grid_ends(ni, nj)


def _grid1_ends(ni):
    return _grid_ends(ni)


def _ffn_fwd(x, gamma, wi, wo, tm, name, ex=NO_EXCHANGE):
    S = x.shape[0]
    wg_s, wu_s, wo_s, row, vec, hid = _ffn_specs(tm)

    def body(x_ref, g_ref, wg_ref, wu_ref, wo_ref, y_ref, xn_ref, gs_ref, us_ref, xn_s, acc):
        j = pl.program_id(1)

        @pl.when(j == 0)
        def _():
            xn = _bf(_rms_fwd(x_ref[...], g_ref[...])[0])
            xn_s[...] = xn
            xn_ref[...] = xn
            acc[...] = jnp.zeros_like(acc)

        wo2 = wo_ref[...].reshape(FF_SHARD, D_MODEL)
        for r in range(tm // FFN_PART_ROWS):
            rows = slice(r * FFN_PART_ROWS, (r + 1) * FFN_PART_ROWS)
            xn = xn_s[rows, :]
            g = _dot(xn, wg_ref[...])
            u = _dot(xn, wu_ref[...])
            gs_ref[rows, :] = _bf(g)
            us_ref[rows, :] = _bf(u)
            a = g * _sig(g) * u
            acc[rows, :] += _dot(_bf(a), wo2)

        @pl.when(j == FF_BLOCKS - 1)
        def _():
            y_ref[...] = x_ref[...] + 0.5 * acc[...]

    return _carrier_call(
        body, ex, _grid2_ends(S // tm, FF_BLOCKS), name,
        out_shape=(jax.ShapeDtypeStruct((S, D_MODEL), F32), jax.ShapeDtypeStruct((S, D_MODEL), BF16),
                   jax.ShapeDtypeStruct((FF_BLOCKS, S, FF_SHARD), BF16),
                   jax.ShapeDtypeStruct((FF_BLOCKS, S, FF_SHARD), BF16)),
        grid=(S // tm, FF_BLOCKS),
        in_specs=[row, vec, wg_s, wu_s, wo_s],
        out_specs=(row, row, hid, hid),
        scratch_shapes=[pltpu.VMEM((tm, D_MODEL), BF16), pltpu.VMEM((tm, D_MODEL), F32)],
        args=(x, gamma, wi, wi, wo))


def _ffn_bwd(dy, x, gamma, gs, us, wi, wo, tm, name, ex=NO_EXCHANGE):
    S = x.shape[0]
    wg_s, wu_s, wo_s, row, vec, hid = _ffn_specs(tm)
    dgu_s = pl.BlockSpec((2, None, tm, FF_SHARD), lambda i, j: (0, j, i, 0))

    def body(dy_ref, x_hbm, g_ref, gs_ref, us_ref, wg_ref, wu_ref, wo_ref,
             dx_ref, a_ref, dgu_ref, dgam_ref, dyh_s, x_buf, x_sem):
        i, j = pl.program_id(0), pl.program_id(1)
        acc = dx_ref
        x_copy = pltpu.make_async_copy(x_hbm.at[pl.ds(pl.multiple_of(i * tm, tm), tm), :], x_buf, x_sem)

        @pl.when(j == 0)
        def _():
            x_copy.start()
            dyh_s[...] = _bf(0.5 * dy_ref[...])
            acc[...] = jnp.zeros_like(acc)

        @pl.when((i == 0) & (j == 0))
        def _():
            dgam_ref[...] = jnp.zeros_like(dgam_ref)

        wo2 = wo_ref[...].reshape(FF_SHARD, D_MODEL)
        for r in range(tm // FFN_PART_ROWS):
            rows = slice(r * FFN_PART_ROWS, (r + 1) * FFN_PART_ROWS)
            da = _dot_nt(dyh_s[rows, :], wo2)
            g = gs_ref[rows, :].astype(F32)
            u = us_ref[rows, :].astype(F32)
            sg = _sig(g)
            sl = g * sg
            a_ref[rows, :] = _bf(sl * u)
            dg = _bf(da * u * (sg * (1.0 + g * (1.0 - sg))))
            du = _bf(da * sl)
            dgu_ref[0, rows, :] = dg
            dgu_ref[1, rows, :] = du
            acc[rows, :] += _dot_nt(dg, wg_ref[...]) + _dot_nt(du, wu_ref[...])

        @pl.when(j == FF_BLOCKS - 1)
        def _():
            x_copy.wait()
            dx, dgam = _rms_bwd(acc[...], x_buf[...], g_ref[...])
            dx_ref[...] = dy_ref[...] + dx
            dgam_ref[...] += jnp.sum(dgam, axis=0, keepdims=True)

    return _carrier_call(
        body, ex, _grid2_ends(S // tm, FF_BLOCKS), name,
        out_shape=(jax.ShapeDtypeStruct((S, D_MODEL), F32),
                   jax.ShapeDtypeStruct((FF_BLOCKS, S, FF_SHARD), BF16),
                   jax.ShapeDtypeStruct((2, FF_BLOCKS, S, FF_SHARD), BF16),
                   jax.ShapeDtypeStruct((1, D_MODEL), F32)),
        grid=(S // tm, FF_BLOCKS),
        in_specs=[row, pl.BlockSpec(memory_space=pl.ANY), vec, hid, hid, wg_s, wu_s, wo_s],
        out_specs=(row, hid, dgu_s, vec),
        scratch_shapes=[pltpu.VMEM((tm, D_MODEL), BF16), pltpu.VMEM((tm, D_MODEL), F32),
                        pltpu.SemaphoreType.DMA(())],
        args=(dy, x, gamma, gs, us, wi, wi, wo))


def _mixin_fwd(h, gamma, w_qc, tm, name):
    S = h.shape[0]
    nqkv = Q_DIM + 2 * KV_DIM

    def body(h_ref, g_ref, w_ref, un_ref, qkv_ref, cvg_ref):
        un = _bf(_rms_fwd(h_ref[...], g_ref[...])[0])
        un_ref[...] = un
        z = _dot(un, w_ref[:, :QC_DIM])
        qkv_ref[...] = z[:, :nqkv]
        cvg_ref[...] = z[:, nqkv:]

    row = lambda w: pl.BlockSpec((tm, w), lambda i: (i, 0))
    return pl.pallas_call(
        body, name=name,
        out_shape=(jax.ShapeDtypeStruct((S, D_MODEL), BF16), jax.ShapeDtypeStruct((S, nqkv), F32),
                   jax.ShapeDtypeStruct((S, 2 * CONV_DIM), F32)),
        grid=(S // tm,),
        in_specs=[row(D_MODEL), pl.BlockSpec((1, D_MODEL), lambda i: (0, 0)),
                  pl.BlockSpec((D_MODEL, IN_DIM), lambda i: (0, 0))],
        out_specs=(row(D_MODEL), row(nqkv), row(2 * CONV_DIM)),
        compiler_params=_params(("parallel",)),
    )(h, gamma, w_qc)


def _mixin_bwd(dh2, h1, gamma, dq, dkv, dcvg, dgpre, w_in, tm, name):
    S = h1.shape[0]
    nqkv = Q_DIM + 2 * KV_DIM
    n_in = QC_DIM + 2 * D_MODEL

    def body(dh2_ref, h1_ref, g_ref, dq_ref, dkv_ref, dcvg_ref, dgp_ref, win_ref,
             dh1_ref, dgam_ref, dz_ref):
        @pl.when(pl.program_id(0) == 0)
        def _():
            dgam_ref[...] = jnp.zeros_like(dgam_ref)

        wqc = win_ref[:, :QC_DIM]
        dq, dkv, dcvg = _bf(dq_ref[...]), _bf(dkv_ref[...]), _bf(dcvg_ref[...])
        dz_ref[:, :Q_DIM] = dq
        dz_ref[:, Q_DIM:nqkv] = dkv
        dz_ref[:, nqkv:QC_DIM] = dcvg
        dz_ref[:, QC_DIM:] = dgp_ref[...]
        dun = _dot_nt(dq, wqc[:, :Q_DIM])
        dun += _dot_nt(dkv, wqc[:, Q_DIM:nqkv])
        dun += _dot_nt(dcvg, wqc[:, nqkv:])
        dun += _dot_nt(dgp_ref[...], win_ref[:, QC_DIM:])
        dx, dgam = _rms_bwd(dun, h1_ref[...], g_ref[...])
        dh1_ref[...] = dh2_ref[...] + dx
        dgam_ref[...] += jnp.sum(dgam, axis=0, keepdims=True)

    row = lambda w: pl.BlockSpec((tm, w), lambda i: (i, 0))
    vec = pl.BlockSpec((1, D_MODEL), lambda i: (0, 0))
    return pl.pallas_call(
        body, name=name,
        out_shape=(jax.ShapeDtypeStruct((S, D_MODEL), F32), jax.ShapeDtypeStruct((1, D_MODEL), F32),
                   jax.ShapeDtypeStruct((S, n_in), BF16)),
        grid=(S // tm,),
        in_specs=[row(D_MODEL), row(D_MODEL), vec, row(Q_DIM), row(2 * KV_DIM), row(2 * CONV_DIM),
                  row(2 * D_MODEL), pl.BlockSpec((D_MODEL, IN_DIM), lambda i: (0, 0))],
        out_specs=(row(D_MODEL), vec, row(n_in)),
        compiler_params=_params(("arbitrary",)),
    )(dh2, h1, gamma, dq, dkv, dcvg, dgpre, w_in)


TQ = 512
QB = TQ // BLOCK


def _attn_in_specs(S):
    nkb = S // BLOCK
    return [
        pl.BlockSpec((TQ, Q_DIM), lambda i: (i, 0)),
        pl.BlockSpec((BLOCK, 2 * KV_DIM), lambda i: (jnp.maximum(i * QB - 1, 0), Q_DIM // (2 * KV_DIM))),
        pl.BlockSpec((TQ, 2 * KV_DIM), lambda i: (i, Q_DIM // (2 * KV_DIM))),
        pl.BlockSpec((BLOCK, 2 * KV_DIM), lambda i: (jnp.minimum(i * QB + QB, nkb - 1), Q_DIM // (2 * KV_DIM))),
        pl.BlockSpec((1, 128), lambda i: (0, 0)),
        pl.BlockSpec((1, KV_DIM), lambda i: (0, 0)),
        pl.BlockSpec((N_HEADS, 128), lambda i: (0, 0)),
        pl.BlockSpec((N_HEADS, BLOCK, WIN), lambda i: (0, 0, 0)),
    ]


GROUP_ROWS = 4 * BLOCK


def _half_rstd(x, low):
    x2 = x * x
    z = jnp.zeros_like(x2)
    r0 = lax.rsqrt(jnp.sum(jnp.where(low, x2, z), axis=-1, keepdims=True) * (1.0 / HEAD_DIM) + 1e-6)
    r1 = lax.rsqrt(jnp.sum(jnp.where(low, z, x2), axis=-1, keepdims=True) * (1.0 / HEAD_DIM) + 1e-6)
    return jnp.where(low, r0, r1)


def _kv_windows(kvp_ref, kvc_ref, kvn_ref, kg_ref, low):
    kv = jnp.concatenate([kvp_ref[...], kvc_ref[...], kvn_ref[...]], axis=0)
    k, v = kv[:, :KV_DIM], kv[:, KV_DIM:]
    kn = k * _half_rstd(k, low) * kg_ref[...]
    kr, vr = pltpu.roll(kn, HEAD_DIM, 1), pltpu.roll(v, HEAD_DIM, 1)
    kdup = [_bf(jnp.where(low, kn, kr)), _bf(jnp.where(low, kr, kn))]
    vdup = [_bf(jnp.where(low, v, vr)), _bf(jnp.where(low, vr, v))]
    return kdup, vdup


def _stack_heads(x_ref, t, kh, low):
    rows = slice(t * BLOCK, (t + 1) * BLOCK)
    xa = x_ref[rows, 256 * kh:256 * kh + 128]
    xb = x_ref[rows, 256 * kh + 128:256 * kh + 256]
    z = jnp.zeros_like(xa)
    return jnp.concatenate([jnp.where(low, xa, z), jnp.where(low, z, xa),
                            jnp.where(low, xb, z), jnp.where(low, z, xb)], axis=0)


def _stacked_q(q_ref, qg_ref, t, kh, low):
    qraw = _stack_heads(q_ref, t, kh, low)
    rq = lax.rsqrt(jnp.sum(qraw * qraw, axis=-1, keepdims=True) * (1.0 / HEAD_DIM) + 1e-6)
    return qraw, rq, _bf(qraw * rq * (qg_ref[...] * (HEAD_DIM ** -0.5)))


def _unstack_heads(ov, low):
    return (jnp.where(low, ov[0:128], ov[128:256]), jnp.where(low, ov[256:384], ov[384:512]))


def _edge_bias(i, t, S):
    kpos = i * TQ + (t - 1) * BLOCK + lax.broadcasted_iota(jnp.int32, (1, WIN), 1)
    return jnp.where((kpos < 0) | (kpos >= S), NEG_INF, 0.0)


def _group_exp(lhs, kw, bias_ref, sink_ref, kh, edge):
    s = _dot_nt(lhs, kw) + bias_ref[4 * kh:4 * kh + 4].reshape(GROUP_ROWS, WIN)
    if edge is not None:
        s = s + edge
    sk = jnp.concatenate(
        [jnp.broadcast_to(sink_ref[4 * kh + r:4 * kh + r + 1, 0:1], (BLOCK, 1)) for r in range(4)], axis=0)
    m = jnp.maximum(jnp.max(s, axis=-1, keepdims=True), sk)
    return jnp.exp(s - m), jnp.exp(sk - m)


def _attn_fwd(qkv, qg, kg, sinkb, bias, name, ex=NO_EXCHANGE):
    S = qkv.shape[0]

    def body(q_ref, kvp_ref, kvc_ref, kvn_ref, qg_ref, kg_ref, sink_ref, bias_ref, o_ref):
        i = pl.program_id(0)
        low = lax.broadcasted_iota(jnp.int32, (1, 128), 1) < HEAD_DIM
        ones = jnp.ones((WIN, 128), BF16)
        kdup, vdup = _kv_windows(kvp_ref, kvc_ref, kvn_ref, kg_ref, low)
        for t in range(QB):
            edge = _edge_bias(i, t, S) if t in (0, QB - 1) else None
            rows = slice(t * BLOCK, (t + 1) * BLOCK)
            for kh in range(KV_HEADS):
                _, _, lhs = _stacked_q(q_ref, qg_ref, t, kh, low)
                kw = kdup[kh][t * BLOCK:t * BLOCK + WIN]
                vw = vdup[kh][t * BLOCK:t * BLOCK + WIN]
                e, es = _group_exp(lhs, kw, bias_ref, sink_ref, kh, edge)
                eb = _bf(e)
                ov = _dot(eb, vw) * (1.0 / (_dot(eb, ones) + es))
                oa, ob = _unstack_heads(ov, low)
                o_ref[rows, 256 * kh:256 * kh + 128] = _bf(oa)
                o_ref[rows, 256 * kh + 128:256 * kh + 256] = _bf(ob)

    return _carrier_call(
        body, ex, _grid1_ends(S // TQ), name, out_shape=(jax.ShapeDtypeStruct((S, Q_DIM), BF16),),
        grid=(S // TQ,), in_specs=_attn_in_specs(S),
        out_specs=(pl.BlockSpec((TQ, Q_DIM), lambda i: (i, 0)),), scratch_shapes=[],
        args=(qkv, qkv, qkv, qkv, qg, kg, sinkb, bias))


def _attn_bwd(do, qkv, qg, kg, sinkb, bias, dbias_in, name, ex=NO_EXCHANGE):
    S = qkv.shape[0]
    nkb = S // BLOCK
    nsteps = S // TQ

    def body(do_ref, q_ref, kvp_ref, kvc_ref, kvn_ref, qg_ref, kg_ref, sink_ref, bias_ref, dbin_ref,
             dq_ref, dkp_ref, dvp_ref, dbias_ref, dsink_ref, dqg_ref, dqg_s):
        i = pl.program_id(0)

        @pl.when(i == 0)
        def _():
            dbias_ref[...] = dbin_ref[...]
            dsink_ref[...] = jnp.zeros_like(dsink_ref)
            dqg_s[...] = jnp.zeros_like(dqg_s)

        low = lax.broadcasted_iota(jnp.int32, (1, 128), 1) < HEAD_DIM
        own = ((lax.broadcasted_iota(jnp.int32, (GROUP_ROWS, 128), 1) >> 6) & 1) == (
            (lax.broadcasted_iota(jnp.int32, (GROUP_ROWS, 128), 0) >> 7) & 1)
        gq = qg_ref[...] * (HEAD_DIM ** -0.5)
        ones = jnp.ones((WIN, 128), BF16)
        kdup, vdup = _kv_windows(kvp_ref, kvc_ref, kvn_ref, kg_ref, low)
        for t in range(QB):
            edge = _edge_bias(i, t, S) if t in (0, QB - 1) else None
            rows = slice(t * BLOCK, (t + 1) * BLOCK)
            dk_dup, dv_dup = [], []
            for kh in range(KV_HEADS):
                qraw, rq, lhs = _stacked_q(q_ref, qg_ref, t, kh, low)
                dos = _bf(_stack_heads(do_ref, t, kh, low))
                kw = kdup[kh][t * BLOCK:t * BLOCK + WIN]
                vw = vdup[kh][t * BLOCK:t * BLOCK + WIN]
                e, es = _group_exp(lhs, kw, bias_ref, sink_ref, kh, edge)
                inv = 1.0 / (_dot(_bf(e), ones) + es)
                pr = e * jnp.concatenate([inv] * (WIN // 128), axis=1)
                dpr = _dot_nt(dos, vw)
                delta = jnp.sum(pr * dpr, axis=-1, keepdims=True)
                ds = pr * (dpr - delta)
                dbias_ref[4 * kh:4 * kh + 4] += ds.reshape(4, BLOCK, WIN)
                dsk = es * inv[:, 0:1] * delta
                for r in range(4):
                    dsink_ref[4 * kh + r:4 * kh + r + 1, :] -= jnp.broadcast_to(
                        jnp.sum(dsk[r * BLOCK:(r + 1) * BLOCK], axis=0, keepdims=True), (1, 128))
                dsb = _bf(ds)
                dqs = _dot(dsb, kw)
                qhat = qraw * rq
                dxhat = jnp.where(own, dqs, 0.0) * gq
                dq_st = rq * (dxhat - qhat * (jnp.sum(dxhat * qhat, axis=-1, keepdims=True) * (1.0 / HEAD_DIM)))
                dq_ref[rows, 256 * kh:256 * kh + 128] = dq_st[0:128] + dq_st[128:256]
                dq_ref[rows, 256 * kh + 128:256 * kh + 256] = dq_st[256:384] + dq_st[384:512]
                dqg_s[...] += jnp.sum((dqs * qhat).reshape(GROUP_ROWS // 8, 8, 128), axis=0)
                dkx = _dot_tn(dsb, lhs)
                dvx = _dot_tn(_bf(pr), dos)
                dk_dup.append(dkx + pltpu.roll(dkx, HEAD_DIM, 1))
                dv_dup.append(dvx + pltpu.roll(dvx, HEAD_DIM, 1))
            dkp_ref[t] = jnp.where(low, dk_dup[0], dk_dup[1])
            dvp_ref[t] = jnp.where(low, dv_dup[0], dv_dup[1])

        @pl.when(i == nsteps - 1)
        def _():
            acc = dqg_s[...] * (HEAD_DIM ** -0.5)
            acc = acc + pltpu.roll(acc, HEAD_DIM, 1)
            dqg_ref[...] = jnp.broadcast_to(jnp.sum(acc, axis=0, keepdims=True), (8, 128))

    const2 = lambda shape: pl.BlockSpec(shape, lambda i: (0,) * len(shape))
    part = pl.BlockSpec((QB, WIN, KV_DIM), lambda i: (i, 0, 0))
    return _carrier_call(
        body, ex, _grid1_ends(nsteps), name,
        out_shape=(jax.ShapeDtypeStruct((S, Q_DIM), F32), jax.ShapeDtypeStruct((nkb, WIN, KV_DIM), F32),
                   jax.ShapeDtypeStruct((nkb, WIN, KV_DIM), F32),
                   jax.ShapeDtypeStruct((N_HEADS, BLOCK, WIN), F32), jax.ShapeDtypeStruct((N_HEADS, 128), F32),
                   jax.ShapeDtypeStruct((8, 128), F32)),
        grid=(nsteps,),
        in_specs=[pl.BlockSpec((TQ, Q_DIM), lambda i: (i, 0))] + _attn_in_specs(S)
        + [const2((N_HEADS, BLOCK, WIN))],
        out_specs=(pl.BlockSpec((TQ, Q_DIM), lambda i: (i, 0)), part, part,
                   const2((N_HEADS, BLOCK, WIN)), const2((N_HEADS, 128)), const2((8, 128))),
        scratch_shapes=[pltpu.VMEM((8, 128), F32)],
        args=(do, qkv, qkv, qkv, qkv, qg, kg, sinkb, bias, dbias_in))


def _kv_fold(dkp, dvp, qkv, kg, name):
    nkb = dkp.shape[0]
    S = nkb * BLOCK
    nsteps = S // TQ

    def body(kp_p, kp_c, kp_n, vp_p, vp_c, vp_n, kv_ref, kg_ref, dkv_ref, dkg_ref, dkg_s):
        i = pl.program_id(0)

        @pl.when(i == 0)
        def _():
            dkg_s[...] = jnp.zeros_like(dkg_s)

        def fold(p_ref, c_ref, n_ref):
            blocks = []
            for t in range(QB):
                acc = c_ref[t, BLOCK:2 * BLOCK, :]
                if t > 0:
                    acc = acc + c_ref[t - 1, 2 * BLOCK:, :]
                else:
                    acc = acc + jnp.where(i > 0, p_ref[0, 2 * BLOCK:, :], 0.0)
                if t < QB - 1:
                    acc = acc + c_ref[t + 1, :BLOCK, :]
                else:
                    acc = acc + jnp.where(i < nsteps - 1, n_ref[0, :BLOCK, :], 0.0)
                blocks.append(acc)
            return jnp.concatenate(blocks, axis=0)

        dkn = fold(kp_p, kp_c, kp_n)
        dv = fold(vp_p, vp_c, vp_n)
        k = kv_ref[:, :KV_DIM]
        low = lax.broadcasted_iota(jnp.int32, (1, 128), 1) < HEAD_DIM
        rk = _half_rstd(k, low)
        khat = k * rk
        dxhat = dkn * kg_ref[...]
        prod = dxhat * khat
        z = jnp.zeros_like(prod)
        mean = jnp.where(low, jnp.sum(jnp.where(low, prod, z), axis=-1, keepdims=True),
                         jnp.sum(jnp.where(low, z, prod), axis=-1, keepdims=True)) * (1.0 / HEAD_DIM)
        dkv_ref[:, :KV_DIM] = rk * (dxhat - khat * mean)
        dkv_ref[:, KV_DIM:] = dv
        dkg_s[...] += jnp.sum((dkn * khat).reshape(TQ // 8, 8, KV_DIM), axis=0)

        @pl.when(i == nsteps - 1)
        def _():
            acc = dkg_s[...] + pltpu.roll(dkg_s[...], HEAD_DIM, 1)
            dkg_ref[...] = jnp.broadcast_to(jnp.sum(acc, axis=0, keepdims=True), (8, 128))

    prev = pl.BlockSpec((1, WIN, KV_DIM), lambda i: (jnp.maximum(i * QB - 1, 0), 0, 0))
    cur = pl.BlockSpec((QB, WIN, KV_DIM), lambda i: (i, 0, 0))
    nxt = pl.BlockSpec((1, WIN, KV_DIM), lambda i: (jnp.minimum(i * QB + QB, nkb - 1), 0, 0))
    return pl.pallas_call(
        body, name=name,
        out_shape=(jax.ShapeDtypeStruct((S, 2 * KV_DIM), F32), jax.ShapeDtypeStruct((8, 128), F32)),
        grid=(nsteps,),
        in_specs=[prev, cur, nxt, prev, cur, nxt,
                  pl.BlockSpec((TQ, 2 * KV_DIM), lambda i: (i, Q_DIM // (2 * KV_DIM))),
                  pl.BlockSpec((1, KV_DIM), lambda i: (0, 0))],
        out_specs=(pl.BlockSpec((TQ, 2 * KV_DIM), lambda i: (i, 0)), pl.BlockSpec((8, 128), lambda i: (0, 0))),
        scratch_shapes=[pltpu.VMEM((8, KV_DIM), F32)],
        compiler_params=_params(("arbitrary",)),
    )(dkp, dkp, dkp, dvp, dvp, dvp, qkv, kg)


BIAS_COLS = BLOCK * WIN
BIAS_CHUNK = 6144


def _bias_table(rel_bias_t, onehot, band):
    def body(rb_ref, oh_ref, band_ref, o_ref):
        o_ref[...] = _dot(rb_ref[...], oh_ref[...], HI) + band_ref[...]

    return pl.pallas_call(
        body, name="bias_table", out_shape=jax.ShapeDtypeStruct((N_HEADS, BIAS_COLS), F32),
        grid=(BIAS_COLS // BIAS_CHUNK,),
        in_specs=[pl.BlockSpec((N_HEADS, NUM_BUCKETS), lambda i: (0, 0)),
                  pl.BlockSpec((NUM_BUCKETS, BIAS_CHUNK), lambda i: (0, i)),
                  pl.BlockSpec((1, BIAS_CHUNK), lambda i: (0, i))],
        out_specs=pl.BlockSpec((N_HEADS, BIAS_CHUNK), lambda i: (0, i)),
        compiler_params=_params(("parallel",)),
    )(rel_bias_t, onehot, band)


def _bias_grad(dbias, onehot):
    def body(db_ref, oh_ref, o_ref):
        @pl.when(pl.program_id(0) == 0)
        def _():
            o_ref[...] = jnp.zeros_like(o_ref)

        o_ref[...] += lax.dot_general(db_ref[...], oh_ref[...], (((1,), (1,)), ((), ())),
                                      preferred_element_type=F32, precision=HI)

    return pl.pallas_call(
        body, name="bias_grad", out_shape=jax.ShapeDtypeStruct((N_HEADS, NUM_BUCKETS), F32),
        grid=(BIAS_COLS // BIAS_CHUNK,),
        in_specs=[pl.BlockSpec((N_HEADS, BIAS_CHUNK), lambda i: (0, i)),
                  pl.BlockSpec((NUM_BUCKETS, BIAS_CHUNK), lambda i: (0, i))],
        out_specs=pl.BlockSpec((N_HEADS, NUM_BUCKETS), lambda i: (0, 0)),
        compiler_params=_params(("arbitrary",)),
    )(dbias, onehot)


def _bucket_onehot():
    half = NUM_BUCKETS // 2
    max_exact = half // 2
    rel = jnp.arange(WIN)[None, :] - BLOCK - jnp.arange(BLOCK)[:, None]
    n = jnp.abs(rel)
    ret = jnp.where(rel > 0, half, 0)
    nf = jnp.maximum(n, 1).astype(F32)
    large = max_exact + (jnp.log(nf / max_exact) / np.log(MAX_DISTANCE / max_exact)
                         * (half - max_exact)).astype(jnp.int32)
    large = jnp.minimum(large, half - 1)
    bucket = (ret + jnp.where(n < max_exact, n, large)).reshape(1, BIAS_COLS)
    band = jnp.where(n <= BLOCK, 0.0, NEG_INF).astype(F32).reshape(1, BIAS_COLS)
    return (bucket == jnp.arange(NUM_BUCKETS)[:, None]).astype(F32), band


def _halo_specs(tm, width, S):
    r = tm // HALO
    last = S // HALO - 1
    return [pl.BlockSpec((HALO, width), lambda i: (jnp.maximum(i * r - 1, 0), 0)),
            pl.BlockSpec((tm, width), lambda i: (i, 0)),
            pl.BlockSpec((HALO, width), lambda i: (jnp.minimum(i * r + r, last), 0))]


def _with_halo(p_ref, c_ref, n_ref):
    return jnp.concatenate([p_ref[...], c_ref[...], n_ref[...]], axis=0)


def _row_valid(i, tm, S):
    g = i * tm - HALO + lax.broadcasted_iota(jnp.int32, (tm + 2 * HALO, 1), 0)
    return (g >= 0) & (g < S)


def _shifted(x):
    n = x.shape[0]
    return [x if b == 0 else pltpu.roll(x, n - b, 0) for b in range(8)]


def _tap(sh, off, tm):
    a, b = off // 8, off % 8
    return sh[b][8 * a:8 * a + tm]


def _conv_fwd(cvg, cw, cb, lg, lb, tm, name, ex=NO_EXCHANGE):
    S = cvg.shape[0]

    def body(p_ref, c_ref, n_ref, cw_ref, cb_ref, lg_ref, lb_ref, act_ref, yc_ref):
        i = pl.program_id(0)
        z = _with_halo(p_ref, c_ref, n_ref)
        glu = jnp.where(_row_valid(i, tm, S), z[:, :CONV_DIM] * _sig(z[:, CONV_DIM:]), 0.0)
        sh = _shifted(glu)
        y = jnp.zeros((tm, CONV_DIM), F32) + cb_ref[...]
        for w in range(CONV_WIDTH):
            y = y + _tap(sh, w + 1, tm) * cw_ref[w:w + 1, :]
        yc_ref[...] = y
        mu = jnp.mean(y, axis=-1, keepdims=True)
        yc = y - mu
        rstd = lax.rsqrt(jnp.mean(yc * yc, axis=-1, keepdims=True) + 1e-5)
        ln = yc * rstd * lg_ref[...] + lb_ref[...]
        act_ref[...] = _bf(ln * _sig(ln))

    vec = pl.BlockSpec((1, CONV_DIM), lambda i: (0, 0))
    row = pl.BlockSpec((tm, CONV_DIM), lambda i: (i, 0))
    return _carrier_call(
        body, ex, _grid_ends(S // tm), name,
        out_shape=(jax.ShapeDtypeStruct((S, CONV_DIM), BF16), jax.ShapeDtypeStruct((S, CONV_DIM), F32)),
        grid=(S // tm,),
        in_specs=_halo_specs(tm, 2 * CONV_DIM, S) + [pl.BlockSpec((32, CONV_DIM), lambda i: (0, 0)), vec, vec, vec],
        out_specs=(row, row), scratch_shapes=[], args=(cvg, cvg, cvg, cw, cb, lg, lb))


def _conv_bwd(dact, yconv, cvg, cw, lg, lb, tm, name, ex=NO_EXCHANGE):
    S = cvg.shape[0]
    nsteps = S // tm

    def body(dp, dc, dn, yp, yc_, yn, zp, zc, zn, cw_ref, lg_ref, lb_ref,
             dz_ref, dcw_ref, dvec_ref, dcw_s, dvec_s, shg_s, shd_s):
        i = pl.program_id(0)

        @pl.when(i == 0)
        def _():
            dcw_s[...] = jnp.zeros_like(dcw_s)
            dvec_s[...] = jnp.zeros_like(dvec_s)

        valid = _row_valid(i, tm, S)
        own = (lax.broadcasted_iota(jnp.int32, (tm + 2 * HALO, 1), 0) >= HALO) & (
            lax.broadcasted_iota(jnp.int32, (tm + 2 * HALO, 1), 0) < HALO + tm)
        y = _with_halo(yp, yc_, yn)
        dact_ = _with_halo(dp, dc, dn)
        mu = jnp.mean(y, axis=-1, keepdims=True)
        ycen = y - mu
        rstd = lax.rsqrt(jnp.mean(ycen * ycen, axis=-1, keepdims=True) + 1e-5)
        yhat = ycen * rstd
        ln = yhat * lg_ref[...] + lb_ref[...]
        sg = _sig(ln)
        dln = dact_ * (sg * (1.0 + ln * (1.0 - sg)))
        dyhat = dln * lg_ref[...]
        dy = rstd * (dyhat - jnp.mean(dyhat, axis=-1, keepdims=True)
                     - yhat * jnp.mean(dyhat * yhat, axis=-1, keepdims=True))
        dy = jnp.where(valid, dy, 0.0)
        dln_own = jnp.where(own, dln, 0.0)
        nr = (tm + 2 * HALO) // 8
        dvec_s[0] += jnp.sum(jnp.where(own, dy, 0.0).reshape(nr, 8, CONV_DIM), axis=0)
        dvec_s[1] += jnp.sum((dln_own * yhat).reshape(nr, 8, CONV_DIM), axis=0)
        dvec_s[2] += jnp.sum(dln_own.reshape(nr, 8, CONV_DIM), axis=0)
        z = _with_halo(zp, zc, zn)
        glu = jnp.where(valid, z[:, :CONV_DIM] * _sig(z[:, CONV_DIM:]), 0.0)
        for b, (g_b, d_b) in enumerate(zip(_shifted(glu), _shifted(dy))):
            shg_s[b] = g_b
            shd_s[b] = d_b
        for cb in range(CONV_DIM // 128):
            lanes = slice(128 * cb, 128 * (cb + 1))
            for rb in range(tm // CROWS):
                r0 = rb * CROWS
                dy_own = shd_s[0, HALO + r0:HALO + r0 + CROWS, lanes]
                dglu = jnp.zeros((CROWS, 128), F32)
                for w in range(CONV_WIDTH):
                    a, b = divmod(CONV_WIDTH - w, 8)
                    dglu = dglu + shd_s[b, 8 * a + r0:8 * a + r0 + CROWS, lanes] * cw_ref[w:w + 1, lanes]
                    a, b = divmod(w + 1, 8)
                    prod = dy_own * shg_s[b, 8 * a + r0:8 * a + r0 + CROWS, lanes]
                    dcw_s[w, :, lanes] += jnp.sum(prod.reshape(CROWS // 8, 8, 128), axis=0)
                cv = zc[r0:r0 + CROWS, lanes]
                sg_o = _sig(zc[r0:r0 + CROWS, CONV_DIM + 128 * cb:CONV_DIM + 128 * (cb + 1)])
                dz_ref[r0:r0 + CROWS, lanes] = dglu * sg_o
                dz_ref[r0:r0 + CROWS, CONV_DIM + 128 * cb:CONV_DIM + 128 * (cb + 1)] = (
                    dglu * cv * sg_o * (1.0 - sg_o))

        @pl.when(i == nsteps - 1)
        def _():
            dcw_ref[...] = jnp.sum(dcw_s[...], axis=1)
            dvec_ref[...] = jnp.sum(dvec_s[...], axis=1)

    vec = pl.BlockSpec((1, CONV_DIM), lambda i: (0, 0))
    return _carrier_call(
        body, ex, _grid_ends(nsteps), name,
        out_shape=(jax.ShapeDtypeStruct((S, 2 * CONV_DIM), F32), jax.ShapeDtypeStruct((32, CONV_DIM), F32),
                   jax.ShapeDtypeStruct((8, CONV_DIM), F32)),
        grid=(nsteps,),
        in_specs=_halo_specs(tm, CONV_DIM, S) + _halo_specs(tm, CONV_DIM, S) + _halo_specs(tm, 2 * CONV_DIM, S)
        + [pl.BlockSpec((32, CONV_DIM), lambda i: (0, 0)), vec, vec],
        out_specs=(pl.BlockSpec((tm, 2 * CONV_DIM), lambda i: (i, 0)),
                   pl.BlockSpec((32, CONV_DIM), lambda i: (0, 0)), pl.BlockSpec((8, CONV_DIM), lambda i: (0, 0))),
        scratch_shapes=[pltpu.VMEM((32, 8, CONV_DIM), F32), pltpu.VMEM((8, 8, CONV_DIM), F32),
                        pltpu.VMEM((8, tm + 2 * HALO, CONV_DIM), F32), pltpu.VMEM((8, tm + 2 * HALO, CONV_DIM), F32)],
        args=(dact, dact, dact, yconv, yconv, yconv, cvg, cvg, cvg, cw, lg, lb))


def _merge_parts(un, o, cact, win_ref, wao_ref, wco_ref):
    g = _dot(un, win_ref[:, QC_DIM:])
    ga, gc = _sig(g[:, :D_MODEL]), _sig(g[:, D_MODEL:])
    ya = _dot(o, wao_ref[...])
    yc = _dot(cact, wco_ref[...])
    return ga, gc, ya, yc


def _merge_specs(tm):
    row = lambda w: pl.BlockSpec((tm, w), lambda i: (i, 0))
    full = lambda a, b: pl.BlockSpec((a, b), lambda i: (0, 0))
    weights = [full(D_MODEL, IN_DIM), full(Q_DIM, D_MODEL), full(CONV_DIM, D_MODEL), full(D_MODEL, D_MODEL)]
    return row, weights


def _merge_fwd(h1, un, o, cact, w_g, w_ao, w_co, w_o, tm, name, ex=NO_EXCHANGE):
    S = h1.shape[0]
    row, weights = _merge_specs(tm)

    def body(h1_ref, un_ref, o_ref, c_ref, wg_ref, wao_ref, wco_ref, wo_ref, h2_ref):
        ga, gc, ya, yc = _merge_parts(un_ref[...], o_ref[...], c_ref[...], wg_ref, wao_ref, wco_ref)
        h2_ref[...] = h1_ref[...] + _dot(_bf(ga * ya + gc * yc), wo_ref[...])

    return _carrier_call(
        body, ex, _grid_ends(S // tm), name, out_shape=(jax.ShapeDtypeStruct((S, D_MODEL), F32),),
        grid=(S // tm,),
        in_specs=[row(D_MODEL), row(D_MODEL), row(Q_DIM), row(CONV_DIM)] + weights,
        out_specs=(row(D_MODEL),), scratch_shapes=[], args=(h1, un, o, cact, w_g, w_ao, w_co, w_o))


def _merge_bwd(dh2, un, o, cact, w_g, w_ao, w_co, w_o, tm, name):
    S = dh2.shape[0]
    row, weights = _merge_specs(tm)

    def body(dh2_ref, un_ref, o_ref, c_ref, wg_ref, wao_ref, wco_ref, wo_ref,
             do_ref, dc_ref, mix_ref, dya_ref, dyc_ref, dgp_ref):
        ga, gc, ya, yc = _merge_parts(un_ref[...], o_ref[...], c_ref[...], wg_ref, wao_ref, wco_ref)
        mix_ref[...] = _bf(ga * ya + gc * yc)
        dmix = _dot_nt(_bf(dh2_ref[...]), wo_ref[...])
        dya = _bf(dmix * ga)
        dyc = _bf(dmix * gc)
        dya_ref[...] = dya
        dyc_ref[...] = dyc
        dgp_ref[:, :D_MODEL] = _bf(dmix * ya * ga * (1.0 - ga))
        dgp_ref[:, D_MODEL:] = _bf(dmix * yc * gc * (1.0 - gc))
        do_ref[...] = _dot_nt(dya, wao_ref[...])
        dc_ref[...] = _dot_nt(dyc, wco_ref[...])

    return pl.pallas_call(
        body, name=name,
        out_shape=(jax.ShapeDtypeStruct((S, Q_DIM), F32), jax.ShapeDtypeStruct((S, CONV_DIM), F32),
                   jax.ShapeDtypeStruct((S, D_MODEL), BF16), jax.ShapeDtypeStruct((S, D_MODEL), BF16),
                   jax.ShapeDtypeStruct((S, D_MODEL), BF16), jax.ShapeDtypeStruct((S, 2 * D_MODEL), BF16)),
        grid=(S // tm,),
        in_specs=[row(D_MODEL), row(D_MODEL), row(Q_DIM), row(CONV_DIM)] + weights,
        out_specs=(row(Q_DIM), row(CONV_DIM), row(D_MODEL), row(D_MODEL), row(D_MODEL), row(2 * D_MODEL)),
        compiler_params=_params(("parallel",)),
    )(dh2, un, o, cact, w_g, w_ao, w_co, w_o)


def _pe_specs(tm, layer):
    row = pl.BlockSpec((tm, D_MODEL), lambda i: (i, 0))
    vec = pl.BlockSpec((1, D_MODEL), lambda i: (0, 0))
    p_s = pl.BlockSpec((None, None, tm, 256), lambda i: (layer, 0, i, 0))
    wpp = pl.BlockSpec((256, D_MODEL), lambda i: (0, 0))
    wpg = pl.BlockSpec((D_MODEL, D_MODEL), lambda i: (0, 0))
    return row, vec, p_s, wpp, wpg


def _pe_fwd(h, gamma, p, layer, w_pp, w_pg, tm, name):
    S = h.shape[0]
    row, vec, p_s, wpp, wpg = _pe_specs(tm, layer)

    def body(h_ref, g_ref, p_ref, wpp_ref, wpg_ref, x_ref, hn_ref):
        hn = _bf(_rms_fwd(h_ref[...], g_ref[...])[0])
        hn_ref[...] = hn
        gate = _sig(_dot(hn, wpg_ref[...]))
        x_ref[...] = h_ref[...] + _dot(_bf(p_ref[...]), wpp_ref[...]) * gate

    return pl.pallas_call(
        body, name=name,
        out_shape=(jax.ShapeDtypeStruct((S, D_MODEL), F32), jax.ShapeDtypeStruct((S, D_MODEL), BF16)),
        grid=(S // tm,), in_specs=[row, vec, p_s, wpp, wpg], out_specs=(row, row),
        compiler_params=_params(("parallel",)),
    )(h, gamma, p, w_pp, w_pg)


def _pe_bwd(dx, h, gamma, hn, p, layer, w_pp, w_pg, tm, name):
    S = h.shape[0]
    row, vec, p_s, wpp, wpg = _pe_specs(tm, layer)

    def body(dx_ref, h_ref, g_ref, hn_ref, p_ref, wpp_ref, wpg_ref, dh_ref, dgp_ref, dpr_ref, dgam_ref):
        @pl.when(pl.program_id(0) == 0)
        def _():
            dgam_ref[...] = jnp.zeros_like(dgam_ref)

        for r in range(tm // FFN_PART_ROWS):
            rows = slice(r * FFN_PART_ROWS, (r + 1) * FFN_PART_ROWS)
            dxv = dx_ref[rows, :]
            gate = _sig(_dot(hn_ref[rows, :], wpg_ref[...]))
            proj = _dot(_bf(p_ref[rows, :]), wpp_ref[...])
            dpr_ref[rows, :] = _bf(dxv * gate)
            dgp = _bf(dxv * proj * gate * (1.0 - gate))
            dgp_ref[rows, :] = dgp
            dxn, dgam = _rms_bwd(_dot_nt(dgp, wpg_ref[...]), h_ref[rows, :], g_ref[...])
            dh_ref[rows, :] = dxv + dxn
            dgam_ref[...] += jnp.sum(dgam, axis=0, keepdims=True)

    return pl.pallas_call(
        body, name=name,
        out_shape=(jax.ShapeDtypeStruct((S, D_MODEL), F32), jax.ShapeDtypeStruct((S, D_MODEL), BF16),
                   jax.ShapeDtypeStruct((S, D_MODEL), BF16), jax.ShapeDtypeStruct((1, D_MODEL), F32)),
        grid=(S // tm,), in_specs=[row, row, vec, row, p_s, wpp, wpg], out_specs=(row, row, row, vec),
        compiler_params=_params(("arbitrary",)),
    )(dx, h, gamma, hn, p, w_pp, w_pg)


def _loss_head(y, target, tm):
    S = y.shape[0]

    def body(y_ref, t_ref, dy_ref, l_ref):
        @pl.when(pl.program_id(0) == 0)
        def _():
            l_ref[...] = jnp.zeros_like(l_ref)

        diff = y_ref[...] - t_ref[...]
        dy_ref[...] = diff * (1.0 / D_MODEL)
        sq = jnp.sum((diff * diff).reshape(tm // 8, 8, D_MODEL), axis=0)
        part = sq[:, 0:128]
        for k in range(1, D_MODEL // 128):
            part = part + sq[:, 128 * k:128 * (k + 1)]
        l_ref[...] += part

    row = pl.BlockSpec((tm, D_MODEL), lambda i: (i, 0))
    return pl.pallas_call(
        body, name="loss_head",
        out_shape=(jax.ShapeDtypeStruct((S, D_MODEL), F32), jax.ShapeDtypeStruct((8, 128), F32)),
        grid=(S // tm,), in_specs=[row, row], out_specs=(row, pl.BlockSpec((8, 128), lambda i: (0, 0))),
        compiler_params=_params(("arbitrary",)),
    )(y, target)


def _adamw(parts, w, m, v, name, ex=NO_EXCHANGE):
    nl = len(parts)
    R, C = w.shape
    K = R // nl
    tr = next((c for c in range(min(K, 256) // 16 * 16, 15, -16) if K % c == 0), K)
    nk = K // tr

    def body(*refs):
        p_refs = refs[:nl]
        w_ref, m_ref, v_ref, g_ref, d_ref, nm_ref, nv_ref = refs[nl:]
        for lyr in range(nl):
            @pl.when(pl.program_id(0) == lyr)
            def _(p_ref=p_refs[lyr]):
                g = p_ref[0].astype(F32)
                for k in range(1, N_DEV):
                    g = g + p_ref[k].astype(F32)
                g_ref[...] = g
                nm = ADAM_B1 * m_ref[...] + (1.0 - ADAM_B1) * g
                nv = ADAM_B2 * v_ref[...] + (1.0 - ADAM_B2) * (g * g)
                nm_ref[...] = nm
                nv_ref[...] = nv
                m_hat = nm / (1.0 - ADAM_B1 ** ADAM_STEP)
                v_hat = nv / (1.0 - ADAM_B2 ** ADAM_STEP)
                d_ref[...] = -ADAM_LR * (m_hat / (jnp.sqrt(v_hat) + ADAM_EPS) + ADAM_WD * w_ref[...])

    def part_spec(lyr):
        return pl.BlockSpec((N_DEV, tr, C), lambda l, i: (0, jnp.where(l == lyr, i, jnp.where(l < lyr, 0, nk - 1)), 0))

    blk = pl.BlockSpec((tr, C), lambda l, i: (l * nk + i, 0))
    out = jax.ShapeDtypeStruct((R, C), F32)
    return _carrier_call(
        body, ex, _grid_ends(nl, nk), name, out_shape=(out, out, out, out), grid=(nl, nk),
        in_specs=[part_spec(lyr) for lyr in range(nl)] + [blk, blk, blk],
        out_specs=(blk, blk, blk, blk), scratch_shapes=[], args=(*parts, w, m, v))


SHARDED = ("w_ffn1_in", "w_ffn1_out", "w_in", "conv_w", "w_attn_out", "w_conv_out", "w_o",
           "w_ffn2_in", "w_ffn2_out", "w_pe_gate", "w_pe_proj")
COL_SHARDED = ("w_ffn1_in", "w_in", "conv_w", "w_attn_out", "w_conv_out", "w_ffn2_in", "w_pe_proj")
SMALL = ("rel_bias", "norm_ffn1", "norm_mix", "q_norm", "k_norm", "sink", "conv_b", "conv_ln_g", "conv_ln_b",
         "norm_ffn2", "norm_pe")
WEIGHTS = ("rel_bias", "norm_ffn1", "w_ffn1_in", "w_ffn1_out", "norm_mix", "w_in", "q_norm", "k_norm", "sink",
           "conv_w", "conv_b", "conv_ln_g", "conv_ln_b", "w_attn_out", "w_conv_out", "w_o", "norm_ffn2",
           "w_ffn2_in", "w_ffn2_out", "norm_pe", "w_pe_gate", "w_pe_proj")


def _natural(g):
    k, n = g.shape[1], g.shape[2]
    return jnp.transpose(g, (1, 0, 2)).reshape(k, N_DEV * n)


def _blocked(w):
    k, n = w.shape[0], w.shape[1] // N_DEV
    return jnp.transpose(w.reshape(k, N_DEV, n), (1, 0, 2))


def kernel(x, p, rel_bias, norm_ffn1, w_ffn1_in, w_ffn1_out, norm_mix, w_in, q_norm, k_norm, sink, conv_w, conv_b, conv_ln_g, conv_ln_b, w_attn_out, w_conv_out, w_o, norm_ffn2, w_ffn2_in, w_ffn2_out, norm_pe, w_pe_gate, w_pe_proj, loss_target, m_rel_bias, m_norm_ffn1, m_w_ffn1_in, m_w_ffn1_out, m_norm_mix, m_w_in, m_q_norm, m_k_norm, m_sink, m_conv_w, m_conv_b, m_conv_ln_g, m_conv_ln_b, m_w_attn_out, m_w_conv_out, m_w_o, m_norm_ffn2, m_w_ffn2_in, m_w_ffn2_out, m_norm_pe, m_w_pe_gate, m_w_pe_proj, v_rel_bias, v_norm_ffn1, v_w_ffn1_in, v_w_ffn1_out, v_norm_mix, v_w_in, v_q_norm, v_k_norm, v_sink, v_conv_w, v_conv_b, v_conv_ln_g, v_conv_ln_b, v_w_attn_out, v_w_conv_out, v_w_o, v_norm_ffn2, v_w_ffn2_in, v_w_ffn2_out, v_norm_pe, v_w_pe_gate, v_w_pe_proj):
    W = dict(rel_bias=rel_bias, norm_ffn1=norm_ffn1, w_ffn1_in=w_ffn1_in, w_ffn1_out=w_ffn1_out, norm_mix=norm_mix,
             w_in=w_in, q_norm=q_norm, k_norm=k_norm, sink=sink, conv_w=conv_w, conv_b=conv_b, conv_ln_g=conv_ln_g,
             conv_ln_b=conv_ln_b, w_attn_out=w_attn_out, w_conv_out=w_conv_out, w_o=w_o, norm_ffn2=norm_ffn2,
             w_ffn2_in=w_ffn2_in, w_ffn2_out=w_ffn2_out, norm_pe=norm_pe, w_pe_gate=w_pe_gate, w_pe_proj=w_pe_proj)
    M = dict(rel_bias=m_rel_bias, norm_ffn1=m_norm_ffn1, w_ffn1_in=m_w_ffn1_in, w_ffn1_out=m_w_ffn1_out,
             norm_mix=m_norm_mix, w_in=m_w_in, q_norm=m_q_norm, k_norm=m_k_norm, sink=m_sink, conv_w=m_conv_w,
             conv_b=m_conv_b, conv_ln_g=m_conv_ln_g, conv_ln_b=m_conv_ln_b, w_attn_out=m_w_attn_out,
             w_conv_out=m_w_conv_out, w_o=m_w_o, norm_ffn2=m_norm_ffn2, w_ffn2_in=m_w_ffn2_in,
             w_ffn2_out=m_w_ffn2_out, norm_pe=m_norm_pe, w_pe_gate=m_w_pe_gate, w_pe_proj=m_w_pe_proj)
    V = dict(rel_bias=v_rel_bias, norm_ffn1=v_norm_ffn1, w_ffn1_in=v_w_ffn1_in, w_ffn1_out=v_w_ffn1_out,
             norm_mix=v_norm_mix, w_in=v_w_in, q_norm=v_q_norm, k_norm=v_k_norm, sink=v_sink, conv_w=v_conv_w,
             conv_b=v_conv_b, conv_ln_g=v_conv_ln_g, conv_ln_b=v_conv_ln_b, w_attn_out=v_w_attn_out,
             w_conv_out=v_w_conv_out, w_o=v_w_o, norm_ffn2=v_norm_ffn2, w_ffn2_in=v_w_ffn2_in,
             w_ffn2_out=v_w_ffn2_out, norm_pe=v_norm_pe, w_pe_gate=v_w_pe_gate, w_pe_proj=v_w_pe_proj)

    L = w_in.shape[0]
    S = x.shape[1]
    tm = min(512, S)
    tm_ffn = min(1024, S)
    xs = x[0]
    target = loss_target[0]
    vec = lambda a: a.reshape(1, -1)

    half, full = {}, {}

    def carried(stage1_items, stage2_items):
        s1 = [it for it in stage1_items if it[1] < L]
        s2 = [it for it in stage2_items if it[1] < L]
        ex = _stage1([W[n][l] if n == "conv_w" else W[n][l].astype(BF16) for n, l in s1]) + _stage2(
            [half.pop(it) for it in s2])
        return ex, s1, s2

    def landed(got, s1, s2):
        half.update(zip(s1, got[:len(s1)]))
        full.update(zip(s2, got[len(s1):]))

    onehot, band = _bucket_onehot()
    bias = _bias_table(rel_bias.T, onehot, band).reshape(N_HEADS, BLOCK, WIN)

    layers, saved = [], []
    h = xs
    ex, s1, s2 = carried([("w_ffn1_in", 0), ("w_ffn1_out", 0), ("w_in", 0), ("conv_w", 0)], [])
    landed(_exchange(ex, "allgather_first"), s1, s2)
    ex, s1, s2 = carried([], [("w_ffn1_in", 0), ("w_ffn1_out", 0)])
    landed(_exchange(ex, "allgather_relay"), s1, s2)
    for l in range(L):
        sv = dict(x0=h)
        G = dict(wi1=full.pop(("w_ffn1_in", l)), wo1=full.pop(("w_ffn1_out", l)))
        ex, s1, s2 = carried(
            [(n, l) for n in ("w_ffn2_in", "w_ffn2_out", "w_attn_out", "w_conv_out", "w_o")],
            [("w_in", l), ("conv_w", l)])
        (h1, sv["xn1"], sv["g1"], sv["u1"]), got = _ffn_fwd(
            h, vec(norm_ffn1[l]), G["wi1"], G["wo1"], tm_ffn, "ffn1_fwd", ex)
        landed(got, s1, s2)
        G.update(w_in=_natural(full.pop(("w_in", l))),
                 conv_w=jnp.pad(_natural(full.pop(("conv_w", l))), ((0, 1), (0, 0))))
        sv["h1"] = h1
        sv["un"], sv["qkv"], sv["cvg"] = _mixin_fwd(h1, vec(norm_mix[l]), G["w_in"], tm, "mixin_fwd")
        sv["qg"] = vec(jnp.tile(q_norm[l], 2))
        sv["kg"] = vec(jnp.tile(k_norm[l], KV_HEADS))
        sv["sinkb"] = jnp.broadcast_to(sink[l][:, None], (N_HEADS, 128))
        ex, s1, s2 = carried(
            [("w_pe_gate", l), ("w_pe_proj", l), ("w_ffn1_in", l + 1)],
            [(n, l) for n in ("w_ffn2_in", "w_ffn2_out", "w_attn_out", "w_conv_out", "w_o")])
        (sv["o"],), got = _attn_fwd(sv["qkv"], sv["qg"], sv["kg"], sv["sinkb"], bias, "attn_fwd", ex)
        landed(got, s1, s2)
        G.update(wi2=full.pop(("w_ffn2_in", l)), wo2=full.pop(("w_ffn2_out", l)),
                 w_ao=_natural(full.pop(("w_attn_out", l))), w_co=_natural(full.pop(("w_conv_out", l))),
                 w_o=full.pop(("w_o", l)).reshape(D_MODEL, D_MODEL))
        ex, s1, s2 = carried([("w_ffn1_out", l + 1)],
                             [("w_pe_gate", l), ("w_pe_proj", l), ("w_ffn1_in", l + 1)])
        (sv["cact"], sv["yconv"]), got = _conv_fwd(
            sv["cvg"], G["conv_w"], vec(conv_b[l]), vec(conv_ln_g[l]), vec(conv_ln_b[l]), tm, "conv_fwd", ex)
        landed(got, s1, s2)
        G.update(w_pg=full.pop(("w_pe_gate", l)).reshape(D_MODEL, D_MODEL),
                 w_pp=_natural(full.pop(("w_pe_proj", l))))
        ex, s1, s2 = carried([("w_in", l + 1), ("conv_w", l + 1)], [("w_ffn1_out", l + 1)])
        (h2,), got = _merge_fwd(h1, sv["un"], sv["o"], sv["cact"], G["w_in"], G["w_ao"], G["w_co"], G["w_o"], tm,
                                "merge_fwd", ex)
        landed(got, s1, s2)
        sv["h2"] = h2
        (h3, sv["xn2"], sv["g2"], sv["u2"]), _ = _ffn_fwd(
            h2, vec(norm_ffn2[l]), G["wi2"], G["wo2"], tm_ffn, "ffn2_fwd")
        sv["h3"] = h3
        h, sv["hn"] = _pe_fwd(h3, vec(norm_pe[l]), p, l, G["w_pp"], G["w_pg"], tm, "pe_fwd")
        layers.append(G)
        saved.append(sv)

    dh, lparts = _loss_head(h, target, tm)
    loss = lax.psum((0.5 / D_MODEL) * jnp.sum(lparts), AXES)

    dbias = jnp.zeros((N_HEADS, BLOCK, WIN), F32)
    small_g = {n: [None] * L for n in SMALL if n != "rel_bias"}
    recv = {n: [None] * L for n in SHARDED}

    def keep(names, l, got):
        for n, r in zip(names, got):
            recv[n][l] = r

    pending = None
    for l in reversed(range(L)):
        G, sv = layers[l], saved[l]
        dh3, dgp_pe, dproj, dg_pe = _pe_bwd(dh, sv["h3"], vec(norm_pe[l]), sv["hn"], p, l, G["w_pp"], G["w_pg"],
                                            tm, "pe_bwd")
        gw_pg = _matmul_tn(sv["hn"][None], dgp_pe[None], 1, "dw_pe_gate")
        gw_pp = _matmul_tn(p.reshape(L, S, p.shape[-1]), dproj[None], 1, "dw_pe_proj", a_index=l)
        (dh2, a2, dgu2, dg_n2), got = _ffn_bwd(
            dh3, sv["h2"], vec(norm_ffn2[l]), sv["g2"], sv["u2"], G["wi2"], G["wo2"], tm_ffn, "ffn2_bwd",
            _Exchange(pending, False) if pending else NO_EXCHANGE)
        if pending:
            keep(("w_ffn1_in",), l + 1, got)
        gwo2 = _matmul_tn(a2, dh3[None], FF_BLOCKS, "dw_ffn2_out", scale=0.5)
        gwi2 = _matmul_tn(sv["xn2"][None], dgu2.reshape(2 * FF_BLOCKS, S, FF_SHARD), 2 * FF_BLOCKS, "dw_ffn2_in",
                          ts=TS_BF16)
        do, dcact, mix, dya, dyc, dgpre = _merge_bwd(dh2, sv["un"], sv["o"], sv["cact"], G["w_in"], G["w_ao"],
                                                     G["w_co"], G["w_o"], tm, "merge_bwd")
        gw_o = _matmul_tn(mix[None], dh2[None], 1, "dw_o")
        gw_ao = _matmul_tn(sv["o"][None], dya[None], 1, "dw_attn_out")
        gw_co = _matmul_tn(sv["cact"][None], dyc[None], 1, "dw_conv_out")
        (dq, dkp, dvp, dbias, dsink, dqg), got = _attn_bwd(
            do, sv["qkv"], sv["qg"], sv["kg"], sv["sinkb"], bias, dbias, "attn_bwd",
            _Exchange([gwi2, gw_pg.reshape(N_DEV, D_MODEL // N_DEV, D_MODEL), _blocked(gw_pp[0])], False))
        keep(("w_ffn2_in", "w_pe_gate", "w_pe_proj"), l, got)
        dkv, dkg = _kv_fold(dkp, dvp, sv["qkv"], sv["kg"], "kv_fold")
        (dcvg, dcw, dcvec), got = _conv_bwd(
            dcact, sv["yconv"], sv["cvg"], G["conv_w"], vec(conv_ln_g[l]), vec(conv_ln_b[l]), tm, "conv_bwd",
            _Exchange([gwo2.reshape(N_DEV, FF_SHARD // 2, D_MODEL)], False))
        keep(("w_ffn2_out",), l, got)
        dh1, dg_mix, dz = _mixin_bwd(dh2, sv["h1"], vec(norm_mix[l]), dq, dkv, dcvg, dgpre, G["w_in"],
                                     tm, "mixin_bwd")
        gw_in = _matmul_tn(sv["un"][None], dz[None], 1, "dw_in")[0]
        mid_send = [_blocked(gw_in), _blocked(dcw[:CONV_WIDTH]), _blocked(gw_ao[0]), _blocked(gw_co[0]),
                    gw_o.reshape(N_DEV, D_MODEL // N_DEV, D_MODEL)]
        (dh, a1, dgu1, dg_n1), got = _ffn_bwd(
            dh1, sv["x0"], vec(norm_ffn1[l]), sv["g1"], sv["u1"], G["wi1"], G["wo1"], tm_ffn, "ffn1_bwd",
            _Exchange(mid_send, False))
        keep(("w_in", "conv_w", "w_attn_out", "w_conv_out", "w_o"), l, got)
        gwo1 = _matmul_tn(a1, dh1[None], FF_BLOCKS, "dw_ffn1_out", scale=0.5)
        gwi1, got = _matmul_tn(sv["xn1"][None], dgu1.reshape(2 * FF_BLOCKS, S, FF_SHARD), 2 * FF_BLOCKS,
                               "dw_ffn1_in", ts=TS_BF16,
                               ex=_Exchange([gwo1.reshape(N_DEV, FF_SHARD // 2, D_MODEL)], False))
        keep(("w_ffn1_out",), l, got)
        pending = [gwi1]
        small_g["norm_ffn1"][l] = dg_n1[0]
        small_g["norm_mix"][l] = dg_mix[0]
        small_g["q_norm"][l] = dqg[0, :HEAD_DIM]
        small_g["k_norm"][l] = dkg[0, :HEAD_DIM]
        small_g["sink"][l] = dsink[:, 0]
        small_g["conv_b"][l] = dcvec[0]
        small_g["conv_ln_g"][l] = dcvec[1]
        small_g["conv_ln_b"][l] = dcvec[2]
        small_g["norm_ffn2"][l] = dg_n2[0]
        small_g["norm_pe"][l] = dg_pe[0]

    keep(("w_ffn1_in",), 0, _exchange(_Exchange(pending, False), "grad_exchange_last"))
    grad_x = dh[None]
    drb = _bias_grad(dbias.reshape(N_HEADS, BIAS_COLS), onehot).T

    res = {}
    for n in SHARDED:
        shp = W[n].shape
        rows, cols = shp[0] * shp[1], shp[2]
        parts = recv[n]
        if shp[1] % 8:
            parts = [jnp.stack(recv[n], axis=1).reshape(N_DEV, rows, cols)]
        res[n] = [o.reshape(shp) for o in _adamw(
            parts, W[n].reshape(rows, cols), M[n].reshape(rows, cols), V[n].reshape(rows, cols), "adamw_" + n)[0]]

    flat_g = jnp.concatenate([drb.reshape(-1)] + [jnp.stack(small_g[n]).reshape(-1) for n in SMALL[1:]])
    n_small = flat_g.shape[0]
    rows_s = -(-n_small // 1024 // 8) * 8
    pad = lambda a: jnp.pad(a, (0, rows_s * 1024 - n_small)).reshape(rows_s, 1024)
    flat = lambda d: pad(jnp.concatenate([d[n].reshape(-1) for n in SMALL]))
    (parts_s,) = _exchange(_Exchange([pad(flat_g)], True), "small_allgather")
    outs_s = _adamw([parts_s], flat(W), flat(M), flat(V), "adamw_small")[0]
    off = 0
    for n in SMALL:
        size = W[n].size
        res[n] = [o.reshape(-1)[off:off + size].reshape(W[n].shape) for o in outs_s]
        off += size

    out = [loss, grad_x]
    for k in range(4):
        out += [res[n][k] for n in WEIGHTS]
    return tuple(out)
```

```python
import functools

import jax
import jax.numpy as jnp
import numpy as np
from jax import lax
from jax.experimental import pallas as pl
from jax.experimental.pallas import tpu as pltpu

F32 = jnp.float32
BF16 = jnp.bfloat16
MESH_ID = pl.DeviceIdType.MESH
AXES = ("x", "y", "c")
N_DEV = 8

D_MODEL = 1024
N_HEADS = 8
KV_HEADS = 2
HEAD_DIM = 64
Q_DIM = 512
KV_DIM = 128
BLOCK = 128
WIN = 3 * BLOCK
NUM_BUCKETS = 32
MAX_DISTANCE = 128
CONV_DIM = 512
CONV_WIDTH = 31
D_FF = 2816
FF_SHARD = 2 * D_FF // N_DEV
FF_BLOCKS = D_FF // FF_SHARD
QC_DIM = Q_DIM + 2 * KV_DIM + 2 * CONV_DIM
IN_DIM = QC_DIM + 2 * D_MODEL
NEG_INF = -1e9
HALO = 16
CROWS = 64
FFN_PART_ROWS = 256
TS_BF16 = 4096

ADAM_LR = 0.001
ADAM_B1 = 0.9
ADAM_B2 = 0.999
ADAM_EPS = 1e-08
ADAM_WD = 0.01
ADAM_STEP = 10

VMEM_LIMIT = 56 * 1024 * 1024
HI = lax.Precision.HIGHEST


def _params(sem):
    return pltpu.CompilerParams(dimension_semantics=sem, vmem_limit_bytes=VMEM_LIMIT)


def _dot(a, b, precision=None):
    return jnp.dot(a, b, preferred_element_type=F32, precision=precision)


def _dot_nt(a, b):
    return lax.dot_general(a, b, (((1,), (1,)), ((), ())), preferred_element_type=F32)


def _dot_tn(a, b):
    return lax.dot_general(a, b, (((0,), (0,)), ((), ())), preferred_element_type=F32)


def _sig(x):
    return 1.0 / (1.0 + jnp.exp(-x))


def _bf(x):
    return x.astype(BF16)


def _rms_fwd(x, gamma):
    r = lax.rsqrt(jnp.mean(x * x, axis=-1, keepdims=True) + 1e-6)
    return x * r * gamma, r


def _rms_bwd(dy, x, gamma):
    r = lax.rsqrt(jnp.mean(x * x, axis=-1, keepdims=True) + 1e-6)
    xhat = x * r
    dxhat = dy * gamma
    dx = r * (dxhat - xhat * jnp.mean(dxhat * xhat, axis=-1, keepdims=True))
    return dx, dy * xhat


SAME_CORE = (2, 4, 6)


class _Exchange:
    def __init__(self, arrs=(), gather=True, kinds=None):
        self.arrs = list(arrs)
        self.n = n = len(self.arrs)
        self.kinds = list(kinds) if kinds is not None else ["gather" if gather else "scatter"] * n
        self.out_shape = tuple(
            jax.ShapeDtypeStruct(((N_DEV,) + a.shape) if k in ("gather", "stage1") else a.shape, a.dtype)
            for a, k in zip(self.arrs, self.kinds))
        self.aliases = {t: t for t, k in enumerate(self.kinds) if k == "stage2"}
        self.specs = [pl.BlockSpec(memory_space=pl.ANY)] * n
        self.scratch = [pltpu.SemaphoreType.DMA((7 * n,)), pltpu.SemaphoreType.DMA((7 * n,)),
                        pltpu.SemaphoreType.DMA((n,))] if n else []

    def __add__(self, other):
        return _Exchange(self.arrs + other.arrs, kinds=self.kinds + other.kinds)

    def _copies(self, ins, outs, sems):
        n = self.n
        send_sems, recv_sems, local_sems = sems
        x, y, c = lax.axis_index("x"), lax.axis_index("y"), lax.axis_index("c")
        me = 4 * x + 2 * y + c
        copies = []
        for t, kind in enumerate(self.kinds):
            if kind != "stage2":
                copies.append(pltpu.make_async_copy(
                    ins[t].at[me] if kind == "scatter" else ins[t], outs[t].at[me], local_sems.at[t]))
            offsets = {"gather": range(1, N_DEV), "scatter": range(1, N_DEV),
                       "stage1": (1,) + SAME_CORE, "stage2": SAME_CORE}[kind]
            for d in offsets:
                px = 1 - x if d & 4 else x
                py = 1 - y if d & 2 else y
                pc = 1 - c if d & 1 else c
                peer = 4 * px + 2 * py + pc
                if kind == "stage2":
                    src, dst, to = ins[t].at[peer], outs[t].at[peer], (x, y, 1 - c)
                else:
                    src, dst, to = (ins[t].at[peer] if kind == "scatter" else ins[t]), outs[t].at[me], (px, py, pc)
                k = (d - 1) * n + t
                copies.append(pltpu.make_async_remote_copy(
                    src_ref=src, dst_ref=dst, send_sem=send_sems.at[k], recv_sem=recv_sems.at[k],
                    device_id=to, device_id_type=MESH_ID))
        return copies

    def start(self, ins, outs, sems):
        for cp in self._copies(ins, outs, sems):
            cp.start()

    def wait(self, ins, outs, sems):
        for cp in self._copies(ins, outs, sems):
            cp.wait()


NO_EXCHANGE = _Exchange()


def _stage1(arrs):
    return _Exchange(arrs, kinds=["stage1"] * len(arrs))


def _stage2(arrs):
    return _Exchange(arrs, kinds=["stage2"] * len(arrs))


def _exchange(ex, name):
    n = ex.n

    def body(*refs):
        ins, outs, sems = refs[:n], refs[n:2 * n], refs[2 * n:]
        ex.start(ins, outs, sems)
        ex.wait(ins, outs, sems)

    return pl.pallas_call(
        body, name=name, out_shape=ex.out_shape, in_specs=ex.specs, out_specs=tuple(ex.specs),
        scratch_shapes=ex.scratch, input_output_aliases=ex.aliases,
    )(*ex.arrs)


def _carrier_call(body, ex, first_last, name, out_shape, grid, in_specs, out_specs, scratch_shapes, args):
    n_in, n_out, n_scr, n = len(in_specs), len(out_shape), len(scratch_shapes), ex.n

    def full(*refs):
        a, ci = refs[:n_in], refs[n_in:n_in + n]
        o = refs[n_in + n:n_in + n + n_out]
        co = refs[n_in + n + n_out:n_in + 2 * n + n_out]
        scr = refs[n_in + 2 * n + n_out:n_in + 2 * n + n_out + n_scr]
        sems = refs[n_in + 2 * n + n_out + n_scr:]
        first, last = first_last()
        if n:
            @pl.when(first)
            def _():
                ex.start(ci, co, sems)

        body(*a, *o, *scr)
        if n:
            @pl.when(last)
            def _():
                ex.wait(ci, co, sems)

    outs = pl.pallas_call(
        full, name=name, out_shape=tuple(out_shape) + ex.out_shape, grid=grid,
        in_specs=list(in_specs) + ex.specs, out_specs=tuple(out_specs) + tuple(ex.specs),
        scratch_shapes=list(scratch_shapes) + ex.scratch,
        input_output_aliases={n_in + t: n_out + u for t, u in ex.aliases.items()},
        compiler_params=_params(("arbitrary",) * len(grid)),
    )(*args, *ex.arrs)
    return outs[:n_out], outs[n_out:]


def _matmul_tn(a, b, nb, name, scale=1.0, out_dtype=BF16, ts=2048, ex=None, a_index=None):
    ba, S, K = a.shape
    bb, _, N = b.shape
    ts = min(ts, S)
    tn = N if N <= 1024 else next(c for c in (1280, 1024, 768, 512, 256) if N % c == 0)
    assert S % ts == 0
    ns = S // ts

    def body(a_ref, b_ref, o_ref, acc):
        s = pl.program_id(2)

        @pl.when(s == 0)
        def _():
            acc[...] = jnp.zeros_like(acc)

        acc[...] += _dot_tn(_bf(a_ref[...]), _bf(b_ref[...]))

        @pl.when(s == ns - 1)
        def _():
            o_ref[...] = (acc[...] * scale).astype(out_dtype)

    a_map = (lambda i, j, s: (a_index, s, 0)) if a_index is not None else (
        (lambda i, j, s: (i, s, 0)) if ba > 1 else (lambda i, j, s: (0, s, 0)))
    in_specs = [pl.BlockSpec((None, ts, K), a_map),
                pl.BlockSpec((None, ts, tn), (lambda i, j, s: (i, s, j)) if bb > 1 else (lambda i, j, s: (0, s, j)))]
    out_spec = pl.BlockSpec((None, K, tn), lambda i, j, s: (i, 0, j))
    out_shape = jax.ShapeDtypeStruct((nb, K, N), out_dtype)
    scratch = [pltpu.VMEM((K, tn), F32)]
    grid = (nb, N // tn, ns)
    if ex is not None:
        (out,), got = _carrier_call(body, ex, _grid_ends(*grid), name, out_shape=(out_shape,), grid=grid,
                                    in_specs=in_specs, out_specs=(out_spec,), scratch_shapes=scratch, args=(a, b))
        return out, got
    return pl.pallas_call(
        body, name=name, out_shape=out_shape, grid=grid, in_specs=in_specs, out_specs=out_spec,
        scratch_shapes=scratch, compiler_params=_params(("parallel", "parallel", "arbitrary")),
    )(a, b)


def _ffn_specs(tm):
    wg = pl.BlockSpec((None, D_MODEL, FF_SHARD), lambda i, j: (j, 0, 0))
    wu = pl.BlockSpec((None, D_MODEL, FF_SHARD), lambda i, j: (j + FF_BLOCKS, 0, 0))
    wo = pl.BlockSpec((2, FF_SHARD // 2, D_MODEL), lambda i, j: (j, 0, 0))
    row = pl.BlockSpec((tm, D_MODEL), lambda i, j: (i, 0))
    vec = pl.BlockSpec((1, D_MODEL), lambda i, j: (0, 0))
    hid = pl.BlockSpec((None, tm, FF_SHARD), lambda i, j: (j, i, 0))
    return wg, wu, wo, row, vec, hid


def _grid_ends(*grid):
    def first_last():
        first, last = None, None
        for d, n in enumerate(grid):
            i = pl.program_id(d)
            first = (i == 0) if first is None else first & (i == 0)
            last = (i == n - 1) if last is None else last & (i == n - 1)
        return first, last
    return first_last


def _grid2_ends(ni, nj):
    return _grid_ends(ni, nj)


def _grid1_ends(ni):
    return _grid_ends(ni)


def _ffn_fwd(x, gamma, wi, wo, tm, name, ex=NO_EXCHANGE):
    S = x.shape[0]
    wg_s, wu_s, wo_s, row, vec, hid = _ffn_specs(tm)

    def body(x_ref, g_ref, wg_ref, wu_ref, wo_ref, y_ref, xn_ref, gs_ref, us_ref, xn_s, acc):
        j = pl.program_id(1)

        @pl.when(j == 0)
        def _():
            xn = _bf(_rms_fwd(x_ref[...], g_ref[...])[0])
            xn_s[...] = xn
            xn_ref[...] = xn
            acc[...] = jnp.zeros_like(acc)

        wo2 = wo_ref[...].reshape(FF_SHARD, D_MODEL)
        for r in range(tm // FFN_PART_ROWS):
            rows = slice(r * FFN_PART_ROWS, (r + 1) * FFN_PART_ROWS)
            xn = xn_s[rows, :]
            g = _dot(xn, wg_ref[...])
            u = _dot(xn, wu_ref[...])
            gs_ref[rows, :] = _bf(g)
            us_ref[rows, :] = _bf(u)
            a = g * _sig(g) * u
            acc[rows, :] += _dot(_bf(a), wo2)

        @pl.when(j == FF_BLOCKS - 1)
        def _():
            y_ref[...] = x_ref[...] + 0.5 * acc[...]

    return _carrier_call(
        body, ex, _grid2_ends(S // tm, FF_BLOCKS), name,
        out_shape=(jax.ShapeDtypeStruct((S, D_MODEL), F32), jax.ShapeDtypeStruct((S, D_MODEL), BF16),
                   jax.ShapeDtypeStruct((FF_BLOCKS, S, FF_SHARD), BF16),
                   jax.ShapeDtypeStruct((FF_BLOCKS, S, FF_SHARD), BF16)),
        grid=(S // tm, FF_BLOCKS),
        in_specs=[row, vec, wg_s, wu_s, wo_s],
        out_specs=(row, row, hid, hid),
        scratch_shapes=[pltpu.VMEM((tm, D_MODEL), BF16), pltpu.VMEM((tm, D_MODEL), F32)],
        args=(x, gamma, wi, wi, wo))


def _ffn_bwd(dy, x, gamma, gs, us, wi, wo, tm, name, ex=NO_EXCHANGE):
    S = x.shape[0]
    wg_s, wu_s, wo_s, row, vec, hid = _ffn_specs(tm)
    dgu_s = pl.BlockSpec((2, None, tm, FF_SHARD), lambda i, j: (0, j, i, 0))

    def body(dy_ref, x_hbm, g_ref, gs_ref, us_ref, wg_ref, wu_ref, wo_ref,
             dx_ref, a_ref, dgu_ref, dgam_ref, dyh_s, x_buf, x_sem):
        i, j = pl.program_id(0), pl.program_id(1)
        acc = dx_ref
        x_copy = pltpu.make_async_copy(x_hbm.at[pl.ds(pl.multiple_of(i * tm, tm), tm), :], x_buf, x_sem)

        @pl.when(j == 0)
        def _():
            x_copy.start()
            dyh_s[...] = _bf(0.5 * dy_ref[...])
            acc[...] = jnp.zeros_like(acc)

        @pl.when((i == 0) & (j == 0))
        def _():
            dgam_ref[...] = jnp.zeros_like(dgam_ref)

        wo2 = wo_ref[...].reshape(FF_SHARD, D_MODEL)
        for r in range(tm // FFN_PART_ROWS):
            rows = slice(r * FFN_PART_ROWS, (r + 1) * FFN_PART_ROWS)
            da = _dot_nt(dyh_s[rows, :], wo2)
            g = gs_ref[rows, :].astype(F32)
            u = us_ref[rows, :].astype(F32)
            sg = _sig(g)
            sl = g * sg
            a_ref[rows, :] = _bf(sl * u)
            dg = _bf(da * u * (sg * (1.0 + g * (1.0 - sg))))
            du = _bf(da * sl)
            dgu_ref[0, rows, :] = dg
            dgu_ref[1, rows, :] = du
            acc[rows, :] += _dot_nt(dg, wg_ref[...]) + _dot_nt(du, wu_ref[...])

        @pl.when(j == FF_BLOCKS - 1)
        def _():
            x_copy.wait()
            dx, dgam = _rms_bwd(acc[...], x_buf[...], g_ref[...])
            dx_ref[...] = dy_ref[...] + dx
            dgam_ref[...] += jnp.sum(dgam, axis=0, keepdims=True)

    return _carrier_call(
        body, ex, _grid2_ends(S // tm, FF_BLOCKS), name,
        out_shape=(jax.ShapeDtypeStruct((S, D_MODEL), F32),
                   jax.ShapeDtypeStruct((FF_BLOCKS, S, FF_SHARD), BF16),
                   jax.ShapeDtypeStruct((2, FF_BLOCKS, S, FF_SHARD), BF16),
                   jax.ShapeDtypeStruct((1, D_MODEL), F32)),
        grid=(S // tm, FF_BLOCKS),
        in_specs=[row, pl.BlockSpec(memory_space=pl.ANY), vec, hid, hid, wg_s, wu_s, wo_s],
        out_specs=(row, hid, dgu_s, vec),
        scratch_shapes=[pltpu.VMEM((tm, D_MODEL), BF16), pltpu.VMEM((tm, D_MODEL), F32),
                        pltpu.SemaphoreType.DMA(())],
        args=(dy, x, gamma, gs, us, wi, wi, wo))


def _mixin_fwd(h, gamma, w_qc, tm, name):
    S = h.shape[0]
    nqkv = Q_DIM + 2 * KV_DIM

    def body(h_ref, g_ref, w_ref, un_ref, qkv_ref, cvg_ref):
        un = _bf(_rms_fwd(h_ref[...], g_ref[...])[0])
        un_ref[...] = un
        z = _dot(un, w_ref[:, :QC_DIM])
        qkv_ref[...] = z[:, :nqkv]
        cvg_ref[...] = z[:, nqkv:]

    row = lambda w: pl.BlockSpec((tm, w), lambda i: (i, 0))
    return pl.pallas_call(
        body, name=name,
        out_shape=(jax.ShapeDtypeStruct((S, D_MODEL), BF16), jax.ShapeDtypeStruct((S, nqkv), F32),
                   jax.ShapeDtypeStruct((S, 2 * CONV_DIM), F32)),
        grid=(S // tm,),
        in_specs=[row(D_MODEL), pl.BlockSpec((1, D_MODEL), lambda i: (0, 0)),
                  pl.BlockSpec((D_MODEL, IN_DIM), lambda i: (0, 0))],
        out_specs=(row(D_MODEL), row(nqkv), row(2 * CONV_DIM)),
        compiler_params=_params(("parallel",)),
    )(h, gamma, w_qc)


def _mixin_bwd(dh2, h1, gamma, dq, dkv, dcvg, dgpre, w_in, tm, name):
    S = h1.shape[0]
    nqkv = Q_DIM + 2 * KV_DIM
    n_in = QC_DIM + 2 * D_MODEL

    def body(dh2_ref, h1_ref, g_ref, dq_ref, dkv_ref, dcvg_ref, dgp_ref, win_ref,
             dh1_ref, dgam_ref, dz_ref):
        @pl.when(pl.program_id(0) == 0)
        def _():
            dgam_ref[...] = jnp.zeros_like(dgam_ref)

        wqc = win_ref[:, :QC_DIM]
        dq, dkv, dcvg = _bf(dq_ref[...]), _bf(dkv_ref[...]), _bf(dcvg_ref[...])
        dz_ref[:, :Q_DIM] = dq
        dz_ref[:, Q_DIM:nqkv] = dkv
        dz_ref[:, nqkv:QC_DIM] = dcvg
        dz_ref[:, QC_DIM:] = dgp_ref[...]
        dun = _dot_nt(dq, wqc[:, :Q_DIM])
        dun += _dot_nt(dkv, wqc[:, Q_DIM:nqkv])
        dun += _dot_nt(dcvg, wqc[:, nqkv:])
        dun += _dot_nt(dgp_ref[...], win_ref[:, QC_DIM:])
        dx, dgam = _rms_bwd(dun, h1_ref[...], g_ref[...])
        dh1_ref[...] = dh2_ref[...] + dx
        dgam_ref[...] += jnp.sum(dgam, axis=0, keepdims=True)

    row = lambda w: pl.BlockSpec((tm, w), lambda i: (i, 0))
    vec = pl.BlockSpec((1, D_MODEL), lambda i: (0, 0))
    return pl.pallas_call(
        body, name=name,
        out_shape=(jax.ShapeDtypeStruct((S, D_MODEL), F32), jax.ShapeDtypeStruct((1, D_MODEL), F32),
                   jax.ShapeDtypeStruct((S, n_in), BF16)),
        grid=(S // tm,),
        in_specs=[row(D_MODEL), row(D_MODEL), vec, row(Q_DIM), row(2 * KV_DIM), row(2 * CONV_DIM),
                  row(2 * D_MODEL), pl.BlockSpec((D_MODEL, IN_DIM), lambda i: (0, 0))],
        out_specs=(row(D_MODEL), vec, row(n_in)),
        compiler_params=_params(("arbitrary",)),
    )(dh2, h1, gamma, dq, dkv, dcvg, dgpre, w_in)


TQ = 512
QB = TQ // BLOCK


def _attn_in_specs(S):
    nkb = S // BLOCK
    return [
        pl.BlockSpec((TQ, Q_DIM), lambda i: (i, 0)),
        pl.BlockSpec((BLOCK, 2 * KV_DIM), lambda i: (jnp.maximum(i * QB - 1, 0), Q_DIM // (2 * KV_DIM))),
        pl.BlockSpec((TQ, 2 * KV_DIM), lambda i: (i, Q_DIM // (2 * KV_DIM))),
        pl.BlockSpec((BLOCK, 2 * KV_DIM), lambda i: (jnp.minimum(i * QB + QB, nkb - 1), Q_DIM // (2 * KV_DIM))),
        pl.BlockSpec((1, 128), lambda i: (0, 0)),
        pl.BlockSpec((1, KV_DIM), lambda i: (0, 0)),
        pl.BlockSpec((N_HEADS, 128), lambda i: (0, 0)),
        pl.BlockSpec((N_HEADS, BLOCK, WIN), lambda i: (0, 0, 0)),
    ]


GROUP_ROWS = 4 * BLOCK


def _half_rstd(x, low):
    x2 = x * x
    z = jnp.zeros_like(x2)
    r0 = lax.rsqrt(jnp.sum(jnp.where(low, x2, z), axis=-1, keepdims=True) * (1.0 / HEAD_DIM) + 1e-6)
    r1 = lax.rsqrt(jnp.sum(jnp.where(low, z, x2), axis=-1, keepdims=True) * (1.0 / HEAD_DIM) + 1e-6)
    return jnp.where(low, r0, r1)


def _kv_windows(kvp_ref, kvc_ref, kvn_ref, kg_ref, low):
    kv = jnp.concatenate([kvp_ref[...], kvc_ref[...], kvn_ref[...]], axis=0)
    k, v = kv[:, :KV_DIM], kv[:, KV_DIM:]
    kn = k * _half_rstd(k, low) * kg_ref[...]
    kr, vr = pltpu.roll(kn, HEAD_DIM, 1), pltpu.roll(v, HEAD_DIM, 1)
    kdup = [_bf(jnp.where(low, kn, kr)), _bf(jnp.where(low, kr, kn))]
    vdup = [_bf(jnp.where(low, v, vr)), _bf(jnp.where(low, vr, v))]
    return kdup, vdup


def _stack_heads(x_ref, t, kh, low):
    rows = slice(t * BLOCK, (t + 1) * BLOCK)
    xa = x_ref[rows, 256 * kh:256 * kh + 128]
    xb = x_ref[rows, 256 * kh + 128:256 * kh + 256]
    z = jnp.zeros_like(xa)
    return jnp.concatenate([jnp.where(low, xa, z), jnp.where(low, z, xa),
                            jnp.where(low, xb, z), jnp.where(low, z, xb)], axis=0)


def _stacked_q(q_ref, qg_ref, t, kh, low):
    qraw = _stack_heads(q_ref, t, kh, low)
    rq = lax.rsqrt(jnp.sum(qraw * qraw, axis=-1, keepdims=True) * (1.0 / HEAD_DIM) + 1e-6)
    return qraw, rq, _bf(qraw * rq * (qg_ref[...] * (HEAD_DIM ** -0.5)))


def _unstack_heads(ov, low):
    return (jnp.where(low, ov[0:128], ov[128:256]), jnp.where(low, ov[256:384], ov[384:512]))


def _edge_bias(i, t, S):
    kpos = i * TQ + (t - 1) * BLOCK + lax.broadcasted_iota(jnp.int32, (1, WIN), 1)
    return jnp.where((kpos < 0) | (kpos >= S), NEG_INF, 0.0)


def _group_exp(lhs, kw, bias_ref, sink_ref, kh, edge):
    s = _dot_nt(lhs, kw) + bias_ref[4 * kh:4 * kh + 4].reshape(GROUP_ROWS, WIN)
    if edge is not None:
        s = s + edge
    sk = jnp.concatenate(
        [jnp.broadcast_to(sink_ref[4 * kh + r:4 * kh + r + 1, 0:1], (BLOCK, 1)) for r in range(4)], axis=0)
    m = jnp.maximum(jnp.max(s, axis=-1, keepdims=True), sk)
    return jnp.exp(s - m), jnp.exp(sk - m)


def _attn_fwd(qkv, qg, kg, sinkb, bias, name, ex=NO_EXCHANGE):
    S = qkv.shape[0]

    def body(q_ref, kvp_ref, kvc_ref, kvn_ref, qg_ref, kg_ref, sink_ref, bias_ref, o_ref):
        i = pl.program_id(0)
        low = lax.broadcasted_iota(jnp.int32, (1, 128), 1) < HEAD_DIM
        ones = jnp.ones((WIN, 128), BF16)
        kdup, vdup = _kv_windows(kvp_ref, kvc_ref, kvn_ref, kg_ref, low)
        for t in range(QB):
            edge = _edge_bias(i, t, S) if t in (0, QB - 1) else None
            rows = slice(t * BLOCK, (t + 1) * BLOCK)
            for kh in range(KV_HEADS):
                _, _, lhs = _stacked_q(q_ref, qg_ref, t, kh, low)
                kw = kdup[kh][t * BLOCK:t * BLOCK + WIN]
                vw = vdup[kh][t * BLOCK:t * BLOCK + WIN]
                e, es = _group_exp(lhs, kw, bias_ref, sink_ref, kh, edge)
                eb = _bf(e)
                ov = _dot(eb, vw) * (1.0 / (_dot(eb, ones) + es))
                oa, ob = _unstack_heads(ov, low)
                o_ref[rows, 256 * kh:256 * kh + 128] = _bf(oa)
                o_ref[rows, 256 * kh + 128:256 * kh + 256] = _bf(ob)

    return _carrier_call(
        body, ex, _grid1_ends(S // TQ), name, out_shape=(jax.ShapeDtypeStruct((S, Q_DIM), BF16),),
        grid=(S // TQ,), in_specs=_attn_in_specs(S),
        out_specs=(pl.BlockSpec((TQ, Q_DIM), lambda i: (i, 0)),), scratch_shapes=[],
        args=(qkv, qkv, qkv, qkv, qg, kg, sinkb, bias))


def _attn_bwd(do, qkv, qg, kg, sinkb, bias, dbias_in, name, ex=NO_EXCHANGE):
    S = qkv.shape[0]
    nkb = S // BLOCK
    nsteps = S // TQ

    def body(do_ref, q_ref, kvp_ref, kvc_ref, kvn_ref, qg_ref, kg_ref, sink_ref, bias_ref, dbin_ref,
             dq_ref, dkp_ref, dvp_ref, dbias_ref, dsink_ref, dqg_ref, dqg_s):
        i = pl.program_id(0)

        @pl.when(i == 0)
        def _():
            dbias_ref[...] = dbin_ref[...]
            dsink_ref[...] = jnp.zeros_like(dsink_ref)
            dqg_s[...] = jnp.zeros_like(dqg_s)

        low = lax.broadcasted_iota(jnp.int32, (1, 128), 1) < HEAD_DIM
        own = ((lax.broadcasted_iota(jnp.int32, (GROUP_ROWS, 128), 1) >> 6) & 1) == (
            (lax.broadcasted_iota(jnp.int32, (GROUP_ROWS, 128), 0) >> 7) & 1)
        gq = qg_ref[...] * (HEAD_DIM ** -0.5)
        ones = jnp.ones((WIN, 128), BF16)
        kdup, vdup = _kv_windows(kvp_ref, kvc_ref, kvn_ref, kg_ref, low)
        for t in range(QB):
            edge = _edge_bias(i, t, S) if t in (0, QB - 1) else None
            rows = slice(t * BLOCK, (t + 1) * BLOCK)
            dk_dup, dv_dup = [], []
            for kh in range(KV_HEADS):
                qraw, rq, lhs = _stacked_q(q_ref, qg_ref, t, kh, low)
                dos = _bf(_stack_heads(do_ref, t, kh, low))
                kw = kdup[kh][t * BLOCK:t * BLOCK + WIN]
                vw = vdup[kh][t * BLOCK:t * BLOCK + WIN]
                e, es = _group_exp(lhs, kw, bias_ref, sink_ref, kh, edge)
                inv = 1.0 / (_dot(_bf(e), ones) + es)
                pr = e * jnp.concatenate([inv] * (WIN // 128), axis=1)
                dpr = _dot_nt(dos, vw)
                delta = jnp.sum(pr * dpr, axis=-1, keepdims=True)
                ds = pr * (dpr - delta)
                dbias_ref[4 * kh:4 * kh + 4] += ds.reshape(4, BLOCK, WIN)
                dsk = es * inv[:, 0:1] * delta
                for r in range(4):
                    dsink_ref[4 * kh + r:4 * kh + r + 1, :] -= jnp.broadcast_to(
                        jnp.sum(dsk[r * BLOCK:(r + 1) * BLOCK], axis=0, keepdims=True), (1, 128))
                dsb = _bf(ds)
                dqs = _dot(dsb, kw)
                qhat = qraw * rq
                dxhat = jnp.where(own, dqs, 0.0) * gq
                dq_st = rq * (dxhat - qhat * (jnp.sum(dxhat * qhat, axis=-1, keepdims=True) * (1.0 / HEAD_DIM)))
                dq_ref[rows, 256 * kh:256 * kh + 128] = dq_st[0:128] + dq_st[128:256]
                dq_ref[rows, 256 * kh + 128:256 * kh + 256] = dq_st[256:384] + dq_st[384:512]
                dqg_s[...] += jnp.sum((dqs * qhat).reshape(GROUP_ROWS // 8, 8, 128), axis=0)
                dkx = _dot_tn(dsb, lhs)
                dvx = _dot_tn(_bf(pr), dos)
                dk_dup.append(dkx + pltpu.roll(dkx, HEAD_DIM, 1))
                dv_dup.append(dvx + pltpu.roll(dvx, HEAD_DIM, 1))
            dkp_ref[t] = jnp.where(low, dk_dup[0], dk_dup[1])
            dvp_ref[t] = jnp.where(low, dv_dup[0], dv_dup[1])

        @pl.when(i == nsteps - 1)
        def _():
            acc = dqg_s[...] * (HEAD_DIM ** -0.5)
            acc = acc + pltpu.roll(acc, HEAD_DIM, 1)
            dqg_ref[...] = jnp.broadcast_to(jnp.sum(acc, axis=0, keepdims=True), (8, 128))

    const2 = lambda shape: pl.BlockSpec(shape, lambda i: (0,) * len(shape))
    part = pl.BlockSpec((QB, WIN, KV_DIM), lambda i: (i, 0, 0))
    return _carrier_call(
        body, ex, _grid1_ends(nsteps), name,
        out_shape=(jax.ShapeDtypeStruct((S, Q_DIM), F32), jax.ShapeDtypeStruct((nkb, WIN, KV_DIM), F32),
                   jax.ShapeDtypeStruct((nkb, WIN, KV_DIM), F32),
                   jax.ShapeDtypeStruct((N_HEADS, BLOCK, WIN), F32), jax.ShapeDtypeStruct((N_HEADS, 128), F32),
                   jax.ShapeDtypeStruct((8, 128), F32)),
        grid=(nsteps,),
        in_specs=[pl.BlockSpec((TQ, Q_DIM), lambda i: (i, 0))] + _attn_in_specs(S)
        + [const2((N_HEADS, BLOCK, WIN))],
        out_specs=(pl.BlockSpec((TQ, Q_DIM), lambda i: (i, 0)), part, part,
                   const2((N_HEADS, BLOCK, WIN)), const2((N_HEADS, 128)), const2((8, 128))),
        scratch_shapes=[pltpu.VMEM((8, 128), F32)],
        args=(do, qkv, qkv, qkv, qkv, qg, kg, sinkb, bias, dbias_in))


def _kv_fold(dkp, dvp, qkv, kg, name):
    nkb = dkp.shape[0]
    S = nkb * BLOCK
    nsteps = S // TQ

    def body(kp_p, kp_c, kp_n, vp_p, vp_c, vp_n, kv_ref, kg_ref, dkv_ref, dkg_ref, dkg_s):
        i = pl.program_id(0)

        @pl.when(i == 0)
        def _():
            dkg_s[...] = jnp.zeros_like(dkg_s)

        def fold(p_ref, c_ref, n_ref):
            blocks = []
            for t in range(QB):
                acc = c_ref[t, BLOCK:2 * BLOCK, :]
                if t > 0:
                    acc = acc + c_ref[t - 1, 2 * BLOCK:, :]
                else:
                    acc = acc + jnp.where(i > 0, p_ref[0, 2 * BLOCK:, :], 0.0)
                if t < QB - 1:
                    acc = acc + c_ref[t + 1, :BLOCK, :]
                else:
                    acc = acc + jnp.where(i < nsteps - 1, n_ref[0, :BLOCK, :], 0.0)
                blocks.append(acc)
            return jnp.concatenate(blocks, axis=0)

        dkn = fold(kp_p, kp_c, kp_n)
        dv = fold(vp_p, vp_c, vp_n)
        k = kv_ref[:, :KV_DIM]
        low = lax.broadcasted_iota(jnp.int32, (1, 128), 1) < HEAD_DIM
        rk = _half_rstd(k, low)
        khat = k * rk
        dxhat = dkn * kg_ref[...]
        prod = dxhat * khat
        z = jnp.zeros_like(prod)
        mean = jnp.where(low, jnp.sum(jnp.where(low, prod, z), axis=-1, keepdims=True),
                         jnp.sum(jnp.where(low, z, prod), axis=-1, keepdims=True)) * (1.0 / HEAD_DIM)
        dkv_ref[:, :KV_DIM] = rk * (dxhat - khat * mean)
        dkv_ref[:, KV_DIM:] = dv
        dkg_s[...] += jnp.sum((dkn * khat).reshape(TQ // 8, 8, KV_DIM), axis=0)

        @pl.when(i == nsteps - 1)
        def _():
            acc = dkg_s[...] + pltpu.roll(dkg_s[...], HEAD_DIM, 1)
            dkg_ref[...] = jnp.broadcast_to(jnp.sum(acc, axis=0, keepdims=True), (8, 128))

    prev = pl.BlockSpec((1, WIN, KV_DIM), lambda i: (jnp.maximum(i * QB - 1, 0), 0, 0))
    cur = pl.BlockSpec((QB, WIN, KV_DIM), lambda i: (i, 0, 0))
    nxt = pl.BlockSpec((1, WIN, KV_DIM), lambda i: (jnp.minimum(i * QB + QB, nkb - 1), 0, 0))
    return pl.pallas_call(
        body, name=name,
        out_shape=(jax.ShapeDtypeStruct((S, 2 * KV_DIM), F32), jax.ShapeDtypeStruct((8, 128), F32)),
        grid=(nsteps,),
        in_specs=[prev, cur, nxt, prev, cur, nxt,
                  pl.BlockSpec((TQ, 2 * KV_DIM), lambda i: (i, Q_DIM // (2 * KV_DIM))),
                  pl.BlockSpec((1, KV_DIM), lambda i: (0, 0))],
        out_specs=(pl.BlockSpec((TQ, 2 * KV_DIM), lambda i: (i, 0)), pl.BlockSpec((8, 128), lambda i: (0, 0))),
        scratch_shapes=[pltpu.VMEM((8, KV_DIM), F32)],
        compiler_params=_params(("arbitrary",)),
    )(dkp, dkp, dkp, dvp, dvp, dvp, qkv, kg)


BIAS_COLS = BLOCK * WIN
BIAS_CHUNK = 6144


def _bias_table(rel_bias_t, onehot, band):
    def body(rb_ref, oh_ref, band_ref, o_ref):
        o_ref[...] = _dot(rb_ref[...], oh_ref[...], HI) + band_ref[...]

    return pl.pallas_call(
        body, name="bias_table", out_shape=jax.ShapeDtypeStruct((N_HEADS, BIAS_COLS), F32),
        grid=(BIAS_COLS // BIAS_CHUNK,),
        in_specs=[pl.BlockSpec((N_HEADS, NUM_BUCKETS), lambda i: (0, 0)),
                  pl.BlockSpec((NUM_BUCKETS, BIAS_CHUNK), lambda i: (0, i)),
                  pl.BlockSpec((1, BIAS_CHUNK), lambda i: (0, i))],
        out_specs=pl.BlockSpec((N_HEADS, BIAS_CHUNK), lambda i: (0, i)),
        compiler_params=_params(("parallel",)),
    )(rel_bias_t, onehot, band)


def _bias_grad(dbias, onehot):
    def body(db_ref, oh_ref, o_ref):
        @pl.when(pl.program_id(0) == 0)
        def _():
            o_ref[...] = jnp.zeros_like(o_ref)

        o_ref[...] += lax.dot_general(db_ref[...], oh_ref[...], (((1,), (1,)), ((), ())),
                                      preferred_element_type=F32, precision=HI)

    return pl.pallas_call(
        body, name="bias_grad", out_shape=jax.ShapeDtypeStruct((N_HEADS, NUM_BUCKETS), F32),
        grid=(BIAS_COLS // BIAS_CHUNK,),
        in_specs=[pl.BlockSpec((N_HEADS, BIAS_CHUNK), lambda i: (0, i)),
                  pl.BlockSpec((NUM_BUCKETS, BIAS_CHUNK), lambda i: (0, i))],
        out_specs=pl.BlockSpec((N_HEADS, NUM_BUCKETS), lambda i: (0, 0)),
        compiler_params=_params(("arbitrary",)),
    )(dbias, onehot)


def _bucket_onehot():
    half = NUM_BUCKETS // 2
    max_exact = half // 2
    rel = jnp.arange(WIN)[None, :] - BLOCK - jnp.arange(BLOCK)[:, None]
    n = jnp.abs(rel)
    ret = jnp.where(rel > 0, half, 0)
    nf = jnp.maximum(n, 1).astype(F32)
    large = max_exact + (jnp.log(nf / max_exact) / np.log(MAX_DISTANCE / max_exact)
                         * (half - max_exact)).astype(jnp.int32)
    large = jnp.minimum(large, half - 1)
    bucket = (ret + jnp.where(n < max_exact, n, large)).reshape(1, BIAS_COLS)
    band = jnp.where(n <= BLOCK, 0.0, NEG_INF).astype(F32).reshape(1, BIAS_COLS)
    return (bucket == jnp.arange(NUM_BUCKETS)[:, None]).astype(F32), band


def _halo_specs(tm, width, S):
    r = tm // HALO
    last = S // HALO - 1
    return [pl.BlockSpec((HALO, width), lambda i: (jnp.maximum(i * r - 1, 0), 0)),
            pl.BlockSpec((tm, width), lambda i: (i, 0)),
            pl.BlockSpec((HALO, width), lambda i: (jnp.minimum(i * r + r, last), 0))]


def _with_halo(p_ref, c_ref, n_ref):
    return jnp.concatenate([p_ref[...], c_ref[...], n_ref[...]], axis=0)


def _row_valid(i, tm, S):
    g = i * tm - HALO + lax.broadcasted_iota(jnp.int32, (tm + 2 * HALO, 1), 0)
    return (g >= 0) & (g < S)


def _shifted(x):
    n = x.shape[0]
    return [x if b == 0 else pltpu.roll(x, n - b, 0) for b in range(8)]


def _tap(sh, off, tm):
    a, b = off // 8, off % 8
    return sh[b][8 * a:8 * a + tm]


def _conv_fwd(cvg, cw, cb, lg, lb, tm, name, ex=NO_EXCHANGE):
    S = cvg.shape[0]

    def body(p_ref, c_ref, n_ref, cw_ref, cb_ref, lg_ref, lb_ref, act_ref, yc_ref, sh_s):
        i = pl.program_id(0)
        z = _with_halo(p_ref, c_ref, n_ref)
        glu = jnp.where(_row_valid(i, tm, S), z[:, :CONV_DIM] * _sig(z[:, CONV_DIM:]), 0.0)
        for b, g_b in enumerate(_shifted(glu)):
            sh_s[b] = g_b
        for lblk in range(CONV_DIM // 128):
            lanes = slice(128 * lblk, 128 * (lblk + 1))
            for rb in range(tm // CROWS):
                r0 = rb * CROWS
                acc = jnp.zeros((CROWS, 128), F32) + cb_ref[:, lanes]
                for w in range(CONV_WIDTH):
                    a, b = divmod(w + 1, 8)
                    acc = acc + sh_s[b, 8 * a + r0:8 * a + r0 + CROWS, lanes] * cw_ref[w:w + 1, lanes]
                yc_ref[r0:r0 + CROWS, lanes] = acc
        y = yc_ref[...]
        mu = jnp.mean(y, axis=-1, keepdims=True)
        yc = y - mu
        rstd = lax.rsqrt(jnp.mean(yc * yc, axis=-1, keepdims=True) + 1e-5)
        ln = yc * rstd * lg_ref[...] + lb_ref[...]
        act_ref[...] = _bf(ln * _sig(ln))

    vec = pl.BlockSpec((1, CONV_DIM), lambda i: (0, 0))
    row = pl.BlockSpec((tm, CONV_DIM), lambda i: (i, 0))
    return _carrier_call(
        body, ex, _grid_ends(S // tm), name,
        out_shape=(jax.ShapeDtypeStruct((S, CONV_DIM), BF16), jax.ShapeDtypeStruct((S, CONV_DIM), F32)),
        grid=(S // tm,),
        in_specs=_halo_specs(tm, 2 * CONV_DIM, S) + [pl.BlockSpec((32, CONV_DIM), lambda i: (0, 0)), vec, vec, vec],
        out_specs=(row, row), scratch_shapes=[pltpu.VMEM((8, tm + 2 * HALO, CONV_DIM), F32)],
        args=(cvg, cvg, cvg, cw, cb, lg, lb))


def _conv_bwd(dact, yconv, cvg, cw, lg, lb, tm, name, ex=NO_EXCHANGE):
    S = cvg.shape[0]
    nsteps = S // tm

    def body(dp, dc, dn, yp, yc_, yn, zp, zc, zn, cw_ref, lg_ref, lb_ref,
             dz_ref, dcw_ref, dvec_ref, dcw_s, dvec_s, shg_s, shd_s):
        i = pl.program_id(0)

        @pl.when(i == 0)
        def _():
            dcw_s[...] = jnp.zeros_like(dcw_s)
            dvec_s[...] = jnp.zeros_like(dvec_s)

        valid = _row_valid(i, tm, S)
        own = (lax.broadcasted_iota(jnp.int32, (tm + 2 * HALO, 1), 0) >= HALO) & (
            lax.broadcasted_iota(jnp.int32, (tm + 2 * HALO, 1), 0) < HALO + tm)
        y = _with_halo(yp, yc_, yn)
        dact_ = _with_halo(dp, dc, dn)
        mu = jnp.mean(y, axis=-1, keepdims=True)
        ycen = y - mu
        rstd = lax.rsqrt(jnp.mean(ycen * ycen, axis=-1, keepdims=True) + 1e-5)
        yhat = ycen * rstd
        ln = yhat * lg_ref[...] + lb_ref[...]
        sg = _sig(ln)
        dln = dact_ * (sg * (1.0 + ln * (1.0 - sg)))
        dyhat = dln * lg_ref[...]
        dy = rstd * (dyhat - jnp.mean(dyhat, axis=-1, keepdims=True)
                     - yhat * jnp.mean(dyhat * yhat, axis=-1, keepdims=True))
        dy = jnp.where(valid, dy, 0.0)
        dln_own = jnp.where(own, dln, 0.0)
        nr = (tm + 2 * HALO) // 8
        dvec_s[0] += jnp.sum(jnp.where(own, dy, 0.0).reshape(nr, 8, CONV_DIM), axis=0)
        dvec_s[1] += jnp.sum((dln_own * yhat).reshape(nr, 8, CONV_DIM), axis=0)
        dvec_s[2] += jnp.sum(dln_own.reshape(nr, 8, CONV_DIM), axis=0)
        z = _with_halo(zp, zc, zn)
        glu = jnp.where(valid, z[:, :CONV_DIM] * _sig(z[:, CONV_DIM:]), 0.0)
        for b, (g_b, d_b) in enumerate(zip(_shifted(glu), _shifted(dy))):
            shg_s[b] = g_b
            shd_s[b] = d_b
        for cb in range(CONV_DIM // 128):
            lanes = slice(128 * cb, 128 * (cb + 1))
            for rb in range(tm // CROWS):
                r0 = rb * CROWS
                dy_own = shd_s[0, HALO + r0:HALO + r0 + CROWS, lanes]
                dglu = jnp.zeros((CROWS, 128), F32)
                for w in range(CONV_WIDTH):
                    a, b = divmod(CONV_WIDTH - w, 8)
                    dglu = dglu + shd_s[b, 8 * a + r0:8 * a + r0 + CROWS, lanes] * cw_ref[w:w + 1, lanes]
                    a, b = divmod(w + 1, 8)
                    prod = dy_own * shg_s[b, 8 * a + r0:8 * a + r0 + CROWS, lanes]
                    dcw_s[w, :, lanes] += jnp.sum(prod.reshape(CROWS // 8, 8, 128), axis=0)
                cv = zc[r0:r0 + CROWS, lanes]
                sg_o = _sig(zc[r0:r0 + CROWS, CONV_DIM + 128 * cb:CONV_DIM + 128 * (cb + 1)])
                dz_ref[r0:r0 + CROWS, lanes] = dglu * sg_o
                dz_ref[r0:r0 + CROWS, CONV_DIM + 128 * cb:CONV_DIM + 128 * (cb + 1)] = (
                    dglu * cv * sg_o * (1.0 - sg_o))

        @pl.when(i == nsteps - 1)
        def _():
            dcw_ref[...] = jnp.sum(dcw_s[...], axis=1)
            dvec_ref[...] = jnp.sum(dvec_s[...], axis=1)

    vec = pl.BlockSpec((1, CONV_DIM), lambda i: (0, 0))
    return _carrier_call(
        body, ex, _grid_ends(nsteps), name,
        out_shape=(jax.ShapeDtypeStruct((S, 2 * CONV_DIM), F32), jax.ShapeDtypeStruct((32, CONV_DIM), F32),
                   jax.ShapeDtypeStruct((8, CONV_DIM), F32)),
        grid=(nsteps,),
        in_specs=_halo_specs(tm, CONV_DIM, S) + _halo_specs(tm, CONV_DIM, S) + _halo_specs(tm, 2 * CONV_DIM, S)
        + [pl.BlockSpec((32, CONV_DIM), lambda i: (0, 0)), vec, vec],
        out_specs=(pl.BlockSpec((tm, 2 * CONV_DIM), lambda i: (i, 0)),
                   pl.BlockSpec((32, CONV_DIM), lambda i: (0, 0)), pl.BlockSpec((8, CONV_DIM), lambda i: (0, 0))),
        scratch_shapes=[pltpu.VMEM((32, 8, CONV_DIM), F32), pltpu.VMEM((8, 8, CONV_DIM), F32),
                        pltpu.VMEM((8, tm + 2 * HALO, CONV_DIM), F32), pltpu.VMEM((8, tm + 2 * HALO, CONV_DIM), F32)],
        args=(dact, dact, dact, yconv, yconv, yconv, cvg, cvg, cvg, cw, lg, lb))


def _merge_parts(un, o, cact, win_ref, wao_ref, wco_ref):
    g = _dot(un, win_ref[:, QC_DIM:])
    ga, gc = _sig(g[:, :D_MODEL]), _sig(g[:, D_MODEL:])
    ya = _dot(o, wao_ref[...])
    yc = _dot(cact, wco_ref[...])
    return ga, gc, ya, yc


def _merge_specs(tm):
    row = lambda w: pl.BlockSpec((tm, w), lambda i: (i, 0))
    full = lambda a, b: pl.BlockSpec((a, b), lambda i: (0, 0))
    weights = [full(D_MODEL, IN_DIM), full(Q_DIM, D_MODEL), full(CONV_DIM, D_MODEL), full(D_MODEL, D_MODEL)]
    return row, weights


def _merge_fwd(h1, un, o, cact, w_g, w_ao, w_co, w_o, tm, name, ex=NO_EXCHANGE):
    S = h1.shape[0]
    row, weights = _merge_specs(tm)

    def body(h1_ref, un_ref, o_ref, c_ref, wg_ref, wao_ref, wco_ref, wo_ref, h2_ref):
        ga, gc, ya, yc = _merge_parts(un_ref[...], o_ref[...], c_ref[...], wg_ref, wao_ref, wco_ref)
        h2_ref[...] = h1_ref[...] + _dot(_bf(ga * ya + gc * yc), wo_ref[...])

    return _carrier_call(
        body, ex, _grid_ends(S // tm), name, out_shape=(jax.ShapeDtypeStruct((S, D_MODEL), F32),),
        grid=(S // tm,),
        in_specs=[row(D_MODEL), row(D_MODEL), row(Q_DIM), row(CONV_DIM)] + weights,
        out_specs=(row(D_MODEL),), scratch_shapes=[], args=(h1, un, o, cact, w_g, w_ao, w_co, w_o))


def _merge_bwd(dh2, un, o, cact, w_g, w_ao, w_co, w_o, tm, name):
    S = dh2.shape[0]
    row, weights = _merge_specs(tm)

    def body(dh2_ref, un_ref, o_ref, c_ref, wg_ref, wao_ref, wco_ref, wo_ref,
             do_ref, dc_ref, mix_ref, dya_ref, dyc_ref, dgp_ref):
        ga, gc, ya, yc = _merge_parts(un_ref[...], o_ref[...], c_ref[...], wg_ref, wao_ref, wco_ref)
        mix_ref[...] = _bf(ga * ya + gc * yc)
        dmix = _dot_nt(_bf(dh2_ref[...]), wo_ref[...])
        dya = _bf(dmix * ga)
        dyc = _bf(dmix * gc)
        dya_ref[...] = dya
        dyc_ref[...] = dyc
        dgp_ref[:, :D_MODEL] = _bf(dmix * ya * ga * (1.0 - ga))
        dgp_ref[:, D_MODEL:] = _bf(dmix * yc * gc * (1.0 - gc))
        do_ref[...] = _dot_nt(dya, wao_ref[...])
        dc_ref[...] = _dot_nt(dyc, wco_ref[...])

    return pl.pallas_call(
        body, name=name,
        out_shape=(jax.ShapeDtypeStruct((S, Q_DIM), F32), jax.ShapeDtypeStruct((S, CONV_DIM), F32),
                   jax.ShapeDtypeStruct((S, D_MODEL), BF16), jax.ShapeDtypeStruct((S, D_MODEL), BF16),
                   jax.ShapeDtypeStruct((S, D_MODEL), BF16), jax.ShapeDtypeStruct((S, 2 * D_MODEL), BF16)),
        grid=(S // tm,),
        in_specs=[row(D_MODEL), row(D_MODEL), row(Q_DIM), row(CONV_DIM)] + weights,
        out_specs=(row(Q_DIM), row(CONV_DIM), row(D_MODEL), row(D_MODEL), row(D_MODEL), row(2 * D_MODEL)),
        compiler_params=_params(("parallel",)),
    )(dh2, un, o, cact, w_g, w_ao, w_co, w_o)


def _pe_specs(tm, layer):
    row = pl.BlockSpec((tm, D_MODEL), lambda i: (i, 0))
    vec = pl.BlockSpec((1, D_MODEL), lambda i: (0, 0))
    p_s = pl.BlockSpec((None, None, tm, 256), lambda i: (layer, 0, i, 0))
    wpp = pl.BlockSpec((256, D_MODEL), lambda i: (0, 0))
    wpg = pl.BlockSpec((D_MODEL, D_MODEL), lambda i: (0, 0))
    return row, vec, p_s, wpp, wpg


def _pe_fwd(h, gamma, p, layer, w_pp, w_pg, tm, name):
    S = h.shape[0]
    row, vec, p_s, wpp, wpg = _pe_specs(tm, layer)

    def body(h_ref, g_ref, p_ref, wpp_ref, wpg_ref, x_ref, hn_ref):
        hn = _bf(_rms_fwd(h_ref[...], g_ref[...])[0])
        hn_ref[...] = hn
        gate = _sig(_dot(hn, wpg_ref[...]))
        x_ref[...] = h_ref[...] + _dot(_bf(p_ref[...]), wpp_ref[...]) * gate

    return pl.pallas_call(
        body, name=name,
        out_shape=(jax.ShapeDtypeStruct((S, D_MODEL), F32), jax.ShapeDtypeStruct((S, D_MODEL), BF16)),
        grid=(S // tm,), in_specs=[row, vec, p_s, wpp, wpg], out_specs=(row, row),
        compiler_params=_params(("parallel",)),
    )(h, gamma, p, w_pp, w_pg)


def _pe_bwd(dx, h, gamma, hn, p, layer, w_pp, w_pg, tm, name):
    S = h.shape[0]
    row, vec, p_s, wpp, wpg = _pe_specs(tm, layer)

    def body(dx_ref, h_ref, g_ref, hn_ref, p_ref, wpp_ref, wpg_ref, dh_ref, dgp_ref, dpr_ref, dgam_ref):
        @pl.when(pl.program_id(0) == 0)
        def _():
            dgam_ref[...] = jnp.zeros_like(dgam_ref)

        for r in range(tm // FFN_PART_ROWS):
            rows = slice(r * FFN_PART_ROWS, (r + 1) * FFN_PART_ROWS)
            dxv = dx_ref[rows, :]
            gate = _sig(_dot(hn_ref[rows, :], wpg_ref[...]))
            proj = _dot(_bf(p_ref[rows, :]), wpp_ref[...])
            dpr_ref[rows, :] = _bf(dxv * gate)
            dgp = _bf(dxv * proj * gate * (1.0 - gate))
            dgp_ref[rows, :] = dgp
            dxn, dgam = _rms_bwd(_dot_nt(dgp, wpg_ref[...]), h_ref[rows, :], g_ref[...])
            dh_ref[rows, :] = dxv + dxn
            dgam_ref[...] += jnp.sum(dgam, axis=0, keepdims=True)

    return pl.pallas_call(
        body, name=name,
        out_shape=(jax.ShapeDtypeStruct((S, D_MODEL), F32), jax.ShapeDtypeStruct((S, D_MODEL), BF16),
                   jax.ShapeDtypeStruct((S, D_MODEL), BF16), jax.ShapeDtypeStruct((1, D_MODEL), F32)),
        grid=(S // tm,), in_specs=[row, row, vec, row, p_s, wpp, wpg], out_specs=(row, row, row, vec),
        compiler_params=_params(("arbitrary",)),
    )(dx, h, gamma, hn, p, w_pp, w_pg)


def _loss_head(y, target, tm):
    S = y.shape[0]

    def body(y_ref, t_ref, dy_ref, l_ref):
        @pl.when(pl.program_id(0) == 0)
        def _():
            l_ref[...] = jnp.zeros_like(l_ref)

        diff = y_ref[...] - t_ref[...]
        dy_ref[...] = diff * (1.0 / D_MODEL)
        sq = jnp.sum((diff * diff).reshape(tm // 8, 8, D_MODEL), axis=0)
        part = sq[:, 0:128]
        for k in range(1, D_MODEL // 128):
            part = part + sq[:, 128 * k:128 * (k + 1)]
        l_ref[...] += part

    row = pl.BlockSpec((tm, D_MODEL), lambda i: (i, 0))
    return pl.pallas_call(
        body, name="loss_head",
        out_shape=(jax.ShapeDtypeStruct((S, D_MODEL), F32), jax.ShapeDtypeStruct((8, 128), F32)),
        grid=(S // tm,), in_specs=[row, row], out_specs=(row, pl.BlockSpec((8, 128), lambda i: (0, 0))),
        compiler_params=_params(("arbitrary",)),
    )(y, target)


def _adamw(parts, w, m, v, name, ex=NO_EXCHANGE):
    nl = len(parts)
    R, C = w.shape
    K = R // nl
    tr = next((c for c in range(min(K, 256) // 16 * 16, 15, -16) if K % c == 0), K)
    nk = K // tr

    def body(*refs):
        p_refs = refs[:nl]
        w_ref, m_ref, v_ref, g_ref, d_ref, nm_ref, nv_ref = refs[nl:]
        for lyr in range(nl):
            @pl.when(pl.program_id(0) == lyr)
            def _(p_ref=p_refs[lyr]):
                g = p_ref[0].astype(F32)
                for k in range(1, N_DEV):
                    g = g + p_ref[k].astype(F32)
                g_ref[...] = g
                nm = ADAM_B1 * m_ref[...] + (1.0 - ADAM_B1) * g
                nv = ADAM_B2 * v_ref[...] + (1.0 - ADAM_B2) * (g * g)
                nm_ref[...] = nm
                nv_ref[...] = nv
                m_hat = nm / (1.0 - ADAM_B1 ** ADAM_STEP)
                v_hat = nv / (1.0 - ADAM_B2 ** ADAM_STEP)
                d_ref[...] = -ADAM_LR * (m_hat / (jnp.sqrt(v_hat) + ADAM_EPS) + ADAM_WD * w_ref[...])

    def part_spec(lyr):
        return pl.BlockSpec((N_DEV, tr, C), lambda l, i: (0, jnp.where(l == lyr, i, jnp.where(l < lyr, 0, nk - 1)), 0))

    blk = pl.BlockSpec((tr, C), lambda l, i: (l * nk + i, 0))
    out = jax.ShapeDtypeStruct((R, C), F32)
    return _carrier_call(
        body, ex, _grid_ends(nl, nk), name, out_shape=(out, out, out, out), grid=(nl, nk),
        in_specs=[part_spec(lyr) for lyr in range(nl)] + [blk, blk, blk],
        out_specs=(blk, blk, blk, blk), scratch_shapes=[], args=(*parts, w, m, v))


SHARDED = ("w_ffn1_in", "w_ffn1_out", "w_in", "conv_w", "w_attn_out", "w_conv_out", "w_o",
           "w_ffn2_in", "w_ffn2_out", "w_pe_gate", "w_pe_proj")
COL_SHARDED = ("w_ffn1_in", "w_in", "conv_w", "w_attn_out", "w_conv_out", "w_ffn2_in", "w_pe_proj")
SMALL = ("rel_bias", "norm_ffn1", "norm_mix", "q_norm", "k_norm", "sink", "conv_b", "conv_ln_g", "conv_ln_b",
         "norm_ffn2", "norm_pe")
WEIGHTS = ("rel_bias", "norm_ffn1", "w_ffn1_in", "w_ffn1_out", "norm_mix", "w_in", "q_norm", "k_norm", "sink",
           "conv_w", "conv_b", "conv_ln_g", "conv_ln_b", "w_attn_out", "w_conv_out", "w_o", "norm_ffn2",
           "w_ffn2_in", "w_ffn2_out", "norm_pe", "w_pe_gate", "w_pe_proj")


def _natural(g):
    k, n = g.shape[1], g.shape[2]
    return jnp.transpose(g, (1, 0, 2)).reshape(k, N_DEV * n)


def _blocked(w):
    k, n = w.shape[0], w.shape[1] // N_DEV
    return jnp.transpose(w.reshape(k, N_DEV, n), (1, 0, 2))


def kernel(x, p, rel_bias, norm_ffn1, w_ffn1_in, w_ffn1_out, norm_mix, w_in, q_norm, k_norm, sink, conv_w, conv_b, conv_ln_g, conv_ln_b, w_attn_out, w_conv_out, w_o, norm_ffn2, w_ffn2_in, w_ffn2_out, norm_pe, w_pe_gate, w_pe_proj, loss_target, m_rel_bias, m_norm_ffn1, m_w_ffn1_in, m_w_ffn1_out, m_norm_mix, m_w_in, m_q_norm, m_k_norm, m_sink, m_conv_w, m_conv_b, m_conv_ln_g, m_conv_ln_b, m_w_attn_out, m_w_conv_out, m_w_o, m_norm_ffn2, m_w_ffn2_in, m_w_ffn2_out, m_norm_pe, m_w_pe_gate, m_w_pe_proj, v_rel_bias, v_norm_ffn1, v_w_ffn1_in, v_w_ffn1_out, v_norm_mix, v_w_in, v_q_norm, v_k_norm, v_sink, v_conv_w, v_conv_b, v_conv_ln_g, v_conv_ln_b, v_w_attn_out, v_w_conv_out, v_w_o, v_norm_ffn2, v_w_ffn2_in, v_w_ffn2_out, v_norm_pe, v_w_pe_gate, v_w_pe_proj):
    W = dict(rel_bias=rel_bias, norm_ffn1=norm_ffn1, w_ffn1_in=w_ffn1_in, w_ffn1_out=w_ffn1_out, norm_mix=norm_mix,
             w_in=w_in, q_norm=q_norm, k_norm=k_norm, sink=sink, conv_w=conv_w, conv_b=conv_b, conv_ln_g=conv_ln_g,
             conv_ln_b=conv_ln_b, w_attn_out=w_attn_out, w_conv_out=w_conv_out, w_o=w_o, norm_ffn2=norm_ffn2,
             w_ffn2_in=w_ffn2_in, w_ffn2_out=w_ffn2_out, norm_pe=norm_pe, w_pe_gate=w_pe_gate, w_pe_proj=w_pe_proj)
    M = dict(rel_bias=m_rel_bias, norm_ffn1=m_norm_ffn1, w_ffn1_in=m_w_ffn1_in, w_ffn1_out=m_w_ffn1_out,
             norm_mix=m_norm_mix, w_in=m_w_in, q_norm=m_q_norm, k_norm=m_k_norm, sink=m_sink, conv_w=m_conv_w,
             conv_b=m_conv_b, conv_ln_g=m_conv_ln_g, conv_ln_b=m_conv_ln_b, w_attn_out=m_w_attn_out,
             w_conv_out=m_w_conv_out, w_o=m_w_o, norm_ffn2=m_norm_ffn2, w_ffn2_in=m_w_ffn2_in,
             w_ffn2_out=m_w_ffn2_out, norm_pe=m_norm_pe, w_pe_gate=m_w_pe_gate, w_pe_proj=m_w_pe_proj)
    V = dict(rel_bias=v_rel_bias, norm_ffn1=v_norm_ffn1, w_ffn1_in=v_w_ffn1_in, w_ffn1_out=v_w_ffn1_out,
             norm_mix=v_norm_mix, w_in=v_w_in, q_norm=v_q_norm, k_norm=v_k_norm, sink=v_sink, conv_w=v_conv_w,
             conv_b=v_conv_b, conv_ln_g=v_conv_ln_g, conv_ln_b=v_conv_ln_b, w_attn_out=v_w_attn_out,
             w_conv_out=v_w_conv_out, w_o=v_w_o, norm_ffn2=v_norm_ffn2, w_ffn2_in=v_w_ffn2_in,
             w_ffn2_out=v_w_ffn2_out, norm_pe=v_norm_pe, w_pe_gate=v_w_pe_gate, w_pe_proj=v_w_pe_proj)

    L = w_in.shape[0]
    S = x.shape[1]
    tm = min(512, S)
    tm_ffn = min(1024, S)
    xs = x[0]
    target = loss_target[0]
    vec = lambda a: a.reshape(1, -1)

    half, full = {}, {}

    def carried(stage1_items, stage2_items):
        s1 = [it for it in stage1_items if it[1] < L]
        s2 = [it for it in stage2_items if it[1] < L]
        ex = _stage1([W[n][l] if n == "conv_w" else W[n][l].astype(BF16) for n, l in s1]) + _stage2(
            [half.pop(it) for it in s2])
        return ex, s1, s2

    def landed(got, s1, s2):
        half.update(zip(s1, got[:len(s1)]))
        full.update(zip(s2, got[len(s1):]))

    onehot, band = _bucket_onehot()
    bias = _bias_table(rel_bias.T, onehot, band).reshape(N_HEADS, BLOCK, WIN)

    layers, saved = [], []
    h = xs
    ex, s1, s2 = carried([("w_ffn1_in", 0), ("w_ffn1_out", 0), ("w_in", 0), ("conv_w", 0)], [])
    landed(_exchange(ex, "allgather_first"), s1, s2)
    ex, s1, s2 = carried([], [("w_ffn1_in", 0), ("w_ffn1_out", 0)])
    landed(_exchange(ex, "allgather_relay"), s1, s2)
    for l in range(L):
        sv = dict(x0=h)
        G = dict(wi1=full.pop(("w_ffn1_in", l)), wo1=full.pop(("w_ffn1_out", l)))
        ex, s1, s2 = carried(
            [(n, l) for n in ("w_ffn2_in", "w_ffn2_out", "w_attn_out", "w_conv_out", "w_o")],
            [("w_in", l), ("conv_w", l)])
        (h1, sv["xn1"], sv["g1"], sv["u1"]), got = _ffn_fwd(
            h, vec(norm_ffn1[l]), G["wi1"], G["wo1"], tm_ffn, "ffn1_fwd", ex)
        landed(got, s1, s2)
        G.update(w_in=_natural(full.pop(("w_in", l))),
                 conv_w=jnp.pad(_natural(full.pop(("conv_w", l))), ((0, 1), (0, 0))))
        sv["h1"] = h1
        sv["un"], sv["qkv"], sv["cvg"] = _mixin_fwd(h1, vec(norm_mix[l]), G["w_in"], tm, "mixin_fwd")
        sv["qg"] = vec(jnp.tile(q_norm[l], 2))
        sv["kg"] = vec(jnp.tile(k_norm[l], KV_HEADS))
        sv["sinkb"] = jnp.broadcast_to(sink[l][:, None], (N_HEADS, 128))
        ex, s1, s2 = carried(
            [("w_pe_gate", l), ("w_pe_proj", l), ("w_ffn1_in", l + 1)],
            [(n, l) for n in ("w_ffn2_in", "w_ffn2_out", "w_attn_out", "w_conv_out", "w_o")])
        (sv["o"],), got = _attn_fwd(sv["qkv"], sv["qg"], sv["kg"], sv["sinkb"], bias, "attn_fwd", ex)
        landed(got, s1, s2)
        G.update(wi2=full.pop(("w_ffn2_in", l)), wo2=full.pop(("w_ffn2_out", l)),
                 w_ao=_natural(full.pop(("w_attn_out", l))), w_co=_natural(full.pop(("w_conv_out", l))),
                 w_o=full.pop(("w_o", l)).reshape(D_MODEL, D_MODEL))
        ex, s1, s2 = carried([("w_ffn1_out", l + 1)],
                             [("w_pe_gate", l), ("w_pe_proj", l), ("w_ffn1_in", l + 1)])
        (sv["cact"], sv["yconv"]), got = _conv_fwd(
            sv["cvg"], G["conv_w"], vec(conv_b[l]), vec(conv_ln_g[l]), vec(conv_ln_b[l]), tm, "conv_fwd", ex)
        landed(got, s1, s2)
        G.update(w_pg=full.pop(("w_pe_gate", l)).reshape(D_MODEL, D_MODEL),
                 w_pp=_natural(full.pop(("w_pe_proj", l))))
        ex, s1, s2 = carried([("w_in", l + 1), ("conv_w", l + 1)], [("w_ffn1_out", l + 1)])
        (h2,), got = _merge_fwd(h1, sv["un"], sv["o"], sv["cact"], G["w_in"], G["w_ao"], G["w_co"], G["w_o"], tm,
                                "merge_fwd", ex)
        landed(got, s1, s2)
        sv["h2"] = h2
        (h3, sv["xn2"], sv["g2"], sv["u2"]), _ = _ffn_fwd(
            h2, vec(norm_ffn2[l]), G["wi2"], G["wo2"], tm_ffn, "ffn2_fwd")
        sv["h3"] = h3
        h, sv["hn"] = _pe_fwd(h3, vec(norm_pe[l]), p, l, G["w_pp"], G["w_pg"], tm, "pe_fwd")
        layers.append(G)
        saved.append(sv)

    dh, lparts = _loss_head(h, target, tm)
    loss = lax.psum((0.5 / D_MODEL) * jnp.sum(lparts), AXES)

    dbias = jnp.zeros((N_HEADS, BLOCK, WIN), F32)
    small_g = {n: [None] * L for n in SMALL if n != "rel_bias"}
    recv = {n: [None] * L for n in SHARDED}

    def keep(names, l, got):
        for n, r in zip(names, got):
            recv[n][l] = r

    pending = None
    for l in reversed(range(L)):
        G, sv = layers[l], saved[l]
        dh3, dgp_pe, dproj, dg_pe = _pe_bwd(dh, sv["h3"], vec(norm_pe[l]), sv["hn"], p, l, G["w_pp"], G["w_pg"],
                                            tm, "pe_bwd")
        gw_pg = _matmul_tn(sv["hn"][None], dgp_pe[None], 1, "dw_pe_gate")
        gw_pp = _matmul_tn(p.reshape(L, S, p.shape[-1]), dproj[None], 1, "dw_pe_proj", a_index=l)
        (dh2, a2, dgu2, dg_n2), got = _ffn_bwd(
            dh3, sv["h2"], vec(norm_ffn2[l]), sv["g2"], sv["u2"], G["wi2"], G["wo2"], tm_ffn, "ffn2_bwd",
            _Exchange(pending, False) if pending else NO_EXCHANGE)
        if pending:
            keep(("w_ffn1_in",), l + 1, got)
        gwo2 = _matmul_tn(a2, dh3[None], FF_BLOCKS, "dw_ffn2_out", scale=0.5)
        gwi2 = _matmul_tn(sv["xn2"][None], dgu2.reshape(2 * FF_BLOCKS, S, FF_SHARD), 2 * FF_BLOCKS, "dw_ffn2_in",
                          ts=TS_BF16)
        do, dcact, mix, dya, dyc, dgpre = _merge_bwd(dh2, sv["un"], sv["o"], sv["cact"], G["w_in"], G["w_ao"],
                                                     G["w_co"], G["w_o"], tm, "merge_bwd")
        gw_o = _matmul_tn(mix[None], dh2[None], 1, "dw_o")
        gw_ao = _matmul_tn(sv["o"][None], dya[None], 1, "dw_attn_out")
        gw_co = _matmul_tn(sv["cact"][None], dyc[None], 1, "dw_conv_out")
        (dq, dkp, dvp, dbias, dsink, dqg), got = _attn_bwd(
            do, sv["qkv"], sv["qg"], sv["kg"], sv["sinkb"], bias, dbias, "attn_bwd",
            _Exchange([gwi2, gw_pg.reshape(N_DEV, D_MODEL // N_DEV, D_MODEL), _blocked(gw_pp[0])], False))
        keep(("w_ffn2_in", "w_pe_gate", "w_pe_proj"), l, got)
        dkv, dkg = _kv_fold(dkp, dvp, sv["qkv"], sv["kg"], "kv_fold")
        (dcvg, dcw, dcvec), got = _conv_bwd(
            dcact, sv["yconv"], sv["cvg"], G["conv_w"], vec(conv_ln_g[l]), vec(conv_ln_b[l]), tm, "conv_bwd",
            _Exchange([gwo2.reshape(N_DEV, FF_SHARD // 2, D_MODEL)], False))
        keep(("w_ffn2_out",), l, got)
        dh1, dg_mix, dz = _mixin_bwd(dh2, sv["h1"], vec(norm_mix[l]), dq, dkv, dcvg, dgpre, G["w_in"],
                                     tm, "mixin_bwd")
        gw_in = _matmul_tn(sv["un"][None], dz[None], 1, "dw_in")[0]
        mid_send = [_blocked(gw_in), _blocked(dcw[:CONV_WIDTH]), _blocked(gw_ao[0]), _blocked(gw_co[0]),
                    gw_o.reshape(N_DEV, D_MODEL // N_DEV, D_MODEL)]
        (dh, a1, dgu1, dg_n1), got = _ffn_bwd(
            dh1, sv["x0"], vec(norm_ffn1[l]), sv["g1"], sv["u1"], G["wi1"], G["wo1"], tm_ffn, "ffn1_bwd",
            _Exchange(mid_send, False))
        keep(("w_in", "conv_w", "w_attn_out", "w_conv_out", "w_o"), l, got)
        gwo1 = _matmul_tn(a1, dh1[None], FF_BLOCKS, "dw_ffn1_out", scale=0.5)
        gwi1, got = _matmul_tn(sv["xn1"][None], dgu1.reshape(2 * FF_BLOCKS, S, FF_SHARD), 2 * FF_BLOCKS,
                               "dw_ffn1_in", ts=TS_BF16,
                               ex=_Exchange([gwo1.reshape(N_DEV, FF_SHARD // 2, D_MODEL)], False))
        keep(("w_ffn1_out",), l, got)
        pending = [gwi1]
        small_g["norm_ffn1"][l] = dg_n1[0]
        small_g["norm_mix"][l] = dg_mix[0]
        small_g["q_norm"][l] = dqg[0, :HEAD_DIM]
        small_g["k_norm"][l] = dkg[0, :HEAD_DIM]
        small_g["sink"][l] = dsink[:, 0]
        small_g["conv_b"][l] = dcvec[0]
        small_g["conv_ln_g"][l] = dcvec[1]
        small_g["conv_ln_b"][l] = dcvec[2]
        small_g["norm_ffn2"][l] = dg_n2[0]
        small_g["norm_pe"][l] = dg_pe[0]

    keep(("w_ffn1_in",), 0, _exchange(_Exchange(pending, False), "grad_exchange_last"))
    grad_x = dh[None]
    drb = _bias_grad(dbias.reshape(N_HEADS, BIAS_COLS), onehot).T

    res = {}
    for n in SHARDED:
        shp = W[n].shape
        rows, cols = shp[0] * shp[1], shp[2]
        parts = recv[n]
        if shp[1] % 8:
            parts = [jnp.stack(recv[n], axis=1).reshape(N_DEV, rows, cols)]
        res[n] = [o.reshape(shp) for o in _adamw(
            parts, W[n].reshape(rows, cols), M[n].reshape(rows, cols), V[n].reshape(rows, cols), "adamw_" + n)[0]]

    flat_g = jnp.concatenate([drb.reshape(-1)] + [jnp.stack(small_g[n]).reshape(-1) for n in SMALL[1:]])
    n_small = flat_g.shape[0]
    rows_s = -(-n_small // 1024 // 8) * 8
    pad = lambda a: jnp.pad(a, (0, rows_s * 1024 - n_small)).reshape(rows_s, 1024)
    flat = lambda d: pad(jnp.concatenate([d[n].reshape(-1) for n in SMALL]))
    (parts_s,) = _exchange(_Exchange([pad(flat_g)], True), "small_allgather")
    outs_s = _adamw([parts_s], flat(W), flat(M), flat(V), "adamw_small")[0]
    off = 0
    for n in SMALL:
        size = W[n].size
        res[n] = [o.reshape(-1)[off:off + size].reshape(W[n].shape) for o in outs_s]
        off += size

    out = [loss, grad_x]
    for k in range(4):
        out += [res[n][k] for n in WEIGHTS]
    return tuple(out)
```

```python
import functools

import jax
import jax.numpy as jnp
import numpy as np
from jax import lax
from jax.experimental import pallas as pl
from jax.experimental.pallas import tpu as pltpu

F32 = jnp.float32
BF16 = jnp.bfloat16
MESH_ID = pl.DeviceIdType.MESH
AXES = ("x", "y", "c")
N_DEV = 8

D_MODEL = 1024
N_HEADS = 8
KV_HEADS = 2
HEAD_DIM = 64
Q_DIM = 512
KV_DIM = 128
BLOCK = 128
WIN = 3 * BLOCK
NUM_BUCKETS = 32
MAX_DISTANCE = 128
CONV_DIM = 512
CONV_WIDTH = 31
D_FF = 2816
FF_SHARD = 2 * D_FF // N_DEV
FF_BLOCKS = D_FF // FF_SHARD
QC_DIM = Q_DIM + 2 * KV_DIM + 2 * CONV_DIM
IN_DIM = QC_DIM + 2 * D_MODEL
NEG_INF = -1e9
HALO = 16
CROWS = 64
FFN_PART_ROWS = 256
TS_BF16 = 4096

ADAM_LR = 0.001
ADAM_B1 = 0.9
ADAM_B2 = 0.999
ADAM_EPS = 1e-08
ADAM_WD = 0.01
ADAM_STEP = 10

VMEM_LIMIT = 56 * 1024 * 1024
HI = lax.Precision.HIGHEST


def _params(sem):
    return pltpu.CompilerParams(dimension_semantics=sem, vmem_limit_bytes=VMEM_LIMIT)


def _dot(a, b, precision=None):
    return jnp.dot(a, b, preferred_element_type=F32, precision=precision)


def _dot_nt(a, b):
    return lax.dot_general(a, b, (((1,), (1,)), ((), ())), preferred_element_type=F32)


def _dot_tn(a, b):
    return lax.dot_general(a, b, (((0,), (0,)), ((), ())), preferred_element_type=F32)


def _sig(x):
    return 1.0 / (1.0 + jnp.exp(-x))


def _bf(x):
    return x.astype(BF16)


def _rms_fwd(x, gamma):
    r = lax.rsqrt(jnp.mean(x * x, axis=-1, keepdims=True) + 1e-6)
    return x * r * gamma, r


def _rms_bwd(dy, x, gamma):
    r = lax.rsqrt(jnp.mean(x * x, axis=-1, keepdims=True) + 1e-6)
    xhat = x * r
    dxhat = dy * gamma
    dx = r * (dxhat - xhat * jnp.mean(dxhat * xhat, axis=-1, keepdims=True))
    return dx, dy * xhat


SAME_CORE = (2, 4, 6)


class _Exchange:
    def __init__(self, arrs=(), gather=True, kinds=None):
        self.arrs = list(arrs)
        self.n = n = len(self.arrs)
        self.kinds = list(kinds) if kinds is not None else ["gather" if gather else "scatter"] * n
        self.out_shape = tuple(
            jax.ShapeDtypeStruct(((N_DEV,) + a.shape) if k in ("gather", "stage1") else a.shape, a.dtype)
            for a, k in zip(self.arrs, self.kinds))
        self.aliases = {t: t for t, k in enumerate(self.kinds) if k == "stage2"}
        self.specs = [pl.BlockSpec(memory_space=pl.ANY)] * n
        self.scratch = [pltpu.SemaphoreType.DMA((7 * n,)), pltpu.SemaphoreType.DMA((7 * n,)),
                        pltpu.SemaphoreType.DMA((n,))] if n else []

    def __add__(self, other):
        return _Exchange(self.arrs + other.arrs, kinds=self.kinds + other.kinds)

    def _copies(self, ins, outs, sems):
        n = self.n
        send_sems, recv_sems, local_sems = sems
        x, y, c = lax.axis_index("x"), lax.axis_index("y"), lax.axis_index("c")
        me = 4 * x + 2 * y + c
        copies = []
        for t, kind in enumerate(self.kinds):
            if kind != "stage2":
                copies.append(pltpu.make_async_copy(
                    ins[t].at[me] if kind == "scatter" else ins[t], outs[t].at[me], local_sems.at[t]))
            offsets = {"gather": range(1, N_DEV), "scatter": range(1, N_DEV),
                       "stage1": (1,) + SAME_CORE, "stage2": SAME_CORE}[kind]
            for d in offsets:
                px = 1 - x if d & 4 else x
                py = 1 - y if d & 2 else y
                pc = 1 - c if d & 1 else c
                peer = 4 * px + 2 * py + pc
                if kind == "stage2":
                    src, dst, to = ins[t].at[peer], outs[t].at[peer], (x, y, 1 - c)
                else:
                    src, dst, to = (ins[t].at[peer] if kind == "scatter" else ins[t]), outs[t].at[me], (px, py, pc)
                k = (d - 1) * n + t
                copies.append(pltpu.make_async_remote_copy(
                    src_ref=src, dst_ref=dst, send_sem=send_sems.at[k], recv_sem=recv_sems.at[k],
                    device_id=to, device_id_type=MESH_ID))
        return copies

    def start(self, ins, outs, sems):
        for cp in self._copies(ins, outs, sems):
            cp.start()

    def wait(self, ins, outs, sems):
        for cp in self._copies(ins, outs, sems):
            cp.wait()


NO_EXCHANGE = _Exchange()


def _stage1(arrs):
    return _Exchange(arrs, kinds=["stage1"] * len(arrs))


def _stage2(arrs):
    return _Exchange(arrs, kinds=["stage2"] * len(arrs))


def _exchange(ex, name):
    n = ex.n

    def body(*refs):
        ins, outs, sems = refs[:n], refs[n:2 * n], refs[2 * n:]
        ex.start(ins, outs, sems)
        ex.wait(ins, outs, sems)

    return pl.pallas_call(
        body, name=name, out_shape=ex.out_shape, in_specs=ex.specs, out_specs=tuple(ex.specs),
        scratch_shapes=ex.scratch, input_output_aliases=ex.aliases,
    )(*ex.arrs)


def _carrier_call(body, ex, first_last, name, out_shape, grid, in_specs, out_specs, scratch_shapes, args):
    n_in, n_out, n_scr, n = len(in_specs), len(out_shape), len(scratch_shapes), ex.n

    def full(*refs):
        a, ci = refs[:n_in], refs[n_in:n_in + n]
        o = refs[n_in + n:n_in + n + n_out]
        co = refs[n_in + n + n_out:n_in + 2 * n + n_out]
        scr = refs[n_in + 2 * n + n_out:n_in + 2 * n + n_out + n_scr]
        sems = refs[n_in + 2 * n + n_out + n_scr:]
        first, last = first_last()
        if n:
            @pl.when(first)
            def _():
                ex.start(ci, co, sems)

        body(*a, *o, *scr)
        if n:
            @pl.when(last)
            def _():
                ex.wait(ci, co, sems)

    outs = pl.pallas_call(
        full, name=name, out_shape=tuple(out_shape) + ex.out_shape, grid=grid,
        in_specs=list(in_specs) + ex.specs, out_specs=tuple(out_specs) + tuple(ex.specs),
        scratch_shapes=list(scratch_shapes) + ex.scratch,
        input_output_aliases={n_in + t: n_out + u for t, u in ex.aliases.items()},
        compiler_params=_params(("arbitrary",) * len(grid)),
    )(*args, *ex.arrs)
    return outs[:n_out], outs[n_out:]


def _matmul_tn(a, b, nb, name, scale=1.0, out_dtype=BF16, ts=2048, ex=None, a_index=None):
    ba, S, K = a.shape
    bb, _, N = b.shape
    ts = min(ts, S)
    tn = N if N <= 1024 else next(c for c in (1280, 1024, 768, 512, 256) if N % c == 0)
    assert S % ts == 0
    ns = S // ts

    def body(a_ref, b_ref, o_ref, acc):
        s = pl.program_id(2)

        @pl.when(s == 0)
        def _():
            acc[...] = jnp.zeros_like(acc)

        acc[...] += _dot_tn(_bf(a_ref[...]), _bf(b_ref[...]))

        @pl.when(s == ns - 1)
        def _():
            o_ref[...] = (acc[...] * scale).astype(out_dtype)

    a_map = (lambda i, j, s: (a_index, s, 0)) if a_index is not None else (
        (lambda i, j, s: (i, s, 0)) if ba > 1 else (lambda i, j, s: (0, s, 0)))
    in_specs = [pl.BlockSpec((None, ts, K), a_map),
                pl.BlockSpec((None, ts, tn), (lambda i, j, s: (i, s, j)) if bb > 1 else (lambda i, j, s: (0, s, j)))]
    out_spec = pl.BlockSpec((None, K, tn), lambda i, j, s: (i, 0, j))
    out_shape = jax.ShapeDtypeStruct((nb, K, N), out_dtype)
    scratch = [pltpu.VMEM((K, tn), F32)]
    grid = (nb, N // tn, ns)
    if ex is not None:
        (out,), got = _carrier_call(body, ex, _grid_ends(*grid), name, out_shape=(out_shape,), grid=grid,
                                    in_specs=in_specs, out_specs=(out_spec,), scratch_shapes=scratch, args=(a, b))
        return out, got
    return pl.pallas_call(
        body, name=name, out_shape=out_shape, grid=grid, in_specs=in_specs, out_specs=out_spec,
        scratch_shapes=scratch, compiler_params=_params(("parallel", "parallel", "arbitrary")),
    )(a, b)


def _ffn_specs(tm):
    wg = pl.BlockSpec((None, D_MODEL, FF_SHARD), lambda i, j: (j, 0, 0))
    wu = pl.BlockSpec((None, D_MODEL, FF_SHARD), lambda i, j: (j + FF_BLOCKS, 0, 0))
    wo = pl.BlockSpec((2, FF_SHARD // 2, D_MODEL), lambda i, j: (j, 0, 0))
    row = pl.BlockSpec((tm, D_MODEL), lambda i, j: (i, 0))
    vec = pl.BlockSpec((1, D_MODEL), lambda i, j: (0, 0))
    hid = pl.BlockSpec((None, tm, FF_SHARD), lambda i, j: (j, i, 0))
    return wg, wu, wo, row, vec, hid


def _grid_ends(*grid):
    def first_last():
        first, last = None, None
        for d, n in enumerate(grid):
            i = pl.program_id(d)
            first = (i == 0) if first is None else first & (i == 0)
            last = (i == n - 1) if last is None else last & (i == n - 1)
        return first, last
    return first_last


def _grid2_ends(ni, nj):
    return _grid_ends(ni, nj)


def _grid1_ends(ni):
    return _grid_ends(ni)


def _ffn_fwd(x, gamma, wi, wo, tm, name, ex=NO_EXCHANGE):
    S = x.shape[0]
    wg_s, wu_s, wo_s, row, vec, hid = _ffn_specs(tm)

    def body(x_ref, g_ref, wg_ref, wu_ref, wo_ref, y_ref, xn_ref, gs_ref, us_ref, xn_s, acc):
        j = pl.program_id(1)

        @pl.when(j == 0)
        def _():
            xn = _bf(_rms_fwd(x_ref[...], g_ref[...])[0])
            xn_s[...] = xn
            xn_ref[...] = xn
            acc[...] = jnp.zeros_like(acc)

        wo2 = wo_ref[...].reshape(FF_SHARD, D_MODEL)
        for r in range(tm // FFN_PART_ROWS):
            rows = slice(r * FFN_PART_ROWS, (r + 1) * FFN_PART_ROWS)
            xn = xn_s[rows, :]
            g = _dot(xn, wg_ref[...])
            u = _dot(xn, wu_ref[...])
            gs_ref[rows, :] = _bf(g)
            us_ref[rows, :] = _bf(u)
            a = g * _sig(g) * u
            acc[rows, :] += _dot(_bf(a), wo2)

        @pl.when(j == FF_BLOCKS - 1)
        def _():
            y_ref[...] = x_ref[...] + 0.5 * acc[...]

    return _carrier_call(
        body, ex, _grid2_ends(S // tm, FF_BLOCKS), name,
        out_shape=(jax.ShapeDtypeStruct((S, D_MODEL), F32), jax.ShapeDtypeStruct((S, D_MODEL), BF16),
                   jax.ShapeDtypeStruct((FF_BLOCKS, S, FF_SHARD), BF16),
                   jax.ShapeDtypeStruct((FF_BLOCKS, S, FF_SHARD), BF16)),
        grid=(S // tm, FF_BLOCKS),
        in_specs=[row, vec, wg_s, wu_s, wo_s],
        out_specs=(row, row, hid, hid),
        scratch_shapes=[pltpu.VMEM((tm, D_MODEL), BF16), pltpu.VMEM((tm, D_MODEL), F32)],
        args=(x, gamma, wi, wi, wo))


def _ffn_bwd(dy, x, gamma, gs, us, wi, wo, tm, name, ex=NO_EXCHANGE):
    S = x.shape[0]
    wg_s, wu_s, wo_s, row, vec, hid = _ffn_specs(tm)
    dgu_s = pl.BlockSpec((2, None, tm, FF_SHARD), lambda i, j: (0, j, i, 0))

    def body(dy_ref, x_hbm, g_ref, gs_ref, us_ref, wg_ref, wu_ref, wo_ref,
             dx_ref, a_ref, dgu_ref, dgam_ref, dyh_s, x_buf, x_sem):
        i, j = pl.program_id(0), pl.program_id(1)
        acc = dx_ref
        x_copy = pltpu.make_async_copy(x_hbm.at[pl.ds(pl.multiple_of(i * tm, tm), tm), :], x_buf, x_sem)

        @pl.when(j == 0)
        def _():
            x_copy.start()
            dyh_s[...] = _bf(0.5 * dy_ref[...])
            acc[...] = jnp.zeros_like(acc)

        @pl.when((i == 0) & (j == 0))
        def _():
            dgam_ref[...] = jnp.zeros_like(dgam_ref)

        wo2 = wo_ref[...].reshape(FF_SHARD, D_MODEL)
        for r in range(tm // FFN_PART_ROWS):
            rows = slice(r * FFN_PART_ROWS, (r + 1) * FFN_PART_ROWS)
            da = _dot_nt(dyh_s[rows, :], wo2)
            g = gs_ref[rows, :].astype(F32)
            u = us_ref[rows, :].astype(F32)
            sg = _sig(g)
            sl = g * sg
            a_ref[rows, :] = _bf(sl * u)
            dg = _bf(da * u * (sg * (1.0 + g * (1.0 - sg))))
            du = _bf(da * sl)
            dgu_ref[0, rows, :] = dg
            dgu_ref[1, rows, :] = du
            acc[rows, :] += _dot_nt(dg, wg_ref[...]) + _dot_nt(du, wu_ref[...])

        @pl.when(j == FF_BLOCKS - 1)
        def _():
            x_copy.wait()
            dx, dgam = _rms_bwd(acc[...], x_buf[...], g_ref[...])
            dx_ref[...] = dy_ref[...] + dx
            dgam_ref[...] += jnp.sum(dgam, axis=0, keepdims=True)

    return _carrier_call(
        body, ex, _grid2_ends(S // tm, FF_BLOCKS), name,
        out_shape=(jax.ShapeDtypeStruct((S, D_MODEL), F32),
                   jax.ShapeDtypeStruct((FF_BLOCKS, S, FF_SHARD), BF16),
                   jax.ShapeDtypeStruct((2, FF_BLOCKS, S, FF_SHARD), BF16),
                   jax.ShapeDtypeStruct((1, D_MODEL), F32)),
        grid=(S // tm, FF_BLOCKS),
        in_specs=[row, pl.BlockSpec(memory_space=pl.ANY), vec, hid, hid, wg_s, wu_s, wo_s],
        out_specs=(row, hid, dgu_s, vec),
        scratch_shapes=[pltpu.VMEM((tm, D_MODEL), BF16), pltpu.VMEM((tm, D_MODEL), F32),
                        pltpu.SemaphoreType.DMA(())],
        args=(dy, x, gamma, gs, us, wi, wi, wo))


def _mixin_fwd(h, gamma, w_qc, tm, name):
    S = h.shape[0]
    nqkv = Q_DIM + 2 * KV_DIM

    def body(h_ref, g_ref, w_ref, un_ref, qkv_ref, cvg_ref):
        un = _bf(_rms_fwd(h_ref[...], g_ref[...])[0])
        un_ref[...] = un
        z = _dot(un, w_ref[:, :QC_DIM])
        qkv_ref[...] = z[:, :nqkv]
        cvg_ref[...] = z[:, nqkv:]

    row = lambda w: pl.BlockSpec((tm, w), lambda i: (i, 0))
    return pl.pallas_call(
        body, name=name,
        out_shape=(jax.ShapeDtypeStruct((S, D_MODEL), BF16), jax.ShapeDtypeStruct((S, nqkv), F32),
                   jax.ShapeDtypeStruct((S, 2 * CONV_DIM), F32)),
        grid=(S // tm,),
        in_specs=[row(D_MODEL), pl.BlockSpec((1, D_MODEL), lambda i: (0, 0)),
                  pl.BlockSpec((D_MODEL, IN_DIM), lambda i: (0, 0))],
        out_specs=(row(D_MODEL), row(nqkv), row(2 * CONV_DIM)),
        compiler_params=_params(("parallel",)),
    )(h, gamma, w_qc)


def _mixin_bwd(dh2, h1, gamma, dq, dkv, dcvg, dgpre, w_in, tm, name):
    S = h1.shape[0]
    nqkv = Q_DIM + 2 * KV_DIM
    n_in = QC_DIM + 2 * D_MODEL

    def body(dh2_ref, h1_ref, g_ref, dq_ref, dkv_ref, dcvg_ref, dgp_ref, win_ref,
             dh1_ref, dgam_ref, dz_ref):
        @pl.when(pl.program_id(0) == 0)
        def _():
            dgam_ref[...] = jnp.zeros_like(dgam_ref)

        wqc = win_ref[:, :QC_DIM]
        dq, dkv, dcvg = _bf(dq_ref[...]), _bf(dkv_ref[...]), _bf(dcvg_ref[...])
        dz_ref[:, :Q_DIM] = dq
        dz_ref[:, Q_DIM:nqkv] = dkv
        dz_ref[:, nqkv:QC_DIM] = dcvg
        dz_ref[:, QC_DIM:] = dgp_ref[...]
        dun = _dot_nt(dq, wqc[:, :Q_DIM])
        dun += _dot_nt(dkv, wqc[:, Q_DIM:nqkv])
        dun += _dot_nt(dcvg, wqc[:, nqkv:])
        dun += _dot_nt(dgp_ref[...], win_ref[:, QC_DIM:])
        dx, dgam = _rms_bwd(dun, h1_ref[...], g_ref[...])
        dh1_ref[...] = dh2_ref[...] + dx
        dgam_ref[...] += jnp.sum(dgam, axis=0, keepdims=True)

    row = lambda w: pl.BlockSpec((tm, w), lambda i: (i, 0))
    vec = pl.BlockSpec((1, D_MODEL), lambda i: (0, 0))
    return pl.pallas_call(
        body, name=name,
        out_shape=(jax.ShapeDtypeStruct((S, D_MODEL), F32), jax.ShapeDtypeStruct((1, D_MODEL), F32),
                   jax.ShapeDtypeStruct((S, n_in), BF16)),
        grid=(S // tm,),
        in_specs=[row(D_MODEL), row(D_MODEL), vec, row(Q_DIM), row(2 * KV_DIM), row(2 * CONV_DIM),
                  row(2 * D_MODEL), pl.BlockSpec((D_MODEL, IN_DIM), lambda i: (0, 0))],
        out_specs=(row(D_MODEL), vec, row(n_in)),
        compiler_params=_params(("arbitrary",)),
    )(dh2, h1, gamma, dq, dkv, dcvg, dgpre, w_in)


TQ = 512
QB = TQ // BLOCK


def _attn_in_specs(S):
    nkb = S // BLOCK
    return [
        pl.BlockSpec((TQ, Q_DIM), lambda i: (i, 0)),
        pl.BlockSpec((BLOCK, 2 * KV_DIM), lambda i: (jnp.maximum(i * QB - 1, 0), Q_DIM // (2 * KV_DIM))),
        pl.BlockSpec((TQ, 2 * KV_DIM), lambda i: (i, Q_DIM // (2 * KV_DIM))),
        pl.BlockSpec((BLOCK, 2 * KV_DIM), lambda i: (jnp.minimum(i * QB + QB, nkb - 1), Q_DIM // (2 * KV_DIM))),
        pl.BlockSpec((1, 128), lambda i: (0, 0)),
        pl.BlockSpec((1, KV_DIM), lambda i: (0, 0)),
        pl.BlockSpec((N_HEADS, 128), lambda i: (0, 0)),
        pl.BlockSpec((N_HEADS, BLOCK, WIN), lambda i: (0, 0, 0)),
    ]


GROUP_ROWS = 4 * BLOCK


def _half_rstd(x, low):
    x2 = x * x
    z = jnp.zeros_like(x2)
    r0 = lax.rsqrt(jnp.sum(jnp.where(low, x2, z), axis=-1, keepdims=True) * (1.0 / HEAD_DIM) + 1e-6)
    r1 = lax.rsqrt(jnp.sum(jnp.where(low, z, x2), axis=-1, keepdims=True) * (1.0 / HEAD_DIM) + 1e-6)
    return jnp.where(low, r0, r1)


def _kv_windows(kvp_ref, kvc_ref, kvn_ref, kg_ref, low):
    kv = jnp.concatenate([kvp_ref[...], kvc_ref[...], kvn_ref[...]], axis=0)
    k, v = kv[:, :KV_DIM], kv[:, KV_DIM:]
    kn = k * _half_rstd(k, low) * kg_ref[...]
    kr, vr = pltpu.roll(kn, HEAD_DIM, 1), pltpu.roll(v, HEAD_DIM, 1)
    kdup = [_bf(jnp.where(low, kn, kr)), _bf(jnp.where(low, kr, kn))]
    vdup = [_bf(jnp.where(low, v, vr)), _bf(jnp.where(low, vr, v))]
    return kdup, vdup


def _stack_heads(x_ref, t, kh, low):
    rows = slice(t * BLOCK, (t + 1) * BLOCK)
    xa = x_ref[rows, 256 * kh:256 * kh + 128]
    xb = x_ref[rows, 256 * kh + 128:256 * kh + 256]
    z = jnp.zeros_like(xa)
    return jnp.concatenate([jnp.where(low, xa, z), jnp.where(low, z, xa),
                            jnp.where(low, xb, z), jnp.where(low, z, xb)], axis=0)


def _stacked_q(q_ref, qg_ref, t, kh, low):
    qraw = _stack_heads(q_ref, t, kh, low)
    rq = lax.rsqrt(jnp.sum(qraw * qraw, axis=-1, keepdims=True) * (1.0 / HEAD_DIM) + 1e-6)
    return qraw, rq, _bf(qraw * rq * (qg_ref[...] * (HEAD_DIM ** -0.5)))


def _unstack_heads(ov, low):
    return (jnp.where(low, ov[0:128], ov[128:256]), jnp.where(low, ov[256:384], ov[384:512]))


def _edge_bias(i, t, S):
    kpos = i * TQ + (t - 1) * BLOCK + lax.broadcasted_iota(jnp.int32, (1, WIN), 1)
    return jnp.where((kpos < 0) | (kpos >= S), NEG_INF, 0.0)


def _group_exp(lhs, kw, bias_ref, sink_ref, kh, edge):
    s = _dot_nt(lhs, kw) + bias_ref[4 * kh:4 * kh + 4].reshape(GROUP_ROWS, WIN)
    if edge is not None:
        s = s + edge
    sk = jnp.concatenate(
        [jnp.broadcast_to(sink_ref[4 * kh + r:4 * kh + r + 1, 0:1], (BLOCK, 1)) for r in range(4)], axis=0)
    m = jnp.maximum(jnp.max(s, axis=-1, keepdims=True), sk)
    return jnp.exp(s - m), jnp.exp(sk - m)


def _attn_fwd(qkv, qg, kg, sinkb, bias, name, ex=NO_EXCHANGE):
    S = qkv.shape[0]

    def body(q_ref, kvp_ref, kvc_ref, kvn_ref, qg_ref, kg_ref, sink_ref, bias_ref, o_ref):
        i = pl.program_id(0)
        low = lax.broadcasted_iota(jnp.int32, (1, 128), 1) < HEAD_DIM
        ones = jnp.ones((WIN, 128), BF16)
        kdup, vdup = _kv_windows(kvp_ref, kvc_ref, kvn_ref, kg_ref, low)
        for t in range(QB):
            edge = _edge_bias(i, t, S) if t in (0, QB - 1) else None
            rows = slice(t * BLOCK, (t + 1) * BLOCK)
            for kh in range(KV_HEADS):
                _, _, lhs = _stacked_q(q_ref, qg_ref, t, kh, low)
                kw = kdup[kh][t * BLOCK:t * BLOCK + WIN]
                vw = vdup[kh][t * BLOCK:t * BLOCK + WIN]
                e, es = _group_exp(lhs, kw, bias_ref, sink_ref, kh, edge)
                eb = _bf(e)
                ov = _dot(eb, vw) * (1.0 / (_dot(eb, ones) + es))
                oa, ob = _unstack_heads(ov, low)
                o_ref[rows, 256 * kh:256 * kh + 128] = _bf(oa)
                o_ref[rows, 256 * kh + 128:256 * kh + 256] = _bf(ob)

    return _carrier_call(
        body, ex, _grid1_ends(S // TQ), name, out_shape=(jax.ShapeDtypeStruct((S, Q_DIM), BF16),),
        grid=(S // TQ,), in_specs=_attn_in_specs(S),
        out_specs=(pl.BlockSpec((TQ, Q_DIM), lambda i: (i, 0)),), scratch_shapes=[],
        args=(qkv, qkv, qkv, qkv, qg, kg, sinkb, bias))


def _attn_bwd(do, qkv, qg, kg, sinkb, bias, dbias_in, name, ex=NO_EXCHANGE):
    S = qkv.shape[0]
    nkb = S // BLOCK
    nsteps = S // TQ

    def body(do_ref, q_ref, kvp_ref, kvc_ref, kvn_ref, qg_ref, kg_ref, sink_ref, bias_ref, dbin_ref,
             dq_ref, dkp_ref, dvp_ref, dbias_ref, dsink_ref, dqg_ref, dqg_s):
        i = pl.program_id(0)

        @pl.when(i == 0)
        def _():
            dbias_ref[...] = dbin_ref[...]
            dsink_ref[...] = jnp.zeros_like(dsink_ref)
            dqg_s[...] = jnp.zeros_like(dqg_s)

        low = lax.broadcasted_iota(jnp.int32, (1, 128), 1) < HEAD_DIM
        own = ((lax.broadcasted_iota(jnp.int32, (GROUP_ROWS, 128), 1) >> 6) & 1) == (
            (lax.broadcasted_iota(jnp.int32, (GROUP_ROWS, 128), 0) >> 7) & 1)
        gq = qg_ref[...] * (HEAD_DIM ** -0.5)
        ones = jnp.ones((WIN, 128), BF16)
        kdup, vdup = _kv_windows(kvp_ref, kvc_ref, kvn_ref, kg_ref, low)
        for t in range(QB):
            edge = _edge_bias(i, t, S) if t in (0, QB - 1) else None
            rows = slice(t * BLOCK, (t + 1) * BLOCK)
            dk_dup, dv_dup = [], []
            for kh in range(KV_HEADS):
                qraw, rq, lhs = _stacked_q(q_ref, qg_ref, t, kh, low)
                dos = _bf(_stack_heads(do_ref, t, kh, low))
                kw = kdup[kh][t * BLOCK:t * BLOCK + WIN]
                vw = vdup[kh][t * BLOCK:t * BLOCK + WIN]
                e, es = _group_exp(lhs, kw, bias_ref, sink_ref, kh, edge)
                inv = 1.0 / (_dot(_bf(e), ones) + es)
                pr = e * jnp.concatenate([inv] * (WIN // 128), axis=1)
                dpr = _dot_nt(dos, vw)
                delta = jnp.sum(pr * dpr, axis=-1, keepdims=True)
                ds = pr * (dpr - delta)
                dbias_ref[4 * kh:4 * kh + 4] += ds.reshape(4, BLOCK, WIN)
                dsk = es * inv[:, 0:1] * delta
                for r in range(4):
                    dsink_ref[4 * kh + r:4 * kh + r + 1, :] -= jnp.broadcast_to(
                        jnp.sum(dsk[r * BLOCK:(r + 1) * BLOCK], axis=0, keepdims=True), (1, 128))
                dsb = _bf(ds)
                dqs = _dot(dsb, kw)
                qhat = qraw * rq
                dxhat = jnp.where(own, dqs, 0.0) * gq
                dq_st = rq * (dxhat - qhat * (jnp.sum(dxhat * qhat, axis=-1, keepdims=True) * (1.0 / HEAD_DIM)))
                dq_ref[rows, 256 * kh:256 * kh + 128] = dq_st[0:128] + dq_st[128:256]
                dq_ref[rows, 256 * kh + 128:256 * kh + 256] = dq_st[256:384] + dq_st[384:512]
                dqg_s[...] += jnp.sum((dqs * qhat).reshape(GROUP_ROWS // 8, 8, 128), axis=0)
                dkx = _dot_tn(dsb, lhs)
                dvx = _dot_tn(_bf(pr), dos)
                dk_dup.append(dkx + pltpu.roll(dkx, HEAD_DIM, 1))
                dv_dup.append(dvx + pltpu.roll(dvx, HEAD_DIM, 1))
            dkp_ref[t] = jnp.where(low, dk_dup[0], dk_dup[1])
            dvp_ref[t] = jnp.where(low, dv_dup[0], dv_dup[1])

        @pl.when(i == nsteps - 1)
        def _():
            acc = dqg_s[...] * (HEAD_DIM ** -0.5)
            acc = acc + pltpu.roll(acc, HEAD_DIM, 1)
            dqg_ref[...] = jnp.broadcast_to(jnp.sum(acc, axis=0, keepdims=True), (8, 128))

    const2 = lambda shape: pl.BlockSpec(shape, lambda i: (0,) * len(shape))
    part = pl.BlockSpec((QB, WIN, KV_DIM), lambda i: (i, 0, 0))
    return _carrier_call(
        body, ex, _grid1_ends(nsteps), name,
        out_shape=(jax.ShapeDtypeStruct((S, Q_DIM), F32), jax.ShapeDtypeStruct((nkb, WIN, KV_DIM), F32),
                   jax.ShapeDtypeStruct((nkb, WIN, KV_DIM), F32),
                   jax.ShapeDtypeStruct((N_HEADS, BLOCK, WIN), F32), jax.ShapeDtypeStruct((N_HEADS, 128), F32),
                   jax.ShapeDtypeStruct((8, 128), F32)),
        grid=(nsteps,),
        in_specs=[pl.BlockSpec((TQ, Q_DIM), lambda i: (i, 0))] + _attn_in_specs(S)
        + [const2((N_HEADS, BLOCK, WIN))],
        out_specs=(pl.BlockSpec((TQ, Q_DIM), lambda i: (i, 0)), part, part,
                   const2((N_HEADS, BLOCK, WIN)), const2((N_HEADS, 128)), const2((8, 128))),
        scratch_shapes=[pltpu.VMEM((8, 128), F32)],
        args=(do, qkv, qkv, qkv, qkv, qg, kg, sinkb, bias, dbias_in))


def _kv_fold(dkp, dvp, qkv, kg, name):
    nkb = dkp.shape[0]
    S = nkb * BLOCK
    nsteps = S // TQ

    def body(kp_p, kp_c, kp_n, vp_p, vp_c, vp_n, kv_ref, kg_ref, dkv_ref, dkg_ref, dkg_s):
        i = pl.program_id(0)

        @pl.when(i == 0)
        def _():
            dkg_s[...] = jnp.zeros_like(dkg_s)

        def fold(p_ref, c_ref, n_ref):
            blocks = []
            for t in range(QB):
                acc = c_ref[t, BLOCK:2 * BLOCK, :]
                if t > 0:
                    acc = acc + c_ref[t - 1, 2 * BLOCK:, :]
                else:
                    acc = acc + jnp.where(i > 0, p_ref[0, 2 * BLOCK:, :], 0.0)
                if t < QB - 1:
                    acc = acc + c_ref[t + 1, :BLOCK, :]
                else:
                    acc = acc + jnp.where(i < nsteps - 1, n_ref[0, :BLOCK, :], 0.0)
                blocks.append(acc)
            return jnp.concatenate(blocks, axis=0)

        dkn = fold(kp_p, kp_c, kp_n)
        dv = fold(vp_p, vp_c, vp_n)
        k = kv_ref[:, :KV_DIM]
        low = lax.broadcasted_iota(jnp.int32, (1, 128), 1) < HEAD_DIM
        rk = _half_rstd(k, low)
        khat = k * rk
        dxhat = dkn * kg_ref[...]
        prod = dxhat * khat
        z = jnp.zeros_like(prod)
        mean = jnp.where(low, jnp.sum(jnp.where(low, prod, z), axis=-1, keepdims=True),
                         jnp.sum(jnp.where(low, z, prod), axis=-1, keepdims=True)) * (1.0 / HEAD_DIM)
        dkv_ref[:, :KV_DIM] = rk * (dxhat - khat * mean)
        dkv_ref[:, KV_DIM:] = dv
        dkg_s[...] += jnp.sum((dkn * khat).reshape(TQ // 8, 8, KV_DIM), axis=0)

        @pl.when(i == nsteps - 1)
        def _():
            acc = dkg_s[...] + pltpu.roll(dkg_s[...], HEAD_DIM, 1)
            dkg_ref[...] = jnp.broadcast_to(jnp.sum(acc, axis=0, keepdims=True), (8, 128))

    prev = pl.BlockSpec((1, WIN, KV_DIM), lambda i: (jnp.maximum(i * QB - 1, 0), 0, 0))
    cur = pl.BlockSpec((QB, WIN, KV_DIM), lambda i: (i, 0, 0))
    nxt = pl.BlockSpec((1, WIN, KV_DIM), lambda i: (jnp.minimum(i * QB + QB, nkb - 1), 0, 0))
    return pl.pallas_call(
        body, name=name,
        out_shape=(jax.ShapeDtypeStruct((S, 2 * KV_DIM), F32), jax.ShapeDtypeStruct((8, 128), F32)),
        grid=(nsteps,),
        in_specs=[prev, cur, nxt, prev, cur, nxt,
                  pl.BlockSpec((TQ, 2 * KV_DIM), lambda i: (i, Q_DIM // (2 * KV_DIM))),
                  pl.BlockSpec((1, KV_DIM), lambda i: (0, 0))],
        out_specs=(pl.BlockSpec((TQ, 2 * KV_DIM), lambda i: (i, 0)), pl.BlockSpec((8, 128), lambda i: (0, 0))),
        scratch_shapes=[pltpu.VMEM((8, KV_DIM), F32)],
        compiler_params=_params(("arbitrary",)),
    )(dkp, dkp, dkp, dvp, dvp, dvp, qkv, kg)


BIAS_COLS = BLOCK * WIN
BIAS_CHUNK = 6144


def _bias_table(rel_bias_t, onehot, band):
    def body(rb_ref, oh_ref, band_ref, o_ref):
        o_ref[...] = _dot(rb_ref[...], oh_ref[...], HI) + band_ref[...]

    return pl.pallas_call(
        body, name="bias_table", out_shape=jax.ShapeDtypeStruct((N_HEADS, BIAS_COLS), F32),
        grid=(BIAS_COLS // BIAS_CHUNK,),
        in_specs=[pl.BlockSpec((N_HEADS, NUM_BUCKETS), lambda i: (0, 0)),
                  pl.BlockSpec((NUM_BUCKETS, BIAS_CHUNK), lambda i: (0, i)),
                  pl.BlockSpec((1, BIAS_CHUNK), lambda i: (0, i))],
        out_specs=pl.BlockSpec((N_HEADS, BIAS_CHUNK), lambda i: (0, i)),
        compiler_params=_params(("parallel",)),
    )(rel_bias_t, onehot, band)


def _bias_grad(dbias, onehot):
    def body(db_ref, oh_ref, o_ref):
        @pl.when(pl.program_id(0) == 0)
        def _():
            o_ref[...] = jnp.zeros_like(o_ref)

        o_ref[...] += lax.dot_general(db_ref[...], oh_ref[...], (((1,), (1,)), ((), ())),
                                      preferred_element_type=F32, precision=HI)

    return pl.pallas_call(
        body, name="bias_grad", out_shape=jax.ShapeDtypeStruct((N_HEADS, NUM_BUCKETS), F32),
        grid=(BIAS_COLS // BIAS_CHUNK,),
        in_specs=[pl.BlockSpec((N_HEADS, BIAS_CHUNK), lambda i: (0, i)),
                  pl.BlockSpec((NUM_BUCKETS, BIAS_CHUNK), lambda i: (0, i))],
        out_specs=pl.BlockSpec((N_HEADS, NUM_BUCKETS), lambda i: (0, 0)),
        compiler_params=_params(("arbitrary",)),
    )(dbias, onehot)


def _bucket_onehot():
    half = NUM_BUCKETS // 2
    max_exact = half // 2
    rel = jnp.arange(WIN)[None, :] - BLOCK - jnp.arange(BLOCK)[:, None]
    n = jnp.abs(rel)
    ret = jnp.where(rel > 0, half, 0)
    nf = jnp.maximum(n, 1).astype(F32)
    large = max_exact + (jnp.log(nf / max_exact) / np.log(MAX_DISTANCE / max_exact)
                         * (half - max_exact)).astype(jnp.int32)
    large = jnp.minimum(large, half - 1)
    bucket = (ret + jnp.where(n < max_exact, n, large)).reshape(1, BIAS_COLS)
    band = jnp.where(n <= BLOCK, 0.0, NEG_INF).astype(F32).reshape(1, BIAS_COLS)
    return (bucket == jnp.arange(NUM_BUCKETS)[:, None]).astype(F32), band


def _halo_specs(tm, width, S):
    r = tm // HALO
    last = S // HALO - 1
    return [pl.BlockSpec((HALO, width), lambda i: (jnp.maximum(i * r - 1, 0), 0)),
            pl.BlockSpec((tm, width), lambda i: (i, 0)),
            pl.BlockSpec((HALO, width), lambda i: (jnp.minimum(i * r + r, last), 0))]


def _with_halo(p_ref, c_ref, n_ref):
    return jnp.concatenate([p_ref[...], c_ref[...], n_ref[...]], axis=0)


def _row_valid(i, tm, S):
    g = i * tm - HALO + lax.broadcasted_iota(jnp.int32, (tm + 2 * HALO, 1), 0)
    return (g >= 0) & (g < S)


def _shifted(x):
    n = x.shape[0]
    return [x if b == 0 else pltpu.roll(x, n - b, 0) for b in range(8)]


def _tap(sh, off, tm):
    a, b = off // 8, off % 8
    return sh[b][8 * a:8 * a + tm]


def _conv_fwd(cvg, cw, cb, lg, lb, tm, name, ex=NO_EXCHANGE):
    S = cvg.shape[0]

    def body(p_ref, c_ref, n_ref, cw_ref, cb_ref, lg_ref, lb_ref, act_ref, yc_ref):
        i = pl.program_id(0)
        z = _with_halo(p_ref, c_ref, n_ref)
        glu = jnp.where(_row_valid(i, tm, S), z[:, :CONV_DIM] * _sig(z[:, CONV_DIM:]), 0.0)
        sh = _shifted(glu)
        y = jnp.zeros((tm, CONV_DIM), F32) + cb_ref[...]
        for w in range(CONV_WIDTH):
            y = y + _tap(sh, w + 1, tm) * cw_ref[w:w + 1, :]
        yc_ref[...] = y
        mu = jnp.mean(y, axis=-1, keepdims=True)
        yc = y - mu
        rstd = lax.rsqrt(jnp.mean(yc * yc, axis=-1, keepdims=True) + 1e-5)
        ln = yc * rstd * lg_ref[...] + lb_ref[...]
        act_ref[...] = _bf(ln * _sig(ln))

    vec = pl.BlockSpec((1, CONV_DIM), lambda i: (0, 0))
    row = pl.BlockSpec((tm, CONV_DIM), lambda i: (i, 0))
    return _carrier_call(
        body, ex, _grid_ends(S // tm), name,
        out_shape=(jax.ShapeDtypeStruct((S, CONV_DIM), BF16), jax.ShapeDtypeStruct((S, CONV_DIM), F32)),
        grid=(S // tm,),
        in_specs=_halo_specs(tm, 2 * CONV_DIM, S) + [pl.BlockSpec((32, CONV_DIM), lambda i: (0, 0)), vec, vec, vec],
        out_specs=(row, row), scratch_shapes=[], args=(cvg, cvg, cvg, cw, cb, lg, lb))


def _conv_bwd(dact, yconv, cvg, cw, lg, lb, tm, name, ex=NO_EXCHANGE):
    S = cvg.shape[0]
    nsteps = S // tm

    def body(dp, dc, dn, yp, yc_, yn, zp, zc, zn, cw_ref, lg_ref, lb_ref,
             dz_ref, dcw_ref, dvec_ref, dcw_s, dvec_s, shg_s, shd_s):
        i = pl.program_id(0)

        @pl.when(i == 0)
        def _():
            dcw_s[...] = jnp.zeros_like(dcw_s)
            dvec_s[...] = jnp.zeros_like(dvec_s)

        valid = _row_valid(i, tm, S)
        own = (lax.broadcasted_iota(jnp.int32, (tm + 2 * HALO, 1), 0) >= HALO) & (
            lax.broadcasted_iota(jnp.int32, (tm + 2 * HALO, 1), 0) < HALO + tm)
        y = _with_halo(yp, yc_, yn)
        dact_ = _with_halo(dp, dc, dn)
        mu = jnp.mean(y, axis=-1, keepdims=True)
        ycen = y - mu
        rstd = lax.rsqrt(jnp.mean(ycen * ycen, axis=-1, keepdims=True) + 1e-5)
        yhat = ycen * rstd
        ln = yhat * lg_ref[...] + lb_ref[...]
        sg = _sig(ln)
        dln = dact_ * (sg * (1.0 + ln * (1.0 - sg)))
        dyhat = dln * lg_ref[...]
        dy = rstd * (dyhat - jnp.mean(dyhat, axis=-1, keepdims=True)
                     - yhat * jnp.mean(dyhat * yhat, axis=-1, keepdims=True))
        dy = jnp.where(valid, dy, 0.0)
        dln_own = jnp.where(own, dln, 0.0)
        nr = (tm + 2 * HALO) // 8
        dvec_s[0] += jnp.sum(jnp.where(own, dy, 0.0).reshape(nr, 8, CONV_DIM), axis=0)
        dvec_s[1] += jnp.sum((dln_own * yhat).reshape(nr, 8, CONV_DIM), axis=0)
        dvec_s[2] += jnp.sum(dln_own.reshape(nr, 8, CONV_DIM), axis=0)
        z = _with_halo(zp, zc, zn)
        glu = jnp.where(valid, z[:, :CONV_DIM] * _sig(z[:, CONV_DIM:]), 0.0)
        for b, (g_b, d_b) in enumerate(zip(_shifted(glu), _shifted(dy))):
            shg_s[b] = g_b
            shd_s[b] = d_b
        for cb in range(CONV_DIM // 128):
            lanes = slice(128 * cb, 128 * (cb + 1))
            for rb in range(tm // CROWS):
                r0 = rb * CROWS
                dy_own = shd_s[0, HALO + r0:HALO + r0 + CROWS, lanes]
                dglu = jnp.zeros((CROWS, 128), F32)
                for w in range(CONV_WIDTH):
                    a, b = divmod(CONV_WIDTH - w, 8)
                    dglu = dglu + shd_s[b, 8 * a + r0:8 * a + r0 + CROWS, lanes] * cw_ref[w:w + 1, lanes]
                    a, b = divmod(w + 1, 8)
                    prod = dy_own * shg_s[b, 8 * a + r0:8 * a + r0 + CROWS, lanes]
                    dcw_s[w, :, lanes] += jnp.sum(prod.reshape(CROWS // 8, 8, 128), axis=0)
                cv = zc[r0:r0 + CROWS, lanes]
                sg_o = _sig(zc[r0:r0 + CROWS, CONV_DIM + 128 * cb:CONV_DIM + 128 * (cb + 1)])
                dz_ref[r0:r0 + CROWS, lanes] = dglu * sg_o
                dz_ref[r0:r0 + CROWS, CONV_DIM + 128 * cb:CONV_DIM + 128 * (cb + 1)] = (
                    dglu * cv * sg_o * (1.0 - sg_o))

        @pl.when(i == nsteps - 1)
        def _():
            dcw_ref[...] = jnp.sum(dcw_s[...], axis=1)
            dvec_ref[...] = jnp.sum(dvec_s[...], axis=1)

    vec = pl.BlockSpec((1, CONV_DIM), lambda i: (0, 0))
    return _carrier_call(
        body, ex, _grid_ends(nsteps), name,
        out_shape=(jax.ShapeDtypeStruct((S, 2 * CONV_DIM), F32), jax.ShapeDtypeStruct((32, CONV_DIM), F32),
                   jax.ShapeDtypeStruct((8, CONV_DIM), F32)),
        grid=(nsteps,),
        in_specs=_halo_specs(tm, CONV_DIM, S) + _halo_specs(tm, CONV_DIM, S) + _halo_specs(tm, 2 * CONV_DIM, S)
        + [pl.BlockSpec((32, CONV_DIM), lambda i: (0, 0)), vec, vec],
        out_specs=(pl.BlockSpec((tm, 2 * CONV_DIM), lambda i: (i, 0)),
                   pl.BlockSpec((32, CONV_DIM), lambda i: (0, 0)), pl.BlockSpec((8, CONV_DIM), lambda i: (0, 0))),
        scratch_shapes=[pltpu.VMEM((32, 8, CONV_DIM), F32), pltpu.VMEM((8, 8, CONV_DIM), F32),
                        pltpu.VMEM((8, tm + 2 * HALO, CONV_DIM), F32), pltpu.VMEM((8, tm + 2 * HALO, CONV_DIM), F32)],
        args=(dact, dact, dact, yconv, yconv, yconv, cvg, cvg, cvg, cw, lg, lb))


def _merge_parts(un, o, cact, win_ref, wao_ref, wco_ref):
    g = _dot(un, win_ref[:, QC_DIM:])
    ga, gc = _sig(g[:, :D_MODEL]), _sig(g[:, D_MODEL:])
    ya = _dot(o, wao_ref[...])
    yc = _dot(cact, wco_ref[...])
    return ga, gc, ya, yc


def _merge_specs(tm):
    row = lambda w: pl.BlockSpec((tm, w), lambda i: (i, 0))
    full = lambda a, b: pl.BlockSpec((a, b), lambda i: (0, 0))
    weights = [full(D_MODEL, IN_DIM), full(Q_DIM, D_MODEL), full(CONV_DIM, D_MODEL), full(D_MODEL, D_MODEL)]
    return row, weights


def _merge_fwd(h1, un, o, cact, w_g, w_ao, w_co, w_o, tm, name, ex=NO_EXCHANGE):
    S = h1.shape[0]
    row, weights = _merge_specs(tm)

    def body(h1_ref, un_ref, o_ref, c_ref, wg_ref, wao_ref, wco_ref, wo_ref, h2_ref):
        ga, gc, ya, yc = _merge_parts(un_ref[...], o_ref[...], c_ref[...], wg_ref, wao_ref, wco_ref)
        h2_ref[...] = h1_ref[...] + _dot(_bf(ga * ya + gc * yc), wo_ref[...])

    return _carrier_call(
        body, ex, _grid_ends(S // tm), name, out_shape=(jax.ShapeDtypeStruct((S, D_MODEL), F32),),
        grid=(S // tm,),
        in_specs=[row(D_MODEL), row(D_MODEL), row(Q_DIM), row(CONV_DIM)] + weights,
        out_specs=(row(D_MODEL),), scratch_shapes=[], args=(h1, un, o, cact, w_g, w_ao, w_co, w_o))


def _merge_bwd(dh2, un, o, cact, w_g, w_ao, w_co, w_o, tm, name):
    S = dh2.shape[0]
    row, weights = _merge_specs(tm)

    def body(dh2_ref, un_ref, o_ref, c_ref, wg_ref, wao_ref, wco_ref, wo_ref,
             do_ref, dc_ref, mix_ref, dya_ref, dyc_ref, dgp_ref):
        ga, gc, ya, yc = _merge_parts(un_ref[...], o_ref[...], c_ref[...], wg_ref, wao_ref, wco_ref)
        mix_ref[...] = _bf(ga * ya + gc * yc)
        dmix = _dot_nt(_bf(dh2_ref[...]), wo_ref[...])
        dya = _bf(dmix * ga)
        dyc = _bf(dmix * gc)
        dya_ref[...] = dya
        dyc_ref[...] = dyc
        dgp_ref[:, :D_MODEL] = _bf(dmix * ya * ga * (1.0 - ga))
        dgp_ref[:, D_MODEL:] = _bf(dmix * yc * gc * (1.0 - gc))
        do_ref[...] = _dot_nt(dya, wao_ref[...])
        dc_ref[...] = _dot_nt(dyc, wco_ref[...])

    return pl.pallas_call(
        body, name=name,
        out_shape=(jax.ShapeDtypeStruct((S, Q_DIM), F32), jax.ShapeDtypeStruct((S, CONV_DIM), F32),
                   jax.ShapeDtypeStruct((S, D_MODEL), BF16), jax.ShapeDtypeStruct((S, D_MODEL), BF16),
                   jax.ShapeDtypeStruct((S, D_MODEL), BF16), jax.ShapeDtypeStruct((S, 2 * D_MODEL), BF16)),
        grid=(S // tm,),
        in_specs=[row(D_MODEL), row(D_MODEL), row(Q_DIM), row(CONV_DIM)] + weights,
        out_specs=(row(Q_DIM), row(CONV_DIM), row(D_MODEL), row(D_MODEL), row(D_MODEL), row(2 * D_MODEL)),
        compiler_params=_params(("parallel",)),
    )(dh2, un, o, cact, w_g, w_ao, w_co, w_o)


def _pe_specs(tm, layer):
    row = pl.BlockSpec((tm, D_MODEL), lambda i: (i, 0))
    vec = pl.BlockSpec((1, D_MODEL), lambda i: (0, 0))
    p_s = pl.BlockSpec((None, None, tm, 256), lambda i: (layer, 0, i, 0))
    wpp = pl.BlockSpec((256, D_MODEL), lambda i: (0, 0))
    wpg = pl.BlockSpec((D_MODEL, D_MODEL), lambda i: (0, 0))
    return row, vec, p_s, wpp, wpg


def _pe_fwd(h, gamma, p, layer, w_pp, w_pg, tm, name):
    S = h.shape[0]
    row, vec, p_s, wpp, wpg = _pe_specs(tm, layer)

    def body(h_ref, g_ref, p_ref, wpp_ref, wpg_ref, x_ref, hn_ref):
        hn = _bf(_rms_fwd(h_ref[...], g_ref[...])[0])
        hn_ref[...] = hn
        gate = _sig(_dot(hn, wpg_ref[...]))
        x_ref[...] = h_ref[...] + _dot(_bf(p_ref[...]), wpp_ref[...]) * gate

    return pl.pallas_call(
        body, name=name,
        out_shape=(jax.ShapeDtypeStruct((S, D_MODEL), F32), jax.ShapeDtypeStruct((S, D_MODEL), BF16)),
        grid=(S // tm,), in_specs=[row, vec, p_s, wpp, wpg], out_specs=(row, row),
        compiler_params=_params(("parallel",)),
    )(h, gamma, p, w_pp, w_pg)


def _pe_bwd(dx, h, gamma, hn, p, layer, w_pp, w_pg, tm, name):
    S = h.shape[0]
    row, vec, p_s, wpp, wpg = _pe_specs(tm, layer)

    def body(dx_ref, h_ref, g_ref, hn_ref, p_ref, wpp_ref, wpg_ref, dh_ref, dgp_ref, dpr_ref, dgam_ref):
        @pl.when(pl.program_id(0) == 0)
        def _():
            dgam_ref[...] = jnp.zeros_like(dgam_ref)

        for r in range(tm // FFN_PART_ROWS):
            rows = slice(r * FFN_PART_ROWS, (r + 1) * FFN_PART_ROWS)
            dxv = dx_ref[rows, :]
            gate = _sig(_dot(hn_ref[rows, :], wpg_ref[...]))
            proj = _dot(_bf(p_ref[rows, :]), wpp_ref[...])
            dpr_ref[rows, :] = _bf(dxv * gate)
            dgp = _bf(dxv * proj * gate * (1.0 - gate))
            dgp_ref[rows, :] = dgp
            dxn, dgam = _rms_bwd(_dot_nt(dgp, wpg_ref[...]), h_ref[rows, :], g_ref[...])
            dh_ref[rows, :] = dxv + dxn
            dgam_ref[...] += jnp.sum(dgam, axis=0, keepdims=True)

    return pl.pallas_call(
        body, name=name,
        out_shape=(jax.ShapeDtypeStruct((S, D_MODEL), F32), jax.ShapeDtypeStruct((S, D_MODEL), BF16),
                   jax.ShapeDtypeStruct((S, D_MODEL), BF16), jax.ShapeDtypeStruct((1, D_MODEL), F32)),
        grid=(S // tm,), in_specs=[row, row, vec, row, p_s, wpp, wpg], out_specs=(row, row, row, vec),
        compiler_params=_params(("arbitrary",)),
    )(dx, h, gamma, hn, p, w_pp, w_pg)


def _loss_head(y, target, tm):
    S = y.shape[0]

    def body(y_ref, t_ref, dy_ref, l_ref):
        @pl.when(pl.program_id(0) == 0)
        def _():
            l_ref[...] = jnp.zeros_like(l_ref)

        diff = y_ref[...] - t_ref[...]
        dy_ref[...] = diff * (1.0 / D_MODEL)
        sq = jnp.sum((diff * diff).reshape(tm // 8, 8, D_MODEL), axis=0)
        part = sq[:, 0:128]
        for k in range(1, D_MODEL // 128):
            part = part + sq[:, 128 * k:128 * (k + 1)]
        l_ref[...] += part

    row = pl.BlockSpec((tm, D_MODEL), lambda i: (i, 0))
    return pl.pallas_call(
        body, name="loss_head",
        out_shape=(jax.ShapeDtypeStruct((S, D_MODEL), F32), jax.ShapeDtypeStruct((8, 128), F32)),
        grid=(S // tm,), in_specs=[row, row], out_specs=(row, pl.BlockSpec((8, 128), lambda i: (0, 0))),
        compiler_params=_params(("arbitrary",)),
    )(y, target)


def _adamw(parts, w, m, v, name, ex=NO_EXCHANGE):
    nl = len(parts)
    R, C = w.shape
    K = R // nl
    tr = next((c for c in range(min(K, 256) // 16 * 16, 15, -16) if K % c == 0), K)
    nk = K // tr

    def body(*refs):
        p_refs = refs[:nl]
        w_ref, m_ref, v_ref, g_ref, d_ref, nm_ref, nv_ref = refs[nl:]
        for lyr in range(nl):
            @pl.when(pl.program_id(0) == lyr)
            def _(p_ref=p_refs[lyr]):
                g = p_ref[0].astype(F32)
                for k in range(1, N_DEV):
                    g = g + p_ref[k].astype(F32)
                g_ref[...] = g
                nm = ADAM_B1 * m_ref[...] + (1.0 - ADAM_B1) * g
                nv = ADAM_B2 * v_ref[...] + (1.0 - ADAM_B2) * (g * g)
                nm_ref[...] = nm
                nv_ref[...] = nv
                m_hat = nm / (1.0 - ADAM_B1 ** ADAM_STEP)
                v_hat = nv / (1.0 - ADAM_B2 ** ADAM_STEP)
                d_ref[...] = -ADAM_LR * (m_hat / (jnp.sqrt(v_hat) + ADAM_EPS) + ADAM_WD * w_ref[...])

    def part_spec(lyr):
        return pl.BlockSpec((N_DEV, tr, C), lambda l, i: (0, jnp.where(l == lyr, i, jnp.where(l < lyr, 0, nk - 1)), 0))

    blk = pl.BlockSpec((tr, C), lambda l, i: (l * nk + i, 0))
    out = jax.ShapeDtypeStruct((R, C), F32)
    return _carrier_call(
        body, ex, _grid_ends(nl, nk), name, out_shape=(out, out, out, out), grid=(nl, nk),
        in_specs=[part_spec(lyr) for lyr in range(nl)] + [blk, blk, blk],
        out_specs=(blk, blk, blk, blk), scratch_shapes=[], args=(*parts, w, m, v))


SHARDED = ("w_ffn1_in", "w_ffn1_out", "w_in", "conv_w", "w_attn_out", "w_conv_out", "w_o",
           "w_ffn2_in", "w_ffn2_out", "w_pe_gate", "w_pe_proj")
COL_SHARDED = ("w_ffn1_in", "w_in", "conv_w", "w_attn_out", "w_conv_out", "w_ffn2_in", "w_pe_proj")
SMALL = ("rel_bias", "norm_ffn1", "norm_mix", "q_norm", "k_norm", "sink", "conv_b", "conv_ln_g", "conv_ln_b",
         "norm_ffn2", "norm_pe")
WEIGHTS = ("rel_bias", "norm_ffn1", "w_ffn1_in", "w_ffn1_out", "norm_mix", "w_in", "q_norm", "k_norm", "sink",
           "conv_w", "conv_b", "conv_ln_g", "conv_ln_b", "w_attn_out", "w_conv_out", "w_o", "norm_ffn2",
           "w_ffn2_in", "w_ffn2_out", "norm_pe", "w_pe_gate", "w_pe_proj")


def _natural(g):
    k, n = g.shape[1], g.shape[2]
    return jnp.transpose(g, (1, 0, 2)).reshape(k, N_DEV * n)


def _blocked(w):
    k, n = w.shape[0], w.shape[1] // N_DEV
    return jnp.transpose(w.reshape(k, N_DEV, n), (1, 0, 2))


def kernel(x, p, rel_bias, norm_ffn1, w_ffn1_in, w_ffn1_out, norm_mix, w_in, q_norm, k_norm, sink, conv_w, conv_b, conv_ln_g, conv_ln_b, w_attn_out, w_conv_out, w_o, norm_ffn2, w_ffn2_in, w_ffn2_out, norm_pe, w_pe_gate, w_pe_proj, loss_target, m_rel_bias, m_norm_ffn1, m_w_ffn1_in, m_w_ffn1_out, m_norm_mix, m_w_in, m_q_norm, m_k_norm, m_sink, m_conv_w, m_conv_b, m_conv_ln_g, m_conv_ln_b, m_w_attn_out, m_w_conv_out, m_w_o, m_norm_ffn2, m_w_ffn2_in, m_w_ffn2_out, m_norm_pe, m_w_pe_gate, m_w_pe_proj, v_rel_bias, v_norm_ffn1, v_w_ffn1_in, v_w_ffn1_out, v_norm_mix, v_w_in, v_q_norm, v_k_norm, v_sink, v_conv_w, v_conv_b, v_conv_ln_g, v_conv_ln_b, v_w_attn_out, v_w_conv_out, v_w_o, v_norm_ffn2, v_w_ffn2_in, v_w_ffn2_out, v_norm_pe, v_w_pe_gate, v_w_pe_proj):
    W = dict(rel_bias=rel_bias, norm_ffn1=norm_ffn1, w_ffn1_in=w_ffn1_in, w_ffn1_out=w_ffn1_out, norm_mix=norm_mix,
             w_in=w_in, q_norm=q_norm, k_norm=k_norm, sink=sink, conv_w=conv_w, conv_b=conv_b, conv_ln_g=conv_ln_g,
             conv_ln_b=conv_ln_b, w_attn_out=w_attn_out, w_conv_out=w_conv_out, w_o=w_o, norm_ffn2=norm_ffn2,
             w_ffn2_in=w_ffn2_in, w_ffn2_out=w_ffn2_out, norm_pe=norm_pe, w_pe_gate=w_pe_gate, w_pe_proj=w_pe_proj)
    M = dict(rel_bias=m_rel_bias, norm_ffn1=m_norm_ffn1, w_ffn1_in=m_w_ffn1_in, w_ffn1_out=m_w_ffn1_out,
             norm_mix=m_norm_mix, w_in=m_w_in, q_norm=m_q_norm, k_norm=m_k_norm, sink=m_sink, conv_w=m_conv_w,
             conv_b=m_conv_b, conv_ln_g=m_conv_ln_g, conv_ln_b=m_conv_ln_b, w_attn_out=m_w_attn_out,
             w_conv_out=m_w_conv_out, w_o=m_w_o, norm_ffn2=m_norm_ffn2, w_ffn2_in=m_w_ffn2_in,
             w_ffn2_out=m_w_ffn2_out, norm_pe=m_norm_pe, w_pe_gate=m_w_pe_gate, w_pe_proj=m_w_pe_proj)
    V = dict(rel_bias=v_rel_bias, norm_ffn1=v_norm_ffn1, w_ffn1_in=v_w_ffn1_in, w_ffn1_out=v_w_ffn1_out,
             norm_mix=v_norm_mix, w_in=v_w_in, q_norm=v_q_norm, k_norm=v_k_norm, sink=v_sink, conv_w=v_conv_w,
             conv_b=v_conv_b, conv_ln_g=v_conv_ln_g, conv_ln_b=v_conv_ln_b, w_attn_out=v_w_attn_out,
             w_conv_out=v_w_conv_out, w_o=v_w_o, norm_ffn2=v_norm_ffn2, w_ffn2_in=v_w_ffn2_in,
             w_ffn2_out=v_w_ffn2_out, norm_pe=v_norm_pe, w_pe_gate=v_w_pe_gate, w_pe_proj=v_w_pe_proj)

    L = w_in.shape[0]
    S = x.shape[1]
    tm = min(512, S)
    tm_ffn = min(1024, S)
    xs = x[0]
    target = loss_target[0]
    vec = lambda a: a.reshape(1, -1)

    half, full = {}, {}

    def carried(stage1_items, stage2_items):
        s1 = [it for it in stage1_items if it[1] < L]
        s2 = [it for it in stage2_items if it[1] < L]
        ex = _stage1([W[n][l] if n == "conv_w" else W[n][l].astype(BF16) for n, l in s1]) + _stage2(
            [half.pop(it) for it in s2])
        return ex, s1, s2

    def landed(got, s1, s2):
        half.update(zip(s1, got[:len(s1)]))
        full.update(zip(s2, got[len(s1):]))

    onehot, band = _bucket_onehot()
    bias = _bias_table(rel_bias.T, onehot, band).reshape(N_HEADS, BLOCK, WIN)

    layers, saved = [], []
    h = xs
    ex, s1, s2 = carried([("w_ffn1_in", 0), ("w_ffn1_out", 0), ("w_in", 0), ("conv_w", 0)], [])
    landed(_exchange(ex, "allgather_first"), s1, s2)
    ex, s1, s2 = carried([], [("w_ffn1_in", 0), ("w_ffn1_out", 0)])
    landed(_exchange(ex, "allgather_relay"), s1, s2)
    for l in range(L):
        sv = dict(x0=h)
        G = dict(wi1=full.pop(("w_ffn1_in", l)), wo1=full.pop(("w_ffn1_out", l)))
        ex, s1, s2 = carried(
            [(n, l) for n in ("w_ffn2_in", "w_ffn2_out", "w_attn_out", "w_conv_out", "w_o")],
            [("w_in", l), ("conv_w", l)])
        (h1, sv["xn1"], sv["g1"], sv["u1"]), got = _ffn_fwd(
            h, vec(norm_ffn1[l]), G["wi1"], G["wo1"], tm_ffn, "ffn1_fwd", ex)
        landed(got, s1, s2)
        G.update(w_in=_natural(full.pop(("w_in", l))),
                 conv_w=jnp.pad(_natural(full.pop(("conv_w", l))), ((0, 1), (0, 0))))
        sv["h1"] = h1
        sv["un"], sv["qkv"], sv["cvg"] = _mixin_fwd(h1, vec(norm_mix[l]), G["w_in"], tm, "mixin_fwd")
        sv["qg"] = vec(jnp.tile(q_norm[l], 2))
        sv["kg"] = vec(jnp.tile(k_norm[l], KV_HEADS))
        sv["sinkb"] = jnp.broadcast_to(sink[l][:, None], (N_HEADS, 128))
        ex, s1, s2 = carried(
            [("w_pe_gate", l), ("w_pe_proj", l), ("w_ffn1_in", l + 1)],
            [(n, l) for n in ("w_ffn2_in", "w_ffn2_out", "w_attn_out", "w_conv_out", "w_o")])
        (sv["o"],), got = _attn_fwd(sv["qkv"], sv["qg"], sv["kg"], sv["sinkb"], bias, "attn_fwd", ex)
        landed(got, s1, s2)
        G.update(wi2=full.pop(("w_ffn2_in", l)), wo2=full.pop(("w_ffn2_out", l)),
                 w_ao=_natural(full.pop(("w_attn_out", l))), w_co=_natural(full.pop(("w_conv_out", l))),
                 w_o=full.pop(("w_o", l)).reshape(D_MODEL, D_MODEL))
        ex, s1, s2 = carried([("w_ffn1_out", l + 1)],
                             [("w_pe_gate", l), ("w_pe_proj", l), ("w_ffn1_in", l + 1)])
        (sv["cact"], sv["yconv"]), got = _conv_fwd(
            sv["cvg"], G["conv_w"], vec(conv_b[l]), vec(conv_ln_g[l]), vec(conv_ln_b[l]), tm, "conv_fwd", ex)
        landed(got, s1, s2)
        G.update(w_pg=full.pop(("w_pe_gate", l)).reshape(D_MODEL, D_MODEL),
                 w_pp=_natural(full.pop(("w_pe_proj", l))))
        ex, s1, s2 = carried([("w_in", l + 1), ("conv_w", l + 1)], [("w_ffn1_out", l + 1)])
        (h2,), got = _merge_fwd(h1, sv["un"], sv["o"], sv["cact"], G["w_in"], G["w_ao"], G["w_co"], G["w_o"], tm,
                                "merge_fwd", ex)
        landed(got, s1, s2)
        sv["h2"] = h2
        (h3, sv["xn2"], sv["g2"], sv["u2"]), _ = _ffn_fwd(
            h2, vec(norm_ffn2[l]), G["wi2"], G["wo2"], tm_ffn, "ffn2_fwd")
        sv["h3"] = h3
        h, sv["hn"] = _pe_fwd(h3, vec(norm_pe[l]), p, l, G["w_pp"], G["w_pg"], tm, "pe_fwd")
        layers.append(G)
        saved.append(sv)

    dh, lparts = _loss_head(h, target, tm)
    loss = lax.psum((0.5 / D_MODEL) * jnp.sum(lparts), AXES)

    dbias = jnp.zeros((N_HEADS, BLOCK, WIN), F32)
    small_g = {n: [None] * L for n in SMALL if n != "rel_bias"}
    recv = {n: [None] * L for n in SHARDED}

    def keep(names, l, got):
        for n, r in zip(names, got):
            recv[n][l] = r

    pending = None
    for l in reversed(range(L)):
        G, sv = layers[l], saved[l]
        dh3, dgp_pe, dproj, dg_pe = _pe_bwd(dh, sv["h3"], vec(norm_pe[l]), sv["hn"], p, l, G["w_pp"], G["w_pg"],
                                            tm, "pe_bwd")
        gw_pg = _matmul_tn(sv["hn"][None], dgp_pe[None], 1, "dw_pe_gate")
        gw_pp = _matmul_tn(p.reshape(L, S, p.shape[-1]), dproj[None], 1, "dw_pe_proj", a_index=l)
        (dh2, a2, dgu2, dg_n2), got = _ffn_bwd(
            dh3, sv["h2"], vec(norm_ffn2[l]), sv["g2"], sv["u2"], G["wi2"], G["wo2"], tm_ffn, "ffn2_bwd",
            _Exchange(pending, False) if pending else NO_EXCHANGE)
        if pending:
            keep(("w_ffn1_in",), l + 1, got)
        gwo2 = _matmul_tn(a2, dh3[None], FF_BLOCKS, "dw_ffn2_out", scale=0.5)
        gwi2 = _matmul_tn(sv["xn2"][None], dgu2.reshape(2 * FF_BLOCKS, S, FF_SHARD), 2 * FF_BLOCKS, "dw_ffn2_in",
                          ts=TS_BF16)
        do, dcact, mix, dya, dyc, dgpre = _merge_bwd(dh2, sv["un"], sv["o"], sv["cact"], G["w_in"], G["w_ao"],
                                                     G["w_co"], G["w_o"], tm, "merge_bwd")
        gw_o = _matmul_tn(mix[None], dh2[None], 1, "dw_o")
        gw_ao = _matmul_tn(sv["o"][None], dya[None], 1, "dw_attn_out")
        gw_co = _matmul_tn(sv["cact"][None], dyc[None], 1, "dw_conv_out")
        (dq, dkp, dvp, dbias, dsink, dqg), got = _attn_bwd(
            do, sv["qkv"], sv["qg"], sv["kg"], sv["sinkb"], bias, dbias, "attn_bwd",
            _Exchange([gwi2, gw_pg.reshape(N_DEV, D_MODEL // N_DEV, D_MODEL), _blocked(gw_pp[0])], False))
        keep(("w_ffn2_in", "w_pe_gate", "w_pe_proj"), l, got)
        dkv, dkg = _kv_fold(dkp, dvp, sv["qkv"], sv["kg"], "kv_fold")
        (dcvg, dcw, dcvec), got = _conv_bwd(
            dcact, sv["yconv"], sv["cvg"], G["conv_w"], vec(conv_ln_g[l]), vec(conv_ln_b[l]), tm, "conv_bwd",
            _Exchange([gwo2.reshape(N_DEV, FF_SHARD // 2, D_MODEL)], False))
        keep(("w_ffn2_out",), l, got)
        dh1, dg_mix, dz = _mixin_bwd(dh2, sv["h1"], vec(norm_mix[l]), dq, dkv, dcvg, dgpre, G["w_in"],
                                     tm, "mixin_bwd")
        gw_in = _matmul_tn(sv["un"][None], dz[None], 1, "dw_in", ts=TS_BF16)[0]
        mid_send = [_blocked(gw_in), _blocked(dcw[:CONV_WIDTH]), _blocked(gw_ao[0]), _blocked(gw_co[0]),
                    gw_o.reshape(N_DEV, D_MODEL // N_DEV, D_MODEL)]
        (dh, a1, dgu1, dg_n1), got = _ffn_bwd(
            dh1, sv["x0"], vec(norm_ffn1[l]), sv["g1"], sv["u1"], G["wi1"], G["wo1"], tm_ffn, "ffn1_bwd",
            _Exchange(mid_send, False))
        keep(("w_in", "conv_w", "w_attn_out", "w_conv_out", "w_o"), l, got)
        gwo1 = _matmul_tn(a1, dh1[None], FF_BLOCKS, "dw_ffn1_out", scale=0.5)
        gwi1, got = _matmul_tn(sv["xn1"][None], dgu1.reshape(2 * FF_BLOCKS, S, FF_SHARD), 2 * FF_BLOCKS,
                               "dw_ffn1_in", ts=TS_BF16,
                               ex=_Exchange([gwo1.reshape(N_DEV, FF_SHARD // 2, D_MODEL)], False))
        keep(("w_ffn1_out",), l, got)
        pending = [gwi1]
        small_g["norm_ffn1"][l] = dg_n1[0]
        small_g["norm_mix"][l] = dg_mix[0]
        small_g["q_norm"][l] = dqg[0, :HEAD_DIM]
        small_g["k_norm"][l] = dkg[0, :HEAD_DIM]
        small_g["sink"][l] = dsink[:, 0]
        small_g["conv_b"][l] = dcvec[0]
        small_g["conv_ln_g"][l] = dcvec[1]
        small_g["conv_ln_b"][l] = dcvec[2]
        small_g["norm_ffn2"][l] = dg_n2[0]
        small_g["norm_pe"][l] = dg_pe[0]

    keep(("w_ffn1_in",), 0, _exchange(_Exchange(pending, False), "grad_exchange_last"))
    grad_x = dh[None]
    drb = _bias_grad(dbias.reshape(N_HEADS, BIAS_COLS), onehot).T

    res = {}
    for n in SHARDED:
        shp = W[n].shape
        rows, cols = shp[0] * shp[1], shp[2]
        parts = recv[n]
        if shp[1] % 8:
            parts = [jnp.stack(recv[n], axis=1).reshape(N_DEV, rows, cols)]
        res[n] = [o.reshape(shp) for o in _adamw(
            parts, W[n].reshape(rows, cols), M[n].reshape(rows, cols), V[n].reshape(rows, cols), "adamw_" + n)[0]]

    flat_g = jnp.concatenate([drb.reshape(-1)] + [jnp.stack(small_g[n]).reshape(-1) for n in SMALL[1:]])
    n_small = flat_g.shape[0]
    rows_s = -(-n_small // 1024 // 8) * 8
    pad = lambda a: jnp.pad(a, (0, rows_s * 1024 - n_small)).reshape(rows_s, 1024)
    flat = lambda d: pad(jnp.concatenate([d[n].reshape(-1) for n in SMALL]))
    (parts_s,) = _exchange(_Exchange([pad(flat_g)], True), "small_allgather")
    outs_s = _adamw([parts_s], flat(W), flat(M), flat(V), "adamw_small")[0]
    off = 0
    for n in SMALL:
        size = W[n].size
        res[n] = [o.reshape(-1)[off:off + size].reshape(W[n].shape) for o in outs_s]
        off += size

    out = [loss, grad_x]
    for k in range(4):
        out += [res[n][k] for n in WEIGHTS]
    return tuple(out)
```
